```python
import jax, jax.numpy as jnp
from jax import lax
import numpy as np

D_MODEL = 1024
BATCH = 8
SEQ = 8192
DEPTH = 4

SB_HEADS = 8
SB_HEAD_DIM = 64
SB_WIDTH = SB_HEADS * SB_HEAD_DIM
SB_BLOCK = 128
POOL_WINDOWS = (2, 4, 8, 16)
POOL_GROUPS = 4
POOL_GROUP_DIM = 64
POOL_WIDTH = POOL_GROUPS * POOL_GROUP_DIM
GM_GROUPS = 4
GM_GROUP_DIM = 64
GM_WIDTH = GM_GROUPS * GM_GROUP_DIM
GM_CHUNK = 128
N_BRANCH = 3
D_FF = 4 * D_MODEL
RMS_EPS = 1e-6
IN_SIZES = (SB_WIDTH, SB_WIDTH, SB_WIDTH, POOL_WIDTH, GM_WIDTH, GM_WIDTH, N_BRANCH * D_MODEL)
D_IN = sum(IN_SIZES)
IN_SPLITS = tuple(int(s) for s in np.cumsum(IN_SIZES)[:-1])

kernel_name = "hybrid_stickbreak_pool_gmlp_block"


def rms_norm(x, gain):
    xf = x.astype(jnp.float32)
    y = xf * lax.rsqrt(jnp.mean(xf * xf, axis=-1, keepdims=True) + RMS_EPS)
    return (y * gain.astype(jnp.float32)).astype(x.dtype)


def stick_breaking_attention(q, k, v):
    B, S, H, Dh = q.shape
    scale = Dh ** -0.5
    outs = []
    for blk in range(S // SB_BLOCK):
        q0 = blk * SB_BLOCK
        q1 = q0 + SB_BLOCK
        qb = q[:, q0:q1]
        kb = k[:, :q1]
        vb = v[:, :q1]
        z = jnp.einsum('bthd,bshd->bhts', qb, kb).astype(jnp.float32) * scale
        t_idx = q0 + jnp.arange(SB_BLOCK)[:, None]
        s_idx = jnp.arange(q1)[None, :]
        strict = s_idx < t_idx
        log_not = jnp.where(strict, jax.nn.log_sigmoid(-z), 0.0)
        suffix = lax.cumsum(log_not, axis=3, reverse=True) - log_not
        a = jnp.where(strict, jnp.exp(jax.nn.log_sigmoid(z) + suffix), 0.0).astype(v.dtype)
        outs.append(jnp.einsum('bhts,bshd->bthd', a, vb))
    return jnp.concatenate(outs, axis=1)


def multiscale_pool(p, w_pool, pool_scale):
    B, S, _ = p.shape
    pg = p.reshape(B, S, POOL_GROUPS, POOL_GROUP_DIM)
    csum = jnp.cumsum(pg.astype(jnp.float32), axis=1)
    pos = jnp.arange(S, dtype=jnp.float32)
    pooled = []
    for g, w in enumerate(POOL_WINDOWS):
        cg = csum[:, :, g]
        shifted = jnp.pad(cg, ((0, 0), (w, 0), (0, 0)))[:, :S]
        count = jnp.minimum(pos + 1.0, float(w))[None, :, None]
        pooled.append((cg - shifted) / count - pg[:, :, g].astype(jnp.float32))
    pooled = jnp.stack(pooled, axis=2).astype(p.dtype)
    mixed = jnp.einsum('bsgc,gcd->bsgd', pooled, w_pool)
    return mixed.reshape(B, S, POOL_WIDTH) * pool_scale


def chunked_spatial_gating(u, v, gm_gain, w_spatial, b_spatial):
    B, S, _ = u.shape
    u = jax.nn.gelu(u)
    v = rms_norm(jax.nn.gelu(v), gm_gain)
    n_chunks = S // GM_CHUNK
    vc = v.reshape(B, n_chunks, GM_CHUNK, GM_GROUPS, GM_GROUP_DIM)
    causal = jnp.tril(jnp.ones((GM_CHUNK, GM_CHUNK), dtype=bool))
    ws = jnp.where(causal[None], w_spatial, 0.0).astype(v.dtype)
    mixed = jnp.einsum('gtp,bnpgc->bntgc', ws, vc) + b_spatial.T[:, :, None]
    return u * mixed.reshape(B, S, GM_WIDTH)


def _fwd_setup_inputs(seed: int = 0) -> dict:
    key = jax.random.key(seed)
    ks = jax.random.split(key, 18)

    def nrm(k, shape, scale):
        return jax.random.normal(k, shape, jnp.float32) * scale

    def gain(k, shape):
        return 1.0 + 0.05 * jax.random.normal(k, shape, jnp.float32)

    return {
        "x": nrm(ks[0], (BATCH, SEQ, D_MODEL), 1.0),
        "w_in": nrm(ks[1], (DEPTH, D_MODEL, D_IN), D_MODEL ** -0.5),
        "w_pool": nrm(ks[2], (DEPTH, POOL_GROUPS, POOL_GROUP_DIM, POOL_GROUP_DIM), POOL_GROUP_DIM ** -0.5),
        "pool_scale": gain(ks[3], (DEPTH, POOL_WIDTH)),
        "gm_gain": gain(ks[4], (DEPTH, GM_WIDTH)),
        "w_spatial": nrm(ks[5], (DEPTH, GM_GROUPS, GM_CHUNK, GM_CHUNK), GM_CHUNK ** -0.5),
        "b_spatial": gain(ks[6], (DEPTH, GM_GROUPS, GM_CHUNK)),
        "w_br_sb": nrm(ks[7], (DEPTH, SB_WIDTH, D_MODEL), SB_WIDTH ** -0.5),
        "w_br_pool": nrm(ks[8], (DEPTH, POOL_WIDTH, D_MODEL), POOL_WIDTH ** -0.5),
        "w_br_gm": nrm(ks[9], (DEPTH, GM_WIDTH, D_MODEL), GM_WIDTH ** -0.5),
        "w_out": nrm(ks[10], (DEPTH, D_MODEL, D_MODEL), D_MODEL ** -0.5),
        "g_mix_pre": gain(ks[11], (DEPTH, D_MODEL)),
        "g_mix_post": gain(ks[12], (DEPTH, D_MODEL)),
        "g_ff_pre": gain(ks[13], (DEPTH, D_MODEL)),
        "g_ff_post": gain(ks[14], (DEPTH, D_MODEL)),
        "w_ff_in": nrm(ks[15], (DEPTH, D_MODEL, D_FF), D_MODEL ** -0.5),
        "w_ff_out": nrm(ks[16], (DEPTH, D_FF, D_MODEL), D_FF ** -0.5),
    }


def _fwd_reference(x, w_in, w_pool, pool_scale, gm_gain, w_spatial, b_spatial, w_br_sb, w_br_pool,
              w_br_gm, w_out, g_mix_pre, g_mix_post, g_ff_pre, g_ff_post, w_ff_in, w_ff_out):
    B, S, D = x.shape
    for l in range(DEPTH):
        h = rms_norm(x, g_mix_pre[l])
        proj = h @ w_in[l]
        q, k, v, p_in, gm_u, gm_v, gate_in = jnp.split(proj, IN_SPLITS, axis=-1)
        o_sb = stick_breaking_attention(q.reshape(B, S, SB_HEADS, SB_HEAD_DIM),
                                        k.reshape(B, S, SB_HEADS, SB_HEAD_DIM),
                                        v.reshape(B, S, SB_HEADS, SB_HEAD_DIM)).reshape(B, S, SB_WIDTH)
        o_pool = multiscale_pool(p_in, w_pool[l], pool_scale[l])
        o_gm = chunked_spatial_gating(gm_u, gm_v, gm_gain[l], w_spatial[l], b_spatial[l])
        gates = jax.nn.sigmoid(gate_in.reshape(B, S, N_BRANCH, D))
        merged = (gates[:, :, 0] * (o_sb @ w_br_sb[l])
                  + gates[:, :, 1] * (o_pool @ w_br_pool[l])
                  + gates[:, :, 2] * (o_gm @ w_br_gm[l]))
        x = x + rms_norm(merged @ w_out[l], g_mix_post[l])
        h = rms_norm(x, g_ff_pre[l])
        ff = jnp.square(jax.nn.relu(h @ w_ff_in[l])) @ w_ff_out[l]
        x = x + rms_norm(ff, g_ff_post[l])
    return x


import jax as _jax
import jax.numpy as _jnp

TWIN_FORMAT = 'train_step'
FWD_PARAMS = ['x', 'w_in', 'w_pool', 'pool_scale', 'gm_gain', 'w_spatial', 'b_spatial', 'w_br_sb', 'w_br_pool', 'w_br_gm', 'w_out', 'g_mix_pre', 'g_mix_post', 'g_ff_pre', 'g_ff_post', 'w_ff_in', 'w_ff_out']
TWIN_WEIGHTS = ['w_in', 'w_pool', 'pool_scale', 'gm_gain', 'w_spatial', 'b_spatial', 'w_br_sb', 'w_br_pool', 'w_br_gm', 'w_out', 'g_mix_pre', 'g_mix_post', 'g_ff_pre', 'g_ff_post', 'w_ff_in', 'w_ff_out']
TWIN_DIFF_INPUT = 'x'
TWIN_INPUTS = ['x', 'w_in', 'w_pool', 'pool_scale', 'gm_gain', 'w_spatial', 'b_spatial', 'w_br_sb', 'w_br_pool', 'w_br_gm', 'w_out', 'g_mix_pre', 'g_mix_post', 'g_ff_pre', 'g_ff_post', 'w_ff_in', 'w_ff_out', 'loss_target', 'm_w_in', 'm_w_pool', 'm_pool_scale', 'm_gm_gain', 'm_w_spatial', 'm_b_spatial', 'm_w_br_sb', 'm_w_br_pool', 'm_w_br_gm', 'm_w_out', 'm_g_mix_pre', 'm_g_mix_post', 'm_g_ff_pre', 'm_g_ff_post', 'm_w_ff_in', 'm_w_ff_out', 'v_w_in', 'v_w_pool', 'v_pool_scale', 'v_gm_gain', 'v_w_spatial', 'v_b_spatial', 'v_w_br_sb', 'v_w_br_pool', 'v_w_br_gm', 'v_w_out', 'v_g_mix_pre', 'v_g_mix_post', 'v_g_ff_pre', 'v_g_ff_post', 'v_w_ff_in', 'v_w_ff_out']
TWIN_OUTPUTS = ['loss', 'grad_x', 'grad_w_in', 'grad_w_pool', 'grad_pool_scale', 'grad_gm_gain', 'grad_w_spatial', 'grad_b_spatial', 'grad_w_br_sb', 'grad_w_br_pool', 'grad_w_br_gm', 'grad_w_out', 'grad_g_mix_pre', 'grad_g_mix_post', 'grad_g_ff_pre', 'grad_g_ff_post', 'grad_w_ff_in', 'grad_w_ff_out', 'delta_w_in', 'delta_w_pool', 'delta_pool_scale', 'delta_gm_gain', 'delta_w_spatial', 'delta_b_spatial', 'delta_w_br_sb', 'delta_w_br_pool', 'delta_w_br_gm', 'delta_w_out', 'delta_g_mix_pre', 'delta_g_mix_post', 'delta_g_ff_pre', 'delta_g_ff_post', 'delta_w_ff_in', 'delta_w_ff_out', 'new_m_w_in', 'new_m_w_pool', 'new_m_pool_scale', 'new_m_gm_gain', 'new_m_w_spatial', 'new_m_b_spatial', 'new_m_w_br_sb', 'new_m_w_br_pool', 'new_m_w_br_gm', 'new_m_w_out', 'new_m_g_mix_pre', 'new_m_g_mix_post', 'new_m_g_ff_pre', 'new_m_g_ff_post', 'new_m_w_ff_in', 'new_m_w_ff_out', 'new_v_w_in', 'new_v_w_pool', 'new_v_pool_scale', 'new_v_gm_gain', 'new_v_w_spatial', 'new_v_b_spatial', 'new_v_w_br_sb', 'new_v_w_br_pool', 'new_v_w_br_gm', 'new_v_w_out', 'new_v_g_mix_pre', 'new_v_g_mix_post', 'new_v_g_ff_pre', 'new_v_g_ff_post', 'new_v_w_ff_in', 'new_v_w_ff_out']
TWIN_LEAF_KINDS = {'loss': 'loss', 'grad_x': 'grad_x', 'grad_w_in': 'grad_w', 'grad_w_pool': 'grad_w', 'grad_pool_scale': 'grad_w', 'grad_gm_gain': 'grad_w', 'grad_w_spatial': 'grad_w', 'grad_b_spatial': 'grad_w', 'grad_w_br_sb': 'grad_w', 'grad_w_br_pool': 'grad_w', 'grad_w_br_gm': 'grad_w', 'grad_w_out': 'grad_w', 'grad_g_mix_pre': 'grad_w', 'grad_g_mix_post': 'grad_w', 'grad_g_ff_pre': 'grad_w', 'grad_g_ff_post': 'grad_w', 'grad_w_ff_in': 'grad_w', 'grad_w_ff_out': 'grad_w', 'delta_w_in': 'delta_w', 'delta_w_pool': 'delta_w', 'delta_pool_scale': 'delta_w', 'delta_gm_gain': 'delta_w', 'delta_w_spatial': 'delta_w', 'delta_b_spatial': 'delta_w', 'delta_w_br_sb': 'delta_w', 'delta_w_br_pool': 'delta_w', 'delta_w_br_gm': 'delta_w', 'delta_w_out': 'delta_w', 'delta_g_mix_pre': 'delta_w', 'delta_g_mix_post': 'delta_w', 'delta_g_ff_pre': 'delta_w', 'delta_g_ff_post': 'delta_w', 'delta_w_ff_in': 'delta_w', 'delta_w_ff_out': 'delta_w', 'new_m_w_in': 'new_m', 'new_m_w_pool': 'new_m', 'new_m_pool_scale': 'new_m', 'new_m_gm_gain': 'new_m', 'new_m_w_spatial': 'new_m', 'new_m_b_spatial': 'new_m', 'new_m_w_br_sb': 'new_m', 'new_m_w_br_pool': 'new_m', 'new_m_w_br_gm': 'new_m', 'new_m_w_out': 'new_m', 'new_m_g_mix_pre': 'new_m', 'new_m_g_mix_post': 'new_m', 'new_m_g_ff_pre': 'new_m', 'new_m_g_ff_post': 'new_m', 'new_m_w_ff_in': 'new_m', 'new_m_w_ff_out': 'new_m', 'new_v_w_in': 'new_v', 'new_v_w_pool': 'new_v', 'new_v_pool_scale': 'new_v', 'new_v_gm_gain': 'new_v', 'new_v_w_spatial': 'new_v', 'new_v_b_spatial': 'new_v', 'new_v_w_br_sb': 'new_v', 'new_v_w_br_pool': 'new_v', 'new_v_w_br_gm': 'new_v', 'new_v_w_out': 'new_v', 'new_v_g_mix_pre': 'new_v', 'new_v_g_mix_post': 'new_v', 'new_v_g_ff_pre': 'new_v', 'new_v_g_ff_post': 'new_v', 'new_v_w_ff_in': 'new_v', 'new_v_w_ff_out': 'new_v'}


def _forward(args):
    return _fwd_reference(*[args[k] for k in FWD_PARAMS])


def _output_shape():
    def fwd():
        inp = _fwd_setup_inputs(0)
        return _fwd_reference(*[inp[k] for k in FWD_PARAMS])
    out = _jax.eval_shape(fwd)
    return out.shape, out.dtype

N_MICROBATCH = 1
ADAM_LR = 0.001
ADAM_B1 = 0.9
ADAM_B2 = 0.999
ADAM_EPS = 1e-08
ADAM_WD = 0.01
ADAM_STEP = 10
PER_EXAMPLE_BATCH_AXIS = {'x': 0, 'loss_target': 0}
SHARED_INPUTS = []
_WEIGHT_DTYPES = {'w_in': _jnp.float32, 'w_pool': _jnp.float32, 'pool_scale': _jnp.float32, 'gm_gain': _jnp.float32, 'w_spatial': _jnp.float32, 'b_spatial': _jnp.float32, 'w_br_sb': _jnp.float32, 'w_br_pool': _jnp.float32, 'w_br_gm': _jnp.float32, 'w_out': _jnp.float32, 'g_mix_pre': _jnp.float32, 'g_mix_post': _jnp.float32, 'g_ff_pre': _jnp.float32, 'g_ff_post': _jnp.float32, 'w_ff_in': _jnp.float32, 'w_ff_out': _jnp.float32}
MOMENT_SCALE = {'w_in': 6.256772e+00, 'w_pool': 6.501154e+00, 'pool_scale': 7.759303e+00, 'gm_gain': 1.324228e+00, 'w_spatial': 7.261580e-01, 'b_spatial': 1.406877e+00, 'w_br_sb': 1.336421e+01, 'w_br_pool': 3.648277e+00, 'w_br_gm': 1.697364e+01, 'w_out': 2.155822e+01, 'g_mix_pre': 1.449037e+01, 'g_mix_post': 6.813400e+01, 'g_ff_pre': 1.327898e+01, 'g_ff_post': 7.439430e+01, 'w_ff_in': 6.478838e+00, 'w_ff_out': 3.245962e+01}


def _to_microbatches(a, axis):
    t = _jnp.moveaxis(a, axis, 0)
    t = t.reshape((N_MICROBATCH, t.shape[0] // N_MICROBATCH) + t.shape[1:])
    return _jnp.moveaxis(t, 1, axis + 1)


def setup_inputs(seed: int = 0) -> dict:
    inp = _fwd_setup_inputs(seed)
    key = _jax.random.fold_in(_jax.random.key(seed), 7919)
    shape, _ = _output_shape()
    out = dict(inp)
    out["loss_target"] = _jax.random.normal(_jax.random.fold_in(key, 0), shape, _jnp.float32)
    for i, name in enumerate(TWIN_WEIGHTS):
        w = inp[name].astype(_jnp.float32)
        if MOMENT_SCALE is None:
            s = _jnp.sqrt(_jnp.mean(_jnp.square(w)) + 1e-30)
        else:
            s = MOMENT_SCALE[name]
        km, kv = _jax.random.split(_jax.random.fold_in(key, i + 1))
        out[name] = w
        out["m_" + name] = s * _jax.random.normal(km, w.shape, _jnp.float32)
        out["v_" + name] = (s * s) * _jax.random.uniform(kv, w.shape, _jnp.float32, 0.5, 1.5)
    if N_MICROBATCH > 1:
        for name, axis in PER_EXAMPLE_BATCH_AXIS.items():
            out[name] = _to_microbatches(out[name], axis)
    return {'x': out['x'], 'w_in': out['w_in'], 'w_pool': out['w_pool'], 'pool_scale': out['pool_scale'], 'gm_gain': out['gm_gain'], 'w_spatial': out['w_spatial'], 'b_spatial': out['b_spatial'], 'w_br_sb': out['w_br_sb'], 'w_br_pool': out['w_br_pool'], 'w_br_gm': out['w_br_gm'], 'w_out': out['w_out'], 'g_mix_pre': out['g_mix_pre'], 'g_mix_post': out['g_mix_post'], 'g_ff_pre': out['g_ff_pre'], 'g_ff_post': out['g_ff_post'], 'w_ff_in': out['w_ff_in'], 'w_ff_out': out['w_ff_out'], 'loss_target': out['loss_target'], 'm_w_in': out['m_w_in'], 'm_w_pool': out['m_w_pool'], 'm_pool_scale': out['m_pool_scale'], 'm_gm_gain': out['m_gm_gain'], 'm_w_spatial': out['m_w_spatial'], 'm_b_spatial': out['m_b_spatial'], 'm_w_br_sb': out['m_w_br_sb'], 'm_w_br_pool': out['m_w_br_pool'], 'm_w_br_gm': out['m_w_br_gm'], 'm_w_out': out['m_w_out'], 'm_g_mix_pre': out['m_g_mix_pre'], 'm_g_mix_post': out['m_g_mix_post'], 'm_g_ff_pre': out['m_g_ff_pre'], 'm_g_ff_post': out['m_g_ff_post'], 'm_w_ff_in': out['m_w_ff_in'], 'm_w_ff_out': out['m_w_ff_out'], 'v_w_in': out['v_w_in'], 'v_w_pool': out['v_w_pool'], 'v_pool_scale': out['v_pool_scale'], 'v_gm_gain': out['v_gm_gain'], 'v_w_spatial': out['v_w_spatial'], 'v_b_spatial': out['v_b_spatial'], 'v_w_br_sb': out['v_w_br_sb'], 'v_w_br_pool': out['v_w_br_pool'], 'v_w_br_gm': out['v_w_br_gm'], 'v_w_out': out['v_w_out'], 'v_g_mix_pre': out['v_g_mix_pre'], 'v_g_mix_post': out['v_g_mix_post'], 'v_g_ff_pre': out['v_g_ff_pre'], 'v_g_ff_post': out['v_g_ff_post'], 'v_w_ff_in': out['v_w_ff_in'], 'v_w_ff_out': out['v_w_ff_out']}


def _loss(weights, diff, rest, loss_target):
    with _jax.named_scope("forward"):
        args = {**rest, TWIN_DIFF_INPUT: diff, **{k: w.astype(_WEIGHT_DTYPES[k]) for k, w in weights.items()}}
        y = _forward(args)
    with _jax.named_scope("loss_head"):
        err = _jnp.square(y.astype(_jnp.float32) - loss_target)
        return 0.5 * _jnp.sum(_jnp.mean(err, axis=-1)) if err.ndim else 0.5 * err


def _adamw(w, g, m, v):
    m = ADAM_B1 * m + (1.0 - ADAM_B1) * g
    v = ADAM_B2 * v + (1.0 - ADAM_B2) * _jnp.square(g)
    m_hat = m / (1.0 - ADAM_B1 ** ADAM_STEP)
    v_hat = v / (1.0 - ADAM_B2 ** ADAM_STEP)
    delta = -ADAM_LR * (m_hat / (_jnp.sqrt(v_hat) + ADAM_EPS) + ADAM_WD * w)
    return delta, m, v


def reference(x, w_in, w_pool, pool_scale, gm_gain, w_spatial, b_spatial, w_br_sb, w_br_pool, w_br_gm, w_out, g_mix_pre, g_mix_post, g_ff_pre, g_ff_post, w_ff_in, w_ff_out, loss_target, m_w_in, m_w_pool, m_pool_scale, m_gm_gain, m_w_spatial, m_b_spatial, m_w_br_sb, m_w_br_pool, m_w_br_gm, m_w_out, m_g_mix_pre, m_g_mix_post, m_g_ff_pre, m_g_ff_post, m_w_ff_in, m_w_ff_out, v_w_in, v_w_pool, v_pool_scale, v_gm_gain, v_w_spatial, v_b_spatial, v_w_br_sb, v_w_br_pool, v_w_br_gm, v_w_out, v_g_mix_pre, v_g_mix_post, v_g_ff_pre, v_g_ff_post, v_w_ff_in, v_w_ff_out):
    given = dict(x=x, w_in=w_in, w_pool=w_pool, pool_scale=pool_scale, gm_gain=gm_gain, w_spatial=w_spatial, b_spatial=b_spatial, w_br_sb=w_br_sb, w_br_pool=w_br_pool, w_br_gm=w_br_gm, w_out=w_out, g_mix_pre=g_mix_pre, g_mix_post=g_mix_post, g_ff_pre=g_ff_pre, g_ff_post=g_ff_post, w_ff_in=w_ff_in, w_ff_out=w_ff_out, loss_target=loss_target, m_w_in=m_w_in, m_w_pool=m_w_pool, m_pool_scale=m_pool_scale, m_gm_gain=m_gm_gain, m_w_spatial=m_w_spatial, m_b_spatial=m_b_spatial, m_w_br_sb=m_w_br_sb, m_w_br_pool=m_w_br_pool, m_w_br_gm=m_w_br_gm, m_w_out=m_w_out, m_g_mix_pre=m_g_mix_pre, m_g_mix_post=m_g_mix_post, m_g_ff_pre=m_g_ff_pre, m_g_ff_post=m_g_ff_post, m_w_ff_in=m_w_ff_in, m_w_ff_out=m_w_ff_out, v_w_in=v_w_in, v_w_pool=v_w_pool, v_pool_scale=v_pool_scale, v_gm_gain=v_gm_gain, v_w_spatial=v_w_spatial, v_b_spatial=v_b_spatial, v_w_br_sb=v_w_br_sb, v_w_br_pool=v_w_br_pool, v_w_br_gm=v_w_br_gm, v_w_out=v_w_out, v_g_mix_pre=v_g_mix_pre, v_g_mix_post=v_g_mix_post, v_g_ff_pre=v_g_ff_pre, v_g_ff_post=v_g_ff_post, v_w_ff_in=v_w_ff_in, v_w_ff_out=v_w_ff_out)
    weights = {n: given[n] for n in TWIN_WEIGHTS}
    shared = {n: given[n] for n in SHARED_INPUTS}
    per_example = {n: given[n] for n in ['x']}
    grad_fn = _jax.value_and_grad(_loss, argnums=(0, 1))

    def one_microbatch(ex, loss_target):
        ex = dict(ex)
        diff = ex.pop(TWIN_DIFF_INPUT)
        return grad_fn(weights, diff, {**shared, **ex}, loss_target)

    if N_MICROBATCH == 1:
        loss, (grad_w, grad_x) = one_microbatch(per_example, given["loss_target"])
    else:
        def body(carry, xs):
            loss_sum, grad_sum = carry
            l_k, (gw_k, gx_k) = one_microbatch(xs[0], xs[1])
            with _jax.named_scope("update"):
                return (loss_sum + l_k, _jax.tree.map(_jnp.add, grad_sum, gw_k)), gx_k

        init = (_jnp.zeros((), _jnp.float32), _jax.tree.map(_jnp.zeros_like, weights))
        (loss, grad_w), grad_x = _jax.lax.scan(body, init, (per_example, given["loss_target"]))
    with _jax.named_scope("update"):
        delta_w, new_m, new_v = {}, {}, {}
        for n in TWIN_WEIGHTS:
            delta_w[n], new_m[n], new_v[n] = _adamw(weights[n], grad_w[n], given["m_" + n], given["v_" + n])
    return (loss, grad_x, *[grad_w[n] for n in TWIN_WEIGHTS], *[delta_w[n] for n in TWIN_WEIGHTS],
            *[new_m[n] for n in TWIN_WEIGHTS], *[new_v[n] for n in TWIN_WEIGHTS])
```

```python
import functools
import math

import jax
import jax.numpy as jnp
from jax import lax
from jax.experimental import pallas as pl
from jax.experimental.pallas import tpu as pltpu

F32 = jnp.float32
BF16 = jnp.bfloat16

D_MODEL = 1024
SB_HEADS = 8
SB_HEAD_DIM = 64
SB_WIDTH = SB_HEADS * SB_HEAD_DIM
POOL_WINDOWS = (2, 4, 8, 16)
POOL_GROUP_DIM = 64
POOL_WIDTH = 256
POOL_HALO = 16
GM_GROUPS = 4
GM_GROUP_DIM = 64
GM_WIDTH = 256
GM_CHUNK = 128
N_BRANCH = 3
D_FF = 4 * D_MODEL
RMS_EPS = 1e-6
QKV_WIDTH = 3 * SB_WIDTH
REST_WIDTH = POOL_WIDTH + 2 * GM_WIDTH + N_BRANCH * D_MODEL
D_IN = QKV_WIDTH + REST_WIDTH
GATE_COL = POOL_WIDTH + 2 * GM_WIDTH
LANES = 128
N_CHIPS = 4
N_DEV = 8

ADAM_LR = 0.001
ADAM_B1 = 0.9
ADAM_B2 = 0.999
ADAM_EPS = 1e-08
ADAM_WD = 0.01
ADAM_STEP = 10

VMEM_LIMIT = 56 * 1024 * 1024
MESH = pl.DeviceIdType.MESH


def _params(n_grid):
    return pltpu.CompilerParams(dimension_semantics=("arbitrary",) * n_grid, vmem_limit_bytes=VMEM_LIMIT)


def _bf(x):
    return x if x.dtype == BF16 else x.astype(BF16)


def _matmul(a, b, *, name, ta=False, tb=False, out_dtypes=(F32,), n=None, b_col_off=0, bm=1024, bn=1024, bk=2048,
            extras=(), epilogue=None):
    M, K = (a.shape[1], a.shape[0]) if ta else a.shape
    nb = b.shape[0] if tb else b.shape[1]
    n = nb if n is None else n
    bm, bn, bk = min(bm, M), min(bn, n), min(bk, K)
    assert M % bm == 0 and n % bn == 0 and K % bk == 0, (name, M, n, K, bm, bn, bk)
    assert (b.shape[1] if tb else b.shape[0]) == K, (name, a.shape, b.shape)
    nk = K // bk
    dims = (((0 if ta else 1,), (1 if tb else 0,)), ((), ()))
    n_out = len(out_dtypes)
    direct = nk > 1 and epilogue is None and out_dtypes == (F32,)
    use_acc = nk > 1 and not direct

    def body(*refs):
        a_ref, b_ref = refs[:2]
        extra_refs = refs[2:2 + len(extras)]
        out_refs = refs[2 + len(extras):2 + len(extras) + n_out]
        p = lax.dot_general(_bf(a_ref[...]), _bf(b_ref[...]), dims, preferred_element_type=F32)

        def finish(acc):
            outs = (acc,) if epilogue is None else epilogue(acc, *[r[...] for r in extra_refs])
            for r, o in zip(out_refs, outs):
                r[...] = o.astype(r.dtype)

        if nk == 1:
            finish(p)
            return
        k = pl.program_id(2)
        acc_ref = out_refs[0] if direct else refs[-1]

        @pl.when(k == 0)
        def _():
            acc_ref[...] = p

        @pl.when(k > 0)
        def _():
            acc_ref[...] += p

        if use_acc:
            @pl.when(k == nk - 1)
            def _():
                finish(acc_ref[...])

    a_spec = pl.BlockSpec((bk, bm), lambda i, j, k: (k, i)) if ta else pl.BlockSpec((bm, bk), lambda i, j, k: (i, k))
    if tb:
        assert b_col_off == 0
        b_spec = pl.BlockSpec((bn, bk), lambda i, j, k: (j, k))
    else:
        b_spec = pl.BlockSpec((bk, bn), lambda i, j, k: (k, j + b_col_off))
    tile = pl.BlockSpec((bm, bn), lambda i, j, k: (i, j))
    outs = pl.pallas_call(
        body, name=name, grid=(M // bm, n // bn, nk),
        in_specs=[a_spec, b_spec] + [tile] * len(extras),
        out_specs=[tile] * n_out,
        out_shape=[jax.ShapeDtypeStruct((M, n), d) for d in out_dtypes],
        scratch_shapes=[pltpu.VMEM((bm, bn), F32)] if use_acc else [],
        compiler_params=_params(3),
    )(a, b, *extras)
    return outs[0] if n_out == 1 else outs


ROW_TILE = 512


def _rows(S):
    tr = min(ROW_TILE, S)
    assert S % tr == 0
    return tr


def _rstd(x):
    return lax.rsqrt(jnp.mean(x * x, axis=-1, keepdims=True) + RMS_EPS)


def _rms_bwd_math(x, g, dy):
    r = _rstd(x)
    gd = g * dy
    dx = r * gd - x * (r * r * r) * jnp.mean(x * gd, axis=-1, keepdims=True)
    dg = jnp.sum(dy * x * r, axis=0, keepdims=True)
    return dx, dg


def _accumulate(ref, value):
    i = pl.program_id(0)

    @pl.when(i == 0)
    def _():
        ref[...] = value

    @pl.when(i > 0)
    def _():
        ref[...] += value


def _row_spec(tr, width):
    return pl.BlockSpec((tr, width), lambda i: (i, 0))


def _vec_spec(width):
    return pl.BlockSpec((1, width), lambda i: (0, 0))


def _rms_fwd(x, g, *, name):
    S, D = x.shape
    tr = _rows(S)

    def body(x_ref, g_ref, o_ref):
        xf = x_ref[...]
        o_ref[...] = (xf * _rstd(xf) * g_ref[...]).astype(o_ref.dtype)

    return pl.pallas_call(
        body, name=name, grid=(S // tr,), in_specs=[_row_spec(tr, D), _vec_spec(D)], out_specs=_row_spec(tr, D),
        out_shape=jax.ShapeDtypeStruct((S, D), BF16), compiler_params=_params(1))(x, g)


def _resid_rms(x, y, g_post, g_next, *, name):
    S, D = x.shape
    tr = _rows(S)
    with_next = g_next is not None

    def body(*refs):
        if with_next:
            x_ref, y_ref, gp_ref, gn_ref, xo_ref, ho_ref = refs
        else:
            x_ref, y_ref, gp_ref, xo_ref = refs
        yf = y_ref[...]
        xn = x_ref[...] + yf * _rstd(yf) * gp_ref[...]
        xo_ref[...] = xn
        if with_next:
            ho_ref[...] = (xn * _rstd(xn) * gn_ref[...]).astype(ho_ref.dtype)

    row, vec = _row_spec(tr, D), _vec_spec(D)
    ins = [x, y, g_post] + ([g_next] if with_next else [])
    outs = pl.pallas_call(
        body, name=name, grid=(S // tr,), in_specs=[row, row, vec] + ([vec] if with_next else []),
        out_specs=[row] + ([row] if with_next else []),
        out_shape=[jax.ShapeDtypeStruct((S, D), F32)] + ([jax.ShapeDtypeStruct((S, D), BF16)] if with_next else []),
        compiler_params=_params(1))(*ins)
    return (outs[0], outs[1]) if with_next else (outs[0], None)


def _rms_bwd(x, g, dy, *, name):
    S, D = x.shape
    tr = _rows(S)

    def body(x_ref, g_ref, dy_ref, dx_ref, dg_ref):
        dx, dg = _rms_bwd_math(x_ref[...], g_ref[...], dy_ref[...])
        dx_ref[...] = dx.astype(dx_ref.dtype)
        _accumulate(dg_ref, dg)

    row, vec = _row_spec(tr, D), _vec_spec(D)
    return pl.pallas_call(
        body, name=name, grid=(S // tr,), in_specs=[row, vec, row], out_specs=[row, vec],
        out_shape=[jax.ShapeDtypeStruct((S, D), BF16), jax.ShapeDtypeStruct((1, D), F32)],
        compiler_params=_params(1))(x, g, dy)


def _rms_bwd_chain(xa, ga, da, resid, xb, gb, *, name):
    S, D = xa.shape
    tr = _rows(S)
    chain = xb is not None

    def body(*refs):
        if chain:
            xa_ref, ga_ref, da_ref, rs_ref, xb_ref, gb_ref, dx_ref, dga_ref, dxb_ref, dgb_ref = refs
        else:
            xa_ref, ga_ref, da_ref, rs_ref, dx_ref, dga_ref = refs
        dxa, dga = _rms_bwd_math(xa_ref[...], ga_ref[...], da_ref[...])
        dx = rs_ref[...] + dxa
        dx_ref[...] = dx
        _accumulate(dga_ref, dga)
        if chain:
            dxb, dgb = _rms_bwd_math(xb_ref[...], gb_ref[...], dx)
            dxb_ref[...] = dxb.astype(dxb_ref.dtype)
            _accumulate(dgb_ref, dgb)

    row, vec = _row_spec(tr, D), _vec_spec(D)
    ins = [xa, ga, da, resid] + ([xb, gb] if chain else [])
    outs = pl.pallas_call(
        body, name=name, grid=(S // tr,), in_specs=[row, vec, row, row] + ([row, vec] if chain else []),
        out_specs=[row, vec] + ([row, vec] if chain else []),
        out_shape=[jax.ShapeDtypeStruct((S, D), F32), jax.ShapeDtypeStruct((1, D), F32)]
        + ([jax.ShapeDtypeStruct((S, D), BF16), jax.ShapeDtypeStruct((1, D), F32)] if chain else []),
        compiler_params=_params(1))(*ins)
    return tuple(outs) if chain else (outs[0], outs[1], None, None)


def _loss_head(y, target, *, name):
    S, D = y.shape
    tr = _rows(S)
    n_tiles = S // tr

    def body(y_ref, t_ref, dy_ref, loss_ref, acc_ref):
        err = y_ref[...] - t_ref[...]
        dy_ref[...] = err * (1.0 / D)
        _accumulate(acc_ref, jnp.sum(err * err, axis=0, keepdims=True))

        @pl.when(pl.program_id(0) == n_tiles - 1)
        def _():
            loss_ref[...] = jnp.sum(acc_ref[...], axis=1, keepdims=True) * (0.5 / D)

    row = _row_spec(tr, D)
    return pl.pallas_call(
        body, name=name, grid=(n_tiles,), in_specs=[row, row],
        out_specs=[row, pl.BlockSpec((1, 1), lambda i: (0, 0))],
        out_shape=[jax.ShapeDtypeStruct((S, D), F32), jax.ShapeDtypeStruct((1, 1), F32)],
        scratch_shapes=[pltpu.VMEM((1, D), F32)], compiler_params=_params(1))(y, target)


SB_TILE = 256
SB_PAIRS = SB_HEADS * SB_HEAD_DIM // LANES


def _log_sigmoids(z):
    l1p = jnp.log(1.0 + jnp.exp(-jnp.abs(z)))
    return jnp.minimum(z, 0.0) - l1p, jnp.minimum(-z, 0.0) - l1p


def _split_bf16(x):
    hi = x.astype(BF16)
    lo = (x - hi.astype(F32)).astype(BF16)
    return jnp.concatenate([hi, lo], axis=1)


def _tri(T, cmp):
    j = lax.broadcasted_iota(jnp.int32, (T, T), 0)
    s = lax.broadcasted_iota(jnp.int32, (T, T), 1)
    m = jnp.where(cmp(j, s), 1.0, 0.0).astype(BF16)
    return jnp.concatenate([m, m], axis=0)


def _head_masks():
    lane = lax.broadcasted_iota(jnp.int32, (1, LANES), 1)
    return [lane < SB_HEAD_DIM, lane >= SB_HEAD_DIM]


def _sba_fwd(qkv, *, name):
    S = qkv.shape[0]
    T = min(SB_TILE, S)
    nq = S // T
    scale = SB_HEAD_DIM ** -0.5

    def body(q_ref, k_ref, v_ref, o_ref, t_ref):
        i = pl.program_id(1)
        row = lax.broadcasted_iota(jnp.int32, (T, T), 0)
        col = lax.broadcasted_iota(jnp.int32, (T, T), 1)
        strict = col < row
        after = _tri(T, lambda j, s: j > s)
        q = q_ref[...]
        accs = []
        for h, hm in enumerate(_head_masks()):
            qh = jnp.where(hm, q, jnp.zeros_like(q))

            def block(j, carry, diag, qh=qh):
                C, acc = carry
                rows = pl.ds(pl.multiple_of(j * T, T), T)
                kb, vb = k_ref[rows, :], v_ref[rows, :]
                z = lax.dot_general(qh, kb, (((1,), (1,)), ((), ())), preferred_element_type=F32) * scale
                ls, ln = _log_sigmoids(z)
                if diag:
                    ln = jnp.where(strict, ln, 0.0)
                suffix = jnp.dot(_split_bf16(ln), after, preferred_element_type=F32)
                a = jnp.exp(ls + suffix + C)
                if diag:
                    a = jnp.where(strict, a, 0.0)
                acc = acc + jnp.dot(a.astype(BF16), vb, preferred_element_type=F32)
                return C + jnp.sum(ln, axis=1, keepdims=True), acc

            carry = block(i, (jnp.zeros((T, 1), F32), jnp.zeros((T, LANES), F32)), True)
            C, acc = lax.fori_loop(0, i, lambda jj, c: block(i - 1 - jj, c, False), carry)
            accs.append(acc)
            t_ref[h] = jnp.broadcast_to(C, (T, LANES))
        o_ref[...] = jnp.where(_head_masks()[0], accs[0], accs[1]).astype(o_ref.dtype)

    kv = lambda off: pl.BlockSpec((S, LANES), lambda p, i: (0, off + p))
    return pl.pallas_call(
        body, name=name, grid=(SB_PAIRS, nq),
        in_specs=[pl.BlockSpec((T, LANES), lambda p, i: (i, p)), kv(SB_PAIRS), kv(2 * SB_PAIRS)],
        out_specs=[pl.BlockSpec((T, LANES), lambda p, i: (i, p)), pl.BlockSpec((2, T, LANES), lambda p, i: (p, i, 0))],
        out_shape=[jax.ShapeDtypeStruct((S, SB_WIDTH), BF16), jax.ShapeDtypeStruct((SB_HEADS, S, LANES), F32)],
        compiler_params=_params(2))(qkv, qkv, qkv)


def _sba_bwd(qkv, do, tot, *, name):
    S = qkv.shape[0]
    T = min(SB_TILE, S)
    nq = S // T
    scale = SB_HEAD_DIM ** -0.5

    def body(q_ref, k_ref, v_ref, do_ref, t_ref, dq_ref, dk_ref, dv_ref, dk_acc, dv_acc):
        i = pl.program_id(1)

        @pl.when(i == 0)
        def _():
            dk_acc[...] = jnp.zeros_like(dk_acc)
            dv_acc[...] = jnp.zeros_like(dv_acc)

        row = lax.broadcasted_iota(jnp.int32, (T, T), 0)
        col = lax.broadcasted_iota(jnp.int32, (T, T), 1)
        strict = col < row
        upto = _tri(T, lambda j, s: j <= s)
        before = _tri(T, lambda j, s: j < s)
        q, do_t = q_ref[...], do_ref[...]
        dqs = []
        for h, hm in enumerate(_head_masks()):
            qh = jnp.where(hm, q, jnp.zeros_like(q))
            doh = jnp.where(hm, do_t, jnp.zeros_like(do_t))
            total = t_ref[h][:, 0:1]

            def block(j, carry, diag, qh=qh, doh=doh, total=total):
                P, G, dq = carry
                rows = pl.ds(pl.multiple_of(j * T, T), T)
                kb, vb = k_ref[rows, :], v_ref[rows, :]
                z = lax.dot_general(qh, kb, (((1,), (1,)), ((), ())), preferred_element_type=F32) * scale
                ls, ln = _log_sigmoids(z)
                if diag:
                    ln = jnp.where(strict, ln, 0.0)
                beta = jnp.exp(ls)
                suffix = (total - P) - jnp.dot(_split_bf16(ln), upto, preferred_element_type=F32)
                a = jnp.exp(ls + suffix)
                if diag:
                    a = jnp.where(strict, a, 0.0)
                g = a * lax.dot_general(doh, vb, (((1,), (1,)), ((), ())), preferred_element_type=F32)
                g_before = G + jnp.dot(_split_bf16(g), before, preferred_element_type=F32)
                dz = (g * (1.0 - beta) - g_before * beta) * scale
                if diag:
                    dz = jnp.where(strict, dz, 0.0)
                dzb, ab = dz.astype(BF16), a.astype(BF16)
                dq = dq + jnp.dot(dzb, kb, preferred_element_type=F32)
                over_queries = (((0,), (0,)), ((), ()))
                dk_acc[rows, :] += lax.dot_general(dzb, qh, over_queries, preferred_element_type=F32)
                dv_acc[rows, :] += lax.dot_general(ab, doh, over_queries, preferred_element_type=F32)
                return P + jnp.sum(ln, axis=1, keepdims=True), G + jnp.sum(g, axis=1, keepdims=True), dq

            zero = jnp.zeros((T, 1), F32)
            carry = lax.fori_loop(0, i, lambda j, c: block(j, c, False), (zero, zero, jnp.zeros((T, LANES), F32)))
            dqs.append(block(i, carry, True)[2])
        dq_ref[...] = jnp.where(_head_masks()[0], dqs[0], dqs[1]).astype(dq_ref.dtype)

        @pl.when(i == nq - 1)
        def _():
            dk_ref[...] = dk_acc[...].astype(dk_ref.dtype)
            dv_ref[...] = dv_acc[...].astype(dv_ref.dtype)

    kv = lambda off: pl.BlockSpec((S, LANES), lambda p, i: (0, off + p))
    tile = lambda off: pl.BlockSpec((T, LANES), lambda p, i: (i, off + p))
    return pl.pallas_call(
        body, name=name, grid=(SB_PAIRS, nq),
        in_specs=[tile(0), kv(SB_PAIRS), kv(2 * SB_PAIRS), tile(0), pl.BlockSpec((2, T, LANES), lambda p, i: (p, i, 0))],
        out_specs=[tile(0), kv(0), kv(0)],
        out_shape=[jax.ShapeDtypeStruct((S, SB_WIDTH), BF16)] * 3,
        scratch_shapes=[pltpu.VMEM((S, LANES), F32), pltpu.VMEM((S, LANES), F32)],
        compiler_params=_params(2))(qkv, qkv, qkv, do, tot)


POOL_TILE = 512


def _by_group(lane, values):
    return jnp.where(lane < 64, values[0], jnp.where(lane < 128, values[1], jnp.where(lane < 192, values[2], values[3])))


def _pool_inv_count(first_row, n_rows):
    t = first_row + lax.broadcasted_iota(jnp.int32, (n_rows, POOL_WIDTH), 0)
    lane = lax.broadcasted_iota(jnp.int32, (n_rows, POOL_WIDTH), 1)
    window = _by_group(lane, POOL_WINDOWS)
    return 1.0 / jnp.clip(t + 1, 1, window).astype(F32), lane


def _pooled(ext, first_row, R):
    n = R + POOL_HALO
    s2 = ext + pltpu.roll(ext, 1, 0)
    s4 = s2 + pltpu.roll(s2, 2, 0)
    s8 = s4 + pltpu.roll(s4, 4, 0)
    s16 = s8 + pltpu.roll(s8, 8, 0)
    inv, lane = _pool_inv_count(first_row - POOL_HALO, n)
    pooled = _by_group(lane, (s2, s4, s8, s16)) * inv - ext
    return pooled[POOL_HALO:, :]


def _pool_specs(S, R, col):
    per = R // POOL_HALO
    tile = pl.BlockSpec((R, POOL_WIDTH), lambda i: (i, col))
    prev = pl.BlockSpec((POOL_HALO, POOL_WIDTH), lambda i: (jnp.maximum(i * per - 1, 0), col))
    return tile, prev


def _pool_fwd(rest, w_bd, scale, *, name):
    S = rest.shape[0]
    R = min(POOL_TILE, S)

    def body(p_ref, prev_ref, w_ref, s_ref, o_ref, ext_ref):
        i = pl.program_id(0)
        ext_ref[:POOL_HALO, :] = jnp.where(i > 0, prev_ref[...], 0.0)
        ext_ref[POOL_HALO:, :] = p_ref[...]
        pooled = _pooled(ext_ref[...], i * R, R)
        mixed = jnp.dot(pooled.astype(BF16), w_ref[...], preferred_element_type=F32)
        o_ref[...] = (mixed * s_ref[...]).astype(o_ref.dtype)

    tile, prev = _pool_specs(S, R, 0)
    return pl.pallas_call(
        body, name=name, grid=(S // R,),
        in_specs=[tile, prev, pl.BlockSpec((POOL_WIDTH, POOL_WIDTH), lambda i: (0, 0)), _vec_spec(POOL_WIDTH)],
        out_specs=_row_spec(R, POOL_WIDTH), out_shape=jax.ShapeDtypeStruct((S, POOL_WIDTH), BF16),
        scratch_shapes=[pltpu.VMEM((R + POOL_HALO, POOL_WIDTH), F32)], compiler_params=_params(1))(rest, rest, w_bd, scale)


def _pool_bwd(rest, do, w_bd, scale, *, name):
    S = rest.shape[0]
    R = min(POOL_TILE, S)
    n_tiles = S // R
    per = R // POOL_HALO
    n = R + POOL_HALO

    def body(p_ref, prev_ref, do_ref, nxt_ref, w_ref, s_ref, dp_ref, dw_ref, ds_ref, ext_ref, dext_ref):
        i = pl.program_id(0)
        ext_ref[:POOL_HALO, :] = jnp.where(i > 0, prev_ref[...], 0.0)
        ext_ref[POOL_HALO:, :] = p_ref[...]
        pooled = _pooled(ext_ref[...], i * R, R).astype(BF16)
        w = w_ref[...]
        mixed = jnp.dot(pooled, w, preferred_element_type=F32)
        do_t = do_ref[...]
        _accumulate(ds_ref, jnp.sum(do_t * mixed, axis=0, keepdims=True))
        dext_ref[:R, :] = do_t
        dext_ref[R:, :] = jnp.where(i < n_tiles - 1, nxt_ref[...], 0.0)
        dmixed = (dext_ref[...] * s_ref[...]).astype(BF16)
        dpooled = lax.dot_general(dmixed, w, (((1,), (1,)), ((), ())), preferred_element_type=F32)
        _accumulate(dw_ref, lax.dot_general(pooled, dmixed[:R, :], (((0,), (0,)), ((), ())), preferred_element_type=F32))
        inv, lane = _pool_inv_count(i * R, n)
        u = dpooled * inv
        f2 = u + pltpu.roll(u, n - 1, 0)
        f4 = f2 + pltpu.roll(f2, n - 2, 0)
        f8 = f4 + pltpu.roll(f4, n - 4, 0)
        f16 = f8 + pltpu.roll(f8, n - 8, 0)
        dp = _by_group(lane, (f2, f4, f8, f16)) - dpooled
        dp_ref[...] = dp[:R, :].astype(dp_ref.dtype)

    tile, prev = _pool_specs(S, R, 0)
    nxt = pl.BlockSpec((POOL_HALO, POOL_WIDTH), lambda i: (jnp.minimum((i + 1) * per, S // POOL_HALO - 1), 0))
    full = pl.BlockSpec((POOL_WIDTH, POOL_WIDTH), lambda i: (0, 0))
    return pl.pallas_call(
        body, name=name, grid=(n_tiles,),
        in_specs=[tile, prev, _row_spec(R, POOL_WIDTH), nxt, full, _vec_spec(POOL_WIDTH)],
        out_specs=[_row_spec(R, POOL_WIDTH), full, _vec_spec(POOL_WIDTH)],
        out_shape=[jax.ShapeDtypeStruct((S, POOL_WIDTH), BF16), jax.ShapeDtypeStruct((POOL_WIDTH, POOL_WIDTH), F32),
                   jax.ShapeDtypeStruct((1, POOL_WIDTH), F32)],
        scratch_shapes=[pltpu.VMEM((n, POOL_WIDTH), F32), pltpu.VMEM((n, POOL_WIDTH), F32)],
        compiler_params=_params(1))(rest, rest, do, do, w_bd, scale)


GM_TILE = 512
GELU_C = math.sqrt(2.0 / math.pi)
GELU_A = 0.044715


def _gelu(x):
    return 0.5 * x * (1.0 + jnp.tanh(GELU_C * (x + GELU_A * x * x * x)))


def _gelu_and_grad(x):
    t = jnp.tanh(GELU_C * (x + GELU_A * x * x * x))
    y = 0.5 * x * (1.0 + t)
    dy = 0.5 * (1.0 + t) + 0.5 * x * (1.0 - t * t) * (GELU_C * (1.0 + 3.0 * GELU_A * x * x))
    return y, dy


def _group_lane_masks():
    lane = lax.broadcasted_iota(jnp.int32, (1, GM_WIDTH), 1)
    return [(lane >= g * GM_GROUP_DIM) & (lane < (g + 1) * GM_GROUP_DIM) for g in range(GM_GROUPS)]


def _stack_groups(x, masks):
    return jnp.concatenate([jnp.where(m, x, jnp.zeros_like(x)) for m in masks], axis=0)


def _gm_mixed(vn, ws_cat, bias, masks, R):
    chunks = []
    for c in range(R // GM_CHUNK):
        vc = vn[c * GM_CHUNK:(c + 1) * GM_CHUNK, :]
        chunks.append(jnp.dot(ws_cat, _stack_groups(vc, masks), preferred_element_type=F32) + bias)
    return jnp.concatenate(chunks, axis=0)


def _gm_specs(S, R):
    u = pl.BlockSpec((R, GM_WIDTH), lambda i: (i, 1))
    v = pl.BlockSpec((R, GM_WIDTH), lambda i: (i, 2))
    ws = pl.BlockSpec((GM_CHUNK, GM_GROUPS * GM_CHUNK), lambda i: (0, 0))
    bias = pl.BlockSpec((GM_CHUNK, GM_WIDTH), lambda i: (0, 0))
    return u, v, ws, bias


def _gm_fwd(rest, gain, ws_cat, bias, *, name):
    S = rest.shape[0]
    R = min(GM_TILE, S)

    def body(u_ref, v_ref, g_ref, ws_ref, b_ref, o_ref):
        gv = _gelu(v_ref[...])
        vn = (gv * _rstd(gv) * g_ref[...]).astype(BF16)
        mixed = _gm_mixed(vn, ws_ref[...], b_ref[...], _group_lane_masks(), R)
        o_ref[...] = (_gelu(u_ref[...]) * mixed).astype(o_ref.dtype)

    u_spec, v_spec, ws_spec, bias_spec = _gm_specs(S, R)
    return pl.pallas_call(
        body, name=name, grid=(S // R,), in_specs=[u_spec, v_spec, _vec_spec(GM_WIDTH), ws_spec, bias_spec],
        out_specs=_row_spec(R, GM_WIDTH), out_shape=jax.ShapeDtypeStruct((S, GM_WIDTH), BF16),
        compiler_params=_params(1))(rest, rest, gain, ws_cat, bias)


def _gm_bwd(rest, do, gain, ws_cat, wst_cat, bias, *, name):
    S = rest.shape[0]
    R = min(GM_TILE, S)

    def body(u_ref, v_ref, do_ref, g_ref, ws_ref, wst_ref, b_ref, du_ref, dv_ref, dg_ref, dws_ref, db_ref):
        masks = _group_lane_masks()
        gain_v = g_ref[...]
        gu, dgu = _gelu_and_grad(u_ref[...])
        gv, dgv = _gelu_and_grad(v_ref[...])
        r = _rstd(gv)
        vn = (gv * r * gain_v).astype(BF16)
        mixed = _gm_mixed(vn, ws_ref[...], b_ref[...], masks, R)
        do_t = do_ref[...]
        du_ref[...] = (do_t * mixed * dgu).astype(du_ref.dtype)
        dmix = do_t * gu
        dmix_b = dmix.astype(BF16)
        wst = wst_ref[...]
        dvn_chunks, db, dws = [], None, [None] * GM_GROUPS
        for c in range(R // GM_CHUNK):
            rows = slice(c * GM_CHUNK, (c + 1) * GM_CHUNK)
            dc, dcb, vc = dmix[rows, :], dmix_b[rows, :], vn[rows, :]
            db = dc if db is None else db + dc
            dvn_chunks.append(jnp.dot(wst, _stack_groups(dcb, masks), preferred_element_type=F32))
            for g, m in enumerate(masks):
                part = lax.dot_general(jnp.where(m, dcb, jnp.zeros_like(dcb)), vc, (((1,), (1,)), ((), ())),
                                       preferred_element_type=F32)
                dws[g] = part if dws[g] is None else dws[g] + part
        dvn = jnp.concatenate(dvn_chunks, axis=0)
        lane = lax.broadcasted_iota(jnp.int32, (1, LANES), 1)
        db_groups = jnp.zeros((GM_CHUNK, LANES), F32)
        for g, m in enumerate(masks):
            total = jnp.sum(jnp.where(m, db, 0.0), axis=1, keepdims=True)
            db_groups = db_groups + jnp.where(lane == g, total, 0.0)
        _accumulate(db_ref, db_groups)
        i = pl.program_id(0)
        for g in range(GM_GROUPS):
            @pl.when(i == 0)
            def _(g=g):
                dws_ref[g] = dws[g]

            @pl.when(i > 0)
            def _(g=g):
                dws_ref[g] += dws[g]
        _accumulate(dg_ref, jnp.sum(dvn * gv * r, axis=0, keepdims=True))
        gd = gain_v * dvn
        dgv_in = r * gd - gv * (r * r * r) * jnp.mean(gv * gd, axis=-1, keepdims=True)
        dv_ref[...] = (dgv_in * dgv).astype(dv_ref.dtype)

    u_spec, v_spec, ws_spec, bias_spec = _gm_specs(S, R)
    row, vec = _row_spec(R, GM_WIDTH), _vec_spec(GM_WIDTH)
    dws_spec = pl.BlockSpec((GM_GROUPS, GM_CHUNK, GM_CHUNK), lambda i: (0, 0, 0))
    return pl.pallas_call(
        body, name=name, grid=(S // R,), in_specs=[u_spec, v_spec, row, vec, ws_spec, ws_spec, bias_spec],
        out_specs=[row, row, vec, dws_spec, pl.BlockSpec((GM_CHUNK, LANES), lambda i: (0, 0))],
        out_shape=[jax.ShapeDtypeStruct((S, GM_WIDTH), BF16)] * 2
        + [jax.ShapeDtypeStruct((1, GM_WIDTH), F32), jax.ShapeDtypeStruct((GM_GROUPS, GM_CHUNK, GM_CHUNK), F32),
           jax.ShapeDtypeStruct((GM_CHUNK, LANES), F32)],
        compiler_params=_params(1))(rest, rest, do, gain, ws_cat, wst_cat, bias)


GATE_ROWS = 1024
GATE_COLS = 256
GATE_BLOCKS = D_MODEL // GATE_COLS


def _gate_spec(tr, k):
    return pl.BlockSpec((tr, GATE_COLS), lambda i, j: (i, GATE_COL // GATE_COLS + GATE_BLOCKS * k + j))


def _merge_fwd(rest, branches, *, name):
    S = rest.shape[0]
    tr = min(GATE_ROWS, S)

    def body(g0, g1, g2, b0, b1, b2, o_ref):
        acc = None
        for g_ref, b_ref in ((g0, b0), (g1, b1), (g2, b2)):
            term = jax.nn.sigmoid(g_ref[...]) * b_ref[...]
            acc = term if acc is None else acc + term
        o_ref[...] = acc.astype(o_ref.dtype)

    tile = pl.BlockSpec((tr, GATE_COLS), lambda i, j: (i, j))
    return pl.pallas_call(
        body, name=name, grid=(S // tr, GATE_BLOCKS),
        in_specs=[_gate_spec(tr, k) for k in range(N_BRANCH)] + [tile] * N_BRANCH, out_specs=tile,
        out_shape=jax.ShapeDtypeStruct((S, D_MODEL), BF16), compiler_params=_params(2))(rest, rest, rest, *branches)


def _merge_bwd(rest, branches, dmerged, *, name):
    S = rest.shape[0]
    tr = min(GATE_ROWS, S)

    def body(g0, g1, g2, b0, b1, b2, dm_ref, dg0, dg1, dg2, db0, db1, db2):
        dm = dm_ref[...]
        for g_ref, b_ref, dg_ref, db_ref in ((g0, b0, dg0, db0), (g1, b1, dg1, db1), (g2, b2, dg2, db2)):
            s = jax.nn.sigmoid(g_ref[...])
            db_ref[...] = (dm * s).astype(db_ref.dtype)
            dg_ref[...] = (dm * b_ref[...] * s * (1.0 - s)).astype(dg_ref.dtype)

    tile = pl.BlockSpec((tr, GATE_COLS), lambda i, j: (i, j))
    return pl.pallas_call(
        body, name=name, grid=(S // tr, GATE_BLOCKS),
        in_specs=[_gate_spec(tr, k) for k in range(N_BRANCH)] + [tile] * (N_BRANCH + 1), out_specs=[tile] * (2 * N_BRANCH),
        out_shape=[jax.ShapeDtypeStruct((S, D_MODEL), BF16)] * (2 * N_BRANCH),
        compiler_params=_params(2))(rest, rest, rest, *branches, dmerged)


TILE_BYTES = 24 * 1024 * 1024


def _tile_rows(rows, cols, n_arrays):
    padded = -(-cols // LANES) * LANES
    cap = max(8, TILE_BYTES // (2 * n_arrays * padded * 4))
    best = None
    for tr in range(8, min(rows, cap) + 1, 8):
        if rows % tr == 0:
            best = tr
    assert best is not None, (rows, cols)
    return best


def _sum_slots(stack, *, name):
    n, R, C = stack.shape
    tr = _tile_rows(R, C, n + 1)

    def body(s_ref, o_ref):
        acc = s_ref[0]
        for k in range(1, n):
            acc = acc + s_ref[k]
        o_ref[...] = acc

    return pl.pallas_call(
        body, name=name, grid=(R // tr,), in_specs=[pl.BlockSpec((n, tr, C), lambda i: (0, i, 0))],
        out_specs=_row_spec(tr, C), out_shape=jax.ShapeDtypeStruct((R, C), F32), compiler_params=_params(1))(stack)


def _adamw(w, m, v, gstack, *, name):
    R, C = w.shape
    n = gstack.shape[0]
    tr = _tile_rows(R, C, n + 7)

    def body(w_ref, m_ref, v_ref, g_ref, go_ref, d_ref, mo_ref, vo_ref):
        g = g_ref[0]
        for k in range(1, n):
            g = g + g_ref[k]
        m_new = ADAM_B1 * m_ref[...] + (1.0 - ADAM_B1) * g
        v_new = ADAM_B2 * v_ref[...] + (1.0 - ADAM_B2) * jnp.square(g)
        m_hat = m_new / (1.0 - ADAM_B1 ** ADAM_STEP)
        v_hat = v_new / (1.0 - ADAM_B2 ** ADAM_STEP)
        go_ref[...] = g
        d_ref[...] = -ADAM_LR * (m_hat / (jnp.sqrt(v_hat) + ADAM_EPS) + ADAM_WD * w_ref[...])
        mo_ref[...] = m_new
        vo_ref[...] = v_new

    row = _row_spec(tr, C)
    return pl.pallas_call(
        body, name=name, grid=(R // tr,), in_specs=[row, row, row, pl.BlockSpec((n, tr, C), lambda i: (0, i, 0))],
        out_specs=[row] * 4, out_shape=[jax.ShapeDtypeStruct((R, C), F32)] * 4, compiler_params=_params(1))(w, m, v, gstack)


HBM_SPEC = pl.BlockSpec(memory_space=pl.ANY)


def _position():
    return lax.axis_index("x"), lax.axis_index("y"), lax.axis_index("c")


def _other_chips(x, y):
    return [(1 - x, y), (x, 1 - y), (1 - x, 1 - y)]


def _chip_exchange(arrays, *, scatter, name):
    n = len(arrays)

    def body(*refs):
        ins, outs = refs[:n], refs[n:2 * n]
        send_sems, recv_sems, local_sems = refs[2 * n:]
        x, y, c = _position()
        me = 2 * x + y
        copies = []
        for a in range(n):
            own = ins[a].at[me] if scatter else ins[a]
            local = pltpu.make_async_copy(own, outs[a].at[me], local_sems.at[a])
            local.start()
            copies.append(local)
            for k, (px, py) in enumerate(_other_chips(x, y)):
                src = ins[a].at[2 * px + py] if scatter else ins[a]
                remote = pltpu.make_async_remote_copy(
                    src_ref=src, dst_ref=outs[a].at[me], send_sem=send_sems.at[3 * a + k],
                    recv_sem=recv_sems.at[3 * a + k], device_id=(px, py, c), device_id_type=MESH)
                remote.start()
                copies.append(remote)
        for cp in copies:
            cp.wait()

    out_shapes = [jax.ShapeDtypeStruct(a.shape if scatter else (N_CHIPS,) + a.shape, a.dtype) for a in arrays]
    return pl.pallas_call(
        body, name=name, in_specs=[HBM_SPEC] * n, out_specs=[HBM_SPEC] * n, out_shape=out_shapes,
        scratch_shapes=[pltpu.SemaphoreType.DMA((3 * n,)), pltpu.SemaphoreType.DMA((3 * n,)), pltpu.SemaphoreType.DMA((n,))],
    )(*arrays)


def _sibling_pair(arrays, *, name):
    n = len(arrays)

    def body(*refs):
        ins, outs = refs[:n], refs[n:2 * n]
        send_sems, recv_sems, local_sems = refs[2 * n:]
        x, y, c = _position()
        copies = []
        for a in range(n):
            local = pltpu.make_async_copy(ins[a], outs[a].at[c], local_sems.at[a])
            remote = pltpu.make_async_remote_copy(
                src_ref=ins[a], dst_ref=outs[a].at[c], send_sem=send_sems.at[a], recv_sem=recv_sems.at[a],
                device_id=(x, y, 1 - c), device_id_type=MESH)
            local.start()
            remote.start()
            copies += [local, remote]
        for cp in copies:
            cp.wait()

    return pl.pallas_call(
        body, name=name, in_specs=[HBM_SPEC] * n, out_specs=[HBM_SPEC] * n,
        out_shape=[jax.ShapeDtypeStruct((2,) + a.shape, a.dtype) for a in arrays],
        scratch_shapes=[pltpu.SemaphoreType.DMA((n,)), pltpu.SemaphoreType.DMA((n,)), pltpu.SemaphoreType.DMA((n,))],
    )(*arrays)


def _gather_all(buf, *, name):
    R, C = buf.shape

    def body(in_ref, out_ref, send_sems, recv_sems):
        x, y, c = _position()
        out_ref[4 * x + 2 * y + c] = in_ref[...]
        copies = []
        for k in range(1, N_DEV):
            px = 1 - x if k & 4 else x
            py = 1 - y if k & 2 else y
            pc = 1 - c if k & 1 else c
            cp = pltpu.make_async_remote_copy(
                src_ref=in_ref, dst_ref=out_ref.at[4 * x + 2 * y + c], send_sem=send_sems.at[k - 1],
                recv_sem=recv_sems.at[k - 1], device_id=(px, py, pc), device_id_type=MESH)
            cp.start()
            copies.append(cp)
        for cp in copies:
            cp.wait()

    vmem = pl.BlockSpec(memory_space=pltpu.VMEM)
    return pl.pallas_call(
        body, name=name, in_specs=[vmem], out_specs=vmem, out_shape=jax.ShapeDtypeStruct((N_DEV, R, C), buf.dtype),
        scratch_shapes=[pltpu.SemaphoreType.DMA((N_DEV - 1,)), pltpu.SemaphoreType.DMA((N_DEV - 1,))],
        compiler_params=pltpu.CompilerParams(vmem_limit_bytes=VMEM_LIMIT),
    )(buf)


def _relu2(p):
    return p, jnp.square(jnp.maximum(p, 0.0))


def _relu2_grad(p, a):
    return (p * (2.0 * jnp.maximum(a, 0.0)),)


def _mixer_constants(w_pool, w_spatial, b_spatial):
    eye = jnp.eye(len(POOL_WINDOWS), dtype=F32)
    w_bd = (eye[:, None, :, None] * w_pool[:, :, None, :]).reshape(POOL_WIDTH, POOL_WIDTH).astype(BF16)
    causal = jnp.tril(jnp.ones((GM_CHUNK, GM_CHUNK), dtype=bool))
    ws = jnp.where(causal[None], w_spatial, 0.0).astype(BF16)
    ws_cat = ws.transpose(1, 0, 2).reshape(GM_CHUNK, GM_GROUPS * GM_CHUNK)
    wst_cat = ws.transpose(2, 0, 1).reshape(GM_CHUNK, GM_GROUPS * GM_CHUNK)
    bias = jnp.repeat(b_spatial.T, GM_GROUP_DIM, axis=1)
    return w_bd, ws_cat, wst_cat, bias


def _local_step(x, target, big, small):
    L = small["g_mix_pre"].shape[0]
    vec = lambda name, l: small[name][l][None, :]
    consts = [_mixer_constants(small["w_pool"][l], small["w_spatial"][l], small["b_spatial"][l]) for l in range(L)]
    saved = []
    h = _rms_fwd(x, vec("g_mix_pre", 0), name="rms_in")
    for l in range(L):
        w_bd, ws_cat, wst_cat, bias = consts[l]
        qkv = _matmul(h, big["w_in"][l], n=QKV_WIDTH, bn=768, out_dtypes=(BF16,), name="proj_qkv")
        rest = _matmul(h, big["w_in"][l], n=REST_WIDTH, bn=768, b_col_off=QKV_WIDTH // 768, name="proj_rest")
        o_sb, tot = _sba_fwd(qkv, name="sba_fwd")
        o_pool = _pool_fwd(rest, w_bd, vec("pool_scale", l), name="pool_fwd")
        o_gm = _gm_fwd(rest, vec("gm_gain", l), ws_cat, bias, name="gm_fwd")
        branches = (_matmul(o_sb, big["w_br_sb"][l], name="br_sb"), _matmul(o_pool, big["w_br_pool"][l], name="br_pool"),
                    _matmul(o_gm, big["w_br_gm"][l], name="br_gm"))
        merged = _merge_fwd(rest, branches, name="merge_fwd")
        y = _matmul(merged, big["w_out"][l], name="out_proj")
        x1, h2 = _resid_rms(x, y, vec("g_mix_post", l), vec("g_ff_pre", l), name="resid_mix")
        a, r = _matmul(h2, big["w_ff_in"][l], out_dtypes=(F32, BF16), epilogue=_relu2, name="ff_in")
        ff = _matmul(r, big["w_ff_out"][l], name="ff_out")
        g_next = vec("g_mix_pre", l + 1) if l + 1 < L else None
        x2, h_next = _resid_rms(x1, ff, vec("g_ff_post", l), g_next, name="resid_ff" if l + 1 < L else "resid_last")
        saved.append(dict(x=x, h=h, qkv=qkv, rest=rest, o_sb=o_sb, tot=tot, o_pool=o_pool, o_gm=o_gm, branches=branches,
                          merged=merged, y=y, x1=x1, h2=h2, a=a, r=r, ff=ff))
        x, h = x2, h_next

    dx2, loss = _loss_head(x, target, name="loss_head")
    gb = {k: [None] * L for k in ("w_in", "w_br_sb", "w_br_pool", "w_br_gm", "w_out", "w_ff_in", "w_ff_out")}
    gs = {k: [None] * L for k in ("w_pool", "pool_scale", "gm_gain", "w_spatial", "b_spatial", "g_mix_pre",
                                  "g_mix_post", "g_ff_pre", "g_ff_post")}
    d_ff, gs["g_ff_post"][L - 1] = _rms_bwd(saved[-1]["ff"], vec("g_ff_post", L - 1), dx2, name="rms_bwd_last")
    for l in reversed(range(L)):
        s = saved[l]
        w_bd, ws_cat, wst_cat, bias = consts[l]
        da = _matmul(d_ff, big["w_ff_out"][l], tb=True, out_dtypes=(BF16,), extras=(s["a"],), epilogue=_relu2_grad,
                     name="ff_out_dx")
        gb["w_ff_out"][l] = _matmul(s["r"], d_ff, ta=True, name="ff_out_dw")
        dh2 = _matmul(da, big["w_ff_in"][l], tb=True, name="ff_in_dx")
        gb["w_ff_in"][l] = _matmul(s["h2"], da, ta=True, name="ff_in_dw")
        dx1, gs["g_ff_pre"][l], dy, gs["g_mix_post"][l] = _rms_bwd_chain(
            s["x1"], vec("g_ff_pre", l), dh2, dx2, s["y"], vec("g_mix_post", l), name="rms_bwd_mid")
        dmerged = _matmul(dy, big["w_out"][l], tb=True, name="out_proj_dx")
        gb["w_out"][l] = _matmul(s["merged"], dy, ta=True, name="out_proj_dw")
        dg0, dg1, dg2, db_sb, db_pool, db_gm = _merge_bwd(s["rest"], s["branches"], dmerged, name="merge_bwd")
        do_sb = _matmul(db_sb, big["w_br_sb"][l], tb=True, out_dtypes=(BF16,), name="br_sb_dx")
        gb["w_br_sb"][l] = _matmul(s["o_sb"], db_sb, ta=True, name="br_sb_dw")
        do_pool = _matmul(db_pool, big["w_br_pool"][l], tb=True, name="br_pool_dx")
        gb["w_br_pool"][l] = _matmul(s["o_pool"], db_pool, ta=True, name="br_pool_dw")
        do_gm = _matmul(db_gm, big["w_br_gm"][l], tb=True, name="br_gm_dx")
        gb["w_br_gm"][l] = _matmul(s["o_gm"], db_gm, ta=True, name="br_gm_dw")
        dq, dk, dv = _sba_bwd(s["qkv"], do_sb, s["tot"], name="sba_bwd")
        dp, dw_bd, gs["pool_scale"][l] = _pool_bwd(s["rest"], do_pool, w_bd, vec("pool_scale", l), name="pool_bwd")
        du, dgv, gs["gm_gain"][l], dws, db = _gm_bwd(s["rest"], do_gm, vec("gm_gain", l), ws_cat, wst_cat, bias,
                                                      name="gm_bwd")
        gs["w_pool"][l] = jnp.stack([dw_bd[g * 64:(g + 1) * 64, g * 64:(g + 1) * 64] for g in range(len(POOL_WINDOWS))])
        gs["w_spatial"][l] = jnp.where(jnp.tril(jnp.ones((GM_CHUNK, GM_CHUNK), dtype=bool))[None], dws, 0.0)
        gs["b_spatial"][l] = db[:, :GM_GROUPS].T
        dproj = jnp.concatenate([dq, dk, dv, dp, du, dgv, dg0, dg1, dg2], axis=1)
        dh = _matmul(dproj, big["w_in"][l], tb=True, bk=D_IN // 3, name="proj_dx")
        gb["w_in"][l] = _matmul(s["h"], dproj, ta=True, bn=768, name="proj_dw")
        if l > 0:
            dx2, gs["g_mix_pre"][l], d_ff, gs["g_ff_post"][l - 1] = _rms_bwd_chain(
                s["x"], vec("g_mix_pre", l), dh, dx1, saved[l - 1]["ff"], vec("g_ff_post", l - 1), name="rms_bwd_mid")
        else:
            dx2, gs["g_mix_pre"][l], _, _ = _rms_bwd_chain(s["x"], vec("g_mix_pre", l), dh, dx1, None, None,
                                                           name="rms_bwd_first")
    small_grads = {k: jnp.stack([g.reshape(small[k].shape[1:]) for g in v]) for k, v in gs.items()}
    return loss, dx2, gb, small_grads


COLUMN_SHARDED = ("w_in", "w_br_sb", "w_br_pool", "w_br_gm", "w_ff_in")
ROW_SHARDED = ("w_out", "w_ff_out")
BIG_WEIGHTS = COLUMN_SHARDED + ROW_SHARDED
SMALL_WEIGHTS = ("w_pool", "pool_scale", "gm_gain", "w_spatial", "b_spatial", "g_mix_pre", "g_mix_post", "g_ff_pre",
                 "g_ff_post")
WEIGHT_ORDER = ("w_in", "w_pool", "pool_scale", "gm_gain", "w_spatial", "b_spatial", "w_br_sb", "w_br_pool", "w_br_gm",
                "w_out", "g_mix_pre", "g_mix_post", "g_ff_pre", "g_ff_post", "w_ff_in", "w_ff_out")


def _full_weight(name, gathered, l):
    g = gathered[:, l]
    if name in COLUMN_SHARDED:
        return g.transpose(1, 0, 2).reshape(g.shape[1], N_CHIPS * g.shape[2])
    return g.reshape(N_CHIPS * g.shape[1], g.shape[2])


def _parts_by_chip(name, grads):
    g = jnp.stack(grads)
    L = g.shape[0]
    if name in COLUMN_SHARDED:
        r, c = g.shape[1], g.shape[2] // N_CHIPS
        return g.reshape(L, r, N_CHIPS, c).transpose(2, 0, 1, 3).reshape(N_CHIPS, L * r, c)
    r, c = g.shape[1] // N_CHIPS, g.shape[2]
    return g.reshape(L, N_CHIPS, r, c).transpose(1, 0, 2, 3).reshape(N_CHIPS, L * r, c)


def _pack(arrays):
    flat = jnp.concatenate([a.reshape(-1) for a in arrays])
    return flat.reshape(-1, LANES)


def _unpack(buf, like):
    flat, out, at = buf.reshape(-1), [], 0
    for a in like:
        out.append(flat[at:at + a.size].reshape(a.shape))
        at += a.size
    return out


def kernel(x, w_in, w_pool, pool_scale, gm_gain, w_spatial, b_spatial, w_br_sb, w_br_pool, w_br_gm, w_out, g_mix_pre, g_mix_post, g_ff_pre, g_ff_post, w_ff_in, w_ff_out, loss_target, m_w_in, m_w_pool, m_pool_scale, m_gm_gain, m_w_spatial, m_b_spatial, m_w_br_sb, m_w_br_pool, m_w_br_gm, m_w_out, m_g_mix_pre, m_g_mix_post, m_g_ff_pre, m_g_ff_post, m_w_ff_in, m_w_ff_out, v_w_in, v_w_pool, v_pool_scale, v_gm_gain, v_w_spatial, v_b_spatial, v_w_br_sb, v_w_br_pool, v_w_br_gm, v_w_out, v_g_mix_pre, v_g_mix_post, v_g_ff_pre, v_g_ff_post, v_w_ff_in, v_w_ff_out):
    given = dict(locals())
    w = {n: given[n] for n in WEIGHT_ORDER}
    m = {n: given["m_" + n] for n in WEIGHT_ORDER}
    v = {n: given["v_" + n] for n in WEIGHT_ORDER}
    L = w_in.shape[0]

    gathered = _chip_exchange([w[n].astype(BF16) for n in BIG_WEIGHTS], scatter=False, name="gather_weights")
    big = {n: [_full_weight(n, g, l) for l in range(L)] for n, g in zip(BIG_WEIGHTS, gathered)}
    small = {n: w[n] for n in SMALL_WEIGHTS}

    loss, dx, big_grads, small_grads = _local_step(x[0], loss_target[0], big, small)

    received = _chip_exchange([_parts_by_chip(n, big_grads[n]) for n in BIG_WEIGHTS], scatter=True, name="exchange_grads")
    core_sums = [_sum_slots(r, name="sum_chips") for r in received]
    pairs = _sibling_pair(core_sums, name="pair_cores")
    grads, deltas, new_m, new_v = {}, {}, {}, {}
    for n, pair in zip(BIG_WEIGHTS, pairs):
        shape = w[n].shape
        flat = lambda a: a.reshape(-1, shape[-1])
        outs = _adamw(flat(w[n]), flat(m[n]), flat(v[n]), pair, name="adamw_big")
        grads[n], deltas[n], new_m[n], new_v[n] = [o.reshape(shape) for o in outs]

    like = [w[n] for n in SMALL_WEIGHTS]
    all_partials = _gather_all(_pack([small_grads[n] for n in SMALL_WEIGHTS]), name="gather_small_grads")
    outs = _adamw(_pack(like), _pack([m[n] for n in SMALL_WEIGHTS]), _pack([v[n] for n in SMALL_WEIGHTS]), all_partials,
                  name="adamw_small")
    for store, buf in zip((grads, deltas, new_m, new_v), outs):
        store.update(zip(SMALL_WEIGHTS, _unpack(buf, like)))

    total_loss = lax.psum(loss[0, 0], ("x", "y", "c"))
    return (total_loss, dx[None], *[grads[n] for n in WEIGHT_ORDER], *[deltas[n] for n in WEIGHT_ORDER],
            *[new_m[n] for n in WEIGHT_ORDER], *[new_v[n] for n in WEIGHT_ORDER])
```

```python
import functools
import math

import jax
import jax.numpy as jnp
from jax import lax
from jax.experimental import pallas as pl
from jax.experimental.pallas import tpu as pltpu

F32 = jnp.float32
BF16 = jnp.bfloat16

D_MODEL = 1024
SB_HEADS = 8
SB_HEAD_DIM = 64
SB_WIDTH = SB_HEADS * SB_HEAD_DIM
POOL_WINDOWS = (2, 4, 8, 16)
POOL_GROUP_DIM = 64
POOL_WIDTH = 256
POOL_HALO = 16
GM_GROUPS = 4
GM_GROUP_DIM = 64
GM_WIDTH = 256
GM_CHUNK = 128
N_BRANCH = 3
D_FF = 4 * D_MODEL
RMS_EPS = 1e-6
QKV_WIDTH = 3 * SB_WIDTH
REST_WIDTH = POOL_WIDTH + 2 * GM_WIDTH + N_BRANCH * D_MODEL
D_IN = QKV_WIDTH + REST_WIDTH
GATE_COL = POOL_WIDTH + 2 * GM_WIDTH
LANES = 128
N_CHIPS = 4
N_DEV = 8

ADAM_LR = 0.001
ADAM_B1 = 0.9
ADAM_B2 = 0.999
ADAM_EPS = 1e-08
ADAM_WD = 0.01
ADAM_STEP = 10

VMEM_LIMIT = 56 * 1024 * 1024
MESH = pl.DeviceIdType.MESH


def _params(n_grid):
    return pltpu.CompilerParams(dimension_semantics=("arbitrary",) * n_grid, vmem_limit_bytes=VMEM_LIMIT)


def _bf(x):
    return x if x.dtype == BF16 else x.astype(BF16)


def _matmul(a, b, *, name, ta=False, tb=False, out_dtypes=(F32,), n=None, b_col_off=0, bm=1024, bn=1024, bk=2048,
            extras=(), epilogue=None):
    M, K = (a.shape[1], a.shape[0]) if ta else a.shape
    nb = b.shape[0] if tb else b.shape[1]
    n = nb if n is None else n
    bm, bn, bk = min(bm, M), min(bn, n), min(bk, K)
    assert M % bm == 0 and n % bn == 0 and K % bk == 0, (name, M, n, K, bm, bn, bk)
    assert (b.shape[1] if tb else b.shape[0]) == K, (name, a.shape, b.shape)
    nk = K // bk
    dims = (((0 if ta else 1,), (1 if tb else 0,)), ((), ()))
    n_out = len(out_dtypes)
    direct = nk > 1 and epilogue is None and out_dtypes == (F32,)
    use_acc = nk > 1 and not direct

    def body(*refs):
        a_ref, b_ref = refs[:2]
        extra_refs = refs[2:2 + len(extras)]
        out_refs = refs[2 + len(extras):2 + len(extras) + n_out]
        p = lax.dot_general(_bf(a_ref[...]), _bf(b_ref[...]), dims, preferred_element_type=F32)

        def finish(acc):
            outs = (acc,) if epilogue is None else epilogue(acc, *[r[...] for r in extra_refs])
            for r, o in zip(out_refs, outs):
                r[...] = o.astype(r.dtype)

        if nk == 1:
            finish(p)
            return
        k = pl.program_id(2)
        acc_ref = out_refs[0] if direct else refs[-1]

        @pl.when(k == 0)
        def _():
            acc_ref[...] = p

        @pl.when(k > 0)
        def _():
            acc_ref[...] += p

        if use_acc:
            @pl.when(k == nk - 1)
            def _():
                finish(acc_ref[...])

    a_spec = pl.BlockSpec((bk, bm), lambda i, j, k: (k, i)) if ta else pl.BlockSpec((bm, bk), lambda i, j, k: (i, k))
    if tb:
        assert b_col_off == 0
        b_spec = pl.BlockSpec((bn, bk), lambda i, j, k: (j, k))
    else:
        b_spec = pl.BlockSpec((bk, bn), lambda i, j, k: (k, j + b_col_off))
    tile = pl.BlockSpec((bm, bn), lambda i, j, k: (i, j))
    outs = pl.pallas_call(
        body, name=name, grid=(M // bm, n // bn, nk),
        in_specs=[a_spec, b_spec] + [tile] * len(extras),
        out_specs=[tile] * n_out,
        out_shape=[jax.ShapeDtypeStruct((M, n), d) for d in out_dtypes],
        scratch_shapes=[pltpu.VMEM((bm, bn), F32)] if use_acc else [],
        compiler_params=_params(3),
    )(a, b, *extras)
    return outs[0] if n_out == 1 else outs


ROW_TILE = 512


def _rows(S):
    tr = min(ROW_TILE, S)
    assert S % tr == 0
    return tr


def _rstd(x):
    return lax.rsqrt(jnp.mean(x * x, axis=-1, keepdims=True) + RMS_EPS)


def _rms_bwd_math(x, g, dy):
    r = _rstd(x)
    gd = g * dy
    dx = r * gd - x * (r * r * r) * jnp.mean(x * gd, axis=-1, keepdims=True)
    dg = jnp.sum(dy * x * r, axis=0, keepdims=True)
    return dx, dg


def _accumulate(ref, value):
    i = pl.program_id(0)

    @pl.when(i == 0)
    def _():
        ref[...] = value

    @pl.when(i > 0)
    def _():
        ref[...] += value


def _row_spec(tr, width):
    return pl.BlockSpec((tr, width), lambda i: (i, 0))


def _vec_spec(width):
    return pl.BlockSpec((1, width), lambda i: (0, 0))


def _rms_fwd(x, g, *, name):
    S, D = x.shape
    tr = _rows(S)

    def body(x_ref, g_ref, o_ref):
        xf = x_ref[...]
        o_ref[...] = (xf * _rstd(xf) * g_ref[...]).astype(o_ref.dtype)

    return pl.pallas_call(
        body, name=name, grid=(S // tr,), in_specs=[_row_spec(tr, D), _vec_spec(D)], out_specs=_row_spec(tr, D),
        out_shape=jax.ShapeDtypeStruct((S, D), BF16), compiler_params=_params(1))(x, g)


def _resid_rms(x, y, g_post, g_next, *, name):
    S, D = x.shape
    tr = _rows(S)
    with_next = g_next is not None

    def body(*refs):
        if with_next:
            x_ref, y_ref, gp_ref, gn_ref, xo_ref, ho_ref = refs
        else:
            x_ref, y_ref, gp_ref, xo_ref = refs
        yf = y_ref[...]
        xn = x_ref[...] + yf * _rstd(yf) * gp_ref[...]
        xo_ref[...] = xn
        if with_next:
            ho_ref[...] = (xn * _rstd(xn) * gn_ref[...]).astype(ho_ref.dtype)

    row, vec = _row_spec(tr, D), _vec_spec(D)
    ins = [x, y, g_post] + ([g_next] if with_next else [])
    outs = pl.pallas_call(
        body, name=name, grid=(S // tr,), in_specs=[row, row, vec] + ([vec] if with_next else []),
        out_specs=[row] + ([row] if with_next else []),
        out_shape=[jax.ShapeDtypeStruct((S, D), F32)] + ([jax.ShapeDtypeStruct((S, D), BF16)] if with_next else []),
        compiler_params=_params(1))(*ins)
    return (outs[0], outs[1]) if with_next else (outs[0], None)


def _rms_bwd(x, g, dy, *, name):
    S, D = x.shape
    tr = _rows(S)

    def body(x_ref, g_ref, dy_ref, dx_ref, dg_ref):
        dx, dg = _rms_bwd_math(x_ref[...], g_ref[...], dy_ref[...])
        dx_ref[...] = dx.astype(dx_ref.dtype)
        _accumulate(dg_ref, dg)

    row, vec = _row_spec(tr, D), _vec_spec(D)
    return pl.pallas_call(
        body, name=name, grid=(S // tr,), in_specs=[row, vec, row], out_specs=[row, vec],
        out_shape=[jax.ShapeDtypeStruct((S, D), BF16), jax.ShapeDtypeStruct((1, D), F32)],
        compiler_params=_params(1))(x, g, dy)


def _rms_bwd_chain(xa, ga, da, resid, xb, gb, *, name):
    S, D = xa.shape
    tr = _rows(S)
    chain = xb is not None

    def body(*refs):
        if chain:
            xa_ref, ga_ref, da_ref, rs_ref, xb_ref, gb_ref, dx_ref, dga_ref, dxb_ref, dgb_ref = refs
        else:
            xa_ref, ga_ref, da_ref, rs_ref, dx_ref, dga_ref = refs
        dxa, dga = _rms_bwd_math(xa_ref[...], ga_ref[...], da_ref[...])
        dx = rs_ref[...] + dxa
        dx_ref[...] = dx
        _accumulate(dga_ref, dga)
        if chain:
            dxb, dgb = _rms_bwd_math(xb_ref[...], gb_ref[...], dx)
            dxb_ref[...] = dxb.astype(dxb_ref.dtype)
            _accumulate(dgb_ref, dgb)

    row, vec = _row_spec(tr, D), _vec_spec(D)
    ins = [xa, ga, da, resid] + ([xb, gb] if chain else [])
    outs = pl.pallas_call(
        body, name=name, grid=(S // tr,), in_specs=[row, vec, row, row] + ([row, vec] if chain else []),
        out_specs=[row, vec] + ([row, vec] if chain else []),
        out_shape=[jax.ShapeDtypeStruct((S, D), F32), jax.ShapeDtypeStruct((1, D), F32)]
        + ([jax.ShapeDtypeStruct((S, D), BF16), jax.ShapeDtypeStruct((1, D), F32)] if chain else []),
        compiler_params=_params(1))(*ins)
    return tuple(outs) if chain else (outs[0], outs[1], None, None)


def _loss_head(y, target, *, name):
    S, D = y.shape
    tr = _rows(S)
    n_tiles = S // tr

    def body(y_ref, t_ref, dy_ref, loss_ref, acc_ref):
        err = y_ref[...] - t_ref[...]
        dy_ref[...] = err * (1.0 / D)
        _accumulate(acc_ref, jnp.sum(err * err, axis=0, keepdims=True))

        @pl.when(pl.program_id(0) == n_tiles - 1)
        def _():
            loss_ref[...] = jnp.sum(acc_ref[...], axis=1, keepdims=True) * (0.5 / D)

    row = _row_spec(tr, D)
    return pl.pallas_call(
        body, name=name, grid=(n_tiles,), in_specs=[row, row],
        out_specs=[row, pl.BlockSpec((1, 1), lambda i: (0, 0))],
        out_shape=[jax.ShapeDtypeStruct((S, D), F32), jax.ShapeDtypeStruct((1, 1), F32)],
        scratch_shapes=[pltpu.VMEM((1, D), F32)], compiler_params=_params(1))(y, target)


SB_TILE = 256
SB_PAIRS = SB_HEADS * SB_HEAD_DIM // LANES
SB_DEAD_LOG = -110.0


def _log_sigmoids(z):
    l1p = jnp.log(1.0 + jnp.exp(-jnp.abs(z)))
    return jnp.minimum(z, 0.0) - l1p, jnp.minimum(-z, 0.0) - l1p


def _split_bf16(x):
    hi = x.astype(BF16)
    lo = (x - hi.astype(F32)).astype(BF16)
    return jnp.concatenate([hi, lo], axis=1)


def _tri(T, cmp):
    j = lax.broadcasted_iota(jnp.int32, (T, T), 0)
    s = lax.broadcasted_iota(jnp.int32, (T, T), 1)
    m = jnp.where(cmp(j, s), 1.0, 0.0).astype(BF16)
    return jnp.concatenate([m, m], axis=0)


def _head_masks():
    lane = lax.broadcasted_iota(jnp.int32, (1, LANES), 1)
    return [lane < SB_HEAD_DIM, lane >= SB_HEAD_DIM]


def _sba_fwd(qkv, *, name):
    S = qkv.shape[0]
    T = min(SB_TILE, S)
    nq = S // T
    scale = SB_HEAD_DIM ** -0.5

    def body(q_ref, k_ref, v_ref, o_ref, t_ref, first_ref):
        i = pl.program_id(1)
        row = lax.broadcasted_iota(jnp.int32, (T, T), 0)
        col = lax.broadcasted_iota(jnp.int32, (T, T), 1)
        strict = col < row
        after = _tri(T, lambda j, s: j > s)
        q = q_ref[...]
        accs = []
        for h, hm in enumerate(_head_masks()):
            qh = jnp.where(hm, q, jnp.zeros_like(q))

            def block(j, carry, diag, qh=qh):
                C, acc = carry
                rows = pl.ds(pl.multiple_of(j * T, T), T)
                kb, vb = k_ref[rows, :], v_ref[rows, :]
                z = lax.dot_general(qh, kb, (((1,), (1,)), ((), ())), preferred_element_type=F32) * scale
                ls, ln = _log_sigmoids(z)
                if diag:
                    ln = jnp.where(strict, ln, 0.0)
                suffix = jnp.dot(_split_bf16(ln), after, preferred_element_type=F32)
                a = jnp.exp(ls + suffix + C)
                if diag:
                    a = jnp.where(strict, a, 0.0)
                acc = acc + jnp.dot(a.astype(BF16), vb, preferred_element_type=F32)
                return C + jnp.sum(ln, axis=1, keepdims=True), acc

            C, acc = block(i, (jnp.zeros((T, 1), F32), jnp.zeros((T, LANES), F32)), True)

            def alive(state):
                j, C, _ = state
                return jnp.logical_and(j >= 0, jnp.max(C) > SB_DEAD_LOG)

            def step(state, block=block):
                j, C, acc = state
                C, acc = block(j, (C, acc), False)
                return j - 1, C, acc

            j, C, acc = lax.while_loop(alive, step, (i - 1, C, acc))
            accs.append(acc)
            t_ref[h] = jnp.broadcast_to(C, (T, LANES))
            first_ref[8 * h:8 * (h + 1), :] = jnp.full((8, LANES), (j + 1).astype(F32))
        o_ref[...] = jnp.where(_head_masks()[0], accs[0], accs[1]).astype(o_ref.dtype)

    kv = lambda off: pl.BlockSpec((S, LANES), lambda p, i: (0, off + p))
    return pl.pallas_call(
        body, name=name, grid=(SB_PAIRS, nq),
        in_specs=[pl.BlockSpec((T, LANES), lambda p, i: (i, p)), kv(SB_PAIRS), kv(2 * SB_PAIRS)],
        out_specs=[pl.BlockSpec((T, LANES), lambda p, i: (i, p)), pl.BlockSpec((2, T, LANES), lambda p, i: (p, i, 0)),
                   pl.BlockSpec((None, None, 16, LANES), lambda p, i: (p, i, 0, 0))],
        out_shape=[jax.ShapeDtypeStruct((S, SB_WIDTH), BF16), jax.ShapeDtypeStruct((SB_HEADS, S, LANES), F32),
                   jax.ShapeDtypeStruct((SB_PAIRS, nq, 16, LANES), F32)],
        compiler_params=_params(2))(qkv, qkv, qkv)


def _sba_bwd(qkv, do, tot, first, *, name):
    S = qkv.shape[0]
    T = min(SB_TILE, S)
    nq = S // T
    scale = SB_HEAD_DIM ** -0.5

    def body(q_ref, k_ref, v_ref, do_ref, t_ref, first_ref, dq_ref, dk_ref, dv_ref, dk_acc, dv_acc):
        i = pl.program_id(1)

        @pl.when(i == 0)
        def _():
            dk_acc[...] = jnp.zeros_like(dk_acc)
            dv_acc[...] = jnp.zeros_like(dv_acc)

        row = lax.broadcasted_iota(jnp.int32, (T, T), 0)
        col = lax.broadcasted_iota(jnp.int32, (T, T), 1)
        strict = col < row
        upto = _tri(T, lambda j, s: j <= s)
        before = _tri(T, lambda j, s: j < s)
        q, do_t = q_ref[...], do_ref[...]
        dqs = []
        for h, hm in enumerate(_head_masks()):
            qh = jnp.where(hm, q, jnp.zeros_like(q))
            doh = jnp.where(hm, do_t, jnp.zeros_like(do_t))
            total = t_ref[h][:, 0:1]

            def block(j, carry, diag, qh=qh, doh=doh, total=total):
                P, G, dq = carry
                rows = pl.ds(pl.multiple_of(j * T, T), T)
                kb, vb = k_ref[rows, :], v_ref[rows, :]
                z = lax.dot_general(qh, kb, (((1,), (1,)), ((), ())), preferred_element_type=F32) * scale
                ls, ln = _log_sigmoids(z)
                if diag:
                    ln = jnp.where(strict, ln, 0.0)
                beta = jnp.exp(ls)
                suffix = (total - P) - jnp.dot(_split_bf16(ln), upto, preferred_element_type=F32)
                a = jnp.exp(ls + suffix)
                if diag:
                    a = jnp.where(strict, a, 0.0)
                g = a * lax.dot_general(doh, vb, (((1,), (1,)), ((), ())), preferred_element_type=F32)
                g_before = G + jnp.dot(_split_bf16(g), before, preferred_element_type=F32)
                dz = (g * (1.0 - beta) - g_before * beta) * scale
                if diag:
                    dz = jnp.where(strict, dz, 0.0)
                dzb, ab = dz.astype(BF16), a.astype(BF16)
                dq = dq + jnp.dot(dzb, kb, preferred_element_type=F32)
                over_queries = (((0,), (0,)), ((), ()))
                dk_acc[rows, :] += lax.dot_general(dzb, qh, over_queries, preferred_element_type=F32)
                dv_acc[rows, :] += lax.dot_general(ab, doh, over_queries, preferred_element_type=F32)
                return P + jnp.sum(ln, axis=1, keepdims=True), G + jnp.sum(g, axis=1, keepdims=True), dq

            zero = jnp.zeros((T, 1), F32)
            j0 = jnp.clip(jnp.max(first_ref[8 * h:8 * (h + 1), :]).astype(jnp.int32), 0, i)
            carry = lax.fori_loop(j0, i, lambda j, c, block=block: block(j, c, False),
                                  (zero, zero, jnp.zeros((T, LANES), F32)))
            dqs.append(block(i, carry, True)[2])
        dq_ref[...] = jnp.where(_head_masks()[0], dqs[0], dqs[1]).astype(dq_ref.dtype)

        @pl.when(i == nq - 1)
        def _():
            dk_ref[...] = dk_acc[...].astype(dk_ref.dtype)
            dv_ref[...] = dv_acc[...].astype(dv_ref.dtype)

    kv = lambda off: pl.BlockSpec((S, LANES), lambda p, i: (0, off + p))
    tile = lambda off: pl.BlockSpec((T, LANES), lambda p, i: (i, off + p))
    return pl.pallas_call(
        body, name=name, grid=(SB_PAIRS, nq),
        in_specs=[tile(0), kv(SB_PAIRS), kv(2 * SB_PAIRS), tile(0), pl.BlockSpec((2, T, LANES), lambda p, i: (p, i, 0)),
                  pl.BlockSpec((None, None, 16, LANES), lambda p, i: (p, i, 0, 0))],
        out_specs=[tile(0), kv(0), kv(0)],
        out_shape=[jax.ShapeDtypeStruct((S, SB_WIDTH), BF16)] * 3,
        scratch_shapes=[pltpu.VMEM((S, LANES), F32), pltpu.VMEM((S, LANES), F32)],
        compiler_params=_params(2))(qkv, qkv, qkv, do, tot, first)


POOL_TILE = 512


def _by_group(lane, values):
    return jnp.where(lane < 64, values[0], jnp.where(lane < 128, values[1], jnp.where(lane < 192, values[2], values[3])))


def _pool_inv_count(first_row, n_rows):
    t = first_row + lax.broadcasted_iota(jnp.int32, (n_rows, POOL_WIDTH), 0)
    lane = lax.broadcasted_iota(jnp.int32, (n_rows, POOL_WIDTH), 1)
    window = _by_group(lane, POOL_WINDOWS)
    return 1.0 / jnp.clip(t + 1, 1, window).astype(F32), lane


def _pooled(ext, first_row, R):
    n = R + POOL_HALO
    s2 = ext + pltpu.roll(ext, 1, 0)
    s4 = s2 + pltpu.roll(s2, 2, 0)
    s8 = s4 + pltpu.roll(s4, 4, 0)
    s16 = s8 + pltpu.roll(s8, 8, 0)
    inv, lane = _pool_inv_count(first_row - POOL_HALO, n)
    pooled = _by_group(lane, (s2, s4, s8, s16)) * inv - ext
    return pooled[POOL_HALO:, :]


def _pool_specs(S, R, col):
    per = R // POOL_HALO
    tile = pl.BlockSpec((R, POOL_WIDTH), lambda i: (i, col))
    prev = pl.BlockSpec((POOL_HALO, POOL_WIDTH), lambda i: (jnp.maximum(i * per - 1, 0), col))
    return tile, prev


def _pool_fwd(rest, w_bd, scale, *, name):
    S = rest.shape[0]
    R = min(POOL_TILE, S)

    def body(p_ref, prev_ref, w_ref, s_ref, o_ref, ext_ref):
        i = pl.program_id(0)
        ext_ref[:POOL_HALO, :] = jnp.where(i > 0, prev_ref[...], 0.0)
        ext_ref[POOL_HALO:, :] = p_ref[...]
        pooled = _pooled(ext_ref[...], i * R, R)
        mixed = jnp.dot(pooled.astype(BF16), w_ref[...], preferred_element_type=F32)
        o_ref[...] = (mixed * s_ref[...]).astype(o_ref.dtype)

    tile, prev = _pool_specs(S, R, 0)
    return pl.pallas_call(
        body, name=name, grid=(S // R,),
        in_specs=[tile, prev, pl.BlockSpec((POOL_WIDTH, POOL_WIDTH), lambda i: (0, 0)), _vec_spec(POOL_WIDTH)],
        out_specs=_row_spec(R, POOL_WIDTH), out_shape=jax.ShapeDtypeStruct((S, POOL_WIDTH), BF16),
        scratch_shapes=[pltpu.VMEM((R + POOL_HALO, POOL_WIDTH), F32)], compiler_params=_params(1))(rest, rest, w_bd, scale)


def _pool_bwd(rest, do, w_bd, scale, *, name):
    S = rest.shape[0]
    R = min(POOL_TILE, S)
    n_tiles = S // R
    per = R // POOL_HALO
    n = R + POOL_HALO

    def body(p_ref, prev_ref, do_ref, nxt_ref, w_ref, s_ref, dp_ref, dw_ref, ds_ref, ext_ref, dext_ref):
        i = pl.program_id(0)
        ext_ref[:POOL_HALO, :] = jnp.where(i > 0, prev_ref[...], 0.0)
        ext_ref[POOL_HALO:, :] = p_ref[...]
        pooled = _pooled(ext_ref[...], i * R, R).astype(BF16)
        w = w_ref[...]
        mixed = jnp.dot(pooled, w, preferred_element_type=F32)
        do_t = do_ref[...]
        _accumulate(ds_ref, jnp.sum(do_t * mixed, axis=0, keepdims=True))
        dext_ref[:R, :] = do_t
        dext_ref[R:, :] = jnp.where(i < n_tiles - 1, nxt_ref[...], 0.0)
        dmixed = (dext_ref[...] * s_ref[...]).astype(BF16)
        dpooled = lax.dot_general(dmixed, w, (((1,), (1,)), ((), ())), preferred_element_type=F32)
        _accumulate(dw_ref, lax.dot_general(pooled, dmixed[:R, :], (((0,), (0,)), ((), ())), preferred_element_type=F32))
        inv, lane = _pool_inv_count(i * R, n)
        u = dpooled * inv
        f2 = u + pltpu.roll(u, n - 1, 0)
        f4 = f2 + pltpu.roll(f2, n - 2, 0)
        f8 = f4 + pltpu.roll(f4, n - 4, 0)
        f16 = f8 + pltpu.roll(f8, n - 8, 0)
        dp = _by_group(lane, (f2, f4, f8, f16)) - dpooled
        dp_ref[...] = dp[:R, :].astype(dp_ref.dtype)

    tile, prev = _pool_specs(S, R, 0)
    nxt = pl.BlockSpec((POOL_HALO, POOL_WIDTH), lambda i: (jnp.minimum((i + 1) * per, S // POOL_HALO - 1), 0))
    full = pl.BlockSpec((POOL_WIDTH, POOL_WIDTH), lambda i: (0, 0))
    return pl.pallas_call(
        body, name=name, grid=(n_tiles,),
        in_specs=[tile, prev, _row_spec(R, POOL_WIDTH), nxt, full, _vec_spec(POOL_WIDTH)],
        out_specs=[_row_spec(R, POOL_WIDTH), full, _vec_spec(POOL_WIDTH)],
        out_shape=[jax.ShapeDtypeStruct((S, POOL_WIDTH), BF16), jax.ShapeDtypeStruct((POOL_WIDTH, POOL_WIDTH), F32),
                   jax.ShapeDtypeStruct((1, POOL_WIDTH), F32)],
        scratch_shapes=[pltpu.VMEM((n, POOL_WIDTH), F32), pltpu.VMEM((n, POOL_WIDTH), F32)],
        compiler_params=_params(1))(rest, rest, do, do, w_bd, scale)


GM_TILE = 512
GELU_C = math.sqrt(2.0 / math.pi)
GELU_A = 0.044715


def _gelu(x):
    return 0.5 * x * (1.0 + jnp.tanh(GELU_C * (x + GELU_A * x * x * x)))


def _gelu_and_grad(x):
    t = jnp.tanh(GELU_C * (x + GELU_A * x * x * x))
    y = 0.5 * x * (1.0 + t)
    dy = 0.5 * (1.0 + t) + 0.5 * x * (1.0 - t * t) * (GELU_C * (1.0 + 3.0 * GELU_A * x * x))
    return y, dy


def _group_lane_masks():
    lane = lax.broadcasted_iota(jnp.int32, (1, GM_WIDTH), 1)
    return [(lane >= g * GM_GROUP_DIM) & (lane < (g + 1) * GM_GROUP_DIM) for g in range(GM_GROUPS)]


def _stack_groups(x, masks):
    return jnp.concatenate([jnp.where(m, x, jnp.zeros_like(x)) for m in masks], axis=0)


def _gm_mixed(vn, ws_cat, bias, masks, R):
    chunks = []
    for c in range(R // GM_CHUNK):
        vc = vn[c * GM_CHUNK:(c + 1) * GM_CHUNK, :]
        chunks.append(jnp.dot(ws_cat, _stack_groups(vc, masks), preferred_element_type=F32) + bias)
    return jnp.concatenate(chunks, axis=0)


def _gm_specs(S, R):
    u = pl.BlockSpec((R, GM_WIDTH), lambda i: (i, 1))
    v = pl.BlockSpec((R, GM_WIDTH), lambda i: (i, 2))
    ws = pl.BlockSpec((GM_CHUNK, GM_GROUPS * GM_CHUNK), lambda i: (0, 0))
    bias = pl.BlockSpec((GM_CHUNK, GM_WIDTH), lambda i: (0, 0))
    return u, v, ws, bias


def _gm_fwd(rest, gain, ws_cat, bias, *, name):
    S = rest.shape[0]
    R = min(GM_TILE, S)

    def body(u_ref, v_ref, g_ref, ws_ref, b_ref, o_ref):
        gv = _gelu(v_ref[...])
        vn = (gv * _rstd(gv) * g_ref[...]).astype(BF16)
        mixed = _gm_mixed(vn, ws_ref[...], b_ref[...], _group_lane_masks(), R)
        o_ref[...] = (_gelu(u_ref[...]) * mixed).astype(o_ref.dtype)

    u_spec, v_spec, ws_spec, bias_spec = _gm_specs(S, R)
    return pl.pallas_call(
        body, name=name, grid=(S // R,), in_specs=[u_spec, v_spec, _vec_spec(GM_WIDTH), ws_spec, bias_spec],
        out_specs=_row_spec(R, GM_WIDTH), out_shape=jax.ShapeDtypeStruct((S, GM_WIDTH), BF16),
        compiler_params=_params(1))(rest, rest, gain, ws_cat, bias)


def _gm_bwd(rest, do, gain, ws_cat, wst_cat, bias, *, name):
    S = rest.shape[0]
    R = min(GM_TILE, S)

    def body(u_ref, v_ref, do_ref, g_ref, ws_ref, wst_ref, b_ref, du_ref, dv_ref, dg_ref, dws_ref, db_ref):
        masks = _group_lane_masks()
        gain_v = g_ref[...]
        gu, dgu = _gelu_and_grad(u_ref[...])
        gv, dgv = _gelu_and_grad(v_ref[...])
        r = _rstd(gv)
        vn = (gv * r * gain_v).astype(BF16)
        mixed = _gm_mixed(vn, ws_ref[...], b_ref[...], masks, R)
        do_t = do_ref[...]
        du_ref[...] = (do_t * mixed * dgu).astype(du_ref.dtype)
        dmix = do_t * gu
        dmix_b = dmix.astype(BF16)
        wst = wst_ref[...]
        dvn_chunks, db, dws = [], None, [None] * GM_GROUPS
        for c in range(R // GM_CHUNK):
            rows = slice(c * GM_CHUNK, (c + 1) * GM_CHUNK)
            dc, dcb, vc = dmix[rows, :], dmix_b[rows, :], vn[rows, :]
            db = dc if db is None else db + dc
            dvn_chunks.append(jnp.dot(wst, _stack_groups(dcb, masks), preferred_element_type=F32))
            for g, m in enumerate(masks):
                part = lax.dot_general(jnp.where(m, dcb, jnp.zeros_like(dcb)), vc, (((1,), (1,)), ((), ())),
                                       preferred_element_type=F32)
                dws[g] = part if dws[g] is None else dws[g] + part
        dvn = jnp.concatenate(dvn_chunks, axis=0)
        lane = lax.broadcasted_iota(jnp.int32, (1, LANES), 1)
        db_groups = jnp.zeros((GM_CHUNK, LANES), F32)
        for g, m in enumerate(masks):
            total = jnp.sum(jnp.where(m, db, 0.0), axis=1, keepdims=True)
            db_groups = db_groups + jnp.where(lane == g, total, 0.0)
        _accumulate(db_ref, db_groups)
        i = pl.program_id(0)
        for g in range(GM_GROUPS):
            @pl.when(i == 0)
            def _(g=g):
                dws_ref[g] = dws[g]

            @pl.when(i > 0)
            def _(g=g):
                dws_ref[g] += dws[g]
        _accumulate(dg_ref, jnp.sum(dvn * gv * r, axis=0, keepdims=True))
        gd = gain_v * dvn
        dgv_in = r * gd - gv * (r * r * r) * jnp.mean(gv * gd, axis=-1, keepdims=True)
        dv_ref[...] = (dgv_in * dgv).astype(dv_ref.dtype)

    u_spec, v_spec, ws_spec, bias_spec = _gm_specs(S, R)
    row, vec = _row_spec(R, GM_WIDTH), _vec_spec(GM_WIDTH)
    dws_spec = pl.BlockSpec((GM_GROUPS, GM_CHUNK, GM_CHUNK), lambda i: (0, 0, 0))
    return pl.pallas_call(
        body, name=name, grid=(S // R,), in_specs=[u_spec, v_spec, row, vec, ws_spec, ws_spec, bias_spec],
        out_specs=[row, row, vec, dws_spec, pl.BlockSpec((GM_CHUNK, LANES), lambda i: (0, 0))],
        out_shape=[jax.ShapeDtypeStruct((S, GM_WIDTH), BF16)] * 2
        + [jax.ShapeDtypeStruct((1, GM_WIDTH), F32), jax.ShapeDtypeStruct((GM_GROUPS, GM_CHUNK, GM_CHUNK), F32),
           jax.ShapeDtypeStruct((GM_CHUNK, LANES), F32)],
        compiler_params=_params(1))(rest, rest, do, gain, ws_cat, wst_cat, bias)


GATE_ROWS = 1024
GATE_COLS = 256
GATE_BLOCKS = D_MODEL // GATE_COLS


def _gate_spec(tr, k):
    return pl.BlockSpec((tr, GATE_COLS), lambda i, j: (i, GATE_COL // GATE_COLS + GATE_BLOCKS * k + j))


def _merge_fwd(rest, branches, *, name):
    S = rest.shape[0]
    tr = min(GATE_ROWS, S)

    def body(g0, g1, g2, b0, b1, b2, o_ref):
        acc = None
        for g_ref, b_ref in ((g0, b0), (g1, b1), (g2, b2)):
            term = jax.nn.sigmoid(g_ref[...]) * b_ref[...]
            acc = term if acc is None else acc + term
        o_ref[...] = acc.astype(o_ref.dtype)

    tile = pl.BlockSpec((tr, GATE_COLS), lambda i, j: (i, j))
    return pl.pallas_call(
        body, name=name, grid=(S // tr, GATE_BLOCKS),
        in_specs=[_gate_spec(tr, k) for k in range(N_BRANCH)] + [tile] * N_BRANCH, out_specs=tile,
        out_shape=jax.ShapeDtypeStruct((S, D_MODEL), BF16), compiler_params=_params(2))(rest, rest, rest, *branches)


def _merge_bwd(rest, branches, dmerged, *, name):
    S = rest.shape[0]
    tr = min(GATE_ROWS, S)

    def body(g0, g1, g2, b0, b1, b2, dm_ref, dg0, dg1, dg2, db0, db1, db2):
        dm = dm_ref[...]
        for g_ref, b_ref, dg_ref, db_ref in ((g0, b0, dg0, db0), (g1, b1, dg1, db1), (g2, b2, dg2, db2)):
            s = jax.nn.sigmoid(g_ref[...])
            db_ref[...] = (dm * s).astype(db_ref.dtype)
            dg_ref[...] = (dm * b_ref[...] * s * (1.0 - s)).astype(dg_ref.dtype)

    tile = pl.BlockSpec((tr, GATE_COLS), lambda i, j: (i, j))
    return pl.pallas_call(
        body, name=name, grid=(S // tr, GATE_BLOCKS),
        in_specs=[_gate_spec(tr, k) for k in range(N_BRANCH)] + [tile] * (N_BRANCH + 1), out_specs=[tile] * (2 * N_BRANCH),
        out_shape=[jax.ShapeDtypeStruct((S, D_MODEL), BF16)] * (2 * N_BRANCH),
        compiler_params=_params(2))(rest, rest, rest, *branches, dmerged)


TILE_BYTES = 24 * 1024 * 1024


BF16_ROWS = 16


def _tile_rows(rows, cols, n_arrays):
    padded = -(-cols // LANES) * LANES
    cap = max(BF16_ROWS, TILE_BYTES // (2 * n_arrays * padded * 4))
    best = None
    for tr in range(BF16_ROWS, min(rows, cap) + 1, BF16_ROWS):
        if rows % tr == 0:
            best = tr
    assert best is not None, (rows, cols)
    return best


def _sum_slots(stack, *, name):
    n, R, C = stack.shape
    tr = _tile_rows(R, C, n + 1)

    def body(s_ref, o_ref):
        acc = s_ref[0].astype(F32)
        for k in range(1, n):
            acc = acc + s_ref[k].astype(F32)
        o_ref[...] = acc

    return pl.pallas_call(
        body, name=name, grid=(R // tr,), in_specs=[pl.BlockSpec((n, tr, C), lambda i: (0, i, 0))],
        out_specs=_row_spec(tr, C), out_shape=jax.ShapeDtypeStruct((R, C), F32), compiler_params=_params(1))(stack)


def _add_own_half(parts, received, core, *, name):
    n, R, C = parts.shape
    half = R // 2
    tr = _tile_rows(half, C, 3)
    steps = half // tr

    def body(core_ref, own_ref, got_ref, o_ref):
        o_ref[...] = (own_ref[...] + got_ref[...]).astype(o_ref.dtype)

    tile = pl.BlockSpec((None, tr, C), lambda d, i, core_ref: (d, i, 0))
    own = pl.BlockSpec((None, tr, C), lambda d, i, core_ref: (d, core_ref[0] * steps + i, 0))
    return pl.pallas_call(
        body, name=name, out_shape=jax.ShapeDtypeStruct((n, half, C), BF16),
        grid_spec=pltpu.PrefetchScalarGridSpec(num_scalar_prefetch=1, grid=(n, steps), in_specs=[own, tile], out_specs=tile),
        compiler_params=_params(2))(core, parts, received)


def _adamw(w, m, v, gstack, *, name):
    R, C = w.shape
    n = gstack.shape[0]
    tr = _tile_rows(R, C, n + 7)

    def body(w_ref, m_ref, v_ref, g_ref, go_ref, d_ref, mo_ref, vo_ref):
        g = g_ref[0]
        for k in range(1, n):
            g = g + g_ref[k]
        m_new = ADAM_B1 * m_ref[...] + (1.0 - ADAM_B1) * g
        v_new = ADAM_B2 * v_ref[...] + (1.0 - ADAM_B2) * jnp.square(g)
        m_hat = m_new / (1.0 - ADAM_B1 ** ADAM_STEP)
        v_hat = v_new / (1.0 - ADAM_B2 ** ADAM_STEP)
        go_ref[...] = g
        d_ref[...] = -ADAM_LR * (m_hat / (jnp.sqrt(v_hat) + ADAM_EPS) + ADAM_WD * w_ref[...])
        mo_ref[...] = m_new
        vo_ref[...] = v_new

    row = _row_spec(tr, C)
    return pl.pallas_call(
        body, name=name, grid=(R // tr,), in_specs=[row, row, row, pl.BlockSpec((n, tr, C), lambda i: (0, i, 0))],
        out_specs=[row] * 4, out_shape=[jax.ShapeDtypeStruct((R, C), F32)] * 4, compiler_params=_params(1))(w, m, v, gstack)


HBM_SPEC = pl.BlockSpec(memory_space=pl.ANY)


def _position():
    return lax.axis_index("x"), lax.axis_index("y"), lax.axis_index("c")


def _other_chips(x, y):
    return [(1 - x, y), (x, 1 - y), (1 - x, 1 - y)]


def _chip_exchange(arrays, *, scatter, name):
    n = len(arrays)

    def body(*refs):
        ins, outs = refs[:n], refs[n:2 * n]
        send_sems, recv_sems, local_sems = refs[2 * n:]
        x, y, c = _position()
        me = 2 * x + y
        copies = []
        for a in range(n):
            own = ins[a].at[me] if scatter else ins[a]
            local = pltpu.make_async_copy(own, outs[a].at[me], local_sems.at[a])
            local.start()
            copies.append(local)
            for k, (px, py) in enumerate(_other_chips(x, y)):
                src = ins[a].at[2 * px + py] if scatter else ins[a]
                remote = pltpu.make_async_remote_copy(
                    src_ref=src, dst_ref=outs[a].at[me], send_sem=send_sems.at[3 * a + k],
                    recv_sem=recv_sems.at[3 * a + k], device_id=(px, py, c), device_id_type=MESH)
                remote.start()
                copies.append(remote)
        for cp in copies:
            cp.wait()

    out_shapes = [jax.ShapeDtypeStruct(a.shape if scatter else (N_CHIPS,) + a.shape, a.dtype) for a in arrays]
    return pl.pallas_call(
        body, name=name, in_specs=[HBM_SPEC] * n, out_specs=[HBM_SPEC] * n, out_shape=out_shapes,
        scratch_shapes=[pltpu.SemaphoreType.DMA((3 * n,)), pltpu.SemaphoreType.DMA((3 * n,)), pltpu.SemaphoreType.DMA((n,))],
    )(*arrays)


def _sibling_pair(arrays, *, name):
    n = len(arrays)

    def body(*refs):
        ins, outs = refs[:n], refs[n:2 * n]
        send_sems, recv_sems, local_sems = refs[2 * n:]
        x, y, c = _position()
        copies = []
        for a in range(n):
            local = pltpu.make_async_copy(ins[a], outs[a].at[c], local_sems.at[a])
            remote = pltpu.make_async_remote_copy(
                src_ref=ins[a], dst_ref=outs[a].at[c], send_sem=send_sems.at[a], recv_sem=recv_sems.at[a],
                device_id=(x, y, 1 - c), device_id_type=MESH)
            local.start()
            remote.start()
            copies += [local, remote]
        for cp in copies:
            cp.wait()

    return pl.pallas_call(
        body, name=name, in_specs=[HBM_SPEC] * n, out_specs=[HBM_SPEC] * n,
        out_shape=[jax.ShapeDtypeStruct((2,) + a.shape, a.dtype) for a in arrays],
        scratch_shapes=[pltpu.SemaphoreType.DMA((n,)), pltpu.SemaphoreType.DMA((n,)), pltpu.SemaphoreType.DMA((n,))],
    )(*arrays)


def _sibling_other_half(arrays, *, name):
    n = len(arrays)

    def body(*refs):
        ins, outs = refs[:n], refs[n:2 * n]
        send_sems, recv_sems = refs[2 * n:]
        x, y, c = _position()
        copies = []
        for a in range(n):
            half = ins[a].shape[1] // 2
            theirs = ins[a].at[:, pl.ds(pl.multiple_of((1 - c) * half, BF16_ROWS), half), :]
            cp = pltpu.make_async_remote_copy(
                src_ref=theirs, dst_ref=outs[a], send_sem=send_sems.at[a], recv_sem=recv_sems.at[a],
                device_id=(x, y, 1 - c), device_id_type=MESH)
            cp.start()
            copies.append(cp)
        for cp in copies:
            cp.wait()

    return pl.pallas_call(
        body, name=name, in_specs=[HBM_SPEC] * n, out_specs=[HBM_SPEC] * n,
        out_shape=[jax.ShapeDtypeStruct((a.shape[0], a.shape[1] // 2, a.shape[2]), a.dtype) for a in arrays],
        scratch_shapes=[pltpu.SemaphoreType.DMA((n,)), pltpu.SemaphoreType.DMA((n,))],
    )(*arrays)


def _gather_all(buf, *, name):
    R, C = buf.shape

    def body(in_ref, out_ref, send_sems, recv_sems):
        x, y, c = _position()
        out_ref[4 * x + 2 * y + c] = in_ref[...]
        copies = []
        for k in range(1, N_DEV):
            px = 1 - x if k & 4 else x
            py = 1 - y if k & 2 else y
            pc = 1 - c if k & 1 else c
            cp = pltpu.make_async_remote_copy(
                src_ref=in_ref, dst_ref=out_ref.at[4 * x + 2 * y + c], send_sem=send_sems.at[k - 1],
                recv_sem=recv_sems.at[k - 1], device_id=(px, py, pc), device_id_type=MESH)
            cp.start()
            copies.append(cp)
        for cp in copies:
            cp.wait()

    vmem = pl.BlockSpec(memory_space=pltpu.VMEM)
    return pl.pallas_call(
        body, name=name, in_specs=[vmem], out_specs=vmem, out_shape=jax.ShapeDtypeStruct((N_DEV, R, C), buf.dtype),
        scratch_shapes=[pltpu.SemaphoreType.DMA((N_DEV - 1,)), pltpu.SemaphoreType.DMA((N_DEV - 1,))],
        compiler_params=pltpu.CompilerParams(vmem_limit_bytes=VMEM_LIMIT),
    )(buf)


def _relu2(p):
    return p, jnp.square(jnp.maximum(p, 0.0))


def _relu2_grad(p, a):
    return (p * (2.0 * jnp.maximum(a, 0.0)),)


def _mixer_constants(w_pool, w_spatial, b_spatial):
    eye = jnp.eye(len(POOL_WINDOWS), dtype=F32)
    w_bd = (eye[:, None, :, None] * w_pool[:, :, None, :]).reshape(POOL_WIDTH, POOL_WIDTH).astype(BF16)
    causal = jnp.tril(jnp.ones((GM_CHUNK, GM_CHUNK), dtype=bool))
    ws = jnp.where(causal[None], w_spatial, 0.0).astype(BF16)
    ws_cat = ws.transpose(1, 0, 2).reshape(GM_CHUNK, GM_GROUPS * GM_CHUNK)
    wst_cat = ws.transpose(2, 0, 1).reshape(GM_CHUNK, GM_GROUPS * GM_CHUNK)
    bias = jnp.repeat(b_spatial.T, GM_GROUP_DIM, axis=1)
    return w_bd, ws_cat, wst_cat, bias


def _local_step(x, target, big, small):
    L = small["g_mix_pre"].shape[0]
    vec = lambda name, l: small[name][l][None, :]
    consts = [_mixer_constants(small["w_pool"][l], small["w_spatial"][l], small["b_spatial"][l]) for l in range(L)]
    saved = []
    h = _rms_fwd(x, vec("g_mix_pre", 0), name="rms_in")
    for l in range(L):
        w_bd, ws_cat, wst_cat, bias = consts[l]
        qkv = _matmul(h, big["w_in"][l], n=QKV_WIDTH, bn=768, out_dtypes=(BF16,), name="proj_qkv")
        rest = _matmul(h, big["w_in"][l], n=REST_WIDTH, bn=768, b_col_off=QKV_WIDTH // 768, name="proj_rest")
        o_sb, tot, first = _sba_fwd(qkv, name="sba_fwd")
        o_pool = _pool_fwd(rest, w_bd, vec("pool_scale", l), name="pool_fwd")
        o_gm = _gm_fwd(rest, vec("gm_gain", l), ws_cat, bias, name="gm_fwd")
        branches = (_matmul(o_sb, big["w_br_sb"][l], name="br_sb"), _matmul(o_pool, big["w_br_pool"][l], name="br_pool"),
                    _matmul(o_gm, big["w_br_gm"][l], name="br_gm"))
        merged = _merge_fwd(rest, branches, name="merge_fwd")
        y = _matmul(merged, big["w_out"][l], name="out_proj")
        x1, h2 = _resid_rms(x, y, vec("g_mix_post", l), vec("g_ff_pre", l), name="resid_mix")
        a, r = _matmul(h2, big["w_ff_in"][l], out_dtypes=(F32, BF16), epilogue=_relu2, name="ff_in")
        ff = _matmul(r, big["w_ff_out"][l], name="ff_out")
        g_next = vec("g_mix_pre", l + 1) if l + 1 < L else None
        x2, h_next = _resid_rms(x1, ff, vec("g_ff_post", l), g_next, name="resid_ff" if l + 1 < L else "resid_last")
        saved.append(dict(x=x, h=h, qkv=qkv, rest=rest, o_sb=o_sb, tot=tot, first=first, o_pool=o_pool, o_gm=o_gm, branches=branches,
                          merged=merged, y=y, x1=x1, h2=h2, a=a, r=r, ff=ff))
        x, h = x2, h_next

    dx2, loss = _loss_head(x, target, name="loss_head")
    gb = {k: [None] * L for k in ("w_in", "w_br_sb", "w_br_pool", "w_br_gm", "w_out", "w_ff_in", "w_ff_out")}
    gs = {k: [None] * L for k in ("w_pool", "pool_scale", "gm_gain", "w_spatial", "b_spatial", "g_mix_pre",
                                  "g_mix_post", "g_ff_pre", "g_ff_post")}
    d_ff, gs["g_ff_post"][L - 1] = _rms_bwd(saved[-1]["ff"], vec("g_ff_post", L - 1), dx2, name="rms_bwd_last")
    for l in reversed(range(L)):
        s = saved[l]
        w_bd, ws_cat, wst_cat, bias = consts[l]
        da = _matmul(d_ff, big["w_ff_out"][l], tb=True, out_dtypes=(BF16,), extras=(s["a"],), epilogue=_relu2_grad,
                     name="ff_out_dx")
        gb["w_ff_out"][l] = _matmul(s["r"], d_ff, ta=True, name="ff_out_dw")
        dh2 = _matmul(da, big["w_ff_in"][l], tb=True, name="ff_in_dx")
        gb["w_ff_in"][l] = _matmul(s["h2"], da, ta=True, name="ff_in_dw")
        dx1, gs["g_ff_pre"][l], dy, gs["g_mix_post"][l] = _rms_bwd_chain(
            s["x1"], vec("g_ff_pre", l), dh2, dx2, s["y"], vec("g_mix_post", l), name="rms_bwd_mid")
        dmerged = _matmul(dy, big["w_out"][l], tb=True, name="out_proj_dx")
        gb["w_out"][l] = _matmul(s["merged"], dy, ta=True, name="out_proj_dw")
        dg0, dg1, dg2, db_sb, db_pool, db_gm = _merge_bwd(s["rest"], s["branches"], dmerged, name="merge_bwd")
        do_sb = _matmul(db_sb, big["w_br_sb"][l], tb=True, out_dtypes=(BF16,), name="br_sb_dx")
        gb["w_br_sb"][l] = _matmul(s["o_sb"], db_sb, ta=True, name="br_sb_dw")
        do_pool = _matmul(db_pool, big["w_br_pool"][l], tb=True, name="br_pool_dx")
        gb["w_br_pool"][l] = _matmul(s["o_pool"], db_pool, ta=True, name="br_pool_dw")
        do_gm = _matmul(db_gm, big["w_br_gm"][l], tb=True, name="br_gm_dx")
        gb["w_br_gm"][l] = _matmul(s["o_gm"], db_gm, ta=True, name="br_gm_dw")
        dq, dk, dv = _sba_bwd(s["qkv"], do_sb, s["tot"], s["first"], name="sba_bwd")
        dp, dw_bd, gs["pool_scale"][l] = _pool_bwd(s["rest"], do_pool, w_bd, vec("pool_scale", l), name="pool_bwd")
        du, dgv, gs["gm_gain"][l], dws, db = _gm_bwd(s["rest"], do_gm, vec("gm_gain", l), ws_cat, wst_cat, bias,
                                                      name="gm_bwd")
        gs["w_pool"][l] = jnp.stack([dw_bd[g * 64:(g + 1) * 64, g * 64:(g + 1) * 64] for g in range(len(POOL_WINDOWS))])
        gs["w_spatial"][l] = jnp.where(jnp.tril(jnp.ones((GM_CHUNK, GM_CHUNK), dtype=bool))[None], dws, 0.0)
        gs["b_spatial"][l] = db[:, :GM_GROUPS].T
        dproj = jnp.concatenate([dq, dk, dv, dp, du, dgv, dg0, dg1, dg2], axis=1)
        dh = _matmul(dproj, big["w_in"][l], tb=True, bk=D_IN // 3, name="proj_dx")
        gb["w_in"][l] = _matmul(s["h"], dproj, ta=True, bn=768, name="proj_dw")
        if l > 0:
            dx2, gs["g_mix_pre"][l], d_ff, gs["g_ff_post"][l - 1] = _rms_bwd_chain(
                s["x"], vec("g_mix_pre", l), dh, dx1, saved[l - 1]["ff"], vec("g_ff_post", l - 1), name="rms_bwd_mid")
        else:
            dx2, gs["g_mix_pre"][l], _, _ = _rms_bwd_chain(s["x"], vec("g_mix_pre", l), dh, dx1, None, None,
                                                           name="rms_bwd_first")
    small_grads = {k: jnp.stack([g.reshape(small[k].shape[1:]) for g in v]) for k, v in gs.items()}
    return loss, dx2, gb, small_grads


COLUMN_SHARDED = ("w_in", "w_br_sb", "w_br_pool", "w_br_gm", "w_ff_in")
ROW_SHARDED = ("w_out", "w_ff_out")
BIG_WEIGHTS = COLUMN_SHARDED + ROW_SHARDED
SMALL_WEIGHTS = ("w_pool", "pool_scale", "gm_gain", "w_spatial", "b_spatial", "g_mix_pre", "g_mix_post", "g_ff_pre",
                 "g_ff_post")
WEIGHT_ORDER = ("w_in", "w_pool", "pool_scale", "gm_gain", "w_spatial", "b_spatial", "w_br_sb", "w_br_pool", "w_br_gm",
                "w_out", "g_mix_pre", "g_mix_post", "g_ff_pre", "g_ff_post", "w_ff_in", "w_ff_out")


def _full_weight(name, gathered, l):
    g = gathered[:, :, l]
    half, cols = g.shape[2], g.shape[3]
    if name in COLUMN_SHARDED:
        return g.transpose(0, 2, 1, 3).reshape(2 * half, N_CHIPS * cols)
    return g.transpose(1, 0, 2, 3).reshape(N_CHIPS * 2 * half, cols)


def _parts_by_chip(name, grads):
    g = jnp.stack(grads)
    L = g.shape[0]
    if name in COLUMN_SHARDED:
        r, c = g.shape[1], g.shape[2] // N_CHIPS
        return g.reshape(L, r, N_CHIPS, c).transpose(2, 0, 1, 3).reshape(N_CHIPS, L * r, c)
    r, c = g.shape[1] // N_CHIPS, g.shape[2]
    return g.reshape(L, N_CHIPS, r, c).transpose(1, 0, 2, 3).reshape(N_CHIPS, L * r, c)


def _pack(arrays):
    flat = jnp.concatenate([a.reshape(-1) for a in arrays])
    return flat.reshape(-1, LANES)


def _unpack(buf, like):
    flat, out, at = buf.reshape(-1), [], 0
    for a in like:
        out.append(flat[at:at + a.size].reshape(a.shape))
        at += a.size
    return out


def kernel(x, w_in, w_pool, pool_scale, gm_gain, w_spatial, b_spatial, w_br_sb, w_br_pool, w_br_gm, w_out, g_mix_pre, g_mix_post, g_ff_pre, g_ff_post, w_ff_in, w_ff_out, loss_target, m_w_in, m_w_pool, m_pool_scale, m_gm_gain, m_w_spatial, m_b_spatial, m_w_br_sb, m_w_br_pool, m_w_br_gm, m_w_out, m_g_mix_pre, m_g_mix_post, m_g_ff_pre, m_g_ff_post, m_w_ff_in, m_w_ff_out, v_w_in, v_w_pool, v_pool_scale, v_gm_gain, v_w_spatial, v_b_spatial, v_w_br_sb, v_w_br_pool, v_w_br_gm, v_w_out, v_g_mix_pre, v_g_mix_post, v_g_ff_pre, v_g_ff_post, v_w_ff_in, v_w_ff_out):
    given = dict(locals())
    w = {n: given[n] for n in WEIGHT_ORDER}
    m = {n: given["m_" + n] for n in WEIGHT_ORDER}
    v = {n: given["v_" + n] for n in WEIGHT_ORDER}
    L = w_in.shape[0]

    core = lax.axis_index("c")

    def my_rows(a):
        half = a.shape[1] // 2
        return lax.dynamic_slice_in_dim(a.astype(BF16), core * half, half, axis=1)

    halves = _chip_exchange([my_rows(w[n]) for n in BIG_WEIGHTS], scatter=False, name="gather_weights")
    gathered = _sibling_pair(halves, name="pair_weights")
    big = {n: [_full_weight(n, g, l) for l in range(L)] for n, g in zip(BIG_WEIGHTS, gathered)}
    small = {n: w[n] for n in SMALL_WEIGHTS}

    loss, dx, big_grads, small_grads = _local_step(x[0], loss_target[0], big, small)

    parts = [_parts_by_chip(n, big_grads[n]) for n in BIG_WEIGHTS]
    from_sibling = _sibling_other_half(parts, name="swap_grad_halves")
    core_index = core.astype(jnp.int32).reshape(1)
    chip_sums = [_add_own_half(p, s, core_index, name="sum_cores") for p, s in zip(parts, from_sibling)]
    received = _chip_exchange(chip_sums, scatter=True, name="exchange_grads")
    reduced = [_sum_slots(r, name="sum_chips") for r in received]
    pairs = _sibling_pair(reduced, name="pair_grads")
    grads, deltas, new_m, new_v = {}, {}, {}, {}
    for n, pair in zip(BIG_WEIGHTS, pairs):
        shape = w[n].shape
        flat = lambda a: a.reshape(-1, shape[-1])
        outs = _adamw(flat(w[n]), flat(m[n]), flat(v[n]), flat(pair)[None], name="adamw_big")
        grads[n], deltas[n], new_m[n], new_v[n] = [o.reshape(shape) for o in outs]

    like = [w[n] for n in SMALL_WEIGHTS]
    all_partials = _gather_all(_pack([small_grads[n] for n in SMALL_WEIGHTS]), name="gather_small_grads")
    outs = _adamw(_pack(like), _pack([m[n] for n in SMALL_WEIGHTS]), _pack([v[n] for n in SMALL_WEIGHTS]), all_partials,
                  name="adamw_small")
    for store, buf in zip((grads, deltas, new_m, new_v), outs):
        store.update(zip(SMALL_WEIGHTS, _unpack(buf, like)))

    total_loss = lax.psum(loss[0, 0], ("x", "y", "c"))
    return (total_loss, dx[None], *[grads[n] for n in WEIGHT_ORDER], *[deltas[n] for n in WEIGHT_ORDER],
            *[new_m[n] for n in WEIGHT_ORDER], *[new_v[n] for n in WEIGHT_ORDER])
```

```python
import functools
import math

import jax
import jax.numpy as jnp
from jax import lax
from jax.experimental import pallas as pl
from jax.experimental.pallas import tpu as pltpu

F32 = jnp.float32
BF16 = jnp.bfloat16

D_MODEL = 1024
SB_HEADS = 8
SB_HEAD_DIM = 64
SB_WIDTH = SB_HEADS * SB_HEAD_DIM
POOL_WINDOWS = (2, 4, 8, 16)
POOL_GROUP_DIM = 64
POOL_WIDTH = 256
POOL_HALO = 16
GM_GROUPS = 4
GM_GROUP_DIM = 64
GM_WIDTH = 256
GM_CHUNK = 128
N_BRANCH = 3
D_FF = 4 * D_MODEL
RMS_EPS = 1e-6
QKV_WIDTH = 3 * SB_WIDTH
REST_WIDTH = POOL_WIDTH + 2 * GM_WIDTH + N_BRANCH * D_MODEL
D_IN = QKV_WIDTH + REST_WIDTH
GATE_COL = POOL_WIDTH + 2 * GM_WIDTH
LANES = 128
N_CHIPS = 4
N_DEV = 8

ADAM_LR = 0.001
ADAM_B1 = 0.9
ADAM_B2 = 0.999
ADAM_EPS = 1e-08
ADAM_WD = 0.01
ADAM_STEP = 10

VMEM_LIMIT = 56 * 1024 * 1024
MESH = pl.DeviceIdType.MESH


def _params(n_grid):
    return pltpu.CompilerParams(dimension_semantics=("arbitrary",) * n_grid, vmem_limit_bytes=VMEM_LIMIT)


def _bf(x):
    return x if x.dtype == BF16 else x.astype(BF16)


def _matmul(a, b, *, name, ta=False, tb=False, out_dtypes=(F32,), n=None, b_col_off=0, bm=1024, bn=1024, bk=2048,
            extras=(), epilogue=None):
    M, K = (a.shape[1], a.shape[0]) if ta else a.shape
    nb = b.shape[0] if tb else b.shape[1]
    n = nb if n is None else n
    bm, bn, bk = min(bm, M), min(bn, n), min(bk, K)
    assert M % bm == 0 and n % bn == 0 and K % bk == 0, (name, M, n, K, bm, bn, bk)
    assert (b.shape[1] if tb else b.shape[0]) == K, (name, a.shape, b.shape)
    nk = K // bk
    dims = (((0 if ta else 1,), (1 if tb else 0,)), ((), ()))
    n_out = len(out_dtypes)
    direct = nk > 1 and epilogue is None and out_dtypes == (F32,)
    use_acc = nk > 1 and not direct

    def body(*refs):
        a_ref, b_ref = refs[:2]
        extra_refs = refs[2:2 + len(extras)]
        out_refs = refs[2 + len(extras):2 + len(extras) + n_out]
        p = lax.dot_general(_bf(a_ref[...]), _bf(b_ref[...]), dims, preferred_element_type=F32)

        def finish(acc):
            outs = (acc,) if epilogue is None else epilogue(acc, *[r[...] for r in extra_refs])
            for r, o in zip(out_refs, outs):
                r[...] = o.astype(r.dtype)

        if nk == 1:
            finish(p)
            return
        k = pl.program_id(2)
        acc_ref = out_refs[0] if direct else refs[-1]

        @pl.when(k == 0)
        def _():
            acc_ref[...] = p

        @pl.when(k > 0)
        def _():
            acc_ref[...] += p

        if use_acc:
            @pl.when(k == nk - 1)
            def _():
                finish(acc_ref[...])

    a_spec = pl.BlockSpec((bk, bm), lambda i, j, k: (k, i)) if ta else pl.BlockSpec((bm, bk), lambda i, j, k: (i, k))
    if tb:
        assert b_col_off == 0
        b_spec = pl.BlockSpec((bn, bk), lambda i, j, k: (j, k))
    else:
        b_spec = pl.BlockSpec((bk, bn), lambda i, j, k: (k, j + b_col_off))
    tile = pl.BlockSpec((bm, bn), lambda i, j, k: (i, j))
    outs = pl.pallas_call(
        body, name=name, grid=(M // bm, n // bn, nk),
        in_specs=[a_spec, b_spec] + [tile] * len(extras),
        out_specs=[tile] * n_out,
        out_shape=[jax.ShapeDtypeStruct((M, n), d) for d in out_dtypes],
        scratch_shapes=[pltpu.VMEM((bm, bn), F32)] if use_acc else [],
        compiler_params=_params(3),
    )(a, b, *extras)
    return outs[0] if n_out == 1 else outs


ROW_TILE = 512


def _rows(S):
    tr = min(ROW_TILE, S)
    assert S % tr == 0
    return tr


def _rstd(x):
    return lax.rsqrt(jnp.mean(x * x, axis=-1, keepdims=True) + RMS_EPS)


def _rms_bwd_math(x, g, dy):
    r = _rstd(x)
    gd = g * dy
    dx = r * gd - x * (r * r * r) * jnp.mean(x * gd, axis=-1, keepdims=True)
    dg = jnp.sum(dy * x * r, axis=0, keepdims=True)
    return dx, dg


def _accumulate(ref, value):
    i = pl.program_id(0)

    @pl.when(i == 0)
    def _():
        ref[...] = value

    @pl.when(i > 0)
    def _():
        ref[...] += value


def _row_spec(tr, width):
    return pl.BlockSpec((tr, width), lambda i: (i, 0))


def _vec_spec(width):
    return pl.BlockSpec((1, width), lambda i: (0, 0))


def _rms_fwd(x, g, *, name):
    S, D = x.shape
    tr = _rows(S)

    def body(x_ref, g_ref, o_ref):
        xf = x_ref[...]
        o_ref[...] = (xf * _rstd(xf) * g_ref[...]).astype(o_ref.dtype)

    return pl.pallas_call(
        body, name=name, grid=(S // tr,), in_specs=[_row_spec(tr, D), _vec_spec(D)], out_specs=_row_spec(tr, D),
        out_shape=jax.ShapeDtypeStruct((S, D), BF16), compiler_params=_params(1))(x, g)


def _resid_rms(x, y, g_post, g_next, *, name):
    S, D = x.shape
    tr = _rows(S)
    with_next = g_next is not None

    def body(*refs):
        if with_next:
            x_ref, y_ref, gp_ref, gn_ref, xo_ref, ho_ref = refs
        else:
            x_ref, y_ref, gp_ref, xo_ref = refs
        yf = y_ref[...]
        xn = x_ref[...] + yf * _rstd(yf) * gp_ref[...]
        xo_ref[...] = xn
        if with_next:
            ho_ref[...] = (xn * _rstd(xn) * gn_ref[...]).astype(ho_ref.dtype)

    row, vec = _row_spec(tr, D), _vec_spec(D)
    ins = [x, y, g_post] + ([g_next] if with_next else [])
    outs = pl.pallas_call(
        body, name=name, grid=(S // tr,), in_specs=[row, row, vec] + ([vec] if with_next else []),
        out_specs=[row] + ([row] if with_next else []),
        out_shape=[jax.ShapeDtypeStruct((S, D), F32)] + ([jax.ShapeDtypeStruct((S, D), BF16)] if with_next else []),
        compiler_params=_params(1))(*ins)
    return (outs[0], outs[1]) if with_next else (outs[0], None)


def _rms_bwd(x, g, dy, *, name):
    S, D = x.shape
    tr = _rows(S)

    def body(x_ref, g_ref, dy_ref, dx_ref, dg_ref):
        dx, dg = _rms_bwd_math(x_ref[...], g_ref[...], dy_ref[...])
        dx_ref[...] = dx.astype(dx_ref.dtype)
        _accumulate(dg_ref, dg)

    row, vec = _row_spec(tr, D), _vec_spec(D)
    return pl.pallas_call(
        body, name=name, grid=(S // tr,), in_specs=[row, vec, row], out_specs=[row, vec],
        out_shape=[jax.ShapeDtypeStruct((S, D), BF16), jax.ShapeDtypeStruct((1, D), F32)],
        compiler_params=_params(1))(x, g, dy)


def _rms_bwd_chain(xa, ga, da, resid, xb, gb, *, name):
    S, D = xa.shape
    tr = _rows(S)
    chain = xb is not None

    def body(*refs):
        if chain:
            xa_ref, ga_ref, da_ref, rs_ref, xb_ref, gb_ref, dx_ref, dga_ref, dxb_ref, dgb_ref = refs
        else:
            xa_ref, ga_ref, da_ref, rs_ref, dx_ref, dga_ref = refs
        dxa, dga = _rms_bwd_math(xa_ref[...], ga_ref[...], da_ref[...])
        dx = rs_ref[...] + dxa
        dx_ref[...] = dx
        _accumulate(dga_ref, dga)
        if chain:
            dxb, dgb = _rms_bwd_math(xb_ref[...], gb_ref[...], dx)
            dxb_ref[...] = dxb.astype(dxb_ref.dtype)
            _accumulate(dgb_ref, dgb)

    row, vec = _row_spec(tr, D), _vec_spec(D)
    ins = [xa, ga, da, resid] + ([xb, gb] if chain else [])
    outs = pl.pallas_call(
        body, name=name, grid=(S // tr,), in_specs=[row, vec, row, row] + ([row, vec] if chain else []),
        out_specs=[row, vec] + ([row, vec] if chain else []),
        out_shape=[jax.ShapeDtypeStruct((S, D), F32), jax.ShapeDtypeStruct((1, D), F32)]
        + ([jax.ShapeDtypeStruct((S, D), BF16), jax.ShapeDtypeStruct((1, D), F32)] if chain else []),
        compiler_params=_params(1))(*ins)
    return tuple(outs) if chain else (outs[0], outs[1], None, None)


def _loss_head(y, target, *, name):
    S, D = y.shape
    tr = _rows(S)
    n_tiles = S // tr

    def body(y_ref, t_ref, dy_ref, loss_ref, acc_ref):
        err = y_ref[...] - t_ref[...]
        dy_ref[...] = err * (1.0 / D)
        _accumulate(acc_ref, jnp.sum(err * err, axis=0, keepdims=True))

        @pl.when(pl.program_id(0) == n_tiles - 1)
        def _():
            loss_ref[...] = jnp.sum(acc_ref[...], axis=1, keepdims=True) * (0.5 / D)

    row = _row_spec(tr, D)
    return pl.pallas_call(
        body, name=name, grid=(n_tiles,), in_specs=[row, row],
        out_specs=[row, pl.BlockSpec((1, 1), lambda i: (0, 0))],
        out_shape=[jax.ShapeDtypeStruct((S, D), F32), jax.ShapeDtypeStruct((1, 1), F32)],
        scratch_shapes=[pltpu.VMEM((1, D), F32)], compiler_params=_params(1))(y, target)


SB_TILE = 256
SB_PAIRS = SB_HEADS * SB_HEAD_DIM // LANES
SB_DEAD_LOG = -110.0


def _log_sigmoids(z):
    l1p = jnp.log(1.0 + jnp.exp(-jnp.abs(z)))
    return jnp.minimum(z, 0.0) - l1p, jnp.minimum(-z, 0.0) - l1p


def _split_bf16(x):
    hi = x.astype(BF16)
    lo = (x - hi.astype(F32)).astype(BF16)
    return jnp.concatenate([hi, lo], axis=1)


def _tri(T, cmp):
    j = lax.broadcasted_iota(jnp.int32, (T, T), 0)
    s = lax.broadcasted_iota(jnp.int32, (T, T), 1)
    m = jnp.where(cmp(j, s), 1.0, 0.0).astype(BF16)
    return jnp.concatenate([m, m], axis=0)


def _head_masks():
    lane = lax.broadcasted_iota(jnp.int32, (1, LANES), 1)
    return [lane < SB_HEAD_DIM, lane >= SB_HEAD_DIM]


def _sba_fwd(qkv, *, name):
    S = qkv.shape[0]
    T = min(SB_TILE, S)
    nq = S // T
    scale = SB_HEAD_DIM ** -0.5

    def body(q_ref, k_ref, v_ref, o_ref, t_ref, first_ref):
        i = pl.program_id(1)
        row = lax.broadcasted_iota(jnp.int32, (T, T), 0)
        col = lax.broadcasted_iota(jnp.int32, (T, T), 1)
        strict = col < row
        after = _tri(T, lambda j, s: j > s)
        masks = _head_masks()
        q = q_ref[...] * scale
        qs = [jnp.where(hm, q, jnp.zeros_like(q)) for hm in masks]

        def block(j, carry, diag):
            rows = pl.ds(pl.multiple_of(j * T, T), T)
            kb, vb = k_ref[rows, :], v_ref[rows, :]
            out = []
            for qh, (C, acc) in zip(qs, carry):
                z = lax.dot_general(qh, kb, (((1,), (1,)), ((), ())), preferred_element_type=F32)
                ls, ln = _log_sigmoids(z)
                if diag:
                    ln = jnp.where(strict, ln, 0.0)
                suffix = jnp.dot(_split_bf16(ln), after, preferred_element_type=F32)
                a = jnp.exp(ls + suffix + C)
                if diag:
                    a = jnp.where(strict, a, 0.0)
                acc = acc + jnp.dot(a.astype(BF16), vb, preferred_element_type=F32)
                out.append((C + jnp.sum(ln, axis=1, keepdims=True), acc))
            return tuple(out)

        fresh = (jnp.zeros((T, 1), F32), jnp.zeros((T, LANES), F32))
        carry = block(i, (fresh, fresh), True)

        def alive(state):
            j, ((C0, _), (C1, _)) = state
            return jnp.logical_and(j >= 0, jnp.max(jnp.maximum(C0, C1)) > SB_DEAD_LOG)

        def step(state):
            j, carry = state
            return j - 1, block(j, carry, False)

        j, ((C0, acc0), (C1, acc1)) = lax.while_loop(alive, step, (i - 1, carry))
        t_ref[0] = jnp.broadcast_to(C0, (T, LANES))
        t_ref[1] = jnp.broadcast_to(C1, (T, LANES))
        first_ref[...] = jnp.full((8, LANES), (j + 1).astype(F32))
        o_ref[...] = jnp.where(masks[0], acc0, acc1).astype(o_ref.dtype)

    kv = lambda off: pl.BlockSpec((S, LANES), lambda p, i: (0, off + p))
    return pl.pallas_call(
        body, name=name, grid=(SB_PAIRS, nq),
        in_specs=[pl.BlockSpec((T, LANES), lambda p, i: (i, p)), kv(SB_PAIRS), kv(2 * SB_PAIRS)],
        out_specs=[pl.BlockSpec((T, LANES), lambda p, i: (i, p)), pl.BlockSpec((2, T, LANES), lambda p, i: (p, i, 0)),
                   pl.BlockSpec((None, None, 8, LANES), lambda p, i: (p, i, 0, 0))],
        out_shape=[jax.ShapeDtypeStruct((S, SB_WIDTH), BF16), jax.ShapeDtypeStruct((SB_HEADS, S, LANES), F32),
                   jax.ShapeDtypeStruct((SB_PAIRS, nq, 8, LANES), F32)],
        compiler_params=_params(2))(qkv, qkv, qkv)


def _sba_bwd(qkv, do, tot, first, *, name):
    S = qkv.shape[0]
    T = min(SB_TILE, S)
    nq = S // T
    scale = SB_HEAD_DIM ** -0.5

    def body(q_ref, k_ref, v_ref, do_ref, t_ref, first_ref, dq_ref, dk_ref, dv_ref, dk_acc, dv_acc):
        i = pl.program_id(1)

        @pl.when(i == 0)
        def _():
            dk_acc[...] = jnp.zeros_like(dk_acc)
            dv_acc[...] = jnp.zeros_like(dv_acc)

        row = lax.broadcasted_iota(jnp.int32, (T, T), 0)
        col = lax.broadcasted_iota(jnp.int32, (T, T), 1)
        strict = col < row
        upto = _tri(T, lambda j, s: j <= s)
        before = _tri(T, lambda j, s: j < s)
        masks = _head_masks()
        q, do_t = q_ref[...], do_ref[...]
        q = q * scale
        qs = [jnp.where(hm, q, jnp.zeros_like(q)) for hm in masks]
        dos = [jnp.where(hm, do_t, jnp.zeros_like(do_t)) for hm in masks]
        totals = [t_ref[h][:, 0:1] for h in range(2)]
        over_lanes = (((1,), (1,)), ((), ()))
        over_queries = (((0,), (0,)), ((), ()))

        def block(j, carry, diag):
            rows = pl.ds(pl.multiple_of(j * T, T), T)
            kb, vb = k_ref[rows, :], v_ref[rows, :]
            out, dk, dv = [], None, None
            for qh, doh, total, (P, G, dq) in zip(qs, dos, totals, carry):
                z = lax.dot_general(qh, kb, over_lanes, preferred_element_type=F32)
                ls, ln = _log_sigmoids(z)
                if diag:
                    ln = jnp.where(strict, ln, 0.0)
                beta = jnp.exp(ls)
                suffix = (total - P) - jnp.dot(_split_bf16(ln), upto, preferred_element_type=F32)
                a = jnp.exp(ls + suffix)
                if diag:
                    a = jnp.where(strict, a, 0.0)
                g = a * lax.dot_general(doh, vb, over_lanes, preferred_element_type=F32)
                g_before = G + jnp.dot(_split_bf16(g), before, preferred_element_type=F32)
                dz = g * (1.0 - beta) - g_before * beta
                if diag:
                    dz = jnp.where(strict, dz, 0.0)
                dzb, ab = dz.astype(BF16), a.astype(BF16)
                dq = dq + jnp.dot(dzb, kb, preferred_element_type=F32)
                dk_h = lax.dot_general(dzb, qh, over_queries, preferred_element_type=F32)
                dv_h = lax.dot_general(ab, doh, over_queries, preferred_element_type=F32)
                dk = dk_h if dk is None else dk + dk_h
                dv = dv_h if dv is None else dv + dv_h
                out.append((P + jnp.sum(ln, axis=1, keepdims=True), G + jnp.sum(g, axis=1, keepdims=True), dq))
            dk_acc[rows, :] += dk
            dv_acc[rows, :] += dv
            return tuple(out)

        zero = jnp.zeros((T, 1), F32)
        fresh = (zero, zero, jnp.zeros((T, LANES), F32))
        j0 = jnp.clip(jnp.max(first_ref[...]).astype(jnp.int32), 0, i)
        carry = lax.fori_loop(j0, i, lambda j, c: block(j, c, False), (fresh, fresh))
        (_, _, dq0), (_, _, dq1) = block(i, carry, True)
        dq_ref[...] = (jnp.where(masks[0], dq0, dq1) * scale).astype(dq_ref.dtype)

        @pl.when(i == nq - 1)
        def _():
            dk_ref[...] = dk_acc[...].astype(dk_ref.dtype)
            dv_ref[...] = dv_acc[...].astype(dv_ref.dtype)

    kv = lambda off: pl.BlockSpec((S, LANES), lambda p, i: (0, off + p))
    tile = lambda off: pl.BlockSpec((T, LANES), lambda p, i: (i, off + p))
    return pl.pallas_call(
        body, name=name, grid=(SB_PAIRS, nq),
        in_specs=[tile(0), kv(SB_PAIRS), kv(2 * SB_PAIRS), tile(0), pl.BlockSpec((2, T, LANES), lambda p, i: (p, i, 0)),
                  pl.BlockSpec((None, None, 8, LANES), lambda p, i: (p, i, 0, 0))],
        out_specs=[tile(0), kv(0), kv(0)],
        out_shape=[jax.ShapeDtypeStruct((S, SB_WIDTH), BF16)] * 3,
        scratch_shapes=[pltpu.VMEM((S, LANES), F32), pltpu.VMEM((S, LANES), F32)],
        compiler_params=_params(2))(qkv, qkv, qkv, do, tot, first)


POOL_TILE = 512


def _by_group(lane, values):
    return jnp.where(lane < 64, values[0], jnp.where(lane < 128, values[1], jnp.where(lane < 192, values[2], values[3])))


def _pool_inv_count(first_row, n_rows):
    t = first_row + lax.broadcasted_iota(jnp.int32, (n_rows, POOL_WIDTH), 0)
    lane = lax.broadcasted_iota(jnp.int32, (n_rows, POOL_WIDTH), 1)
    window = _by_group(lane, POOL_WINDOWS)
    return 1.0 / jnp.clip(t + 1, 1, window).astype(F32), lane


def _pooled(ext, first_row, R):
    n = R + POOL_HALO
    s2 = ext + pltpu.roll(ext, 1, 0)
    s4 = s2 + pltpu.roll(s2, 2, 0)
    s8 = s4 + pltpu.roll(s4, 4, 0)
    s16 = s8 + pltpu.roll(s8, 8, 0)
    inv, lane = _pool_inv_count(first_row - POOL_HALO, n)
    pooled = _by_group(lane, (s2, s4, s8, s16)) * inv - ext
    return pooled[POOL_HALO:, :]


def _pool_specs(S, R, col):
    per = R // POOL_HALO
    tile = pl.BlockSpec((R, POOL_WIDTH), lambda i: (i, col))
    prev = pl.BlockSpec((POOL_HALO, POOL_WIDTH), lambda i: (jnp.maximum(i * per - 1, 0), col))
    return tile, prev


def _pool_fwd(rest, w_bd, scale, *, name):
    S = rest.shape[0]
    R = min(POOL_TILE, S)

    def body(p_ref, prev_ref, w_ref, s_ref, o_ref, ext_ref):
        i = pl.program_id(0)
        ext_ref[:POOL_HALO, :] = jnp.where(i > 0, prev_ref[...], 0.0)
        ext_ref[POOL_HALO:, :] = p_ref[...]
        pooled = _pooled(ext_ref[...], i * R, R)
        mixed = jnp.dot(pooled.astype(BF16), w_ref[...], preferred_element_type=F32)
        o_ref[...] = (mixed * s_ref[...]).astype(o_ref.dtype)

    tile, prev = _pool_specs(S, R, 0)
    return pl.pallas_call(
        body, name=name, grid=(S // R,),
        in_specs=[tile, prev, pl.BlockSpec((POOL_WIDTH, POOL_WIDTH), lambda i: (0, 0)), _vec_spec(POOL_WIDTH)],
        out_specs=_row_spec(R, POOL_WIDTH), out_shape=jax.ShapeDtypeStruct((S, POOL_WIDTH), BF16),
        scratch_shapes=[pltpu.VMEM((R + POOL_HALO, POOL_WIDTH), F32)], compiler_params=_params(1))(rest, rest, w_bd, scale)


def _pool_bwd(rest, do, w_bd, scale, *, name):
    S = rest.shape[0]
    R = min(POOL_TILE, S)
    n_tiles = S // R
    per = R // POOL_HALO
    n = R + POOL_HALO

    def body(p_ref, prev_ref, do_ref, nxt_ref, w_ref, s_ref, dp_ref, dw_ref, ds_ref, ext_ref, dext_ref):
        i = pl.program_id(0)
        ext_ref[:POOL_HALO, :] = jnp.where(i > 0, prev_ref[...], 0.0)
        ext_ref[POOL_HALO:, :] = p_ref[...]
        pooled = _pooled(ext_ref[...], i * R, R).astype(BF16)
        w = w_ref[...]
        mixed = jnp.dot(pooled, w, preferred_element_type=F32)
        do_t = do_ref[...]
        _accumulate(ds_ref, jnp.sum(do_t * mixed, axis=0, keepdims=True))
        dext_ref[:R, :] = do_t
        dext_ref[R:, :] = jnp.where(i < n_tiles - 1, nxt_ref[...], 0.0)
        dmixed = (dext_ref[...] * s_ref[...]).astype(BF16)
        dpooled = lax.dot_general(dmixed, w, (((1,), (1,)), ((), ())), preferred_element_type=F32)
        _accumulate(dw_ref, lax.dot_general(pooled, dmixed[:R, :], (((0,), (0,)), ((), ())), preferred_element_type=F32))
        inv, lane = _pool_inv_count(i * R, n)
        u = dpooled * inv
        f2 = u + pltpu.roll(u, n - 1, 0)
        f4 = f2 + pltpu.roll(f2, n - 2, 0)
        f8 = f4 + pltpu.roll(f4, n - 4, 0)
        f16 = f8 + pltpu.roll(f8, n - 8, 0)
        dp = _by_group(lane, (f2, f4, f8, f16)) - dpooled
        dp_ref[...] = dp[:R, :].astype(dp_ref.dtype)

    tile, prev = _pool_specs(S, R, 0)
    nxt = pl.BlockSpec((POOL_HALO, POOL_WIDTH), lambda i: (jnp.minimum((i + 1) * per, S // POOL_HALO - 1), 0))
    full = pl.BlockSpec((POOL_WIDTH, POOL_WIDTH), lambda i: (0, 0))
    return pl.pallas_call(
        body, name=name, grid=(n_tiles,),
        in_specs=[tile, prev, _row_spec(R, POOL_WIDTH), nxt, full, _vec_spec(POOL_WIDTH)],
        out_specs=[_row_spec(R, POOL_WIDTH), full, _vec_spec(POOL_WIDTH)],
        out_shape=[jax.ShapeDtypeStruct((S, POOL_WIDTH), BF16), jax.ShapeDtypeStruct((POOL_WIDTH, POOL_WIDTH), F32),
                   jax.ShapeDtypeStruct((1, POOL_WIDTH), F32)],
        scratch_shapes=[pltpu.VMEM((n, POOL_WIDTH), F32), pltpu.VMEM((n, POOL_WIDTH), F32)],
        compiler_params=_params(1))(rest, rest, do, do, w_bd, scale)


GM_TILE = 512
GELU_C = math.sqrt(2.0 / math.pi)
GELU_A = 0.044715


def _gelu(x):
    return 0.5 * x * (1.0 + jnp.tanh(GELU_C * (x + GELU_A * x * x * x)))


def _gelu_and_grad(x):
    t = jnp.tanh(GELU_C * (x + GELU_A * x * x * x))
    y = 0.5 * x * (1.0 + t)
    dy = 0.5 * (1.0 + t) + 0.5 * x * (1.0 - t * t) * (GELU_C * (1.0 + 3.0 * GELU_A * x * x))
    return y, dy


def _group_lane_masks():
    lane = lax.broadcasted_iota(jnp.int32, (1, GM_WIDTH), 1)
    return [(lane >= g * GM_GROUP_DIM) & (lane < (g + 1) * GM_GROUP_DIM) for g in range(GM_GROUPS)]


def _stack_groups(x, masks):
    return jnp.concatenate([jnp.where(m, x, jnp.zeros_like(x)) for m in masks], axis=0)


def _gm_mixed(vn, ws_cat, bias, masks, R):
    chunks = []
    for c in range(R // GM_CHUNK):
        vc = vn[c * GM_CHUNK:(c + 1) * GM_CHUNK, :]
        chunks.append(jnp.dot(ws_cat, _stack_groups(vc, masks), preferred_element_type=F32) + bias)
    return jnp.concatenate(chunks, axis=0)


def _gm_specs(S, R):
    u = pl.BlockSpec((R, GM_WIDTH), lambda i: (i, 1))
    v = pl.BlockSpec((R, GM_WIDTH), lambda i: (i, 2))
    ws = pl.BlockSpec((GM_CHUNK, GM_GROUPS * GM_CHUNK), lambda i: (0, 0))
    bias = pl.BlockSpec((GM_CHUNK, GM_WIDTH), lambda i: (0, 0))
    return u, v, ws, bias


def _gm_fwd(rest, gain, ws_cat, bias, *, name):
    S = rest.shape[0]
    R = min(GM_TILE, S)

    def body(u_ref, v_ref, g_ref, ws_ref, b_ref, o_ref):
        gv = _gelu(v_ref[...])
        vn = (gv * _rstd(gv) * g_ref[...]).astype(BF16)
        mixed = _gm_mixed(vn, ws_ref[...], b_ref[...], _group_lane_masks(), R)
        o_ref[...] = (_gelu(u_ref[...]) * mixed).astype(o_ref.dtype)

    u_spec, v_spec, ws_spec, bias_spec = _gm_specs(S, R)
    return pl.pallas_call(
        body, name=name, grid=(S // R,), in_specs=[u_spec, v_spec, _vec_spec(GM_WIDTH), ws_spec, bias_spec],
        out_specs=_row_spec(R, GM_WIDTH), out_shape=jax.ShapeDtypeStruct((S, GM_WIDTH), BF16),
        compiler_params=_params(1))(rest, rest, gain, ws_cat, bias)


def _gm_bwd(rest, do, gain, ws_cat, wst_cat, bias, *, name):
    S = rest.shape[0]
    R = min(GM_TILE, S)

    def body(u_ref, v_ref, do_ref, g_ref, ws_ref, wst_ref, b_ref, du_ref, dv_ref, dg_ref, dws_ref, db_ref):
        masks = _group_lane_masks()
        gain_v = g_ref[...]
        gu, dgu = _gelu_and_grad(u_ref[...])
        gv, dgv = _gelu_and_grad(v_ref[...])
        r = _rstd(gv)
        vn = (gv * r * gain_v).astype(BF16)
        mixed = _gm_mixed(vn, ws_ref[...], b_ref[...], masks, R)
        do_t = do_ref[...]
        du_ref[...] = (do_t * mixed * dgu).astype(du_ref.dtype)
        dmix = do_t * gu
        dmix_b = dmix.astype(BF16)
        wst = wst_ref[...]
        dvn_chunks, db, dws = [], None, [None] * GM_GROUPS
        for c in range(R // GM_CHUNK):
            rows = slice(c * GM_CHUNK, (c + 1) * GM_CHUNK)
            dc, dcb, vc = dmix[rows, :], dmix_b[rows, :], vn[rows, :]
            db = dc if db is None else db + dc
            dvn_chunks.append(jnp.dot(wst, _stack_groups(dcb, masks), preferred_element_type=F32))
            for g, m in enumerate(masks):
                part = lax.dot_general(jnp.where(m, dcb, jnp.zeros_like(dcb)), vc, (((1,), (1,)), ((), ())),
                                       preferred_element_type=F32)
                dws[g] = part if dws[g] is None else dws[g] + part
        dvn = jnp.concatenate(dvn_chunks, axis=0)
        lane = lax.broadcasted_iota(jnp.int32, (1, LANES), 1)
        db_groups = jnp.zeros((GM_CHUNK, LANES), F32)
        for g, m in enumerate(masks):
            total = jnp.sum(jnp.where(m, db, 0.0), axis=1, keepdims=True)
            db_groups = db_groups + jnp.where(lane == g, total, 0.0)
        _accumulate(db_ref, db_groups)
        i = pl.program_id(0)
        for g in range(GM_GROUPS):
            @pl.when(i == 0)
            def _(g=g):
                dws_ref[g] = dws[g]

            @pl.when(i > 0)
            def _(g=g):
                dws_ref[g] += dws[g]
        _accumulate(dg_ref, jnp.sum(dvn * gv * r, axis=0, keepdims=True))
        gd = gain_v * dvn
        dgv_in = r * gd - gv * (r * r * r) * jnp.mean(gv * gd, axis=-1, keepdims=True)
        dv_ref[...] = (dgv_in * dgv).astype(dv_ref.dtype)

    u_spec, v_spec, ws_spec, bias_spec = _gm_specs(S, R)
    row, vec = _row_spec(R, GM_WIDTH), _vec_spec(GM_WIDTH)
    dws_spec = pl.BlockSpec((GM_GROUPS, GM_CHUNK, GM_CHUNK), lambda i: (0, 0, 0))
    return pl.pallas_call(
        body, name=name, grid=(S // R,), in_specs=[u_spec, v_spec, row, vec, ws_spec, ws_spec, bias_spec],
        out_specs=[row, row, vec, dws_spec, pl.BlockSpec((GM_CHUNK, LANES), lambda i: (0, 0))],
        out_shape=[jax.ShapeDtypeStruct((S, GM_WIDTH), BF16)] * 2
        + [jax.ShapeDtypeStruct((1, GM_WIDTH), F32), jax.ShapeDtypeStruct((GM_GROUPS, GM_CHUNK, GM_CHUNK), F32),
           jax.ShapeDtypeStruct((GM_CHUNK, LANES), F32)],
        compiler_params=_params(1))(rest, rest, do, gain, ws_cat, wst_cat, bias)


GATE_ROWS = 1024
GATE_COLS = 256
GATE_BLOCKS = D_MODEL // GATE_COLS


def _gate_spec(tr, k):
    return pl.BlockSpec((tr, GATE_COLS), lambda i, j: (i, GATE_COL // GATE_COLS + GATE_BLOCKS * k + j))


def _merge_fwd(rest, branches, *, name):
    S = rest.shape[0]
    tr = min(GATE_ROWS, S)

    def body(g0, g1, g2, b0, b1, b2, o_ref):
        acc = None
        for g_ref, b_ref in ((g0, b0), (g1, b1), (g2, b2)):
            term = jax.nn.sigmoid(g_ref[...]) * b_ref[...]
            acc = term if acc is None else acc + term
        o_ref[...] = acc.astype(o_ref.dtype)

    tile = pl.BlockSpec((tr, GATE_COLS), lambda i, j: (i, j))
    return pl.pallas_call(
        body, name=name, grid=(S // tr, GATE_BLOCKS),
        in_specs=[_gate_spec(tr, k) for k in range(N_BRANCH)] + [tile] * N_BRANCH, out_specs=tile,
        out_shape=jax.ShapeDtypeStruct((S, D_MODEL), BF16), compiler_params=_params(2))(rest, rest, rest, *branches)


def _merge_bwd(rest, branches, dmerged, *, name):
    S = rest.shape[0]
    tr = min(GATE_ROWS, S)

    def body(g0, g1, g2, b0, b1, b2, dm_ref, dg0, dg1, dg2, db0, db1, db2):
        dm = dm_ref[...]
        for g_ref, b_ref, dg_ref, db_ref in ((g0, b0, dg0, db0), (g1, b1, dg1, db1), (g2, b2, dg2, db2)):
            s = jax.nn.sigmoid(g_ref[...])
            db_ref[...] = (dm * s).astype(db_ref.dtype)
            dg_ref[...] = (dm * b_ref[...] * s * (1.0 - s)).astype(dg_ref.dtype)

    tile = pl.BlockSpec((tr, GATE_COLS), lambda i, j: (i, j))
    return pl.pallas_call(
        body, name=name, grid=(S // tr, GATE_BLOCKS),
        in_specs=[_gate_spec(tr, k) for k in range(N_BRANCH)] + [tile] * (N_BRANCH + 1), out_specs=[tile] * (2 * N_BRANCH),
        out_shape=[jax.ShapeDtypeStruct((S, D_MODEL), BF16)] * (2 * N_BRANCH),
        compiler_params=_params(2))(rest, rest, rest, *branches, dmerged)


TILE_BYTES = 24 * 1024 * 1024


BF16_ROWS = 16


def _tile_rows(rows, cols, n_arrays):
    padded = -(-cols // LANES) * LANES
    cap = max(BF16_ROWS, TILE_BYTES // (2 * n_arrays * padded * 4))
    best = None
    for tr in range(BF16_ROWS, min(rows, cap) + 1, BF16_ROWS):
        if rows % tr == 0:
            best = tr
    assert best is not None, (rows, cols)
    return best


def _sum_slots(stack, *, name):
    n, R, C = stack.shape
    tr = _tile_rows(R, C, n + 1)

    def body(s_ref, o_ref):
        acc = s_ref[0].astype(F32)
        for k in range(1, n):
            acc = acc + s_ref[k].astype(F32)
        o_ref[...] = acc

    return pl.pallas_call(
        body, name=name, grid=(R // tr,), in_specs=[pl.BlockSpec((n, tr, C), lambda i: (0, i, 0))],
        out_specs=_row_spec(tr, C), out_shape=jax.ShapeDtypeStruct((R, C), F32), compiler_params=_params(1))(stack)


def _add_own_half(parts, received, core, *, name):
    n, R, C = parts.shape
    half = R // 2
    tr = _tile_rows(half, C, 3)
    steps = half // tr

    def body(core_ref, own_ref, got_ref, o_ref):
        o_ref[...] = (own_ref[...] + got_ref[...]).astype(o_ref.dtype)

    tile = pl.BlockSpec((None, tr, C), lambda d, i, core_ref: (d, i, 0))
    own = pl.BlockSpec((None, tr, C), lambda d, i, core_ref: (d, core_ref[0] * steps + i, 0))
    return pl.pallas_call(
        body, name=name, out_shape=jax.ShapeDtypeStruct((n, half, C), BF16),
        grid_spec=pltpu.PrefetchScalarGridSpec(num_scalar_prefetch=1, grid=(n, steps), in_specs=[own, tile], out_specs=tile),
        compiler_params=_params(2))(core, parts, received)


def _adamw_math(w, m, v, g):
    m_new = ADAM_B1 * m + (1.0 - ADAM_B1) * g
    v_new = ADAM_B2 * v + (1.0 - ADAM_B2) * jnp.square(g)
    m_hat = m_new / (1.0 - ADAM_B1 ** ADAM_STEP)
    v_hat = v_new / (1.0 - ADAM_B2 ** ADAM_STEP)
    return -ADAM_LR * (m_hat / (jnp.sqrt(v_hat) + ADAM_EPS) + ADAM_WD * w), m_new, v_new


def _adamw_summed(w, m, v, gstack, *, name):
    R, C = w.shape
    n = gstack.shape[0]
    tr = _tile_rows(R, C, n + 7)

    def body(w_ref, m_ref, v_ref, g_ref, go_ref, d_ref, mo_ref, vo_ref):
        g = g_ref[0]
        for k in range(1, n):
            g = g + g_ref[k]
        go_ref[...] = g
        d_ref[...], mo_ref[...], vo_ref[...] = _adamw_math(w_ref[...], m_ref[...], v_ref[...], g)

    row = _row_spec(tr, C)
    return pl.pallas_call(
        body, name=name, grid=(R // tr,), in_specs=[row, row, row, pl.BlockSpec((n, tr, C), lambda i: (0, i, 0))],
        out_specs=[row] * 4, out_shape=[jax.ShapeDtypeStruct((R, C), F32)] * 4, compiler_params=_params(1))(w, m, v, gstack)


def _adamw_halves(w, m, v, mine, theirs, core, *, name):
    R, C = w.shape
    tr = _tile_rows(R // 2, C, 9)
    steps = R // 2 // tr

    def body(core_ref, w_ref, m_ref, v_ref, mine_ref, theirs_ref, go_ref, d_ref, mo_ref, vo_ref):
        in_my_half = pl.program_id(0) // steps == core_ref[0]
        g = jnp.where(in_my_half, mine_ref[...], theirs_ref[...])
        go_ref[...] = g
        d_ref[...], mo_ref[...], vo_ref[...] = _adamw_math(w_ref[...], m_ref[...], v_ref[...], g)

    row = pl.BlockSpec((tr, C), lambda i, core_ref: (i, 0))
    mine_spec = pl.BlockSpec((tr, C), lambda i, core_ref: (jnp.clip(i - core_ref[0] * steps, 0, steps - 1), 0))
    theirs_spec = pl.BlockSpec((tr, C), lambda i, core_ref: (jnp.clip(i - (1 - core_ref[0]) * steps, 0, steps - 1), 0))
    return pl.pallas_call(
        body, name=name, out_shape=[jax.ShapeDtypeStruct((R, C), F32)] * 4,
        grid_spec=pltpu.PrefetchScalarGridSpec(
            num_scalar_prefetch=1, grid=(R // tr,), in_specs=[row, row, row, mine_spec, theirs_spec], out_specs=[row] * 4),
        compiler_params=_params(1))(core, w, m, v, mine, theirs)


HBM_SPEC = pl.BlockSpec(memory_space=pl.ANY)


def _position():
    return lax.axis_index("x"), lax.axis_index("y"), lax.axis_index("c")


def _other_chips(x, y):
    return [(1 - x, y), (x, 1 - y), (1 - x, 1 - y)]


def _chip_exchange(arrays, *, scatter, name):
    n = len(arrays)

    def body(*refs):
        ins, outs = refs[:n], refs[n:2 * n]
        send_sems, recv_sems, local_sems = refs[2 * n:]
        x, y, c = _position()
        me = 2 * x + y
        copies = []
        for a in range(n):
            own = ins[a].at[me] if scatter else ins[a]
            local = pltpu.make_async_copy(own, outs[a].at[me], local_sems.at[a])
            local.start()
            copies.append(local)
            for k, (px, py) in enumerate(_other_chips(x, y)):
                src = ins[a].at[2 * px + py] if scatter else ins[a]
                remote = pltpu.make_async_remote_copy(
                    src_ref=src, dst_ref=outs[a].at[me], send_sem=send_sems.at[3 * a + k],
                    recv_sem=recv_sems.at[3 * a + k], device_id=(px, py, c), device_id_type=MESH)
                remote.start()
                copies.append(remote)
        for cp in copies:
            cp.wait()

    out_shapes = [jax.ShapeDtypeStruct(a.shape if scatter else (N_CHIPS,) + a.shape, a.dtype) for a in arrays]
    return pl.pallas_call(
        body, name=name, in_specs=[HBM_SPEC] * n, out_specs=[HBM_SPEC] * n, out_shape=out_shapes,
        scratch_shapes=[pltpu.SemaphoreType.DMA((3 * n,)), pltpu.SemaphoreType.DMA((3 * n,)), pltpu.SemaphoreType.DMA((n,))],
    )(*arrays)


def _sibling_swap(arrays, *, name):
    n = len(arrays)

    def body(*refs):
        ins, outs = refs[:n], refs[n:2 * n]
        send_sems, recv_sems = refs[2 * n:]
        x, y, c = _position()
        copies = []
        for a in range(n):
            cp = pltpu.make_async_remote_copy(
                src_ref=ins[a], dst_ref=outs[a], send_sem=send_sems.at[a], recv_sem=recv_sems.at[a],
                device_id=(x, y, 1 - c), device_id_type=MESH)
            cp.start()
            copies.append(cp)
        for cp in copies:
            cp.wait()

    return pl.pallas_call(
        body, name=name, in_specs=[HBM_SPEC] * n, out_specs=[HBM_SPEC] * n,
        out_shape=[jax.ShapeDtypeStruct(a.shape, a.dtype) for a in arrays],
        scratch_shapes=[pltpu.SemaphoreType.DMA((n,)), pltpu.SemaphoreType.DMA((n,))],
    )(*arrays)


def _sibling_other_half(arrays, *, name):
    n = len(arrays)

    def body(*refs):
        ins, outs = refs[:n], refs[n:2 * n]
        send_sems, recv_sems = refs[2 * n:]
        x, y, c = _position()
        copies = []
        for a in range(n):
            half = ins[a].shape[1] // 2
            theirs = ins[a].at[:, pl.ds(pl.multiple_of((1 - c) * half, BF16_ROWS), half), :]
            cp = pltpu.make_async_remote_copy(
                src_ref=theirs, dst_ref=outs[a], send_sem=send_sems.at[a], recv_sem=recv_sems.at[a],
                device_id=(x, y, 1 - c), device_id_type=MESH)
            cp.start()
            copies.append(cp)
        for cp in copies:
            cp.wait()

    return pl.pallas_call(
        body, name=name, in_specs=[HBM_SPEC] * n, out_specs=[HBM_SPEC] * n,
        out_shape=[jax.ShapeDtypeStruct((a.shape[0], a.shape[1] // 2, a.shape[2]), a.dtype) for a in arrays],
        scratch_shapes=[pltpu.SemaphoreType.DMA((n,)), pltpu.SemaphoreType.DMA((n,))],
    )(*arrays)


def _gather_all(buf, *, name):
    R, C = buf.shape

    def body(in_ref, out_ref, send_sems, recv_sems):
        x, y, c = _position()
        out_ref[4 * x + 2 * y + c] = in_ref[...]
        copies = []
        for k in range(1, N_DEV):
            px = 1 - x if k & 4 else x
            py = 1 - y if k & 2 else y
            pc = 1 - c if k & 1 else c
            cp = pltpu.make_async_remote_copy(
                src_ref=in_ref, dst_ref=out_ref.at[4 * x + 2 * y + c], send_sem=send_sems.at[k - 1],
                recv_sem=recv_sems.at[k - 1], device_id=(px, py, pc), device_id_type=MESH)
            cp.start()
            copies.append(cp)
        for cp in copies:
            cp.wait()

    vmem = pl.BlockSpec(memory_space=pltpu.VMEM)
    return pl.pallas_call(
        body, name=name, in_specs=[vmem], out_specs=vmem, out_shape=jax.ShapeDtypeStruct((N_DEV, R, C), buf.dtype),
        scratch_shapes=[pltpu.SemaphoreType.DMA((N_DEV - 1,)), pltpu.SemaphoreType.DMA((N_DEV - 1,))],
        compiler_params=pltpu.CompilerParams(vmem_limit_bytes=VMEM_LIMIT),
    )(buf)


def _relu2(p):
    return p, jnp.square(jnp.maximum(p, 0.0))


def _relu2_grad(p, a):
    return (p * (2.0 * jnp.maximum(a, 0.0)),)


def _mixer_constants(w_pool, w_spatial, b_spatial):
    eye = jnp.eye(len(POOL_WINDOWS), dtype=F32)
    w_bd = (eye[:, None, :, None] * w_pool[:, :, None, :]).reshape(POOL_WIDTH, POOL_WIDTH).astype(BF16)
    causal = jnp.tril(jnp.ones((GM_CHUNK, GM_CHUNK), dtype=bool))
    ws = jnp.where(causal[None], w_spatial, 0.0).astype(BF16)
    ws_cat = ws.transpose(1, 0, 2).reshape(GM_CHUNK, GM_GROUPS * GM_CHUNK)
    wst_cat = ws.transpose(2, 0, 1).reshape(GM_CHUNK, GM_GROUPS * GM_CHUNK)
    bias = jnp.repeat(b_spatial.T, GM_GROUP_DIM, axis=1)
    return w_bd, ws_cat, wst_cat, bias


def _local_step(x, target, big, small):
    L = small["g_mix_pre"].shape[0]
    vec = lambda name, l: small[name][l][None, :]
    consts = [_mixer_constants(small["w_pool"][l], small["w_spatial"][l], small["b_spatial"][l]) for l in range(L)]
    saved = []
    h = _rms_fwd(x, vec("g_mix_pre", 0), name="rms_in")
    for l in range(L):
        w_bd, ws_cat, wst_cat, bias = consts[l]
        qkv = _matmul(h, big["w_in"][l], n=QKV_WIDTH, bn=768, out_dtypes=(BF16,), name="proj_qkv")
        rest = _matmul(h, big["w_in"][l], n=REST_WIDTH, bn=768, b_col_off=QKV_WIDTH // 768, name="proj_rest")
        o_sb, tot, first = _sba_fwd(qkv, name="sba_fwd")
        o_pool = _pool_fwd(rest, w_bd, vec("pool_scale", l), name="pool_fwd")
        o_gm = _gm_fwd(rest, vec("gm_gain", l), ws_cat, bias, name="gm_fwd")
        branches = (_matmul(o_sb, big["w_br_sb"][l], name="br_sb"), _matmul(o_pool, big["w_br_pool"][l], name="br_pool"),
                    _matmul(o_gm, big["w_br_gm"][l], name="br_gm"))
        merged = _merge_fwd(rest, branches, name="merge_fwd")
        y = _matmul(merged, big["w_out"][l], name="out_proj")
        x1, h2 = _resid_rms(x, y, vec("g_mix_post", l), vec("g_ff_pre", l), name="resid_mix")
        a, r = _matmul(h2, big["w_ff_in"][l], out_dtypes=(F32, BF16), epilogue=_relu2, name="ff_in")
        ff = _matmul(r, big["w_ff_out"][l], name="ff_out")
        g_next = vec("g_mix_pre", l + 1) if l + 1 < L else None
        x2, h_next = _resid_rms(x1, ff, vec("g_ff_post", l), g_next, name="resid_ff" if l + 1 < L else "resid_last")
        saved.append(dict(x=x, h=h, qkv=qkv, rest=rest, o_sb=o_sb, tot=tot, first=first, o_pool=o_pool, o_gm=o_gm, branches=branches,
                          merged=merged, y=y, x1=x1, h2=h2, a=a, r=r, ff=ff))
        x, h = x2, h_next

    dx2, loss = _loss_head(x, target, name="loss_head")
    gb = {k: [None] * L for k in ("w_in", "w_br_sb", "w_br_pool", "w_br_gm", "w_out", "w_ff_in", "w_ff_out")}
    gs = {k: [None] * L for k in ("w_pool", "pool_scale", "gm_gain", "w_spatial", "b_spatial", "g_mix_pre",
                                  "g_mix_post", "g_ff_pre", "g_ff_post")}
    d_ff, gs["g_ff_post"][L - 1] = _rms_bwd(saved[-1]["ff"], vec("g_ff_post", L - 1), dx2, name="rms_bwd_last")
    for l in reversed(range(L)):
        s = saved[l]
        w_bd, ws_cat, wst_cat, bias = consts[l]
        da = _matmul(d_ff, big["w_ff_out"][l], tb=True, out_dtypes=(BF16,), extras=(s["a"],), epilogue=_relu2_grad,
                     name="ff_out_dx")
        gb["w_ff_out"][l] = _matmul(s["r"], d_ff, ta=True, name="ff_out_dw")
        dh2 = _matmul(da, big["w_ff_in"][l], tb=True, name="ff_in_dx")
        gb["w_ff_in"][l] = _matmul(s["h2"], da, ta=True, name="ff_in_dw")
        dx1, gs["g_ff_pre"][l], dy, gs["g_mix_post"][l] = _rms_bwd_chain(
            s["x1"], vec("g_ff_pre", l), dh2, dx2, s["y"], vec("g_mix_post", l), name="rms_bwd_mid")
        dmerged = _matmul(dy, big["w_out"][l], tb=True, name="out_proj_dx")
        gb["w_out"][l] = _matmul(s["merged"], dy, ta=True, name="out_proj_dw")
        dg0, dg1, dg2, db_sb, db_pool, db_gm = _merge_bwd(s["rest"], s["branches"], dmerged, name="merge_bwd")
        do_sb = _matmul(db_sb, big["w_br_sb"][l], tb=True, out_dtypes=(BF16,), name="br_sb_dx")
        gb["w_br_sb"][l] = _matmul(s["o_sb"], db_sb, ta=True, name="br_sb_dw")
        do_pool = _matmul(db_pool, big["w_br_pool"][l], tb=True, name="br_pool_dx")
        gb["w_br_pool"][l] = _matmul(s["o_pool"], db_pool, ta=True, name="br_pool_dw")
        do_gm = _matmul(db_gm, big["w_br_gm"][l], tb=True, name="br_gm_dx")
        gb["w_br_gm"][l] = _matmul(s["o_gm"], db_gm, ta=True, name="br_gm_dw")
        dq, dk, dv = _sba_bwd(s["qkv"], do_sb, s["tot"], s["first"], name="sba_bwd")
        dp, dw_bd, gs["pool_scale"][l] = _pool_bwd(s["rest"], do_pool, w_bd, vec("pool_scale", l), name="pool_bwd")
        du, dgv, gs["gm_gain"][l], dws, db = _gm_bwd(s["rest"], do_gm, vec("gm_gain", l), ws_cat, wst_cat, bias,
                                                      name="gm_bwd")
        gs["w_pool"][l] = jnp.stack([dw_bd[g * 64:(g + 1) * 64, g * 64:(g + 1) * 64] for g in range(len(POOL_WINDOWS))])
        gs["w_spatial"][l] = jnp.where(jnp.tril(jnp.ones((GM_CHUNK, GM_CHUNK), dtype=bool))[None], dws, 0.0)
        gs["b_spatial"][l] = db[:, :GM_GROUPS].T
        dproj = jnp.concatenate([dq, dk, dv, dp, du, dgv, dg0, dg1, dg2], axis=1)
        dh = _matmul(dproj, big["w_in"][l], tb=True, bk=D_IN // 3, name="proj_dx")
        gb["w_in"][l] = _matmul(s["h"], dproj, ta=True, bn=768, name="proj_dw")
        if l > 0:
            dx2, gs["g_mix_pre"][l], d_ff, gs["g_ff_post"][l - 1] = _rms_bwd_chain(
                s["x"], vec("g_mix_pre", l), dh, dx1, saved[l - 1]["ff"], vec("g_ff_post", l - 1), name="rms_bwd_mid")
        else:
            dx2, gs["g_mix_pre"][l], _, _ = _rms_bwd_chain(s["x"], vec("g_mix_pre", l), dh, dx1, None, None,
                                                           name="rms_bwd_first")
    small_grads = {k: jnp.stack([g.reshape(small[k].shape[1:]) for g in v]) for k, v in gs.items()}
    return loss, dx2, gb, small_grads


COLUMN_SHARDED = ("w_in", "w_br_sb", "w_br_pool", "w_br_gm", "w_ff_in")
ROW_SHARDED = ("w_out", "w_ff_out")
BIG_WEIGHTS = COLUMN_SHARDED + ROW_SHARDED
SMALL_WEIGHTS = ("w_pool", "pool_scale", "gm_gain", "w_spatial", "b_spatial", "g_mix_pre", "g_mix_post", "g_ff_pre",
                 "g_ff_post")
WEIGHT_ORDER = ("w_in", "w_pool", "pool_scale", "gm_gain", "w_spatial", "b_spatial", "w_br_sb", "w_br_pool", "w_br_gm",
                "w_out", "g_mix_pre", "g_mix_post", "g_ff_pre", "g_ff_post", "w_ff_in", "w_ff_out")


def _full_weight(name, mine, theirs, core, l):
    a, b = mine[:, l], theirs[:, l]
    g = jnp.stack([jnp.where(core == 0, a, b), jnp.where(core == 0, b, a)])
    half, cols = g.shape[2], g.shape[3]
    if name in COLUMN_SHARDED:
        return g.transpose(0, 2, 1, 3).reshape(2 * half, N_CHIPS * cols)
    return g.transpose(1, 0, 2, 3).reshape(N_CHIPS * 2 * half, cols)


def _parts_by_chip(name, grads):
    g = jnp.stack(grads)
    L = g.shape[0]
    if name in COLUMN_SHARDED:
        r, c = g.shape[1], g.shape[2] // N_CHIPS
        return g.reshape(L, r, N_CHIPS, c).transpose(2, 0, 1, 3).reshape(N_CHIPS, L * r, c)
    r, c = g.shape[1] // N_CHIPS, g.shape[2]
    return g.reshape(L, N_CHIPS, r, c).transpose(1, 0, 2, 3).reshape(N_CHIPS, L * r, c)


def _pack(arrays):
    flat = jnp.concatenate([a.reshape(-1) for a in arrays])
    return flat.reshape(-1, LANES)


def _unpack(buf, like):
    flat, out, at = buf.reshape(-1), [], 0
    for a in like:
        out.append(flat[at:at + a.size].reshape(a.shape))
        at += a.size
    return out


def kernel(x, w_in, w_pool, pool_scale, gm_gain, w_spatial, b_spatial, w_br_sb, w_br_pool, w_br_gm, w_out, g_mix_pre, g_mix_post, g_ff_pre, g_ff_post, w_ff_in, w_ff_out, loss_target, m_w_in, m_w_pool, m_pool_scale, m_gm_gain, m_w_spatial, m_b_spatial, m_w_br_sb, m_w_br_pool, m_w_br_gm, m_w_out, m_g_mix_pre, m_g_mix_post, m_g_ff_pre, m_g_ff_post, m_w_ff_in, m_w_ff_out, v_w_in, v_w_pool, v_pool_scale, v_gm_gain, v_w_spatial, v_b_spatial, v_w_br_sb, v_w_br_pool, v_w_br_gm, v_w_out, v_g_mix_pre, v_g_mix_post, v_g_ff_pre, v_g_ff_post, v_w_ff_in, v_w_ff_out):
    given = dict(locals())
    w = {n: given[n] for n in WEIGHT_ORDER}
    m = {n: given["m_" + n] for n in WEIGHT_ORDER}
    v = {n: given["v_" + n] for n in WEIGHT_ORDER}
    L = w_in.shape[0]

    core = lax.axis_index("c")

    def my_rows(a):
        half = a.shape[1] // 2
        return lax.dynamic_slice_in_dim(a.astype(BF16), core * half, half, axis=1)

    halves = _chip_exchange([my_rows(w[n]) for n in BIG_WEIGHTS], scatter=False, name="gather_weights")
    other_halves = _sibling_swap(halves, name="swap_weight_halves")
    big = {n: [_full_weight(n, a, b, core, l) for l in range(L)] for n, a, b in zip(BIG_WEIGHTS, halves, other_halves)}
    small = {n: w[n] for n in SMALL_WEIGHTS}

    loss, dx, big_grads, small_grads = _local_step(x[0], loss_target[0], big, small)

    parts = [_parts_by_chip(n, big_grads[n]) for n in BIG_WEIGHTS]
    from_sibling = _sibling_other_half(parts, name="swap_grad_halves")
    core_index = core.astype(jnp.int32).reshape(1)
    chip_sums = [_add_own_half(p, s, core_index, name="sum_cores") for p, s in zip(parts, from_sibling)]
    received = _chip_exchange(chip_sums, scatter=True, name="exchange_grads")
    reduced = [_sum_slots(r, name="sum_chips") for r in received]
    reduced_by_sibling = _sibling_swap(reduced, name="swap_reduced_halves")
    grads, deltas, new_m, new_v = {}, {}, {}, {}
    for n, mine, theirs in zip(BIG_WEIGHTS, reduced, reduced_by_sibling):
        shape = w[n].shape
        flat = lambda a: a.reshape(-1, shape[-1])
        outs = _adamw_halves(flat(w[n]), flat(m[n]), flat(v[n]), mine, theirs, core_index, name="adamw_big")
        grads[n], deltas[n], new_m[n], new_v[n] = [o.reshape(shape) for o in outs]

    like = [w[n] for n in SMALL_WEIGHTS]
    all_partials = _gather_all(_pack([small_grads[n] for n in SMALL_WEIGHTS]), name="gather_small_grads")
    outs = _adamw_summed(_pack(like), _pack([m[n] for n in SMALL_WEIGHTS]), _pack([v[n] for n in SMALL_WEIGHTS]),
                         all_partials, name="adamw_small")
    for store, buf in zip((grads, deltas, new_m, new_v), outs):
        store.update(zip(SMALL_WEIGHTS, _unpack(buf, like)))

    total_loss = lax.psum(loss[0, 0], ("x", "y", "c"))
    return (total_loss, dx[None], *[grads[n] for n in WEIGHT_ORDER], *[deltas[n] for n in WEIGHT_ORDER],
            *[new_m[n] for n in WEIGHT_ORDER], *[new_v[n] for n in WEIGHT_ORDER])
```

```python
import functools
import math

import jax
import jax.numpy as jnp
from jax import lax
from jax.experimental import pallas as pl
from jax.experimental.pallas import tpu as pltpu

F32 = jnp.float32
BF16 = jnp.bfloat16

D_MODEL = 1024
SB_HEADS = 8
SB_HEAD_DIM = 64
SB_WIDTH = SB_HEADS * SB_HEAD_DIM
POOL_WINDOWS = (2, 4, 8, 16)
POOL_GROUP_DIM = 64
POOL_WIDTH = 256
POOL_HALO = 16
GM_GROUPS = 4
GM_GROUP_DIM = 64
GM_WIDTH = 256
GM_CHUNK = 128
N_BRANCH = 3
D_FF = 4 * D_MODEL
RMS_EPS = 1e-6
QKV_WIDTH = 3 * SB_WIDTH
MIX_WIDTH = POOL_WIDTH + 2 * GM_WIDTH
GATE_WIDTH = N_BRANCH * D_MODEL
D_IN = QKV_WIDTH + MIX_WIDTH + GATE_WIDTH
PROJ_BLOCK = 768
LANES = 128
N_CHIPS = 4
N_DEV = 8

ADAM_LR = 0.001
ADAM_B1 = 0.9
ADAM_B2 = 0.999
ADAM_EPS = 1e-08
ADAM_WD = 0.01
ADAM_STEP = 10

VMEM_LIMIT = 56 * 1024 * 1024
MESH = pl.DeviceIdType.MESH


def _params(n_grid):
    return pltpu.CompilerParams(dimension_semantics=("arbitrary",) * n_grid, vmem_limit_bytes=VMEM_LIMIT)


def _bf(x):
    return x if x.dtype == BF16 else x.astype(BF16)


def _matmul(a, b, *, name, ta=False, tb=False, out_dtypes=(F32,), n=None, b_col_off=0, bm=1024, bn=1024, bk=2048,
            extras=(), epilogue=None):
    M, K = (a.shape[1], a.shape[0]) if ta else a.shape
    nb = b.shape[0] if tb else b.shape[1]
    n = nb if n is None else n
    bm, bn, bk = min(bm, M), min(bn, n), min(bk, K)
    assert M % bm == 0 and n % bn == 0 and K % bk == 0, (name, M, n, K, bm, bn, bk)
    assert (b.shape[1] if tb else b.shape[0]) == K, (name, a.shape, b.shape)
    nk = K // bk
    dims = (((0 if ta else 1,), (1 if tb else 0,)), ((), ()))
    n_out = len(out_dtypes)
    direct = nk > 1 and epilogue is None and out_dtypes == (F32,)
    use_acc = nk > 1 and not direct

    def body(*refs):
        a_ref, b_ref = refs[:2]
        extra_refs = refs[2:2 + len(extras)]
        out_refs = refs[2 + len(extras):2 + len(extras) + n_out]
        p = lax.dot_general(_bf(a_ref[...]), _bf(b_ref[...]), dims, preferred_element_type=F32)

        def finish(acc):
            outs = (acc,) if epilogue is None else epilogue(acc, *[r[...] for r in extra_refs])
            for r, o in zip(out_refs, outs):
                r[...] = o.astype(r.dtype)

        if nk == 1:
            finish(p)
            return
        k = pl.program_id(2)
        acc_ref = out_refs[0] if direct else refs[-1]

        @pl.when(k == 0)
        def _():
            acc_ref[...] = p

        @pl.when(k > 0)
        def _():
            acc_ref[...] += p

        if use_acc:
            @pl.when(k == nk - 1)
            def _():
                finish(acc_ref[...])

    a_spec = pl.BlockSpec((bk, bm), lambda i, j, k: (k, i)) if ta else pl.BlockSpec((bm, bk), lambda i, j, k: (i, k))
    if tb:
        assert b_col_off == 0
        b_spec = pl.BlockSpec((bn, bk), lambda i, j, k: (j, k))
    else:
        b_spec = pl.BlockSpec((bk, bn), lambda i, j, k: (k, j + b_col_off))
    tile = pl.BlockSpec((bm, bn), lambda i, j, k: (i, j))
    outs = pl.pallas_call(
        body, name=name, grid=(M // bm, n // bn, nk),
        in_specs=[a_spec, b_spec] + [tile] * len(extras),
        out_specs=[tile] * n_out,
        out_shape=[jax.ShapeDtypeStruct((M, n), d) for d in out_dtypes],
        scratch_shapes=[pltpu.VMEM((bm, bn), F32)] if use_acc else [],
        compiler_params=_params(3),
    )(a, b, *extras)
    return outs[0] if n_out == 1 else outs


ROW_TILE = 512


def _rows(S):
    tr = min(ROW_TILE, S)
    assert S % tr == 0
    return tr


def _rstd(x):
    return lax.rsqrt(jnp.mean(x * x, axis=-1, keepdims=True) + RMS_EPS)


def _rms_bwd_math(x, g, dy):
    r = _rstd(x)
    gd = g * dy
    dx = r * gd - x * (r * r * r) * jnp.mean(x * gd, axis=-1, keepdims=True)
    dg = jnp.sum(dy * x * r, axis=0, keepdims=True)
    return dx, dg


def _accumulate(ref, value):
    i = pl.program_id(0)

    @pl.when(i == 0)
    def _():
        ref[...] = value

    @pl.when(i > 0)
    def _():
        ref[...] += value


def _row_spec(tr, width):
    return pl.BlockSpec((tr, width), lambda i: (i, 0))


def _vec_spec(width):
    return pl.BlockSpec((1, width), lambda i: (0, 0))


def _rms_fwd(x, g, *, name):
    S, D = x.shape
    tr = _rows(S)

    def body(x_ref, g_ref, o_ref):
        xf = x_ref[...]
        o_ref[...] = (xf * _rstd(xf) * g_ref[...]).astype(o_ref.dtype)

    return pl.pallas_call(
        body, name=name, grid=(S // tr,), in_specs=[_row_spec(tr, D), _vec_spec(D)], out_specs=_row_spec(tr, D),
        out_shape=jax.ShapeDtypeStruct((S, D), BF16), compiler_params=_params(1))(x, g)


def _resid_rms(x, y, g_post, g_next, *, name):
    S, D = x.shape
    tr = _rows(S)
    with_next = g_next is not None

    def body(*refs):
        if with_next:
            x_ref, y_ref, gp_ref, gn_ref, xo_ref, ho_ref = refs
        else:
            x_ref, y_ref, gp_ref, xo_ref = refs
        yf = y_ref[...]
        xn = x_ref[...] + yf * _rstd(yf) * gp_ref[...]
        xo_ref[...] = xn
        if with_next:
            ho_ref[...] = (xn * _rstd(xn) * gn_ref[...]).astype(ho_ref.dtype)

    row, vec = _row_spec(tr, D), _vec_spec(D)
    ins = [x, y, g_post] + ([g_next] if with_next else [])
    outs = pl.pallas_call(
        body, name=name, grid=(S // tr,), in_specs=[row, row, vec] + ([vec] if with_next else []),
        out_specs=[row] + ([row] if with_next else []),
        out_shape=[jax.ShapeDtypeStruct((S, D), F32)] + ([jax.ShapeDtypeStruct((S, D), BF16)] if with_next else []),
        compiler_params=_params(1))(*ins)
    return (outs[0], outs[1]) if with_next else (outs[0], None)


def _rms_bwd(x, g, dy, *, name):
    S, D = x.shape
    tr = _rows(S)

    def body(x_ref, g_ref, dy_ref, dx_ref, dg_ref):
        dx, dg = _rms_bwd_math(x_ref[...], g_ref[...], dy_ref[...])
        dx_ref[...] = dx.astype(dx_ref.dtype)
        _accumulate(dg_ref, dg)

    row, vec = _row_spec(tr, D), _vec_spec(D)
    return pl.pallas_call(
        body, name=name, grid=(S // tr,), in_specs=[row, vec, row], out_specs=[row, vec],
        out_shape=[jax.ShapeDtypeStruct((S, D), BF16), jax.ShapeDtypeStruct((1, D), F32)],
        compiler_params=_params(1))(x, g, dy)


def _rms_bwd_chain(xa, ga, da, resid, xb, gb, *, name):
    S, D = xa.shape
    tr = _rows(S)
    chain = xb is not None

    def body(*refs):
        if chain:
            xa_ref, ga_ref, da_ref, rs_ref, xb_ref, gb_ref, dx_ref, dga_ref, dxb_ref, dgb_ref = refs
        else:
            xa_ref, ga_ref, da_ref, rs_ref, dx_ref, dga_ref = refs
        dxa, dga = _rms_bwd_math(xa_ref[...], ga_ref[...], da_ref[...])
        dx = rs_ref[...] + dxa
        dx_ref[...] = dx
        _accumulate(dga_ref, dga)
        if chain:
            dxb, dgb = _rms_bwd_math(xb_ref[...], gb_ref[...], dx)
            dxb_ref[...] = dxb.astype(dxb_ref.dtype)
            _accumulate(dgb_ref, dgb)

    row, vec = _row_spec(tr, D), _vec_spec(D)
    ins = [xa, ga, da, resid] + ([xb, gb] if chain else [])
    outs = pl.pallas_call(
        body, name=name, grid=(S // tr,), in_specs=[row, vec, row, row] + ([row, vec] if chain else []),
        out_specs=[row, vec] + ([row, vec] if chain else []),
        out_shape=[jax.ShapeDtypeStruct((S, D), F32), jax.ShapeDtypeStruct((1, D), F32)]
        + ([jax.ShapeDtypeStruct((S, D), BF16), jax.ShapeDtypeStruct((1, D), F32)] if chain else []),
        compiler_params=_params(1))(*ins)
    return tuple(outs) if chain else (outs[0], outs[1], None, None)


def _loss_head(y, target, *, name):
    S, D = y.shape
    tr = _rows(S)
    n_tiles = S // tr

    def body(y_ref, t_ref, dy_ref, loss_ref, acc_ref):
        err = y_ref[...] - t_ref[...]
        dy_ref[...] = err * (1.0 / D)
        _accumulate(acc_ref, jnp.sum(err * err, axis=0, keepdims=True))

        @pl.when(pl.program_id(0) == n_tiles - 1)
        def _():
            loss_ref[...] = jnp.sum(acc_ref[...], axis=1, keepdims=True) * (0.5 / D)

    row = _row_spec(tr, D)
    return pl.pallas_call(
        body, name=name, grid=(n_tiles,), in_specs=[row, row],
        out_specs=[row, pl.BlockSpec((1, 1), lambda i: (0, 0))],
        out_shape=[jax.ShapeDtypeStruct((S, D), F32), jax.ShapeDtypeStruct((1, 1), F32)],
        scratch_shapes=[pltpu.VMEM((1, D), F32)], compiler_params=_params(1))(y, target)


SB_TILE = 256
SB_PAIRS = SB_HEADS * SB_HEAD_DIM // LANES
SB_DEAD_LOG = -110.0


def _log_sigmoids(z):
    l1p = jnp.log(1.0 + jnp.exp(-jnp.abs(z)))
    return jnp.minimum(z, 0.0) - l1p, jnp.minimum(-z, 0.0) - l1p


def _split_bf16(x):
    hi = x.astype(BF16)
    lo = (x - hi.astype(F32)).astype(BF16)
    return jnp.concatenate([hi, lo], axis=1)


def _tri(T, cmp):
    j = lax.broadcasted_iota(jnp.int32, (T, T), 0)
    s = lax.broadcasted_iota(jnp.int32, (T, T), 1)
    m = jnp.where(cmp(j, s), 1.0, 0.0).astype(BF16)
    return jnp.concatenate([m, m], axis=0)


def _head_masks():
    lane = lax.broadcasted_iota(jnp.int32, (1, LANES), 1)
    return [lane < SB_HEAD_DIM, lane >= SB_HEAD_DIM]


def _sba_fwd(qkv, *, name):
    S = qkv.shape[0]
    T = min(SB_TILE, S)
    nq = S // T
    scale = SB_HEAD_DIM ** -0.5

    def body(q_ref, k_ref, v_ref, o_ref, t_ref, first_ref):
        i = pl.program_id(1)
        row = lax.broadcasted_iota(jnp.int32, (T, T), 0)
        col = lax.broadcasted_iota(jnp.int32, (T, T), 1)
        strict = col < row
        after = _tri(T, lambda j, s: j > s)
        masks = _head_masks()
        q = q_ref[...] * scale
        qs = [jnp.where(hm, q, jnp.zeros_like(q)) for hm in masks]

        def walk(tiles, carry):
            values, logs = [], []
            for j, diag in tiles:
                rows = pl.ds(pl.multiple_of(j * T, T), T)
                kb = k_ref[rows, :]
                values.append(v_ref[rows, :])
                for qh in qs:
                    z = lax.dot_general(qh, kb, (((1,), (1,)), ((), ())), preferred_element_type=F32)
                    ls, ln = _log_sigmoids(z)
                    logs.append((ls, jnp.where(strict, ln, 0.0) if diag else ln))
            suffixes = [jnp.dot(_split_bf16(ln), after, preferred_element_type=F32) for _, ln in logs]
            for t, (_, diag) in enumerate(tiles):
                out = []
                for h, (C, acc) in enumerate(carry):
                    ls, ln = logs[2 * t + h]
                    a = jnp.exp(ls + suffixes[2 * t + h] + C)
                    if diag:
                        a = jnp.where(strict, a, 0.0)
                    acc = acc + jnp.dot(a.astype(BF16), values[t], preferred_element_type=F32)
                    out.append((C + jnp.sum(ln, axis=1, keepdims=True), acc))
                carry = tuple(out)
            return carry

        fresh = (jnp.zeros((T, 1), F32), jnp.zeros((T, LANES), F32))
        carry = lax.cond(i > 0, lambda: walk([(i, True), (i - 1, False)], (fresh, fresh)),
                         lambda: walk([(i, True)], (fresh, fresh)))

        def alive(state):
            j, ((C0, _), (C1, _)) = state
            return jnp.logical_and(j >= 0, jnp.max(jnp.maximum(C0, C1)) > SB_DEAD_LOG)

        def step(state):
            j, carry = state
            return j - 1, walk([(j, False)], carry)

        j, ((C0, acc0), (C1, acc1)) = lax.while_loop(alive, step, (i - 2, carry))
        t_ref[0] = jnp.broadcast_to(C0, (T, LANES))
        t_ref[1] = jnp.broadcast_to(C1, (T, LANES))
        first_ref[...] = jnp.full((8, LANES), jnp.maximum(j + 1, 0).astype(F32))
        o_ref[...] = jnp.where(masks[0], acc0, acc1).astype(o_ref.dtype)

    kv = lambda off: pl.BlockSpec((S, LANES), lambda p, i: (0, off + p))
    return pl.pallas_call(
        body, name=name, grid=(SB_PAIRS, nq),
        in_specs=[pl.BlockSpec((T, LANES), lambda p, i: (i, p)), kv(SB_PAIRS), kv(2 * SB_PAIRS)],
        out_specs=[pl.BlockSpec((T, LANES), lambda p, i: (i, p)), pl.BlockSpec((2, T, LANES), lambda p, i: (p, i, 0)),
                   pl.BlockSpec((None, None, 8, LANES), lambda p, i: (p, i, 0, 0))],
        out_shape=[jax.ShapeDtypeStruct((S, SB_WIDTH), BF16), jax.ShapeDtypeStruct((SB_HEADS, S, LANES), F32),
                   jax.ShapeDtypeStruct((SB_PAIRS, nq, 8, LANES), F32)],
        compiler_params=_params(2))(qkv, qkv, qkv)


def _sba_bwd(qkv, do, tot, first, *, name):
    S = qkv.shape[0]
    T = min(SB_TILE, S)
    nq = S // T
    scale = SB_HEAD_DIM ** -0.5

    def body(q_ref, k_ref, v_ref, do_ref, t_ref, first_ref, dq_ref, dk_ref, dv_ref, dk_acc, dv_acc):
        i = pl.program_id(1)

        @pl.when(i == 0)
        def _():
            dk_acc[...] = jnp.zeros_like(dk_acc)
            dv_acc[...] = jnp.zeros_like(dv_acc)

        row = lax.broadcasted_iota(jnp.int32, (T, T), 0)
        col = lax.broadcasted_iota(jnp.int32, (T, T), 1)
        strict = col < row
        upto = _tri(T, lambda j, s: j <= s)
        before = _tri(T, lambda j, s: j < s)
        masks = _head_masks()
        q, do_t = q_ref[...], do_ref[...]
        q = q * scale
        qs = [jnp.where(hm, q, jnp.zeros_like(q)) for hm in masks]
        dos = [jnp.where(hm, do_t, jnp.zeros_like(do_t)) for hm in masks]
        totals = [t_ref[h][:, 0:1] for h in range(2)]
        over_lanes = (((1,), (1,)), ((), ()))
        over_queries = (((0,), (0,)), ((), ()))

        def walk(tiles, carry):
            rows = [pl.ds(pl.multiple_of(j * T, T), T) for j, _ in tiles]
            keys = [k_ref[r, :] for r in rows]
            values = [v_ref[r, :] for r in rows]
            chains = [(t, h) for t in range(len(tiles)) for h in range(2)]
            logs, da = {}, {}
            for t, h in chains:
                z = lax.dot_general(qs[h], keys[t], over_lanes, preferred_element_type=F32)
                ls, ln = _log_sigmoids(z)
                logs[t, h] = (ls, jnp.where(strict, ln, 0.0) if tiles[t][1] else ln)
                da[t, h] = lax.dot_general(dos[h], values[t], over_lanes, preferred_element_type=F32)
            upto_sums = {c: jnp.dot(_split_bf16(logs[c][1]), upto, preferred_element_type=F32) for c in chains}
            a, g = {}, {}
            P = [c[0] for c in carry]
            for t, h in chains:
                ls, ln = logs[t, h]
                a_th = jnp.exp(ls + ((totals[h] - P[h]) - upto_sums[t, h]))
                a[t, h] = jnp.where(strict, a_th, 0.0) if tiles[t][1] else a_th
                g[t, h] = a[t, h] * da[t, h]
                P[h] = P[h] + jnp.sum(ln, axis=1, keepdims=True)
            before_sums = {c: jnp.dot(_split_bf16(g[c]), before, preferred_element_type=F32) for c in chains}
            G = [c[1] for c in carry]
            dq = [c[2] for c in carry]
            dz = {}
            for t, h in chains:
                beta = jnp.exp(logs[t, h][0])
                dz_th = g[t, h] * (1.0 - beta) - (G[h] + before_sums[t, h]) * beta
                dz[t, h] = (jnp.where(strict, dz_th, 0.0) if tiles[t][1] else dz_th).astype(BF16)
                G[h] = G[h] + jnp.sum(g[t, h], axis=1, keepdims=True)
            for t, h in chains:
                dq[h] = dq[h] + jnp.dot(dz[t, h], keys[t], preferred_element_type=F32)
            for t in range(len(tiles)):
                dk_acc[rows[t], :] += sum(
                    lax.dot_general(dz[t, h], qs[h], over_queries, preferred_element_type=F32) for h in range(2))
                dv_acc[rows[t], :] += sum(
                    lax.dot_general(a[t, h].astype(BF16), dos[h], over_queries, preferred_element_type=F32)
                    for h in range(2))
            return tuple((P[h], G[h], dq[h]) for h in range(2))

        zero = jnp.zeros((T, 1), F32)
        fresh = (zero, zero, jnp.zeros((T, LANES), F32))
        last_single = jnp.maximum(i - 1, 0)
        j0 = jnp.clip(jnp.max(first_ref[...]).astype(jnp.int32), 0, last_single)
        carry = lax.fori_loop(j0, last_single, lambda j, c: walk([(j, False)], c), (fresh, fresh))
        (_, _, dq0), (_, _, dq1) = lax.cond(i > 0, lambda: walk([(i - 1, False), (i, True)], carry),
                                            lambda: walk([(i, True)], carry))
        dq_ref[...] = (jnp.where(masks[0], dq0, dq1) * scale).astype(dq_ref.dtype)

        @pl.when(i == nq - 1)
        def _():
            dk_ref[...] = dk_acc[...].astype(dk_ref.dtype)
            dv_ref[...] = dv_acc[...].astype(dv_ref.dtype)

    kv = lambda off: pl.BlockSpec((S, LANES), lambda p, i: (0, off + p))
    tile = lambda off: pl.BlockSpec((T, LANES), lambda p, i: (i, off + p))
    return pl.pallas_call(
        body, name=name, grid=(SB_PAIRS, nq),
        in_specs=[tile(0), kv(SB_PAIRS), kv(2 * SB_PAIRS), tile(0), pl.BlockSpec((2, T, LANES), lambda p, i: (p, i, 0)),
                  pl.BlockSpec((None, None, 8, LANES), lambda p, i: (p, i, 0, 0))],
        out_specs=[tile(0), kv(0), kv(0)],
        out_shape=[jax.ShapeDtypeStruct((S, SB_WIDTH), BF16)] * 3,
        scratch_shapes=[pltpu.VMEM((S, LANES), F32), pltpu.VMEM((S, LANES), F32)],
        compiler_params=_params(2))(qkv, qkv, qkv, do, tot, first)


POOL_TILE = 512


def _by_group(lane, values):
    return jnp.where(lane < 64, values[0], jnp.where(lane < 128, values[1], jnp.where(lane < 192, values[2], values[3])))


def _pool_inv_count(first_row, n_rows):
    t = first_row + lax.broadcasted_iota(jnp.int32, (n_rows, POOL_WIDTH), 0)
    lane = lax.broadcasted_iota(jnp.int32, (n_rows, POOL_WIDTH), 1)
    window = _by_group(lane, POOL_WINDOWS)
    return 1.0 / jnp.clip(t + 1, 1, window).astype(F32), lane


def _pooled(ext, first_row, R):
    n = R + POOL_HALO
    s2 = ext + pltpu.roll(ext, 1, 0)
    s4 = s2 + pltpu.roll(s2, 2, 0)
    s8 = s4 + pltpu.roll(s4, 4, 0)
    s16 = s8 + pltpu.roll(s8, 8, 0)
    inv, lane = _pool_inv_count(first_row - POOL_HALO, n)
    pooled = _by_group(lane, (s2, s4, s8, s16)) * inv - ext
    return pooled[POOL_HALO:, :]


def _pool_specs(S, R, col):
    per = R // POOL_HALO
    tile = pl.BlockSpec((R, POOL_WIDTH), lambda i: (i, col))
    prev = pl.BlockSpec((POOL_HALO, POOL_WIDTH), lambda i: (jnp.maximum(i * per - 1, 0), col))
    return tile, prev


def _pool_fwd(rest, w_bd, scale, *, name):
    S = rest.shape[0]
    R = min(POOL_TILE, S)

    def body(p_ref, prev_ref, w_ref, s_ref, o_ref, ext_ref):
        i = pl.program_id(0)
        ext_ref[:POOL_HALO, :] = jnp.where(i > 0, prev_ref[...], 0.0)
        ext_ref[POOL_HALO:, :] = p_ref[...]
        pooled = _pooled(ext_ref[...], i * R, R)
        mixed = jnp.dot(pooled.astype(BF16), w_ref[...], preferred_element_type=F32)
        o_ref[...] = (mixed * s_ref[...]).astype(o_ref.dtype)

    tile, prev = _pool_specs(S, R, 0)
    return pl.pallas_call(
        body, name=name, grid=(S // R,),
        in_specs=[tile, prev, pl.BlockSpec((POOL_WIDTH, POOL_WIDTH), lambda i: (0, 0)), _vec_spec(POOL_WIDTH)],
        out_specs=_row_spec(R, POOL_WIDTH), out_shape=jax.ShapeDtypeStruct((S, POOL_WIDTH), BF16),
        scratch_shapes=[pltpu.VMEM((R + POOL_HALO, POOL_WIDTH), F32)], compiler_params=_params(1))(rest, rest, w_bd, scale)


def _pool_bwd(rest, do, w_bd, scale, *, name):
    S = rest.shape[0]
    R = min(POOL_TILE, S)
    n_tiles = S // R
    per = R // POOL_HALO
    n = R + POOL_HALO

    def body(p_ref, prev_ref, do_ref, nxt_ref, w_ref, s_ref, dp_ref, dw_ref, ds_ref, ext_ref, dext_ref):
        i = pl.program_id(0)
        ext_ref[:POOL_HALO, :] = jnp.where(i > 0, prev_ref[...], 0.0)
        ext_ref[POOL_HALO:, :] = p_ref[...]
        pooled = _pooled(ext_ref[...], i * R, R).astype(BF16)
        w = w_ref[...]
        mixed = jnp.dot(pooled, w, preferred_element_type=F32)
        do_t = do_ref[...]
        _accumulate(ds_ref, jnp.sum(do_t * mixed, axis=0, keepdims=True))
        dext_ref[:R, :] = do_t
        dext_ref[R:, :] = jnp.where(i < n_tiles - 1, nxt_ref[...], 0.0)
        dmixed = (dext_ref[...] * s_ref[...]).astype(BF16)
        dpooled = lax.dot_general(dmixed, w, (((1,), (1,)), ((), ())), preferred_element_type=F32)
        _accumulate(dw_ref, lax.dot_general(pooled, dmixed[:R, :], (((0,), (0,)), ((), ())), preferred_element_type=F32))
        inv, lane = _pool_inv_count(i * R, n)
        u = dpooled * inv
        f2 = u + pltpu.roll(u, n - 1, 0)
        f4 = f2 + pltpu.roll(f2, n - 2, 0)
        f8 = f4 + pltpu.roll(f4, n - 4, 0)
        f16 = f8 + pltpu.roll(f8, n - 8, 0)
        dp = _by_group(lane, (f2, f4, f8, f16)) - dpooled
        dp_ref[...] = dp[:R, :].astype(dp_ref.dtype)

    tile, prev = _pool_specs(S, R, 0)
    nxt = pl.BlockSpec((POOL_HALO, POOL_WIDTH), lambda i: (jnp.minimum((i + 1) * per, S // POOL_HALO - 1), 0))
    full = pl.BlockSpec((POOL_WIDTH, POOL_WIDTH), lambda i: (0, 0))
    return pl.pallas_call(
        body, name=name, grid=(n_tiles,),
        in_specs=[tile, prev, _row_spec(R, POOL_WIDTH), nxt, full, _vec_spec(POOL_WIDTH)],
        out_specs=[_row_spec(R, POOL_WIDTH), full, _vec_spec(POOL_WIDTH)],
        out_shape=[jax.ShapeDtypeStruct((S, POOL_WIDTH), BF16), jax.ShapeDtypeStruct((POOL_WIDTH, POOL_WIDTH), F32),
                   jax.ShapeDtypeStruct((1, POOL_WIDTH), F32)],
        scratch_shapes=[pltpu.VMEM((n, POOL_WIDTH), F32), pltpu.VMEM((n, POOL_WIDTH), F32)],
        compiler_params=_params(1))(rest, rest, do, do, w_bd, scale)


GM_TILE = 512
GELU_C = math.sqrt(2.0 / math.pi)
GELU_A = 0.044715


def _gelu(x):
    return 0.5 * x * (1.0 + jnp.tanh(GELU_C * (x + GELU_A * x * x * x)))


def _gelu_and_grad(x):
    t = jnp.tanh(GELU_C * (x + GELU_A * x * x * x))
    y = 0.5 * x * (1.0 + t)
    dy = 0.5 * (1.0 + t) + 0.5 * x * (1.0 - t * t) * (GELU_C * (1.0 + 3.0 * GELU_A * x * x))
    return y, dy


def _group_lane_masks():
    lane = lax.broadcasted_iota(jnp.int32, (1, GM_WIDTH), 1)
    return [(lane >= g * GM_GROUP_DIM) & (lane < (g + 1) * GM_GROUP_DIM) for g in range(GM_GROUPS)]


def _stack_groups(x, masks):
    return jnp.concatenate([jnp.where(m, x, jnp.zeros_like(x)) for m in masks], axis=0)


def _gm_mixed(vn, ws_cat, bias, masks, R):
    chunks = []
    for c in range(R // GM_CHUNK):
        vc = vn[c * GM_CHUNK:(c + 1) * GM_CHUNK, :]
        chunks.append(jnp.dot(ws_cat, _stack_groups(vc, masks), preferred_element_type=F32) + bias)
    return jnp.concatenate(chunks, axis=0)


def _gm_specs(S, R):
    u = pl.BlockSpec((R, GM_WIDTH), lambda i: (i, 1))
    v = pl.BlockSpec((R, GM_WIDTH), lambda i: (i, 2))
    ws = pl.BlockSpec((GM_CHUNK, GM_GROUPS * GM_CHUNK), lambda i: (0, 0))
    bias = pl.BlockSpec((GM_CHUNK, GM_WIDTH), lambda i: (0, 0))
    return u, v, ws, bias


def _gm_fwd(rest, gain, ws_cat, bias, *, name):
    S = rest.shape[0]
    R = min(GM_TILE, S)

    def body(u_ref, v_ref, g_ref, ws_ref, b_ref, o_ref):
        gv = _gelu(v_ref[...])
        vn = (gv * _rstd(gv) * g_ref[...]).astype(BF16)
        mixed = _gm_mixed(vn, ws_ref[...], b_ref[...], _group_lane_masks(), R)
        o_ref[...] = (_gelu(u_ref[...]) * mixed).astype(o_ref.dtype)

    u_spec, v_spec, ws_spec, bias_spec = _gm_specs(S, R)
    return pl.pallas_call(
        body, name=name, grid=(S // R,), in_specs=[u_spec, v_spec, _vec_spec(GM_WIDTH), ws_spec, bias_spec],
        out_specs=_row_spec(R, GM_WIDTH), out_shape=jax.ShapeDtypeStruct((S, GM_WIDTH), BF16),
        compiler_params=_params(1))(rest, rest, gain, ws_cat, bias)


def _gm_bwd(rest, do, gain, ws_cat, wst_cat, bias, *, name):
    S = rest.shape[0]
    R = min(GM_TILE, S)

    def body(u_ref, v_ref, do_ref, g_ref, ws_ref, wst_ref, b_ref, du_ref, dv_ref, dg_ref, dws_ref, db_ref):
        masks = _group_lane_masks()
        gain_v = g_ref[...]
        gu, dgu = _gelu_and_grad(u_ref[...])
        gv, dgv = _gelu_and_grad(v_ref[...])
        r = _rstd(gv)
        vn = (gv * r * gain_v).astype(BF16)
        mixed = _gm_mixed(vn, ws_ref[...], b_ref[...], masks, R)
        do_t = do_ref[...]
        du_ref[...] = (do_t * mixed * dgu).astype(du_ref.dtype)
        dmix = do_t * gu
        dmix_b = dmix.astype(BF16)
        wst = wst_ref[...]
        dvn_chunks, db, dws = [], None, [None] * GM_GROUPS
        for c in range(R // GM_CHUNK):
            rows = slice(c * GM_CHUNK, (c + 1) * GM_CHUNK)
            dc, dcb, vc = dmix[rows, :], dmix_b[rows, :], vn[rows, :]
            db = dc if db is None else db + dc
            dvn_chunks.append(jnp.dot(wst, _stack_groups(dcb, masks), preferred_element_type=F32))
            for g, m in enumerate(masks):
                part = lax.dot_general(jnp.where(m, dcb, jnp.zeros_like(dcb)), vc, (((1,), (1,)), ((), ())),
                                       preferred_element_type=F32)
                dws[g] = part if dws[g] is None else dws[g] + part
        dvn = jnp.concatenate(dvn_chunks, axis=0)
        lane = lax.broadcasted_iota(jnp.int32, (1, LANES), 1)
        db_groups = jnp.zeros((GM_CHUNK, LANES), F32)
        for g, m in enumerate(masks):
            total = jnp.sum(jnp.where(m, db, 0.0), axis=1, keepdims=True)
            db_groups = db_groups + jnp.where(lane == g, total, 0.0)
        _accumulate(db_ref, db_groups)
        i = pl.program_id(0)
        for g in range(GM_GROUPS):
            @pl.when(i == 0)
            def _(g=g):
                dws_ref[g] = dws[g]

            @pl.when(i > 0)
            def _(g=g):
                dws_ref[g] += dws[g]
        _accumulate(dg_ref, jnp.sum(dvn * gv * r, axis=0, keepdims=True))
        gd = gain_v * dvn
        dgv_in = r * gd - gv * (r * r * r) * jnp.mean(gv * gd, axis=-1, keepdims=True)
        dv_ref[...] = (dgv_in * dgv).astype(dv_ref.dtype)

    u_spec, v_spec, ws_spec, bias_spec = _gm_specs(S, R)
    row, vec = _row_spec(R, GM_WIDTH), _vec_spec(GM_WIDTH)
    dws_spec = pl.BlockSpec((GM_GROUPS, GM_CHUNK, GM_CHUNK), lambda i: (0, 0, 0))
    return pl.pallas_call(
        body, name=name, grid=(S // R,), in_specs=[u_spec, v_spec, row, vec, ws_spec, ws_spec, bias_spec],
        out_specs=[row, row, vec, dws_spec, pl.BlockSpec((GM_CHUNK, LANES), lambda i: (0, 0))],
        out_shape=[jax.ShapeDtypeStruct((S, GM_WIDTH), BF16)] * 2
        + [jax.ShapeDtypeStruct((1, GM_WIDTH), F32), jax.ShapeDtypeStruct((GM_GROUPS, GM_CHUNK, GM_CHUNK), F32),
           jax.ShapeDtypeStruct((GM_CHUNK, LANES), F32)],
        compiler_params=_params(1))(rest, rest, do, gain, ws_cat, wst_cat, bias)


GATE_ROWS = 1024
GATE_COLS = 256
GATE_BLOCKS = D_MODEL // GATE_COLS


def _gate_spec(tr, k):
    return pl.BlockSpec((tr, GATE_COLS), lambda i, j: (i, GATE_BLOCKS * k + j))


def _merge_fwd(gates, branches, *, name):
    S = gates.shape[0]
    tr = min(GATE_ROWS, S)

    def body(g0, g1, g2, b0, b1, b2, o_ref):
        acc = None
        for g_ref, b_ref in ((g0, b0), (g1, b1), (g2, b2)):
            term = jax.nn.sigmoid(g_ref[...].astype(F32)) * b_ref[...].astype(F32)
            acc = term if acc is None else acc + term
        o_ref[...] = acc.astype(o_ref.dtype)

    tile = pl.BlockSpec((tr, GATE_COLS), lambda i, j: (i, j))
    return pl.pallas_call(
        body, name=name, grid=(S // tr, GATE_BLOCKS),
        in_specs=[_gate_spec(tr, k) for k in range(N_BRANCH)] + [tile] * N_BRANCH, out_specs=tile,
        out_shape=jax.ShapeDtypeStruct((S, D_MODEL), BF16), compiler_params=_params(2))(gates, gates, gates, *branches)


def _merge_bwd(gates, branches, dmerged, *, name):
    S = gates.shape[0]
    tr = min(GATE_ROWS, S)

    def body(g0, g1, g2, b0, b1, b2, dm_ref, dg0, dg1, dg2, db0, db1, db2):
        dm = dm_ref[...].astype(F32)
        for g_ref, b_ref, dg_ref, db_ref in ((g0, b0, dg0, db0), (g1, b1, dg1, db1), (g2, b2, dg2, db2)):
            s = jax.nn.sigmoid(g_ref[...].astype(F32))
            db_ref[...] = (dm * s).astype(db_ref.dtype)
            dg_ref[...] = (dm * b_ref[...].astype(F32) * s * (1.0 - s)).astype(dg_ref.dtype)

    tile = pl.BlockSpec((tr, GATE_COLS), lambda i, j: (i, j))
    return pl.pallas_call(
        body, name=name, grid=(S // tr, GATE_BLOCKS),
        in_specs=[_gate_spec(tr, k) for k in range(N_BRANCH)] + [tile] * (N_BRANCH + 1), out_specs=[tile] * (2 * N_BRANCH),
        out_shape=[jax.ShapeDtypeStruct((S, D_MODEL), BF16)] * (2 * N_BRANCH),
        compiler_params=_params(2))(gates, gates, gates, *branches, dmerged)


TILE_BYTES = 24 * 1024 * 1024


BF16_ROWS = 16


def _tile_rows(rows, cols, n_arrays):
    padded = -(-cols // LANES) * LANES
    cap = max(BF16_ROWS, TILE_BYTES // (2 * n_arrays * padded * 4))
    best = None
    for tr in range(BF16_ROWS, min(rows, cap) + 1, BF16_ROWS):
        if rows % tr == 0:
            best = tr
    assert best is not None, (rows, cols)
    return best


def _sum_slots(stack, *, name):
    n, R, C = stack.shape
    tr = _tile_rows(R, C, n + 1)

    def body(s_ref, o_ref):
        acc = s_ref[0].astype(F32)
        for k in range(1, n):
            acc = acc + s_ref[k].astype(F32)
        o_ref[...] = acc

    return pl.pallas_call(
        body, name=name, grid=(R // tr,), in_specs=[pl.BlockSpec((n, tr, C), lambda i: (0, i, 0))],
        out_specs=_row_spec(tr, C), out_shape=jax.ShapeDtypeStruct((R, C), F32), compiler_params=_params(1))(stack)


def _add_own_half(parts, received, core, *, name):
    n, R, C = parts.shape
    half = R // 2
    tr = _tile_rows(half, C, 3)
    steps = half // tr

    def body(core_ref, own_ref, got_ref, o_ref):
        o_ref[...] = (own_ref[...] + got_ref[...]).astype(o_ref.dtype)

    tile = pl.BlockSpec((None, tr, C), lambda d, i, core_ref: (d, i, 0))
    own = pl.BlockSpec((None, tr, C), lambda d, i, core_ref: (d, core_ref[0] * steps + i, 0))
    return pl.pallas_call(
        body, name=name, out_shape=jax.ShapeDtypeStruct((n, half, C), BF16),
        grid_spec=pltpu.PrefetchScalarGridSpec(num_scalar_prefetch=1, grid=(n, steps), in_specs=[own, tile], out_specs=tile),
        compiler_params=_params(2))(core, parts, received)


def _adamw_math(w, m, v, g):
    m_new = ADAM_B1 * m + (1.0 - ADAM_B1) * g
    v_new = ADAM_B2 * v + (1.0 - ADAM_B2) * jnp.square(g)
    m_hat = m_new / (1.0 - ADAM_B1 ** ADAM_STEP)
    v_hat = v_new / (1.0 - ADAM_B2 ** ADAM_STEP)
    return -ADAM_LR * (m_hat / (jnp.sqrt(v_hat) + ADAM_EPS) + ADAM_WD * w), m_new, v_new


def _adamw_summed(w, m, v, gstack, *, name):
    R, C = w.shape
    n = gstack.shape[0]
    tr = _tile_rows(R, C, n + 7)

    def body(w_ref, m_ref, v_ref, g_ref, go_ref, d_ref, mo_ref, vo_ref):
        g = g_ref[0]
        for k in range(1, n):
            g = g + g_ref[k]
        go_ref[...] = g
        d_ref[...], mo_ref[...], vo_ref[...] = _adamw_math(w_ref[...], m_ref[...], v_ref[...], g)

    row = _row_spec(tr, C)
    return pl.pallas_call(
        body, name=name, grid=(R // tr,), in_specs=[row, row, row, pl.BlockSpec((n, tr, C), lambda i: (0, i, 0))],
        out_specs=[row] * 4, out_shape=[jax.ShapeDtypeStruct((R, C), F32)] * 4, compiler_params=_params(1))(w, m, v, gstack)


def _adamw_halves(w, m, v, mine, theirs, core, *, name):
    R, C = w.shape
    tr = _tile_rows(R // 2, C, 9)
    steps = R // 2 // tr

    def body(core_ref, w_ref, m_ref, v_ref, mine_ref, theirs_ref, go_ref, d_ref, mo_ref, vo_ref):
        in_my_half = pl.program_id(0) // steps == core_ref[0]
        g = jnp.where(in_my_half, mine_ref[...], theirs_ref[...])
        go_ref[...] = g
        d_ref[...], mo_ref[...], vo_ref[...] = _adamw_math(w_ref[...], m_ref[...], v_ref[...], g)

    row = pl.BlockSpec((tr, C), lambda i, core_ref: (i, 0))
    mine_spec = pl.BlockSpec((tr, C), lambda i, core_ref: (jnp.clip(i - core_ref[0] * steps, 0, steps - 1), 0))
    theirs_spec = pl.BlockSpec((tr, C), lambda i, core_ref: (jnp.clip(i - (1 - core_ref[0]) * steps, 0, steps - 1), 0))
    return pl.pallas_call(
        body, name=name, out_shape=[jax.ShapeDtypeStruct((R, C), F32)] * 4,
        grid_spec=pltpu.PrefetchScalarGridSpec(
            num_scalar_prefetch=1, grid=(R // tr,), in_specs=[row, row, row, mine_spec, theirs_spec], out_specs=[row] * 4),
        compiler_params=_params(1))(core, w, m, v, mine, theirs)


HBM_SPEC = pl.BlockSpec(memory_space=pl.ANY)


def _position():
    return lax.axis_index("x"), lax.axis_index("y"), lax.axis_index("c")


def _other_chips(x, y):
    return [(1 - x, y), (x, 1 - y), (1 - x, 1 - y)]


def _chip_exchange(arrays, *, scatter, name):
    n = len(arrays)

    def body(*refs):
        ins, outs = refs[:n], refs[n:2 * n]
        send_sems, recv_sems, local_sems = refs[2 * n:]
        x, y, c = _position()
        me = 2 * x + y
        copies = []
        for a in range(n):
            own = ins[a].at[me] if scatter else ins[a]
            local = pltpu.make_async_copy(own, outs[a].at[me], local_sems.at[a])
            local.start()
            copies.append(local)
            for k, (px, py) in enumerate(_other_chips(x, y)):
                src = ins[a].at[2 * px + py] if scatter else ins[a]
                remote = pltpu.make_async_remote_copy(
                    src_ref=src, dst_ref=outs[a].at[me], send_sem=send_sems.at[3 * a + k],
                    recv_sem=recv_sems.at[3 * a + k], device_id=(px, py, c), device_id_type=MESH)
                remote.start()
                copies.append(remote)
        for cp in copies:
            cp.wait()

    out_shapes = [jax.ShapeDtypeStruct(a.shape if scatter else (N_CHIPS,) + a.shape, a.dtype) for a in arrays]
    return pl.pallas_call(
        body, name=name, in_specs=[HBM_SPEC] * n, out_specs=[HBM_SPEC] * n, out_shape=out_shapes,
        scratch_shapes=[pltpu.SemaphoreType.DMA((3 * n,)), pltpu.SemaphoreType.DMA((3 * n,)), pltpu.SemaphoreType.DMA((n,))],
    )(*arrays)


def _sibling_swap(arrays, *, name):
    n = len(arrays)

    def body(*refs):
        ins, outs = refs[:n], refs[n:2 * n]
        send_sems, recv_sems = refs[2 * n:]
        x, y, c = _position()
        copies = []
        for a in range(n):
            cp = pltpu.make_async_remote_copy(
                src_ref=ins[a], dst_ref=outs[a], send_sem=send_sems.at[a], recv_sem=recv_sems.at[a],
                device_id=(x, y, 1 - c), device_id_type=MESH)
            cp.start()
            copies.append(cp)
        for cp in copies:
            cp.wait()

    return pl.pallas_call(
        body, name=name, in_specs=[HBM_SPEC] * n, out_specs=[HBM_SPEC] * n,
        out_shape=[jax.ShapeDtypeStruct(a.shape, a.dtype) for a in arrays],
        scratch_shapes=[pltpu.SemaphoreType.DMA((n,)), pltpu.SemaphoreType.DMA((n,))],
    )(*arrays)


def _sibling_other_half(arrays, *, name):
    n = len(arrays)

    def body(*refs):
        ins, outs = refs[:n], refs[n:2 * n]
        send_sems, recv_sems = refs[2 * n:]
        x, y, c = _position()
        copies = []
        for a in range(n):
            half = ins[a].shape[1] // 2
            theirs = ins[a].at[:, pl.ds(pl.multiple_of((1 - c) * half, BF16_ROWS), half), :]
            cp = pltpu.make_async_remote_copy(
                src_ref=theirs, dst_ref=outs[a], send_sem=send_sems.at[a], recv_sem=recv_sems.at[a],
                device_id=(x, y, 1 - c), device_id_type=MESH)
            cp.start()
            copies.append(cp)
        for cp in copies:
            cp.wait()

    return pl.pallas_call(
        body, name=name, in_specs=[HBM_SPEC] * n, out_specs=[HBM_SPEC] * n,
        out_shape=[jax.ShapeDtypeStruct((a.shape[0], a.shape[1] // 2, a.shape[2]), a.dtype) for a in arrays],
        scratch_shapes=[pltpu.SemaphoreType.DMA((n,)), pltpu.SemaphoreType.DMA((n,))],
    )(*arrays)


def _gather_all(buf, *, name):
    R, C = buf.shape

    def body(in_ref, out_ref, send_sems, recv_sems):
        x, y, c = _position()
        out_ref[4 * x + 2 * y + c] = in_ref[...]
        copies = []
        for k in range(1, N_DEV):
            px = 1 - x if k & 4 else x
            py = 1 - y if k & 2 else y
            pc = 1 - c if k & 1 else c
            cp = pltpu.make_async_remote_copy(
                src_ref=in_ref, dst_ref=out_ref.at[4 * x + 2 * y + c], send_sem=send_sems.at[k - 1],
                recv_sem=recv_sems.at[k - 1], device_id=(px, py, pc), device_id_type=MESH)
            cp.start()
            copies.append(cp)
        for cp in copies:
            cp.wait()

    vmem = pl.BlockSpec(memory_space=pltpu.VMEM)
    return pl.pallas_call(
        body, name=name, in_specs=[vmem], out_specs=vmem, out_shape=jax.ShapeDtypeStruct((N_DEV, R, C), buf.dtype),
        scratch_shapes=[pltpu.SemaphoreType.DMA((N_DEV - 1,)), pltpu.SemaphoreType.DMA((N_DEV - 1,))],
        compiler_params=pltpu.CompilerParams(vmem_limit_bytes=VMEM_LIMIT),
    )(buf)


def _relu2(p):
    return p, jnp.square(jnp.maximum(p, 0.0))


def _relu2_grad(p, a):
    return (p * (2.0 * jnp.maximum(a.astype(F32), 0.0)),)


def _mixer_constants(w_pool, w_spatial, b_spatial):
    eye = jnp.eye(len(POOL_WINDOWS), dtype=F32)
    w_bd = (eye[:, None, :, None] * w_pool[:, :, None, :]).reshape(POOL_WIDTH, POOL_WIDTH).astype(BF16)
    causal = jnp.tril(jnp.ones((GM_CHUNK, GM_CHUNK), dtype=bool))
    ws = jnp.where(causal[None], w_spatial, 0.0).astype(BF16)
    ws_cat = ws.transpose(1, 0, 2).reshape(GM_CHUNK, GM_GROUPS * GM_CHUNK)
    wst_cat = ws.transpose(2, 0, 1).reshape(GM_CHUNK, GM_GROUPS * GM_CHUNK)
    bias = jnp.repeat(b_spatial.T, GM_GROUP_DIM, axis=1)
    return w_bd, ws_cat, wst_cat, bias


def _local_step(x, target, big, small):
    L = small["g_mix_pre"].shape[0]
    vec = lambda name, l: small[name][l][None, :]
    consts = [_mixer_constants(small["w_pool"][l], small["w_spatial"][l], small["b_spatial"][l]) for l in range(L)]
    saved = []
    h = _rms_fwd(x, vec("g_mix_pre", 0), name="rms_in")
    for l in range(L):
        w_bd, ws_cat, wst_cat, bias = consts[l]
        proj = lambda n, off, dtype, name: _matmul(h, big["w_in"][l], n=n, bn=PROJ_BLOCK, b_col_off=off // PROJ_BLOCK,
                                                   out_dtypes=(dtype,), name=name)
        qkv = proj(QKV_WIDTH, 0, BF16, "proj_qkv")
        rest = proj(MIX_WIDTH, QKV_WIDTH, F32, "proj_mix")
        gates = proj(GATE_WIDTH, QKV_WIDTH + MIX_WIDTH, BF16, "proj_gates")
        o_sb, tot, first = _sba_fwd(qkv, name="sba_fwd")
        o_pool = _pool_fwd(rest, w_bd, vec("pool_scale", l), name="pool_fwd")
        o_gm = _gm_fwd(rest, vec("gm_gain", l), ws_cat, bias, name="gm_fwd")
        branches = (_matmul(o_sb, big["w_br_sb"][l], out_dtypes=(BF16,), name="br_sb"),
                    _matmul(o_pool, big["w_br_pool"][l], out_dtypes=(BF16,), name="br_pool"),
                    _matmul(o_gm, big["w_br_gm"][l], out_dtypes=(BF16,), name="br_gm"))
        merged = _merge_fwd(gates, branches, name="merge_fwd")
        y = _matmul(merged, big["w_out"][l], name="out_proj")
        x1, h2 = _resid_rms(x, y, vec("g_mix_post", l), vec("g_ff_pre", l), name="resid_mix")
        a, r = _matmul(h2, big["w_ff_in"][l], out_dtypes=(BF16, BF16), epilogue=_relu2, name="ff_in")
        ff = _matmul(r, big["w_ff_out"][l], name="ff_out")
        g_next = vec("g_mix_pre", l + 1) if l + 1 < L else None
        x2, h_next = _resid_rms(x1, ff, vec("g_ff_post", l), g_next, name="resid_ff" if l + 1 < L else "resid_last")
        saved.append(dict(x=x, h=h, qkv=qkv, rest=rest, gates=gates, o_sb=o_sb, tot=tot, first=first, o_pool=o_pool,
                          o_gm=o_gm, branches=branches, merged=merged, y=y, x1=x1, h2=h2, a=a, r=r, ff=ff))
        x, h = x2, h_next

    dx2, loss = _loss_head(x, target, name="loss_head")
    gb = {k: [None] * L for k in ("w_in", "w_br_sb", "w_br_pool", "w_br_gm", "w_out", "w_ff_in", "w_ff_out")}
    gs = {k: [None] * L for k in ("w_pool", "pool_scale", "gm_gain", "w_spatial", "b_spatial", "g_mix_pre",
                                  "g_mix_post", "g_ff_pre", "g_ff_post")}
    d_ff, gs["g_ff_post"][L - 1] = _rms_bwd(saved[-1]["ff"], vec("g_ff_post", L - 1), dx2, name="rms_bwd_last")
    for l in reversed(range(L)):
        s = saved[l]
        w_bd, ws_cat, wst_cat, bias = consts[l]
        da = _matmul(d_ff, big["w_ff_out"][l], tb=True, out_dtypes=(BF16,), extras=(s["a"],), epilogue=_relu2_grad,
                     name="ff_out_dx")
        gb["w_ff_out"][l] = _matmul(s["r"], d_ff, ta=True, name="ff_out_dw")
        dh2 = _matmul(da, big["w_ff_in"][l], tb=True, name="ff_in_dx")
        gb["w_ff_in"][l] = _matmul(s["h2"], da, ta=True, name="ff_in_dw")
        dx1, gs["g_ff_pre"][l], dy, gs["g_mix_post"][l] = _rms_bwd_chain(
            s["x1"], vec("g_ff_pre", l), dh2, dx2, s["y"], vec("g_mix_post", l), name="rms_bwd_mid")
        dmerged = _matmul(dy, big["w_out"][l], tb=True, out_dtypes=(BF16,), name="out_proj_dx")
        gb["w_out"][l] = _matmul(s["merged"], dy, ta=True, name="out_proj_dw")
        dg0, dg1, dg2, db_sb, db_pool, db_gm = _merge_bwd(s["gates"], s["branches"], dmerged, name="merge_bwd")
        do_sb = _matmul(db_sb, big["w_br_sb"][l], tb=True, out_dtypes=(BF16,), name="br_sb_dx")
        gb["w_br_sb"][l] = _matmul(s["o_sb"], db_sb, ta=True, name="br_sb_dw")
        do_pool = _matmul(db_pool, big["w_br_pool"][l], tb=True, name="br_pool_dx")
        gb["w_br_pool"][l] = _matmul(s["o_pool"], db_pool, ta=True, name="br_pool_dw")
        do_gm = _matmul(db_gm, big["w_br_gm"][l], tb=True, name="br_gm_dx")
        gb["w_br_gm"][l] = _matmul(s["o_gm"], db_gm, ta=True, name="br_gm_dw")
        dq, dk, dv = _sba_bwd(s["qkv"], do_sb, s["tot"], s["first"], name="sba_bwd")
        dp, dw_bd, gs["pool_scale"][l] = _pool_bwd(s["rest"], do_pool, w_bd, vec("pool_scale", l), name="pool_bwd")
        du, dgv, gs["gm_gain"][l], dws, db = _gm_bwd(s["rest"], do_gm, vec("gm_gain", l), ws_cat, wst_cat, bias,
                                                      name="gm_bwd")
        gs["w_pool"][l] = jnp.stack([dw_bd[g * 64:(g + 1) * 64, g * 64:(g + 1) * 64] for g in range(len(POOL_WINDOWS))])
        gs["w_spatial"][l] = jnp.where(jnp.tril(jnp.ones((GM_CHUNK, GM_CHUNK), dtype=bool))[None], dws, 0.0)
        gs["b_spatial"][l] = db[:, :GM_GROUPS].T
        dproj = jnp.concatenate([dq, dk, dv, dp, du, dgv, dg0, dg1, dg2], axis=1)
        dh = _matmul(dproj, big["w_in"][l], tb=True, bk=D_IN // 3, name="proj_dx")
        gb["w_in"][l] = _matmul(s["h"], dproj, ta=True, bn=768, name="proj_dw")
        if l > 0:
            dx2, gs["g_mix_pre"][l], d_ff, gs["g_ff_post"][l - 1] = _rms_bwd_chain(
                s["x"], vec("g_mix_pre", l), dh, dx1, saved[l - 1]["ff"], vec("g_ff_post", l - 1), name="rms_bwd_mid")
        else:
            dx2, gs["g_mix_pre"][l], _, _ = _rms_bwd_chain(s["x"], vec("g_mix_pre", l), dh, dx1, None, None,
                                                           name="rms_bwd_first")
    small_grads = {k: jnp.stack([g.reshape(small[k].shape[1:]) for g in v]) for k, v in gs.items()}
    return loss, dx2, gb, small_grads


COLUMN_SHARDED = ("w_in", "w_br_sb", "w_br_pool", "w_br_gm", "w_ff_in")
ROW_SHARDED = ("w_out", "w_ff_out")
BIG_WEIGHTS = COLUMN_SHARDED + ROW_SHARDED
SMALL_WEIGHTS = ("w_pool", "pool_scale", "gm_gain", "w_spatial", "b_spatial", "g_mix_pre", "g_mix_post", "g_ff_pre",
                 "g_ff_post")
WEIGHT_ORDER = ("w_in", "w_pool", "pool_scale", "gm_gain", "w_spatial", "b_spatial", "w_br_sb", "w_br_pool", "w_br_gm",
                "w_out", "g_mix_pre", "g_mix_post", "g_ff_pre", "g_ff_post", "w_ff_in", "w_ff_out")


def _full_weight(name, mine, theirs, core, l):
    a, b = mine[:, l], theirs[:, l]
    g = jnp.stack([jnp.where(core == 0, a, b), jnp.where(core == 0, b, a)])
    half, cols = g.shape[2], g.shape[3]
    if name in COLUMN_SHARDED:
        return g.transpose(0, 2, 1, 3).reshape(2 * half, N_CHIPS * cols)
    return g.transpose(1, 0, 2, 3).reshape(N_CHIPS * 2 * half, cols)


def _parts_by_chip(name, grads):
    g = jnp.stack(grads)
    L = g.shape[0]
    if name in COLUMN_SHARDED:
        r, c = g.shape[1], g.shape[2] // N_CHIPS
        return g.reshape(L, r, N_CHIPS, c).transpose(2, 0, 1, 3).reshape(N_CHIPS, L * r, c)
    r, c = g.shape[1] // N_CHIPS, g.shape[2]
    return g.reshape(L, N_CHIPS, r, c).transpose(1, 0, 2, 3).reshape(N_CHIPS, L * r, c)


def _pack(arrays):
    flat = jnp.concatenate([a.reshape(-1) for a in arrays])
    return flat.reshape(-1, LANES)


def _unpack(buf, like):
    flat, out, at = buf.reshape(-1), [], 0
    for a in like:
        out.append(flat[at:at + a.size].reshape(a.shape))
        at += a.size
    return out


def kernel(x, w_in, w_pool, pool_scale, gm_gain, w_spatial, b_spatial, w_br_sb, w_br_pool, w_br_gm, w_out, g_mix_pre, g_mix_post, g_ff_pre, g_ff_post, w_ff_in, w_ff_out, loss_target, m_w_in, m_w_pool, m_pool_scale, m_gm_gain, m_w_spatial, m_b_spatial, m_w_br_sb, m_w_br_pool, m_w_br_gm, m_w_out, m_g_mix_pre, m_g_mix_post, m_g_ff_pre, m_g_ff_post, m_w_ff_in, m_w_ff_out, v_w_in, v_w_pool, v_pool_scale, v_gm_gain, v_w_spatial, v_b_spatial, v_w_br_sb, v_w_br_pool, v_w_br_gm, v_w_out, v_g_mix_pre, v_g_mix_post, v_g_ff_pre, v_g_ff_post, v_w_ff_in, v_w_ff_out):
    given = dict(locals())
    w = {n: given[n] for n in WEIGHT_ORDER}
    m = {n: given["m_" + n] for n in WEIGHT_ORDER}
    v = {n: given["v_" + n] for n in WEIGHT_ORDER}
    L = w_in.shape[0]

    core = lax.axis_index("c")

    def my_rows(a):
        half = a.shape[1] // 2
        return lax.dynamic_slice_in_dim(a.astype(BF16), core * half, half, axis=1)

    halves = _chip_exchange([my_rows(w[n]) for n in BIG_WEIGHTS], scatter=False, name="gather_weights")
    other_halves = _sibling_swap(halves, name="swap_weight_halves")
    big = {n: [_full_weight(n, a, b, core, l) for l in range(L)] for n, a, b in zip(BIG_WEIGHTS, halves, other_halves)}
    small = {n: w[n] for n in SMALL_WEIGHTS}

    loss, dx, big_grads, small_grads = _local_step(x[0], loss_target[0], big, small)

    parts = [_parts_by_chip(n, big_grads[n]) for n in BIG_WEIGHTS]
    from_sibling = _sibling_other_half(parts, name="swap_grad_halves")
    core_index = core.astype(jnp.int32).reshape(1)
    chip_sums = [_add_own_half(p, s, core_index, name="sum_cores") for p, s in zip(parts, from_sibling)]
    received = _chip_exchange(chip_sums, scatter=True, name="exchange_grads")
    reduced = [_sum_slots(r, name="sum_chips") for r in received]
    reduced_by_sibling = _sibling_swap(reduced, name="swap_reduced_halves")
    grads, deltas, new_m, new_v = {}, {}, {}, {}
    for n, mine, theirs in zip(BIG_WEIGHTS, reduced, reduced_by_sibling):
        shape = w[n].shape
        flat = lambda a: a.reshape(-1, shape[-1])
        outs = _adamw_halves(flat(w[n]), flat(m[n]), flat(v[n]), mine, theirs, core_index, name="adamw_big")
        grads[n], deltas[n], new_m[n], new_v[n] = [o.reshape(shape) for o in outs]

    like = [w[n] for n in SMALL_WEIGHTS]
    all_partials = _gather_all(_pack([small_grads[n] for n in SMALL_WEIGHTS]), name="gather_small_grads")
    outs = _adamw_summed(_pack(like), _pack([m[n] for n in SMALL_WEIGHTS]), _pack([v[n] for n in SMALL_WEIGHTS]),
                         all_partials, name="adamw_small")
    for store, buf in zip((grads, deltas, new_m, new_v), outs):
        store.update(zip(SMALL_WEIGHTS, _unpack(buf, like)))

    total_loss = lax.psum(loss[0, 0], ("x", "y", "c"))
    return (total_loss, dx[None], *[grads[n] for n in WEIGHT_ORDER], *[deltas[n] for n in WEIGHT_ORDER],
            *[new_m[n] for n in WEIGHT_ORDER], *[new_v[n] for n in WEIGHT_ORDER])
```

```python
import functools
import math

import jax
import jax.numpy as jnp
from jax import lax
from jax.experimental import pallas as pl
from jax.experimental.pallas import tpu as pltpu

F32 = jnp.float32
BF16 = jnp.bfloat16

D_MODEL = 1024
SB_HEADS = 8
SB_HEAD_DIM = 64
SB_WIDTH = SB_HEADS * SB_HEAD_DIM
POOL_WINDOWS = (2, 4, 8, 16)
POOL_GROUP_DIM = 64
POOL_WIDTH = 256
POOL_HALO = 16
GM_GROUPS = 4
GM_GROUP_DIM = 64
GM_WIDTH = 256
GM_CHUNK = 128
N_BRANCH = 3
D_FF = 4 * D_MODEL
RMS_EPS = 1e-6
QKV_WIDTH = 3 * SB_WIDTH
MIX_WIDTH = POOL_WIDTH + 2 * GM_WIDTH
GATE_WIDTH = N_BRANCH * D_MODEL
D_IN = QKV_WIDTH + MIX_WIDTH + GATE_WIDTH
PROJ_BLOCK = 768
LANES = 128
N_CHIPS = 4
N_DEV = 8

ADAM_LR = 0.001
ADAM_B1 = 0.9
ADAM_B2 = 0.999
ADAM_EPS = 1e-08
ADAM_WD = 0.01
ADAM_STEP = 10

VMEM_LIMIT = 56 * 1024 * 1024
MESH = pl.DeviceIdType.MESH


def _params(n_grid):
    return pltpu.CompilerParams(dimension_semantics=("arbitrary",) * n_grid, vmem_limit_bytes=VMEM_LIMIT)


def _bf(x):
    return x if x.dtype == BF16 else x.astype(BF16)


def _matmul(a, b, *, name, ta=False, tb=False, out_dtypes=(F32,), n=None, b_col_off=0, bm=1024, bn=1024, bk=2048,
            extras=(), epilogue=None):
    M, K = (a.shape[1], a.shape[0]) if ta else a.shape
    nb = b.shape[0] if tb else b.shape[1]
    n = nb if n is None else n
    bm, bn, bk = min(bm, M), min(bn, n), min(bk, K)
    assert M % bm == 0 and n % bn == 0 and K % bk == 0, (name, M, n, K, bm, bn, bk)
    assert (b.shape[1] if tb else b.shape[0]) == K, (name, a.shape, b.shape)
    nk = K // bk
    dims = (((0 if ta else 1,), (1 if tb else 0,)), ((), ()))
    n_out = len(out_dtypes)
    direct = nk > 1 and epilogue is None and out_dtypes == (F32,)
    use_acc = nk > 1 and not direct

    def body(*refs):
        a_ref, b_ref = refs[:2]
        extra_refs = refs[2:2 + len(extras)]
        out_refs = refs[2 + len(extras):2 + len(extras) + n_out]
        p = lax.dot_general(_bf(a_ref[...]), _bf(b_ref[...]), dims, preferred_element_type=F32)

        def finish(acc):
            outs = (acc,) if epilogue is None else epilogue(acc, *[r[...] for r in extra_refs])
            for r, o in zip(out_refs, outs):
                r[...] = o.astype(r.dtype)

        if nk == 1:
            finish(p)
            return
        k = pl.program_id(2)
        acc_ref = out_refs[0] if direct else refs[-1]

        @pl.when(k == 0)
        def _():
            acc_ref[...] = p

        @pl.when(k > 0)
        def _():
            acc_ref[...] += p

        if use_acc:
            @pl.when(k == nk - 1)
            def _():
                finish(acc_ref[...])

    a_spec = pl.BlockSpec((bk, bm), lambda i, j, k: (k, i)) if ta else pl.BlockSpec((bm, bk), lambda i, j, k: (i, k))
    if tb:
        assert b_col_off == 0
        b_spec = pl.BlockSpec((bn, bk), lambda i, j, k: (j, k))
    else:
        b_spec = pl.BlockSpec((bk, bn), lambda i, j, k: (k, j + b_col_off))
    tile = pl.BlockSpec((bm, bn), lambda i, j, k: (i, j))
    outs = pl.pallas_call(
        body, name=name, grid=(M // bm, n // bn, nk),
        in_specs=[a_spec, b_spec] + [tile] * len(extras),
        out_specs=[tile] * n_out,
        out_shape=[jax.ShapeDtypeStruct((M, n), d) for d in out_dtypes],
        scratch_shapes=[pltpu.VMEM((bm, bn), F32)] if use_acc else [],
        compiler_params=_params(3),
    )(a, b, *extras)
    return outs[0] if n_out == 1 else outs


ROW_TILE = 512


def _rows(S):
    tr = min(ROW_TILE, S)
    assert S % tr == 0
    return tr


def _rstd(x):
    return lax.rsqrt(jnp.mean(x * x, axis=-1, keepdims=True) + RMS_EPS)


def _rms_bwd_math(x, g, dy):
    r = _rstd(x)
    gd = g * dy
    dx = r * gd - x * (r * r * r) * jnp.mean(x * gd, axis=-1, keepdims=True)
    dg = jnp.sum(dy * x * r, axis=0, keepdims=True)
    return dx, dg


def _accumulate(ref, value):
    i = pl.program_id(0)

    @pl.when(i == 0)
    def _():
        ref[...] = value

    @pl.when(i > 0)
    def _():
        ref[...] += value


def _row_spec(tr, width):
    return pl.BlockSpec((tr, width), lambda i: (i, 0))


def _vec_spec(width):
    return pl.BlockSpec((1, width), lambda i: (0, 0))


def _rms_fwd(x, g, *, name):
    S, D = x.shape
    tr = _rows(S)

    def body(x_ref, g_ref, o_ref):
        xf = x_ref[...]
        o_ref[...] = (xf * _rstd(xf) * g_ref[...]).astype(o_ref.dtype)

    return pl.pallas_call(
        body, name=name, grid=(S // tr,), in_specs=[_row_spec(tr, D), _vec_spec(D)], out_specs=_row_spec(tr, D),
        out_shape=jax.ShapeDtypeStruct((S, D), BF16), compiler_params=_params(1))(x, g)


def _resid_rms(x, y, g_post, g_next, *, name):
    S, D = x.shape
    tr = _rows(S)
    with_next = g_next is not None

    def body(*refs):
        if with_next:
            x_ref, y_ref, gp_ref, gn_ref, xo_ref, ho_ref = refs
        else:
            x_ref, y_ref, gp_ref, xo_ref = refs
        yf = y_ref[...]
        xn = x_ref[...] + yf * _rstd(yf) * gp_ref[...]
        xo_ref[...] = xn
        if with_next:
            ho_ref[...] = (xn * _rstd(xn) * gn_ref[...]).astype(ho_ref.dtype)

    row, vec = _row_spec(tr, D), _vec_spec(D)
    ins = [x, y, g_post] + ([g_next] if with_next else [])
    outs = pl.pallas_call(
        body, name=name, grid=(S // tr,), in_specs=[row, row, vec] + ([vec] if with_next else []),
        out_specs=[row] + ([row] if with_next else []),
        out_shape=[jax.ShapeDtypeStruct((S, D), F32)] + ([jax.ShapeDtypeStruct((S, D), BF16)] if with_next else []),
        compiler_params=_params(1))(*ins)
    return (outs[0], outs[1]) if with_next else (outs[0], None)


def _rms_bwd(x, g, dy, *, name):
    S, D = x.shape
    tr = _rows(S)

    def body(x_ref, g_ref, dy_ref, dx_ref, dg_ref):
        dx, dg = _rms_bwd_math(x_ref[...], g_ref[...], dy_ref[...])
        dx_ref[...] = dx.astype(dx_ref.dtype)
        _accumulate(dg_ref, dg)

    row, vec = _row_spec(tr, D), _vec_spec(D)
    return pl.pallas_call(
        body, name=name, grid=(S // tr,), in_specs=[row, vec, row], out_specs=[row, vec],
        out_shape=[jax.ShapeDtypeStruct((S, D), BF16), jax.ShapeDtypeStruct((1, D), F32)],
        compiler_params=_params(1))(x, g, dy)


def _rms_bwd_chain(xa, ga, da, resid, xb, gb, *, name):
    S, D = xa.shape
    tr = _rows(S)
    chain = xb is not None

    def body(*refs):
        if chain:
            xa_ref, ga_ref, da_ref, rs_ref, xb_ref, gb_ref, dx_ref, dga_ref, dxb_ref, dgb_ref = refs
        else:
            xa_ref, ga_ref, da_ref, rs_ref, dx_ref, dga_ref = refs
        dxa, dga = _rms_bwd_math(xa_ref[...], ga_ref[...], da_ref[...])
        dx = rs_ref[...] + dxa
        dx_ref[...] = dx
        _accumulate(dga_ref, dga)
        if chain:
            dxb, dgb = _rms_bwd_math(xb_ref[...], gb_ref[...], dx)
            dxb_ref[...] = dxb.astype(dxb_ref.dtype)
            _accumulate(dgb_ref, dgb)

    row, vec = _row_spec(tr, D), _vec_spec(D)
    ins = [xa, ga, da, resid] + ([xb, gb] if chain else [])
    outs = pl.pallas_call(
        body, name=name, grid=(S // tr,), in_specs=[row, vec, row, row] + ([row, vec] if chain else []),
        out_specs=[row, vec] + ([row, vec] if chain else []),
        out_shape=[jax.ShapeDtypeStruct((S, D), F32), jax.ShapeDtypeStruct((1, D), F32)]
        + ([jax.ShapeDtypeStruct((S, D), BF16), jax.ShapeDtypeStruct((1, D), F32)] if chain else []),
        compiler_params=_params(1))(*ins)
    return tuple(outs) if chain else (outs[0], outs[1], None, None)


def _loss_head(y, target, *, name):
    S, D = y.shape
    tr = _rows(S)
    n_tiles = S // tr

    def body(y_ref, t_ref, dy_ref, loss_ref, acc_ref):
        err = y_ref[...] - t_ref[...]
        dy_ref[...] = err * (1.0 / D)
        _accumulate(acc_ref, jnp.sum(err * err, axis=0, keepdims=True))

        @pl.when(pl.program_id(0) == n_tiles - 1)
        def _():
            loss_ref[...] = jnp.sum(acc_ref[...], axis=1, keepdims=True) * (0.5 / D)

    row = _row_spec(tr, D)
    return pl.pallas_call(
        body, name=name, grid=(n_tiles,), in_specs=[row, row],
        out_specs=[row, pl.BlockSpec((1, 1), lambda i: (0, 0))],
        out_shape=[jax.ShapeDtypeStruct((S, D), F32), jax.ShapeDtypeStruct((1, 1), F32)],
        scratch_shapes=[pltpu.VMEM((1, D), F32)], compiler_params=_params(1))(y, target)


SB_TILE = 256
SB_PAIRS = SB_HEADS * SB_HEAD_DIM // LANES
SB_DEAD_LOG = -110.0


def _log_sigmoids(z):
    l1p = jnp.log(1.0 + jnp.exp(-jnp.abs(z)))
    return jnp.minimum(z, 0.0) - l1p, jnp.minimum(-z, 0.0) - l1p


def _split_bf16(x):
    hi = x.astype(BF16)
    lo = (x - hi.astype(F32)).astype(BF16)
    return jnp.concatenate([hi, lo], axis=1)


def _tri(T, cmp):
    j = lax.broadcasted_iota(jnp.int32, (T, T), 0)
    s = lax.broadcasted_iota(jnp.int32, (T, T), 1)
    m = jnp.where(cmp(j, s), 1.0, 0.0).astype(BF16)
    return jnp.concatenate([m, m], axis=0)


def _head_masks():
    lane = lax.broadcasted_iota(jnp.int32, (1, LANES), 1)
    return [lane < SB_HEAD_DIM, lane >= SB_HEAD_DIM]


def _cargo(refs, n_in, n_out, cargo, scatter):
    n = len(cargo)
    if not n:
        return refs, lambda first: None, lambda last: None
    ins = refs[n_in:n_in + n]
    outs = refs[n_in + n + n_out:n_in + n + n_out + n]
    sems = refs[len(refs) - 3:]
    own = refs[:n_in] + refs[n_in + n:n_in + n + n_out] + refs[n_in + n + n_out + n:len(refs) - 3]

    def start(first):
        @pl.when(first)
        def _():
            for cp in _chip_copies(ins, outs, *sems, scatter=scatter):
                cp.start()

    def finish(last):
        @pl.when(last)
        def _():
            for cp in _chip_copies(ins, outs, *sems, scatter=scatter):
                cp.wait()

    return own, start, finish


def _sba_fwd(qkv, *, name, cargo=()):
    S = qkv.shape[0]
    T = min(SB_TILE, S)
    nq = S // T
    scale = SB_HEAD_DIM ** -0.5

    def body(*refs):
        (q_ref, k_ref, v_ref, o_ref, t_ref, first_ref), start_cargo, finish_cargo = _cargo(refs, 3, 3, cargo, False)
        p, i = pl.program_id(0), pl.program_id(1)
        start_cargo(jnp.logical_and(p == 0, i == 0))
        row = lax.broadcasted_iota(jnp.int32, (T, T), 0)
        col = lax.broadcasted_iota(jnp.int32, (T, T), 1)
        strict = col < row
        after = _tri(T, lambda j, s: j > s)
        masks = _head_masks()
        q = q_ref[...] * scale
        qs = [jnp.where(hm, q, jnp.zeros_like(q)) for hm in masks]

        def walk(tiles, carry):
            values, logs = [], []
            for j, diag in tiles:
                rows = pl.ds(pl.multiple_of(j * T, T), T)
                kb = k_ref[rows, :]
                values.append(v_ref[rows, :])
                for qh in qs:
                    z = lax.dot_general(qh, kb, (((1,), (1,)), ((), ())), preferred_element_type=F32)
                    ls, ln = _log_sigmoids(z)
                    logs.append((ls, jnp.where(strict, ln, 0.0) if diag else ln))
            suffixes = [jnp.dot(_split_bf16(ln), after, preferred_element_type=F32) for _, ln in logs]
            for t, (_, diag) in enumerate(tiles):
                out = []
                for h, (C, acc) in enumerate(carry):
                    ls, ln = logs[2 * t + h]
                    a = jnp.exp(ls + suffixes[2 * t + h] + C)
                    if diag:
                        a = jnp.where(strict, a, 0.0)
                    acc = acc + jnp.dot(a.astype(BF16), values[t], preferred_element_type=F32)
                    out.append((C + jnp.sum(ln, axis=1, keepdims=True), acc))
                carry = tuple(out)
            return carry

        fresh = (jnp.zeros((T, 1), F32), jnp.zeros((T, LANES), F32))
        carry = lax.cond(i > 0, lambda: walk([(i, True), (i - 1, False)], (fresh, fresh)),
                         lambda: walk([(i, True)], (fresh, fresh)))

        def alive(state):
            j, ((C0, _), (C1, _)) = state
            return jnp.logical_and(j >= 0, jnp.max(jnp.maximum(C0, C1)) > SB_DEAD_LOG)

        def step(state):
            j, carry = state
            return j - 1, walk([(j, False)], carry)

        j, ((C0, acc0), (C1, acc1)) = lax.while_loop(alive, step, (i - 2, carry))
        t_ref[0] = jnp.broadcast_to(C0, (T, LANES))
        t_ref[1] = jnp.broadcast_to(C1, (T, LANES))
        first_ref[...] = jnp.full((8, LANES), jnp.maximum(j + 1, 0).astype(F32))
        o_ref[...] = jnp.where(masks[0], acc0, acc1).astype(o_ref.dtype)
        finish_cargo(jnp.logical_and(p == SB_PAIRS - 1, i == nq - 1))

    kv = lambda off: pl.BlockSpec((S, LANES), lambda p, i: (0, off + p))
    n = len(cargo)
    return pl.pallas_call(
        body, name=name, grid=(SB_PAIRS, nq),
        in_specs=[pl.BlockSpec((T, LANES), lambda p, i: (i, p)), kv(SB_PAIRS), kv(2 * SB_PAIRS)] + [HBM_SPEC] * n,
        out_specs=[pl.BlockSpec((T, LANES), lambda p, i: (i, p)), pl.BlockSpec((2, T, LANES), lambda p, i: (p, i, 0)),
                   pl.BlockSpec((None, None, 8, LANES), lambda p, i: (p, i, 0, 0))] + [HBM_SPEC] * n,
        out_shape=[jax.ShapeDtypeStruct((S, SB_WIDTH), BF16), jax.ShapeDtypeStruct((SB_HEADS, S, LANES), F32),
                   jax.ShapeDtypeStruct((SB_PAIRS, nq, 8, LANES), F32)] + _chip_exchange_shapes(cargo, False),
        scratch_shapes=_chip_exchange_semaphores(n) if n else [],
        compiler_params=_params(2))(qkv, qkv, qkv, *cargo)


def _sba_bwd(qkv, do, tot, first, *, name, cargo=()):
    S = qkv.shape[0]
    T = min(SB_TILE, S)
    nq = S // T
    scale = SB_HEAD_DIM ** -0.5

    def body(*refs):
        own, start_cargo, finish_cargo = _cargo(refs, 6, 3, cargo, True)
        q_ref, k_ref, v_ref, do_ref, t_ref, first_ref, dq_ref, dk_ref, dv_ref, dk_acc, dv_acc = own
        p, i = pl.program_id(0), pl.program_id(1)
        start_cargo(jnp.logical_and(p == 0, i == 0))

        @pl.when(i == 0)
        def _():
            dk_acc[...] = jnp.zeros_like(dk_acc)
            dv_acc[...] = jnp.zeros_like(dv_acc)

        row = lax.broadcasted_iota(jnp.int32, (T, T), 0)
        col = lax.broadcasted_iota(jnp.int32, (T, T), 1)
        strict = col < row
        upto = _tri(T, lambda j, s: j <= s)
        before = _tri(T, lambda j, s: j < s)
        masks = _head_masks()
        q, do_t = q_ref[...], do_ref[...]
        q = q * scale
        qs = [jnp.where(hm, q, jnp.zeros_like(q)) for hm in masks]
        dos = [jnp.where(hm, do_t, jnp.zeros_like(do_t)) for hm in masks]
        totals = [t_ref[h][:, 0:1] for h in range(2)]
        over_lanes = (((1,), (1,)), ((), ()))
        over_queries = (((0,), (0,)), ((), ()))

        def walk(tiles, carry):
            rows = [pl.ds(pl.multiple_of(j * T, T), T) for j, _ in tiles]
            keys = [k_ref[r, :] for r in rows]
            values = [v_ref[r, :] for r in rows]
            chains = [(t, h) for t in range(len(tiles)) for h in range(2)]
            logs, da = {}, {}
            for t, h in chains:
                z = lax.dot_general(qs[h], keys[t], over_lanes, preferred_element_type=F32)
                ls, ln = _log_sigmoids(z)
                logs[t, h] = (ls, jnp.where(strict, ln, 0.0) if tiles[t][1] else ln)
                da[t, h] = lax.dot_general(dos[h], values[t], over_lanes, preferred_element_type=F32)
            upto_sums = {c: jnp.dot(_split_bf16(logs[c][1]), upto, preferred_element_type=F32) for c in chains}
            a, g = {}, {}
            P = [c[0] for c in carry]
            for t, h in chains:
                ls, ln = logs[t, h]
                a_th = jnp.exp(ls + ((totals[h] - P[h]) - upto_sums[t, h]))
                a[t, h] = jnp.where(strict, a_th, 0.0) if tiles[t][1] else a_th
                g[t, h] = a[t, h] * da[t, h]
                P[h] = P[h] + jnp.sum(ln, axis=1, keepdims=True)
            before_sums = {c: jnp.dot(_split_bf16(g[c]), before, preferred_element_type=F32) for c in chains}
            G = [c[1] for c in carry]
            dq = [c[2] for c in carry]
            dz = {}
            for t, h in chains:
                beta = jnp.exp(logs[t, h][0])
                dz_th = g[t, h] * (1.0 - beta) - (G[h] + before_sums[t, h]) * beta
                dz[t, h] = (jnp.where(strict, dz_th, 0.0) if tiles[t][1] else dz_th).astype(BF16)
                G[h] = G[h] + jnp.sum(g[t, h], axis=1, keepdims=True)
            for t, h in chains:
                dq[h] = dq[h] + jnp.dot(dz[t, h], keys[t], preferred_element_type=F32)
            for t in range(len(tiles)):
                dk_acc[rows[t], :] += sum(
                    lax.dot_general(dz[t, h], qs[h], over_queries, preferred_element_type=F32) for h in range(2))
                dv_acc[rows[t], :] += sum(
                    lax.dot_general(a[t, h].astype(BF16), dos[h], over_queries, preferred_element_type=F32)
                    for h in range(2))
            return tuple((P[h], G[h], dq[h]) for h in range(2))

        zero = jnp.zeros((T, 1), F32)
        fresh = (zero, zero, jnp.zeros((T, LANES), F32))
        last_single = jnp.maximum(i - 1, 0)
        j0 = jnp.clip(jnp.max(first_ref[...]).astype(jnp.int32), 0, last_single)
        carry = lax.fori_loop(j0, last_single, lambda j, c: walk([(j, False)], c), (fresh, fresh))
        (_, _, dq0), (_, _, dq1) = lax.cond(i > 0, lambda: walk([(i - 1, False), (i, True)], carry),
                                            lambda: walk([(i, True)], carry))
        dq_ref[...] = (jnp.where(masks[0], dq0, dq1) * scale).astype(dq_ref.dtype)

        @pl.when(i == nq - 1)
        def _():
            dk_ref[...] = dk_acc[...].astype(dk_ref.dtype)
            dv_ref[...] = dv_acc[...].astype(dv_ref.dtype)

        finish_cargo(jnp.logical_and(p == SB_PAIRS - 1, i == nq - 1))

    kv = lambda off: pl.BlockSpec((S, LANES), lambda p, i: (0, off + p))
    tile = lambda off: pl.BlockSpec((T, LANES), lambda p, i: (i, off + p))
    n = len(cargo)
    return pl.pallas_call(
        body, name=name, grid=(SB_PAIRS, nq),
        in_specs=[tile(0), kv(SB_PAIRS), kv(2 * SB_PAIRS), tile(0), pl.BlockSpec((2, T, LANES), lambda p, i: (p, i, 0)),
                  pl.BlockSpec((None, None, 8, LANES), lambda p, i: (p, i, 0, 0))] + [HBM_SPEC] * n,
        out_specs=[tile(0), kv(0), kv(0)] + [HBM_SPEC] * n,
        out_shape=[jax.ShapeDtypeStruct((S, SB_WIDTH), BF16)] * 3 + _chip_exchange_shapes(cargo, True),
        scratch_shapes=[pltpu.VMEM((S, LANES), F32), pltpu.VMEM((S, LANES), F32)]
        + (_chip_exchange_semaphores(n) if n else []),
        compiler_params=_params(2))(qkv, qkv, qkv, do, tot, first, *cargo)


POOL_TILE = 512


def _by_group(lane, values):
    return jnp.where(lane < 64, values[0], jnp.where(lane < 128, values[1], jnp.where(lane < 192, values[2], values[3])))


def _pool_inv_count(first_row, n_rows):
    t = first_row + lax.broadcasted_iota(jnp.int32, (n_rows, POOL_WIDTH), 0)
    lane = lax.broadcasted_iota(jnp.int32, (n_rows, POOL_WIDTH), 1)
    window = _by_group(lane, POOL_WINDOWS)
    return 1.0 / jnp.clip(t + 1, 1, window).astype(F32), lane


def _pooled(ext, first_row, R):
    n = R + POOL_HALO
    s2 = ext + pltpu.roll(ext, 1, 0)
    s4 = s2 + pltpu.roll(s2, 2, 0)
    s8 = s4 + pltpu.roll(s4, 4, 0)
    s16 = s8 + pltpu.roll(s8, 8, 0)
    inv, lane = _pool_inv_count(first_row - POOL_HALO, n)
    pooled = _by_group(lane, (s2, s4, s8, s16)) * inv - ext
    return pooled[POOL_HALO:, :]


def _pool_specs(S, R, col):
    per = R // POOL_HALO
    tile = pl.BlockSpec((R, POOL_WIDTH), lambda i: (i, col))
    prev = pl.BlockSpec((POOL_HALO, POOL_WIDTH), lambda i: (jnp.maximum(i * per - 1, 0), col))
    return tile, prev


def _pool_fwd(rest, w_bd, scale, *, name):
    S = rest.shape[0]
    R = min(POOL_TILE, S)

    def body(p_ref, prev_ref, w_ref, s_ref, o_ref, ext_ref):
        i = pl.program_id(0)
        ext_ref[:POOL_HALO, :] = jnp.where(i > 0, prev_ref[...], 0.0)
        ext_ref[POOL_HALO:, :] = p_ref[...]
        pooled = _pooled(ext_ref[...], i * R, R)
        mixed = jnp.dot(pooled.astype(BF16), w_ref[...], preferred_element_type=F32)
        o_ref[...] = (mixed * s_ref[...]).astype(o_ref.dtype)

    tile, prev = _pool_specs(S, R, 0)
    return pl.pallas_call(
        body, name=name, grid=(S // R,),
        in_specs=[tile, prev, pl.BlockSpec((POOL_WIDTH, POOL_WIDTH), lambda i: (0, 0)), _vec_spec(POOL_WIDTH)],
        out_specs=_row_spec(R, POOL_WIDTH), out_shape=jax.ShapeDtypeStruct((S, POOL_WIDTH), BF16),
        scratch_shapes=[pltpu.VMEM((R + POOL_HALO, POOL_WIDTH), F32)], compiler_params=_params(1))(rest, rest, w_bd, scale)


def _pool_bwd(rest, do, w_bd, scale, *, name):
    S = rest.shape[0]
    R = min(POOL_TILE, S)
    n_tiles = S // R
    per = R // POOL_HALO
    n = R + POOL_HALO

    def body(p_ref, prev_ref, do_ref, nxt_ref, w_ref, s_ref, dp_ref, dw_ref, ds_ref, ext_ref, dext_ref):
        i = pl.program_id(0)
        ext_ref[:POOL_HALO, :] = jnp.where(i > 0, prev_ref[...], 0.0)
        ext_ref[POOL_HALO:, :] = p_ref[...]
        pooled = _pooled(ext_ref[...], i * R, R).astype(BF16)
        w = w_ref[...]
        mixed = jnp.dot(pooled, w, preferred_element_type=F32)
        do_t = do_ref[...]
        _accumulate(ds_ref, jnp.sum(do_t * mixed, axis=0, keepdims=True))
        dext_ref[:R, :] = do_t
        dext_ref[R:, :] = jnp.where(i < n_tiles - 1, nxt_ref[...], 0.0)
        dmixed = (dext_ref[...] * s_ref[...]).astype(BF16)
        dpooled = lax.dot_general(dmixed, w, (((1,), (1,)), ((), ())), preferred_element_type=F32)
        _accumulate(dw_ref, lax.dot_general(pooled, dmixed[:R, :], (((0,), (0,)), ((), ())), preferred_element_type=F32))
        inv, lane = _pool_inv_count(i * R, n)
        u = dpooled * inv
        f2 = u + pltpu.roll(u, n - 1, 0)
        f4 = f2 + pltpu.roll(f2, n - 2, 0)
        f8 = f4 + pltpu.roll(f4, n - 4, 0)
        f16 = f8 + pltpu.roll(f8, n - 8, 0)
        dp = _by_group(lane, (f2, f4, f8, f16)) - dpooled
        dp_ref[...] = dp[:R, :].astype(dp_ref.dtype)

    tile, prev = _pool_specs(S, R, 0)
    nxt = pl.BlockSpec((POOL_HALO, POOL_WIDTH), lambda i: (jnp.minimum((i + 1) * per, S // POOL_HALO - 1), 0))
    full = pl.BlockSpec((POOL_WIDTH, POOL_WIDTH), lambda i: (0, 0))
    return pl.pallas_call(
        body, name=name, grid=(n_tiles,),
        in_specs=[tile, prev, _row_spec(R, POOL_WIDTH), nxt, full, _vec_spec(POOL_WIDTH)],
        out_specs=[_row_spec(R, POOL_WIDTH), full, _vec_spec(POOL_WIDTH)],
        out_shape=[jax.ShapeDtypeStruct((S, POOL_WIDTH), BF16), jax.ShapeDtypeStruct((POOL_WIDTH, POOL_WIDTH), F32),
                   jax.ShapeDtypeStruct((1, POOL_WIDTH), F32)],
        scratch_shapes=[pltpu.VMEM((n, POOL_WIDTH), F32), pltpu.VMEM((n, POOL_WIDTH), F32)],
        compiler_params=_params(1))(rest, rest, do, do, w_bd, scale)


GM_TILE = 512
GELU_C = math.sqrt(2.0 / math.pi)
GELU_A = 0.044715


def _gelu(x):
    return 0.5 * x * (1.0 + jnp.tanh(GELU_C * (x + GELU_A * x * x * x)))


def _gelu_and_grad(x):
    t = jnp.tanh(GELU_C * (x + GELU_A * x * x * x))
    y = 0.5 * x * (1.0 + t)
    dy = 0.5 * (1.0 + t) + 0.5 * x * (1.0 - t * t) * (GELU_C * (1.0 + 3.0 * GELU_A * x * x))
    return y, dy


def _group_lane_masks():
    lane = lax.broadcasted_iota(jnp.int32, (1, GM_WIDTH), 1)
    return [(lane >= g * GM_GROUP_DIM) & (lane < (g + 1) * GM_GROUP_DIM) for g in range(GM_GROUPS)]


def _stack_groups(x, masks):
    return jnp.concatenate([jnp.where(m, x, jnp.zeros_like(x)) for m in masks], axis=0)


def _gm_mixed(vn, ws_cat, bias, masks, R):
    chunks = []
    for c in range(R // GM_CHUNK):
        vc = vn[c * GM_CHUNK:(c + 1) * GM_CHUNK, :]
        chunks.append(jnp.dot(ws_cat, _stack_groups(vc, masks), preferred_element_type=F32) + bias)
    return jnp.concatenate(chunks, axis=0)


def _gm_specs(S, R):
    u = pl.BlockSpec((R, GM_WIDTH), lambda i: (i, 1))
    v = pl.BlockSpec((R, GM_WIDTH), lambda i: (i, 2))
    ws = pl.BlockSpec((GM_CHUNK, GM_GROUPS * GM_CHUNK), lambda i: (0, 0))
    bias = pl.BlockSpec((GM_CHUNK, GM_WIDTH), lambda i: (0, 0))
    return u, v, ws, bias


def _gm_fwd(rest, gain, ws_cat, bias, *, name):
    S = rest.shape[0]
    R = min(GM_TILE, S)

    def body(u_ref, v_ref, g_ref, ws_ref, b_ref, o_ref):
        gv = _gelu(v_ref[...])
        vn = (gv * _rstd(gv) * g_ref[...]).astype(BF16)
        mixed = _gm_mixed(vn, ws_ref[...], b_ref[...], _group_lane_masks(), R)
        o_ref[...] = (_gelu(u_ref[...]) * mixed).astype(o_ref.dtype)

    u_spec, v_spec, ws_spec, bias_spec = _gm_specs(S, R)
    return pl.pallas_call(
        body, name=name, grid=(S // R,), in_specs=[u_spec, v_spec, _vec_spec(GM_WIDTH), ws_spec, bias_spec],
        out_specs=_row_spec(R, GM_WIDTH), out_shape=jax.ShapeDtypeStruct((S, GM_WIDTH), BF16),
        compiler_params=_params(1))(rest, rest, gain, ws_cat, bias)


def _gm_bwd(rest, do, gain, ws_cat, wst_cat, bias, *, name):
    S = rest.shape[0]
    R = min(GM_TILE, S)

    def body(u_ref, v_ref, do_ref, g_ref, ws_ref, wst_ref, b_ref, du_ref, dv_ref, dg_ref, dws_ref, db_ref):
        masks = _group_lane_masks()
        gain_v = g_ref[...]
        gu, dgu = _gelu_and_grad(u_ref[...])
        gv, dgv = _gelu_and_grad(v_ref[...])
        r = _rstd(gv)
        vn = (gv * r * gain_v).astype(BF16)
        mixed = _gm_mixed(vn, ws_ref[...], b_ref[...], masks, R)
        do_t = do_ref[...]
        du_ref[...] = (do_t * mixed * dgu).astype(du_ref.dtype)
        dmix = do_t * gu
        dmix_b = dmix.astype(BF16)
        wst = wst_ref[...]
        dvn_chunks, db, dws = [], None, [None] * GM_GROUPS
        for c in range(R // GM_CHUNK):
            rows = slice(c * GM_CHUNK, (c + 1) * GM_CHUNK)
            dc, dcb, vc = dmix[rows, :], dmix_b[rows, :], vn[rows, :]
            db = dc if db is None else db + dc
            dvn_chunks.append(jnp.dot(wst, _stack_groups(dcb, masks), preferred_element_type=F32))
            for g, m in enumerate(masks):
                part = lax.dot_general(jnp.where(m, dcb, jnp.zeros_like(dcb)), vc, (((1,), (1,)), ((), ())),
                                       preferred_element_type=F32)
                dws[g] = part if dws[g] is None else dws[g] + part
        dvn = jnp.concatenate(dvn_chunks, axis=0)
        lane = lax.broadcasted_iota(jnp.int32, (1, LANES), 1)
        db_groups = jnp.zeros((GM_CHUNK, LANES), F32)
        for g, m in enumerate(masks):
            total = jnp.sum(jnp.where(m, db, 0.0), axis=1, keepdims=True)
            db_groups = db_groups + jnp.where(lane == g, total, 0.0)
        _accumulate(db_ref, db_groups)
        i = pl.program_id(0)
        for g in range(GM_GROUPS):
            @pl.when(i == 0)
            def _(g=g):
                dws_ref[g] = dws[g]

            @pl.when(i > 0)
            def _(g=g):
                dws_ref[g] += dws[g]
        _accumulate(dg_ref, jnp.sum(dvn * gv * r, axis=0, keepdims=True))
        gd = gain_v * dvn
        dgv_in = r * gd - gv * (r * r * r) * jnp.mean(gv * gd, axis=-1, keepdims=True)
        dv_ref[...] = (dgv_in * dgv).astype(dv_ref.dtype)

    u_spec, v_spec, ws_spec, bias_spec = _gm_specs(S, R)
    row, vec = _row_spec(R, GM_WIDTH), _vec_spec(GM_WIDTH)
    dws_spec = pl.BlockSpec((GM_GROUPS, GM_CHUNK, GM_CHUNK), lambda i: (0, 0, 0))
    return pl.pallas_call(
        body, name=name, grid=(S // R,), in_specs=[u_spec, v_spec, row, vec, ws_spec, ws_spec, bias_spec],
        out_specs=[row, row, vec, dws_spec, pl.BlockSpec((GM_CHUNK, LANES), lambda i: (0, 0))],
        out_shape=[jax.ShapeDtypeStruct((S, GM_WIDTH), BF16)] * 2
        + [jax.ShapeDtypeStruct((1, GM_WIDTH), F32), jax.ShapeDtypeStruct((GM_GROUPS, GM_CHUNK, GM_CHUNK), F32),
           jax.ShapeDtypeStruct((GM_CHUNK, LANES), F32)],
        compiler_params=_params(1))(rest, rest, do, gain, ws_cat, wst_cat, bias)


GATE_ROWS = 1024
GATE_COLS = 256
GATE_BLOCKS = D_MODEL // GATE_COLS


def _gate_spec(tr, k):
    return pl.BlockSpec((tr, GATE_COLS), lambda i, j: (i, GATE_BLOCKS * k + j))


def _merge_fwd(gates, branches, *, name):
    S = gates.shape[0]
    tr = min(GATE_ROWS, S)

    def body(g0, g1, g2, b0, b1, b2, o_ref):
        acc = None
        for g_ref, b_ref in ((g0, b0), (g1, b1), (g2, b2)):
            term = jax.nn.sigmoid(g_ref[...].astype(F32)) * b_ref[...].astype(F32)
            acc = term if acc is None else acc + term
        o_ref[...] = acc.astype(o_ref.dtype)

    tile = pl.BlockSpec((tr, GATE_COLS), lambda i, j: (i, j))
    return pl.pallas_call(
        body, name=name, grid=(S // tr, GATE_BLOCKS),
        in_specs=[_gate_spec(tr, k) for k in range(N_BRANCH)] + [tile] * N_BRANCH, out_specs=tile,
        out_shape=jax.ShapeDtypeStruct((S, D_MODEL), BF16), compiler_params=_params(2))(gates, gates, gates, *branches)


def _merge_bwd(gates, branches, dmerged, *, name):
    S = gates.shape[0]
    tr = min(GATE_ROWS, S)

    def body(g0, g1, g2, b0, b1, b2, dm_ref, dg0, dg1, dg2, db0, db1, db2):
        dm = dm_ref[...].astype(F32)
        for g_ref, b_ref, dg_ref, db_ref in ((g0, b0, dg0, db0), (g1, b1, dg1, db1), (g2, b2, dg2, db2)):
            s = jax.nn.sigmoid(g_ref[...].astype(F32))
            db_ref[...] = (dm * s).astype(db_ref.dtype)
            dg_ref[...] = (dm * b_ref[...].astype(F32) * s * (1.0 - s)).astype(dg_ref.dtype)

    tile = pl.BlockSpec((tr, GATE_COLS), lambda i, j: (i, j))
    return pl.pallas_call(
        body, name=name, grid=(S // tr, GATE_BLOCKS),
        in_specs=[_gate_spec(tr, k) for k in range(N_BRANCH)] + [tile] * (N_BRANCH + 1), out_specs=[tile] * (2 * N_BRANCH),
        out_shape=[jax.ShapeDtypeStruct((S, D_MODEL), BF16)] * (2 * N_BRANCH),
        compiler_params=_params(2))(gates, gates, gates, *branches, dmerged)


TILE_BYTES = 24 * 1024 * 1024


BF16_ROWS = 16


def _tile_rows(rows, cols, n_arrays):
    padded = -(-cols // LANES) * LANES
    cap = max(BF16_ROWS, TILE_BYTES // (2 * n_arrays * padded * 4))
    best = None
    for tr in range(BF16_ROWS, min(rows, cap) + 1, BF16_ROWS):
        if rows % tr == 0:
            best = tr
    assert best is not None, (rows, cols)
    return best


def _sum_slots(stack, *, name):
    n, R, C = stack.shape
    tr = _tile_rows(R, C, n + 1)

    def body(s_ref, o_ref):
        acc = s_ref[0].astype(F32)
        for k in range(1, n):
            acc = acc + s_ref[k].astype(F32)
        o_ref[...] = acc

    return pl.pallas_call(
        body, name=name, grid=(R // tr,), in_specs=[pl.BlockSpec((n, tr, C), lambda i: (0, i, 0))],
        out_specs=_row_spec(tr, C), out_shape=jax.ShapeDtypeStruct((R, C), F32), compiler_params=_params(1))(stack)


def _add_own_half(parts, received, core, *, name):
    n, R, C = parts.shape
    half = R // 2
    tr = _tile_rows(half, C, 3)
    steps = half // tr

    def body(core_ref, own_ref, got_ref, o_ref):
        o_ref[...] = (own_ref[...] + got_ref[...]).astype(o_ref.dtype)

    tile = pl.BlockSpec((None, tr, C), lambda d, i, core_ref: (d, i, 0))
    own = pl.BlockSpec((None, tr, C), lambda d, i, core_ref: (d, core_ref[0] * steps + i, 0))
    return pl.pallas_call(
        body, name=name, out_shape=jax.ShapeDtypeStruct((n, half, C), BF16),
        grid_spec=pltpu.PrefetchScalarGridSpec(num_scalar_prefetch=1, grid=(n, steps), in_specs=[own, tile], out_specs=tile),
        compiler_params=_params(2))(core, parts, received)


def _adamw_math(w, m, v, g):
    m_new = ADAM_B1 * m + (1.0 - ADAM_B1) * g
    v_new = ADAM_B2 * v + (1.0 - ADAM_B2) * jnp.square(g)
    m_hat = m_new / (1.0 - ADAM_B1 ** ADAM_STEP)
    v_hat = v_new / (1.0 - ADAM_B2 ** ADAM_STEP)
    return -ADAM_LR * (m_hat / (jnp.sqrt(v_hat) + ADAM_EPS) + ADAM_WD * w), m_new, v_new


def _adamw_summed(w, m, v, gstack, *, name):
    R, C = w.shape
    n = gstack.shape[0]
    tr = _tile_rows(R, C, n + 7)

    def body(w_ref, m_ref, v_ref, g_ref, go_ref, d_ref, mo_ref, vo_ref):
        g = g_ref[0]
        for k in range(1, n):
            g = g + g_ref[k]
        go_ref[...] = g
        d_ref[...], mo_ref[...], vo_ref[...] = _adamw_math(w_ref[...], m_ref[...], v_ref[...], g)

    row = _row_spec(tr, C)
    return pl.pallas_call(
        body, name=name, grid=(R // tr,), in_specs=[row, row, row, pl.BlockSpec((n, tr, C), lambda i: (0, i, 0))],
        out_specs=[row] * 4, out_shape=[jax.ShapeDtypeStruct((R, C), F32)] * 4, compiler_params=_params(1))(w, m, v, gstack)


def _adamw_halves(w, m, v, mine, theirs, core, *, name):
    L, r, C = w.shape
    tr = _tile_rows(r // 2, C, 9)
    steps = r // 2 // tr

    def body(core_ref, w_ref, m_ref, v_ref, mine_ref, theirs_ref, go_ref, d_ref, mo_ref, vo_ref):
        in_my_half = pl.program_id(1) // steps == core_ref[0]
        g = jnp.where(in_my_half, mine_ref[...], theirs_ref[...])
        go_ref[...] = g
        d_ref[...], mo_ref[...], vo_ref[...] = _adamw_math(w_ref[...], m_ref[...], v_ref[...], g)

    row = pl.BlockSpec((None, tr, C), lambda l, i, core_ref: (l, i, 0))
    half = pl.BlockSpec((None, tr, C), lambda l, i, core_ref: (l, i % steps, 0))
    return pl.pallas_call(
        body, name=name, out_shape=[jax.ShapeDtypeStruct((L, r, C), F32)] * 4,
        grid_spec=pltpu.PrefetchScalarGridSpec(
            num_scalar_prefetch=1, grid=(L, r // tr), in_specs=[row, row, row, half, half], out_specs=[row] * 4),
        compiler_params=_params(2))(core, w, m, v, mine, theirs)


HBM_SPEC = pl.BlockSpec(memory_space=pl.ANY)


def _position():
    return lax.axis_index("x"), lax.axis_index("y"), lax.axis_index("c")


def _other_chips(x, y):
    return [(1 - x, y), (x, 1 - y), (1 - x, 1 - y)]


def _chip_exchange(arrays, *, scatter, name):
    n = len(arrays)

    def body(*refs):
        copies = _chip_copies(refs[:n], refs[n:2 * n], *refs[2 * n:], scatter=scatter)
        for cp in copies:
            cp.start()
        for cp in copies:
            cp.wait()

    return pl.pallas_call(
        body, name=name, in_specs=[HBM_SPEC] * n, out_specs=[HBM_SPEC] * n, out_shape=_chip_exchange_shapes(arrays, scatter),
        scratch_shapes=_chip_exchange_semaphores(n))(*arrays)


def _chip_exchange_shapes(arrays, scatter):
    return [jax.ShapeDtypeStruct(a.shape if scatter else (N_CHIPS,) + a.shape, a.dtype) for a in arrays]


def _chip_exchange_semaphores(n):
    return [pltpu.SemaphoreType.DMA((3 * n,)), pltpu.SemaphoreType.DMA((3 * n,)), pltpu.SemaphoreType.DMA((n,))]


def _chip_copies(ins, outs, send_sems, recv_sems, local_sems, *, scatter):
    x, y, c = _position()
    me = 2 * x + y
    copies = []
    for a in range(len(ins)):
        own = ins[a].at[me] if scatter else ins[a]
        copies.append(pltpu.make_async_copy(own, outs[a].at[me], local_sems.at[a]))
        for k, (px, py) in enumerate(_other_chips(x, y)):
            src = ins[a].at[2 * px + py] if scatter else ins[a]
            copies.append(pltpu.make_async_remote_copy(
                src_ref=src, dst_ref=outs[a].at[me], send_sem=send_sems.at[3 * a + k],
                recv_sem=recv_sems.at[3 * a + k], device_id=(px, py, c), device_id_type=MESH))
    return copies


def _sibling_swap(arrays, *, name):
    n = len(arrays)

    def body(*refs):
        ins, outs = refs[:n], refs[n:2 * n]
        send_sems, recv_sems = refs[2 * n:]
        x, y, c = _position()
        copies = []
        for a in range(n):
            cp = pltpu.make_async_remote_copy(
                src_ref=ins[a], dst_ref=outs[a], send_sem=send_sems.at[a], recv_sem=recv_sems.at[a],
                device_id=(x, y, 1 - c), device_id_type=MESH)
            cp.start()
            copies.append(cp)
        for cp in copies:
            cp.wait()

    return pl.pallas_call(
        body, name=name, in_specs=[HBM_SPEC] * n, out_specs=[HBM_SPEC] * n,
        out_shape=[jax.ShapeDtypeStruct(a.shape, a.dtype) for a in arrays],
        scratch_shapes=[pltpu.SemaphoreType.DMA((n,)), pltpu.SemaphoreType.DMA((n,))],
    )(*arrays)


def _sibling_other_half(arrays, *, name):
    n = len(arrays)

    def body(*refs):
        ins, outs = refs[:n], refs[n:2 * n]
        send_sems, recv_sems = refs[2 * n:]
        x, y, c = _position()
        copies = []
        for a in range(n):
            half = ins[a].shape[1] // 2
            theirs = ins[a].at[:, pl.ds(pl.multiple_of((1 - c) * half, BF16_ROWS), half), :]
            cp = pltpu.make_async_remote_copy(
                src_ref=theirs, dst_ref=outs[a], send_sem=send_sems.at[a], recv_sem=recv_sems.at[a],
                device_id=(x, y, 1 - c), device_id_type=MESH)
            cp.start()
            copies.append(cp)
        for cp in copies:
            cp.wait()

    return pl.pallas_call(
        body, name=name, in_specs=[HBM_SPEC] * n, out_specs=[HBM_SPEC] * n,
        out_shape=[jax.ShapeDtypeStruct((a.shape[0], a.shape[1] // 2, a.shape[2]), a.dtype) for a in arrays],
        scratch_shapes=[pltpu.SemaphoreType.DMA((n,)), pltpu.SemaphoreType.DMA((n,))],
    )(*arrays)


def _gather_all(buf, *, name):
    R, C = buf.shape

    def body(in_ref, out_ref, send_sems, recv_sems):
        x, y, c = _position()
        out_ref[4 * x + 2 * y + c] = in_ref[...]
        copies = []
        for k in range(1, N_DEV):
            px = 1 - x if k & 4 else x
            py = 1 - y if k & 2 else y
            pc = 1 - c if k & 1 else c
            cp = pltpu.make_async_remote_copy(
                src_ref=in_ref, dst_ref=out_ref.at[4 * x + 2 * y + c], send_sem=send_sems.at[k - 1],
                recv_sem=recv_sems.at[k - 1], device_id=(px, py, pc), device_id_type=MESH)
            cp.start()
            copies.append(cp)
        for cp in copies:
            cp.wait()

    vmem = pl.BlockSpec(memory_space=pltpu.VMEM)
    return pl.pallas_call(
        body, name=name, in_specs=[vmem], out_specs=vmem, out_shape=jax.ShapeDtypeStruct((N_DEV, R, C), buf.dtype),
        scratch_shapes=[pltpu.SemaphoreType.DMA((N_DEV - 1,)), pltpu.SemaphoreType.DMA((N_DEV - 1,))],
        compiler_params=pltpu.CompilerParams(vmem_limit_bytes=VMEM_LIMIT),
    )(buf)


def _relu2(p):
    return p, jnp.square(jnp.maximum(p, 0.0))


def _relu2_grad(p, a):
    return (p * (2.0 * jnp.maximum(a.astype(F32), 0.0)),)


def _mixer_constants(w_pool, w_spatial, b_spatial):
    eye = jnp.eye(len(POOL_WINDOWS), dtype=F32)
    w_bd = (eye[:, None, :, None] * w_pool[:, :, None, :]).reshape(POOL_WIDTH, POOL_WIDTH).astype(BF16)
    causal = jnp.tril(jnp.ones((GM_CHUNK, GM_CHUNK), dtype=bool))
    ws = jnp.where(causal[None], w_spatial, 0.0).astype(BF16)
    ws_cat = ws.transpose(1, 0, 2).reshape(GM_CHUNK, GM_GROUPS * GM_CHUNK)
    wst_cat = ws.transpose(2, 0, 1).reshape(GM_CHUNK, GM_GROUPS * GM_CHUNK)
    bias = jnp.repeat(b_spatial.T, GM_GROUP_DIM, axis=1)
    return w_bd, ws_cat, wst_cat, bias


def _local_step(x, target, half_shards, small, core):
    L = small["g_mix_pre"].shape[0]
    vec = lambda name, l: small[name][l][None, :]
    consts = [_mixer_constants(small["w_pool"][l], small["w_spatial"][l], small["b_spatial"][l]) for l in range(L)]
    core_index = core.astype(jnp.int32).reshape(1)
    layer_halves = lambda l: [half_shards[n][l] for n in BIG_WEIGHTS]

    def full_weights(gathered):
        theirs = _sibling_swap(gathered, name="swap_weight_halves")
        return {n: _full_weight(n, a, b, core) for n, a, b in zip(BIG_WEIGHTS, gathered, theirs)}

    weights = [full_weights(_chip_exchange(layer_halves(0), scatter=False, name="gather_weights"))]
    saved = []
    h = _rms_fwd(x, vec("g_mix_pre", 0), name="rms_in")
    for l in range(L):
        w_bd, ws_cat, wst_cat, bias = consts[l]
        proj = lambda n, off, dtype, name: _matmul(h, weights[l]["w_in"], n=n, bn=PROJ_BLOCK, b_col_off=off // PROJ_BLOCK,
                                                   out_dtypes=(dtype,), name=name)
        qkv = proj(QKV_WIDTH, 0, BF16, "proj_qkv")
        rest = proj(MIX_WIDTH, QKV_WIDTH, F32, "proj_mix")
        gates = proj(GATE_WIDTH, QKV_WIDTH + MIX_WIDTH, BF16, "proj_gates")
        if l + 1 < L:
            o_sb, tot, first, *gathered = _sba_fwd(qkv, cargo=layer_halves(l + 1), name="sba_fwd_gather")
            weights.append(full_weights(gathered))
        else:
            o_sb, tot, first = _sba_fwd(qkv, name="sba_fwd")
        o_pool = _pool_fwd(rest, w_bd, vec("pool_scale", l), name="pool_fwd")
        o_gm = _gm_fwd(rest, vec("gm_gain", l), ws_cat, bias, name="gm_fwd")
        branches = (_matmul(o_sb, weights[l]["w_br_sb"], out_dtypes=(BF16,), name="br_sb"),
                    _matmul(o_pool, weights[l]["w_br_pool"], out_dtypes=(BF16,), name="br_pool"),
                    _matmul(o_gm, weights[l]["w_br_gm"], out_dtypes=(BF16,), name="br_gm"))
        merged = _merge_fwd(gates, branches, name="merge_fwd")
        y = _matmul(merged, weights[l]["w_out"], name="out_proj")
        x1, h2 = _resid_rms(x, y, vec("g_mix_post", l), vec("g_ff_pre", l), name="resid_mix")
        a, r = _matmul(h2, weights[l]["w_ff_in"], out_dtypes=(BF16, BF16), epilogue=_relu2, name="ff_in")
        ff = _matmul(r, weights[l]["w_ff_out"], name="ff_out")
        g_next = vec("g_mix_pre", l + 1) if l + 1 < L else None
        x2, h_next = _resid_rms(x1, ff, vec("g_ff_post", l), g_next, name="resid_ff" if l + 1 < L else "resid_last")
        saved.append(dict(x=x, h=h, qkv=qkv, rest=rest, gates=gates, o_sb=o_sb, tot=tot, first=first, o_pool=o_pool,
                          o_gm=o_gm, branches=branches, merged=merged, y=y, x1=x1, h2=h2, a=a, r=r, ff=ff))
        x, h = x2, h_next

    dx2, loss = _loss_head(x, target, name="loss_head")
    gb = {k: [None] * L for k in ("w_in", "w_br_sb", "w_br_pool", "w_br_gm", "w_out", "w_ff_in", "w_ff_out")}
    gs = {k: [None] * L for k in ("w_pool", "pool_scale", "gm_gain", "w_spatial", "b_spatial", "g_mix_pre",
                                  "g_mix_post", "g_ff_pre", "g_ff_post")}
    d_ff, gs["g_ff_post"][L - 1] = _rms_bwd(saved[-1]["ff"], vec("g_ff_post", L - 1), dx2, name="rms_bwd_last")
    received = [None] * L
    chip_sums = ()
    for l in reversed(range(L)):
        s = saved[l]
        w_bd, ws_cat, wst_cat, bias = consts[l]
        da = _matmul(d_ff, weights[l]["w_ff_out"], tb=True, out_dtypes=(BF16,), extras=(s["a"],), epilogue=_relu2_grad,
                     name="ff_out_dx")
        gb["w_ff_out"][l] = _matmul(s["r"], d_ff, ta=True, name="ff_out_dw")
        dh2 = _matmul(da, weights[l]["w_ff_in"], tb=True, name="ff_in_dx")
        gb["w_ff_in"][l] = _matmul(s["h2"], da, ta=True, name="ff_in_dw")
        dx1, gs["g_ff_pre"][l], dy, gs["g_mix_post"][l] = _rms_bwd_chain(
            s["x1"], vec("g_ff_pre", l), dh2, dx2, s["y"], vec("g_mix_post", l), name="rms_bwd_mid")
        dmerged = _matmul(dy, weights[l]["w_out"], tb=True, out_dtypes=(BF16,), name="out_proj_dx")
        gb["w_out"][l] = _matmul(s["merged"], dy, ta=True, name="out_proj_dw")
        dg0, dg1, dg2, db_sb, db_pool, db_gm = _merge_bwd(s["gates"], s["branches"], dmerged, name="merge_bwd")
        do_sb = _matmul(db_sb, weights[l]["w_br_sb"], tb=True, out_dtypes=(BF16,), name="br_sb_dx")
        gb["w_br_sb"][l] = _matmul(s["o_sb"], db_sb, ta=True, name="br_sb_dw")
        do_pool = _matmul(db_pool, weights[l]["w_br_pool"], tb=True, name="br_pool_dx")
        gb["w_br_pool"][l] = _matmul(s["o_pool"], db_pool, ta=True, name="br_pool_dw")
        do_gm = _matmul(db_gm, weights[l]["w_br_gm"], tb=True, name="br_gm_dx")
        gb["w_br_gm"][l] = _matmul(s["o_gm"], db_gm, ta=True, name="br_gm_dw")
        if chip_sums:
            dq, dk, dv, *received[l + 1] = _sba_bwd(s["qkv"], do_sb, s["tot"], s["first"], cargo=chip_sums,
                                                    name="sba_bwd_exchange")
        else:
            dq, dk, dv = _sba_bwd(s["qkv"], do_sb, s["tot"], s["first"], name="sba_bwd")
        dp, dw_bd, gs["pool_scale"][l] = _pool_bwd(s["rest"], do_pool, w_bd, vec("pool_scale", l), name="pool_bwd")
        du, dgv, gs["gm_gain"][l], dws, db = _gm_bwd(s["rest"], do_gm, vec("gm_gain", l), ws_cat, wst_cat, bias,
                                                      name="gm_bwd")
        gs["w_pool"][l] = jnp.stack([dw_bd[g * 64:(g + 1) * 64, g * 64:(g + 1) * 64] for g in range(len(POOL_WINDOWS))])
        gs["w_spatial"][l] = jnp.where(jnp.tril(jnp.ones((GM_CHUNK, GM_CHUNK), dtype=bool))[None], dws, 0.0)
        gs["b_spatial"][l] = db[:, :GM_GROUPS].T
        dproj = jnp.concatenate([dq, dk, dv, dp, du, dgv, dg0, dg1, dg2], axis=1)
        dh = _matmul(dproj, weights[l]["w_in"], tb=True, bk=D_IN // 3, name="proj_dx")
        gb["w_in"][l] = _matmul(s["h"], dproj, ta=True, bn=768, name="proj_dw")
        parts = [_parts_by_chip(n, gb[n][l]) for n in BIG_WEIGHTS]
        from_sibling = _sibling_other_half(parts, name="swap_grad_halves")
        chip_sums = [_add_own_half(p, f, core_index, name="sum_cores") for p, f in zip(parts, from_sibling)]
        if l == 0:
            received[0] = _chip_exchange(chip_sums, scatter=True, name="exchange_grads")
        if l > 0:
            dx2, gs["g_mix_pre"][l], d_ff, gs["g_ff_post"][l - 1] = _rms_bwd_chain(
                s["x"], vec("g_mix_pre", l), dh, dx1, saved[l - 1]["ff"], vec("g_ff_post", l - 1), name="rms_bwd_mid")
        else:
            dx2, gs["g_mix_pre"][l], _, _ = _rms_bwd_chain(s["x"], vec("g_mix_pre", l), dh, dx1, None, None,
                                                           name="rms_bwd_first")
    small_grads = {k: jnp.stack([g.reshape(small[k].shape[1:]) for g in v]) for k, v in gs.items()}
    return loss, dx2, received, small_grads


COLUMN_SHARDED = ("w_in", "w_br_sb", "w_br_pool", "w_br_gm", "w_ff_in")
ROW_SHARDED = ("w_out", "w_ff_out")
BIG_WEIGHTS = COLUMN_SHARDED + ROW_SHARDED
SMALL_WEIGHTS = ("w_pool", "pool_scale", "gm_gain", "w_spatial", "b_spatial", "g_mix_pre", "g_mix_post", "g_ff_pre",
                 "g_ff_post")
WEIGHT_ORDER = ("w_in", "w_pool", "pool_scale", "gm_gain", "w_spatial", "b_spatial", "w_br_sb", "w_br_pool", "w_br_gm",
                "w_out", "g_mix_pre", "g_mix_post", "g_ff_pre", "g_ff_post", "w_ff_in", "w_ff_out")


def _full_weight(name, mine, theirs, core):
    g = jnp.stack([jnp.where(core == 0, mine, theirs), jnp.where(core == 0, theirs, mine)])
    half, cols = g.shape[2], g.shape[3]
    if name in COLUMN_SHARDED:
        return g.transpose(0, 2, 1, 3).reshape(2 * half, N_CHIPS * cols)
    return g.transpose(1, 0, 2, 3).reshape(N_CHIPS * 2 * half, cols)


def _parts_by_chip(name, grad):
    if name in COLUMN_SHARDED:
        r, c = grad.shape[0], grad.shape[1] // N_CHIPS
        return grad.reshape(r, N_CHIPS, c).transpose(1, 0, 2)
    return grad.reshape(N_CHIPS, grad.shape[0] // N_CHIPS, grad.shape[1])


def _pack(arrays):
    flat = jnp.concatenate([a.reshape(-1) for a in arrays])
    return flat.reshape(-1, LANES)


def _unpack(buf, like):
    flat, out, at = buf.reshape(-1), [], 0
    for a in like:
        out.append(flat[at:at + a.size].reshape(a.shape))
        at += a.size
    return out


def kernel(x, w_in, w_pool, pool_scale, gm_gain, w_spatial, b_spatial, w_br_sb, w_br_pool, w_br_gm, w_out, g_mix_pre, g_mix_post, g_ff_pre, g_ff_post, w_ff_in, w_ff_out, loss_target, m_w_in, m_w_pool, m_pool_scale, m_gm_gain, m_w_spatial, m_b_spatial, m_w_br_sb, m_w_br_pool, m_w_br_gm, m_w_out, m_g_mix_pre, m_g_mix_post, m_g_ff_pre, m_g_ff_post, m_w_ff_in, m_w_ff_out, v_w_in, v_w_pool, v_pool_scale, v_gm_gain, v_w_spatial, v_b_spatial, v_w_br_sb, v_w_br_pool, v_w_br_gm, v_w_out, v_g_mix_pre, v_g_mix_post, v_g_ff_pre, v_g_ff_post, v_w_ff_in, v_w_ff_out):
    given = dict(locals())
    w = {n: given[n] for n in WEIGHT_ORDER}
    m = {n: given["m_" + n] for n in WEIGHT_ORDER}
    v = {n: given["v_" + n] for n in WEIGHT_ORDER}
    L = w_in.shape[0]

    core = lax.axis_index("c")

    def my_rows(a):
        half = a.shape[1] // 2
        return lax.dynamic_slice_in_dim(a.astype(BF16), core * half, half, axis=1)

    half_shards = {n: my_rows(w[n]) for n in BIG_WEIGHTS}
    small = {n: w[n] for n in SMALL_WEIGHTS}
    loss, dx, received, small_grads = _local_step(x[0], loss_target[0], half_shards, small, core)

    L = w_in.shape[0]
    core_index = core.astype(jnp.int32).reshape(1)
    reduced = []
    for k in range(len(BIG_WEIGHTS)):
        from_chips = jnp.concatenate([received[l][k] for l in range(L)], axis=1)
        reduced.append(_sum_slots(from_chips, name="sum_chips"))
    reduced_by_sibling = _sibling_swap(reduced, name="swap_reduced_halves")
    grads, deltas, new_m, new_v = {}, {}, {}, {}
    for n, mine, theirs in zip(BIG_WEIGHTS, reduced, reduced_by_sibling):
        by_layer = lambda a: a.reshape(L, -1, a.shape[-1])
        grads[n], deltas[n], new_m[n], new_v[n] = _adamw_halves(
            w[n], m[n], v[n], by_layer(mine), by_layer(theirs), core_index, name="adamw_big")

    like = [w[n] for n in SMALL_WEIGHTS]
    all_partials = _gather_all(_pack([small_grads[n] for n in SMALL_WEIGHTS]), name="gather_small_grads")
    outs = _adamw_summed(_pack(like), _pack([m[n] for n in SMALL_WEIGHTS]), _pack([v[n] for n in SMALL_WEIGHTS]),
                         all_partials, name="adamw_small")
    for store, buf in zip((grads, deltas, new_m, new_v), outs):
        store.update(zip(SMALL_WEIGHTS, _unpack(buf, like)))

    total_loss = lax.psum(loss[0, 0], ("x", "y", "c"))
    return (total_loss, dx[None], *[grads[n] for n in WEIGHT_ORDER], *[deltas[n] for n in WEIGHT_ORDER],
            *[new_m[n] for n in WEIGHT_ORDER], *[new_v[n] for n in WEIGHT_ORDER])
```

```python
import functools
import math

import jax
import jax.numpy as jnp
from jax import lax
from jax.experimental import pallas as pl
from jax.experimental.pallas import tpu as pltpu

F32 = jnp.float32
BF16 = jnp.bfloat16

D_MODEL = 1024
SB_HEADS = 8
SB_HEAD_DIM = 64
SB_WIDTH = SB_HEADS * SB_HEAD_DIM
POOL_WINDOWS = (2, 4, 8, 16)
POOL_GROUP_DIM = 64
POOL_WIDTH = 256
POOL_HALO = 16
GM_GROUPS = 4
GM_GROUP_DIM = 64
GM_WIDTH = 256
GM_CHUNK = 128
N_BRANCH = 3
D_FF = 4 * D_MODEL
RMS_EPS = 1e-6
QKV_WIDTH = 3 * SB_WIDTH
MIX_WIDTH = POOL_WIDTH + 2 * GM_WIDTH
GATE_WIDTH = N_BRANCH * D_MODEL
D_IN = QKV_WIDTH + MIX_WIDTH + GATE_WIDTH
PROJ_BLOCK = 768
LANES = 128
N_CHIPS = 4
N_DEV = 8

ADAM_LR = 0.001
ADAM_B1 = 0.9
ADAM_B2 = 0.999
ADAM_EPS = 1e-08
ADAM_WD = 0.01
ADAM_STEP = 10

VMEM_LIMIT = 56 * 1024 * 1024
MESH = pl.DeviceIdType.MESH


def _params(n_grid):
    return pltpu.CompilerParams(dimension_semantics=("arbitrary",) * n_grid, vmem_limit_bytes=VMEM_LIMIT)


def _bf(x):
    return x if x.dtype == BF16 else x.astype(BF16)


def _matmul(a, b, *, name, ta=False, tb=False, out_dtypes=(F32,), n=None, b_col_off=0, bm=1024, bn=1024, bk=2048,
            extras=(), epilogue=None):
    M, K = (a.shape[1], a.shape[0]) if ta else a.shape
    nb = b.shape[0] if tb else b.shape[1]
    n = nb if n is None else n
    bm, bn, bk = min(bm, M), min(bn, n), min(bk, K)
    assert M % bm == 0 and n % bn == 0 and K % bk == 0, (name, M, n, K, bm, bn, bk)
    assert (b.shape[1] if tb else b.shape[0]) == K, (name, a.shape, b.shape)
    nk = K // bk
    dims = (((0 if ta else 1,), (1 if tb else 0,)), ((), ()))
    n_out = len(out_dtypes)
    direct = nk > 1 and epilogue is None and out_dtypes == (F32,)
    use_acc = nk > 1 and not direct

    def body(*refs):
        a_ref, b_ref = refs[:2]
        extra_refs = refs[2:2 + len(extras)]
        out_refs = refs[2 + len(extras):2 + len(extras) + n_out]
        p = lax.dot_general(_bf(a_ref[...]), _bf(b_ref[...]), dims, preferred_element_type=F32)

        def finish(acc):
            outs = (acc,) if epilogue is None else epilogue(acc, *[r[...] for r in extra_refs])
            for r, o in zip(out_refs, outs):
                r[...] = o.astype(r.dtype)

        if nk == 1:
            finish(p)
            return
        k = pl.program_id(2)
        acc_ref = out_refs[0] if direct else refs[-1]

        @pl.when(k == 0)
        def _():
            acc_ref[...] = p

        @pl.when(k > 0)
        def _():
            acc_ref[...] += p

        if use_acc:
            @pl.when(k == nk - 1)
            def _():
                finish(acc_ref[...])

    a_spec = pl.BlockSpec((bk, bm), lambda i, j, k: (k, i)) if ta else pl.BlockSpec((bm, bk), lambda i, j, k: (i, k))
    if tb:
        b_spec = pl.BlockSpec((bn, bk), lambda i, j, k: (j + b_col_off, k))
    else:
        b_spec = pl.BlockSpec((bk, bn), lambda i, j, k: (k, j + b_col_off))
    tile = pl.BlockSpec((bm, bn), lambda i, j, k: (i, j))
    outs = pl.pallas_call(
        body, name=name, grid=(M // bm, n // bn, nk),
        in_specs=[a_spec, b_spec] + [tile] * len(extras),
        out_specs=[tile] * n_out,
        out_shape=[jax.ShapeDtypeStruct((M, n), d) for d in out_dtypes],
        scratch_shapes=[pltpu.VMEM((bm, bn), F32)] if use_acc else [],
        compiler_params=_params(3),
    )(a, b, *extras)
    return outs[0] if n_out == 1 else outs


ROW_TILE = 512


def _rows(S):
    tr = min(ROW_TILE, S)
    assert S % tr == 0
    return tr


def _rstd(x):
    return lax.rsqrt(jnp.mean(x * x, axis=-1, keepdims=True) + RMS_EPS)


def _rms_bwd_math(x, g, dy):
    r = _rstd(x)
    gd = g * dy
    dx = r * gd - x * (r * r * r) * jnp.mean(x * gd, axis=-1, keepdims=True)
    dg = jnp.sum(dy * x * r, axis=0, keepdims=True)
    return dx, dg


def _accumulate(ref, value):
    i = pl.program_id(0)

    @pl.when(i == 0)
    def _():
        ref[...] = value

    @pl.when(i > 0)
    def _():
        ref[...] += value


def _row_spec(tr, width):
    return pl.BlockSpec((tr, width), lambda i: (i, 0))


def _vec_spec(width):
    return pl.BlockSpec((1, width), lambda i: (0, 0))


def _rms_fwd(x, g, *, name):
    S, D = x.shape
    tr = _rows(S)

    def body(x_ref, g_ref, o_ref):
        xf = x_ref[...]
        o_ref[...] = (xf * _rstd(xf) * g_ref[...]).astype(o_ref.dtype)

    return pl.pallas_call(
        body, name=name, grid=(S // tr,), in_specs=[_row_spec(tr, D), _vec_spec(D)], out_specs=_row_spec(tr, D),
        out_shape=jax.ShapeDtypeStruct((S, D), BF16), compiler_params=_params(1))(x, g)


def _resid_rms(x, y, g_post, g_next, *, name):
    S, D = x.shape
    tr = _rows(S)
    with_next = g_next is not None

    def body(*refs):
        if with_next:
            x_ref, y_ref, gp_ref, gn_ref, xo_ref, ho_ref = refs
        else:
            x_ref, y_ref, gp_ref, xo_ref = refs
        yf = y_ref[...]
        xn = x_ref[...] + yf * _rstd(yf) * gp_ref[...]
        xo_ref[...] = xn
        if with_next:
            ho_ref[...] = (xn * _rstd(xn) * gn_ref[...]).astype(ho_ref.dtype)

    row, vec = _row_spec(tr, D), _vec_spec(D)
    ins = [x, y, g_post] + ([g_next] if with_next else [])
    outs = pl.pallas_call(
        body, name=name, grid=(S // tr,), in_specs=[row, row, vec] + ([vec] if with_next else []),
        out_specs=[row] + ([row] if with_next else []),
        out_shape=[jax.ShapeDtypeStruct((S, D), F32)] + ([jax.ShapeDtypeStruct((S, D), BF16)] if with_next else []),
        compiler_params=_params(1))(*ins)
    return (outs[0], outs[1]) if with_next else (outs[0], None)


def _rms_bwd(x, g, dy, *, name):
    S, D = x.shape
    tr = _rows(S)

    def body(x_ref, g_ref, dy_ref, dx_ref, dg_ref):
        dx, dg = _rms_bwd_math(x_ref[...], g_ref[...], dy_ref[...])
        dx_ref[...] = dx.astype(dx_ref.dtype)
        _accumulate(dg_ref, dg)

    row, vec = _row_spec(tr, D), _vec_spec(D)
    return pl.pallas_call(
        body, name=name, grid=(S // tr,), in_specs=[row, vec, row], out_specs=[row, vec],
        out_shape=[jax.ShapeDtypeStruct((S, D), BF16), jax.ShapeDtypeStruct((1, D), F32)],
        compiler_params=_params(1))(x, g, dy)


def _rms_bwd_chain(xa, ga, da, resid, xb, gb, *, name):
    S, D = xa.shape
    tr = _rows(S)
    chain = xb is not None

    def body(*refs):
        if chain:
            xa_ref, ga_ref, da_ref, rs_ref, xb_ref, gb_ref, dx_ref, dga_ref, dxb_ref, dgb_ref = refs
        else:
            xa_ref, ga_ref, da_ref, rs_ref, dx_ref, dga_ref = refs
        dxa, dga = _rms_bwd_math(xa_ref[...], ga_ref[...], da_ref[...])
        dx = rs_ref[...] + dxa
        dx_ref[...] = dx
        _accumulate(dga_ref, dga)
        if chain:
            dxb, dgb = _rms_bwd_math(xb_ref[...], gb_ref[...], dx)
            dxb_ref[...] = dxb.astype(dxb_ref.dtype)
            _accumulate(dgb_ref, dgb)

    row, vec = _row_spec(tr, D), _vec_spec(D)
    ins = [xa, ga, da, resid] + ([xb, gb] if chain else [])
    outs = pl.pallas_call(
        body, name=name, grid=(S // tr,), in_specs=[row, vec, row, row] + ([row, vec] if chain else []),
        out_specs=[row, vec] + ([row, vec] if chain else []),
        out_shape=[jax.ShapeDtypeStruct((S, D), F32), jax.ShapeDtypeStruct((1, D), F32)]
        + ([jax.ShapeDtypeStruct((S, D), BF16), jax.ShapeDtypeStruct((1, D), F32)] if chain else []),
        compiler_params=_params(1))(*ins)
    return tuple(outs) if chain else (outs[0], outs[1], None, None)


def _loss_head(y, target, *, name):
    S, D = y.shape
    tr = _rows(S)
    n_tiles = S // tr

    def body(y_ref, t_ref, dy_ref, loss_ref, acc_ref):
        err = y_ref[...] - t_ref[...]
        dy_ref[...] = err * (1.0 / D)
        _accumulate(acc_ref, jnp.sum(err * err, axis=0, keepdims=True))

        @pl.when(pl.program_id(0) == n_tiles - 1)
        def _():
            loss_ref[...] = jnp.sum(acc_ref[...], axis=1, keepdims=True) * (0.5 / D)

    row = _row_spec(tr, D)
    return pl.pallas_call(
        body, name=name, grid=(n_tiles,), in_specs=[row, row],
        out_specs=[row, pl.BlockSpec((1, 1), lambda i: (0, 0))],
        out_shape=[jax.ShapeDtypeStruct((S, D), F32), jax.ShapeDtypeStruct((1, 1), F32)],
        scratch_shapes=[pltpu.VMEM((1, D), F32)], compiler_params=_params(1))(y, target)


SB_TILE = 256
SB_PAIRS = SB_HEADS * SB_HEAD_DIM // LANES
SB_DEAD_LOG = -110.0


def _log_sigmoids(z):
    l1p = jnp.log(1.0 + jnp.exp(-jnp.abs(z)))
    return jnp.minimum(z, 0.0) - l1p, jnp.minimum(-z, 0.0) - l1p


def _split_bf16(x):
    hi = x.astype(BF16)
    lo = (x - hi.astype(F32)).astype(BF16)
    return jnp.concatenate([hi, lo], axis=1)


def _tri(T, cmp):
    j = lax.broadcasted_iota(jnp.int32, (T, T), 0)
    s = lax.broadcasted_iota(jnp.int32, (T, T), 1)
    m = jnp.where(cmp(j, s), 1.0, 0.0).astype(BF16)
    return jnp.concatenate([m, m], axis=0)


def _head_masks():
    lane = lax.broadcasted_iota(jnp.int32, (1, LANES), 1)
    return [lane < SB_HEAD_DIM, lane >= SB_HEAD_DIM]


def _cargo(refs, n_in, n_out, cargo, scatter):
    n = len(cargo)
    if not n:
        return refs, lambda first: None, lambda last: None
    ins = refs[n_in:n_in + n]
    outs = refs[n_in + n + n_out:n_in + n + n_out + n]
    sems = refs[len(refs) - 3:]
    own = refs[:n_in] + refs[n_in + n:n_in + n + n_out] + refs[n_in + n + n_out + n:len(refs) - 3]

    def start(first):
        @pl.when(first)
        def _():
            for cp in _chip_copies(ins, outs, *sems, scatter=scatter):
                cp.start()

    def finish(last):
        @pl.when(last)
        def _():
            for cp in _chip_copies(ins, outs, *sems, scatter=scatter):
                cp.wait()

    return own, start, finish


def _sba_fwd(qkv, *, name, cargo=()):
    S = qkv.shape[0]
    T = min(SB_TILE, S)
    nq = S // T
    scale = SB_HEAD_DIM ** -0.5

    def body(*refs):
        (q_ref, k_ref, v_ref, o_ref, t_ref, first_ref), start_cargo, finish_cargo = _cargo(refs, 3, 3, cargo, False)
        p, i = pl.program_id(0), pl.program_id(1)
        start_cargo(jnp.logical_and(p == 0, i == 0))
        row = lax.broadcasted_iota(jnp.int32, (T, T), 0)
        col = lax.broadcasted_iota(jnp.int32, (T, T), 1)
        strict = col < row
        after = _tri(T, lambda j, s: j > s)
        masks = _head_masks()
        q = q_ref[...] * scale
        qs = [jnp.where(hm, q, jnp.zeros_like(q)) for hm in masks]

        def walk(tiles, carry):
            values, logs = [], []
            for j, diag in tiles:
                rows = pl.ds(pl.multiple_of(j * T, T), T)
                kb = k_ref[rows, :]
                values.append(v_ref[rows, :])
                for qh in qs:
                    z = lax.dot_general(qh, kb, (((1,), (1,)), ((), ())), preferred_element_type=F32)
                    ls, ln = _log_sigmoids(z)
                    logs.append((ls, jnp.where(strict, ln, 0.0) if diag else ln))
            suffixes = [jnp.dot(_split_bf16(ln), after, preferred_element_type=F32) for _, ln in logs]
            for t, (_, diag) in enumerate(tiles):
                out = []
                for h, (C, acc) in enumerate(carry):
                    ls, ln = logs[2 * t + h]
                    a = jnp.exp(ls + suffixes[2 * t + h] + C)
                    if diag:
                        a = jnp.where(strict, a, 0.0)
                    acc = acc + jnp.dot(a.astype(BF16), values[t], preferred_element_type=F32)
                    out.append((C + jnp.sum(ln, axis=1, keepdims=True), acc))
                carry = tuple(out)
            return carry

        fresh = (jnp.zeros((T, 1), F32), jnp.zeros((T, LANES), F32))
        carry = lax.cond(i > 0, lambda: walk([(i, True), (i - 1, False)], (fresh, fresh)),
                         lambda: walk([(i, True)], (fresh, fresh)))

        def alive(state):
            j, ((C0, _), (C1, _)) = state
            return jnp.logical_and(j >= 0, jnp.max(jnp.maximum(C0, C1)) > SB_DEAD_LOG)

        def step(state):
            j, carry = state
            return j - 1, walk([(j, False)], carry)

        j, ((C0, acc0), (C1, acc1)) = lax.while_loop(alive, step, (i - 2, carry))
        t_ref[0] = jnp.broadcast_to(C0, (T, LANES))
        t_ref[1] = jnp.broadcast_to(C1, (T, LANES))
        first_ref[...] = jnp.full((8, LANES), jnp.maximum(j + 1, 0).astype(F32))
        o_ref[...] = jnp.where(masks[0], acc0, acc1).astype(o_ref.dtype)
        finish_cargo(jnp.logical_and(p == SB_PAIRS - 1, i == nq - 1))

    kv = lambda off: pl.BlockSpec((S, LANES), lambda p, i: (0, off + p))
    n = len(cargo)
    return pl.pallas_call(
        body, name=name, grid=(SB_PAIRS, nq),
        in_specs=[pl.BlockSpec((T, LANES), lambda p, i: (i, p)), kv(SB_PAIRS), kv(2 * SB_PAIRS)] + [HBM_SPEC] * n,
        out_specs=[pl.BlockSpec((T, LANES), lambda p, i: (i, p)), pl.BlockSpec((2, T, LANES), lambda p, i: (p, i, 0)),
                   pl.BlockSpec((None, None, 8, LANES), lambda p, i: (p, i, 0, 0))] + [HBM_SPEC] * n,
        out_shape=[jax.ShapeDtypeStruct((S, SB_WIDTH), BF16), jax.ShapeDtypeStruct((SB_HEADS, S, LANES), F32),
                   jax.ShapeDtypeStruct((SB_PAIRS, nq, 8, LANES), F32)] + _chip_exchange_shapes(cargo, False),
        scratch_shapes=_chip_exchange_semaphores(n) if n else [],
        compiler_params=_params(2))(qkv, qkv, qkv, *cargo)


def _sba_bwd(qkv, do, tot, first, *, name, cargo=()):
    S = qkv.shape[0]
    T = min(SB_TILE, S)
    nq = S // T
    scale = SB_HEAD_DIM ** -0.5

    def body(*refs):
        own, start_cargo, finish_cargo = _cargo(refs, 6, 3, cargo, True)
        q_ref, k_ref, v_ref, do_ref, t_ref, first_ref, dq_ref, dk_ref, dv_ref, dk_acc, dv_acc = own
        p, i = pl.program_id(0), pl.program_id(1)
        start_cargo(jnp.logical_and(p == 0, i == 0))

        @pl.when(i == 0)
        def _():
            dk_acc[...] = jnp.zeros_like(dk_acc)
            dv_acc[...] = jnp.zeros_like(dv_acc)

        row = lax.broadcasted_iota(jnp.int32, (T, T), 0)
        col = lax.broadcasted_iota(jnp.int32, (T, T), 1)
        strict = col < row
        upto = _tri(T, lambda j, s: j <= s)
        before = _tri(T, lambda j, s: j < s)
        masks = _head_masks()
        q, do_t = q_ref[...], do_ref[...]
        q = q * scale
        qs = [jnp.where(hm, q, jnp.zeros_like(q)) for hm in masks]
        dos = [jnp.where(hm, do_t, jnp.zeros_like(do_t)) for hm in masks]
        totals = [t_ref[h][:, 0:1] for h in range(2)]
        over_lanes = (((1,), (1,)), ((), ()))
        over_queries = (((0,), (0,)), ((), ()))

        def walk(tiles, carry):
            rows = [pl.ds(pl.multiple_of(j * T, T), T) for j, _ in tiles]
            keys = [k_ref[r, :] for r in rows]
            values = [v_ref[r, :] for r in rows]
            chains = [(t, h) for t in range(len(tiles)) for h in range(2)]
            logs, da = {}, {}
            for t, h in chains:
                z = lax.dot_general(qs[h], keys[t], over_lanes, preferred_element_type=F32)
                ls, ln = _log_sigmoids(z)
                logs[t, h] = (ls, jnp.where(strict, ln, 0.0) if tiles[t][1] else ln)
                da[t, h] = lax.dot_general(dos[h], values[t], over_lanes, preferred_element_type=F32)
            upto_sums = {c: jnp.dot(_split_bf16(logs[c][1]), upto, preferred_element_type=F32) for c in chains}
            a, g = {}, {}
            P = [c[0] for c in carry]
            for t, h in chains:
                ls, ln = logs[t, h]
                a_th = jnp.exp(ls + ((totals[h] - P[h]) - upto_sums[t, h]))
                a[t, h] = jnp.where(strict, a_th, 0.0) if tiles[t][1] else a_th
                g[t, h] = a[t, h] * da[t, h]
                P[h] = P[h] + jnp.sum(ln, axis=1, keepdims=True)
            before_sums = {c: jnp.dot(_split_bf16(g[c]), before, preferred_element_type=F32) for c in chains}
            G = [c[1] for c in carry]
            dq = [c[2] for c in carry]
            dz = {}
            for t, h in chains:
                beta = jnp.exp(logs[t, h][0])
                dz_th = g[t, h] * (1.0 - beta) - (G[h] + before_sums[t, h]) * beta
                dz[t, h] = (jnp.where(strict, dz_th, 0.0) if tiles[t][1] else dz_th).astype(BF16)
                G[h] = G[h] + jnp.sum(g[t, h], axis=1, keepdims=True)
            for t, h in chains:
                dq[h] = dq[h] + jnp.dot(dz[t, h], keys[t], preferred_element_type=F32)
            for t in range(len(tiles)):
                dk_acc[rows[t], :] += sum(
                    lax.dot_general(dz[t, h], qs[h], over_queries, preferred_element_type=F32) for h in range(2))
                dv_acc[rows[t], :] += sum(
                    lax.dot_general(a[t, h].astype(BF16), dos[h], over_queries, preferred_element_type=F32)
                    for h in range(2))
            return tuple((P[h], G[h], dq[h]) for h in range(2))

        zero = jnp.zeros((T, 1), F32)
        fresh = (zero, zero, jnp.zeros((T, LANES), F32))
        last_single = jnp.maximum(i - 1, 0)
        j0 = jnp.clip(jnp.max(first_ref[...]).astype(jnp.int32), 0, last_single)
        carry = lax.fori_loop(j0, last_single, lambda j, c: walk([(j, False)], c), (fresh, fresh))
        (_, _, dq0), (_, _, dq1) = lax.cond(i > 0, lambda: walk([(i - 1, False), (i, True)], carry),
                                            lambda: walk([(i, True)], carry))
        dq_ref[...] = (jnp.where(masks[0], dq0, dq1) * scale).astype(dq_ref.dtype)

        @pl.when(i == nq - 1)
        def _():
            dk_ref[...] = dk_acc[...].astype(dk_ref.dtype)
            dv_ref[...] = dv_acc[...].astype(dv_ref.dtype)

        finish_cargo(jnp.logical_and(p == SB_PAIRS - 1, i == nq - 1))

    kv = lambda off: pl.BlockSpec((S, LANES), lambda p, i: (0, off + p))
    tile = lambda off: pl.BlockSpec((T, LANES), lambda p, i: (i, off + p))
    n = len(cargo)
    return pl.pallas_call(
        body, name=name, grid=(SB_PAIRS, nq),
        in_specs=[tile(0), kv(SB_PAIRS), kv(2 * SB_PAIRS), tile(0), pl.BlockSpec((2, T, LANES), lambda p, i: (p, i, 0)),
                  pl.BlockSpec((None, None, 8, LANES), lambda p, i: (p, i, 0, 0))] + [HBM_SPEC] * n,
        out_specs=[tile(0), kv(0), kv(0)] + [HBM_SPEC] * n,
        out_shape=[jax.ShapeDtypeStruct((S, SB_WIDTH), BF16)] * 3 + _chip_exchange_shapes(cargo, True),
        scratch_shapes=[pltpu.VMEM((S, LANES), F32), pltpu.VMEM((S, LANES), F32)]
        + (_chip_exchange_semaphores(n) if n else []),
        compiler_params=_params(2))(qkv, qkv, qkv, do, tot, first, *cargo)


POOL_TILE = 512


def _by_group(lane, values):
    return jnp.where(lane < 64, values[0], jnp.where(lane < 128, values[1], jnp.where(lane < 192, values[2], values[3])))


def _pool_inv_count(first_row, n_rows):
    t = first_row + lax.broadcasted_iota(jnp.int32, (n_rows, POOL_WIDTH), 0)
    lane = lax.broadcasted_iota(jnp.int32, (n_rows, POOL_WIDTH), 1)
    window = _by_group(lane, POOL_WINDOWS)
    return 1.0 / jnp.clip(t + 1, 1, window).astype(F32), lane


def _pooled(ext, first_row, R):
    n = R + POOL_HALO
    s2 = ext + pltpu.roll(ext, 1, 0)
    s4 = s2 + pltpu.roll(s2, 2, 0)
    s8 = s4 + pltpu.roll(s4, 4, 0)
    s16 = s8 + pltpu.roll(s8, 8, 0)
    inv, lane = _pool_inv_count(first_row - POOL_HALO, n)
    pooled = _by_group(lane, (s2, s4, s8, s16)) * inv - ext
    return pooled[POOL_HALO:, :]


def _pool_specs(S, R, col):
    per = R // POOL_HALO
    tile = pl.BlockSpec((R, POOL_WIDTH), lambda i: (i, col))
    prev = pl.BlockSpec((POOL_HALO, POOL_WIDTH), lambda i: (jnp.maximum(i * per - 1, 0), col))
    return tile, prev


def _pool_fwd(rest, w_bd, scale, *, name):
    S = rest.shape[0]
    R = min(POOL_TILE, S)

    def body(p_ref, prev_ref, w_ref, s_ref, o_ref, ext_ref):
        i = pl.program_id(0)
        ext_ref[:POOL_HALO, :] = jnp.where(i > 0, prev_ref[...], 0.0)
        ext_ref[POOL_HALO:, :] = p_ref[...]
        pooled = _pooled(ext_ref[...], i * R, R)
        mixed = jnp.dot(pooled.astype(BF16), w_ref[...], preferred_element_type=F32)
        o_ref[...] = (mixed * s_ref[...]).astype(o_ref.dtype)

    tile, prev = _pool_specs(S, R, 0)
    return pl.pallas_call(
        body, name=name, grid=(S // R,),
        in_specs=[tile, prev, pl.BlockSpec((POOL_WIDTH, POOL_WIDTH), lambda i: (0, 0)), _vec_spec(POOL_WIDTH)],
        out_specs=_row_spec(R, POOL_WIDTH), out_shape=jax.ShapeDtypeStruct((S, POOL_WIDTH), BF16),
        scratch_shapes=[pltpu.VMEM((R + POOL_HALO, POOL_WIDTH), F32)], compiler_params=_params(1))(rest, rest, w_bd, scale)


def _pool_bwd(rest, do, w_bd, scale, *, name):
    S = rest.shape[0]
    R = min(POOL_TILE, S)
    n_tiles = S // R
    per = R // POOL_HALO
    n = R + POOL_HALO

    def body(p_ref, prev_ref, do_ref, nxt_ref, w_ref, s_ref, dp_ref, dw_ref, ds_ref, ext_ref, dext_ref):
        i = pl.program_id(0)
        ext_ref[:POOL_HALO, :] = jnp.where(i > 0, prev_ref[...], 0.0)
        ext_ref[POOL_HALO:, :] = p_ref[...]
        pooled = _pooled(ext_ref[...], i * R, R).astype(BF16)
        w = w_ref[...]
        mixed = jnp.dot(pooled, w, preferred_element_type=F32)
        do_t = do_ref[...]
        _accumulate(ds_ref, jnp.sum(do_t * mixed, axis=0, keepdims=True))
        dext_ref[:R, :] = do_t
        dext_ref[R:, :] = jnp.where(i < n_tiles - 1, nxt_ref[...], 0.0)
        dmixed = (dext_ref[...] * s_ref[...]).astype(BF16)
        dpooled = lax.dot_general(dmixed, w, (((1,), (1,)), ((), ())), preferred_element_type=F32)
        _accumulate(dw_ref, lax.dot_general(pooled, dmixed[:R, :], (((0,), (0,)), ((), ())), preferred_element_type=F32))
        inv, lane = _pool_inv_count(i * R, n)
        u = dpooled * inv
        f2 = u + pltpu.roll(u, n - 1, 0)
        f4 = f2 + pltpu.roll(f2, n - 2, 0)
        f8 = f4 + pltpu.roll(f4, n - 4, 0)
        f16 = f8 + pltpu.roll(f8, n - 8, 0)
        dp = _by_group(lane, (f2, f4, f8, f16)) - dpooled
        dp_ref[...] = dp[:R, :].astype(dp_ref.dtype)

    tile, prev = _pool_specs(S, R, 0)
    nxt = pl.BlockSpec((POOL_HALO, POOL_WIDTH), lambda i: (jnp.minimum((i + 1) * per, S // POOL_HALO - 1), 0))
    full = pl.BlockSpec((POOL_WIDTH, POOL_WIDTH), lambda i: (0, 0))
    return pl.pallas_call(
        body, name=name, grid=(n_tiles,),
        in_specs=[tile, prev, _row_spec(R, POOL_WIDTH), nxt, full, _vec_spec(POOL_WIDTH)],
        out_specs=[_row_spec(R, POOL_WIDTH), full, _vec_spec(POOL_WIDTH)],
        out_shape=[jax.ShapeDtypeStruct((S, POOL_WIDTH), BF16), jax.ShapeDtypeStruct((POOL_WIDTH, POOL_WIDTH), F32),
                   jax.ShapeDtypeStruct((1, POOL_WIDTH), F32)],
        scratch_shapes=[pltpu.VMEM((n, POOL_WIDTH), F32), pltpu.VMEM((n, POOL_WIDTH), F32)],
        compiler_params=_params(1))(rest, rest, do, do, w_bd, scale)


GM_TILE = 512
GELU_C = math.sqrt(2.0 / math.pi)
GELU_A = 0.044715


def _gelu(x):
    return 0.5 * x * (1.0 + jnp.tanh(GELU_C * (x + GELU_A * x * x * x)))


def _gelu_and_grad(x):
    t = jnp.tanh(GELU_C * (x + GELU_A * x * x * x))
    y = 0.5 * x * (1.0 + t)
    dy = 0.5 * (1.0 + t) + 0.5 * x * (1.0 - t * t) * (GELU_C * (1.0 + 3.0 * GELU_A * x * x))
    return y, dy


def _group_lane_masks():
    lane = lax.broadcasted_iota(jnp.int32, (1, GM_WIDTH), 1)
    return [(lane >= g * GM_GROUP_DIM) & (lane < (g + 1) * GM_GROUP_DIM) for g in range(GM_GROUPS)]


def _stack_groups(x, masks):
    return jnp.concatenate([jnp.where(m, x, jnp.zeros_like(x)) for m in masks], axis=0)


def _gm_mixed(vn, ws_cat, bias, masks, R):
    chunks = []
    for c in range(R // GM_CHUNK):
        vc = vn[c * GM_CHUNK:(c + 1) * GM_CHUNK, :]
        chunks.append(jnp.dot(ws_cat, _stack_groups(vc, masks), preferred_element_type=F32) + bias)
    return jnp.concatenate(chunks, axis=0)


def _gm_specs(S, R):
    u = pl.BlockSpec((R, GM_WIDTH), lambda i: (i, 1))
    v = pl.BlockSpec((R, GM_WIDTH), lambda i: (i, 2))
    ws = pl.BlockSpec((GM_CHUNK, GM_GROUPS * GM_CHUNK), lambda i: (0, 0))
    bias = pl.BlockSpec((GM_CHUNK, GM_WIDTH), lambda i: (0, 0))
    return u, v, ws, bias


def _gm_fwd(rest, gain, ws_cat, bias, *, name):
    S = rest.shape[0]
    R = min(GM_TILE, S)

    def body(u_ref, v_ref, g_ref, ws_ref, b_ref, o_ref):
        gv = _gelu(v_ref[...])
        vn = (gv * _rstd(gv) * g_ref[...]).astype(BF16)
        mixed = _gm_mixed(vn, ws_ref[...], b_ref[...], _group_lane_masks(), R)
        o_ref[...] = (_gelu(u_ref[...]) * mixed).astype(o_ref.dtype)

    u_spec, v_spec, ws_spec, bias_spec = _gm_specs(S, R)
    return pl.pallas_call(
        body, name=name, grid=(S // R,), in_specs=[u_spec, v_spec, _vec_spec(GM_WIDTH), ws_spec, bias_spec],
        out_specs=_row_spec(R, GM_WIDTH), out_shape=jax.ShapeDtypeStruct((S, GM_WIDTH), BF16),
        compiler_params=_params(1))(rest, rest, gain, ws_cat, bias)


def _gm_bwd(rest, do, gain, ws_cat, wst_cat, bias, *, name):
    S = rest.shape[0]
    R = min(GM_TILE, S)

    def body(u_ref, v_ref, do_ref, g_ref, ws_ref, wst_ref, b_ref, du_ref, dv_ref, dg_ref, dws_ref, db_ref):
        masks = _group_lane_masks()
        gain_v = g_ref[...]
        gu, dgu = _gelu_and_grad(u_ref[...])
        gv, dgv = _gelu_and_grad(v_ref[...])
        r = _rstd(gv)
        vn = (gv * r * gain_v).astype(BF16)
        mixed = _gm_mixed(vn, ws_ref[...], b_ref[...], masks, R)
        do_t = do_ref[...]
        du_ref[...] = (do_t * mixed * dgu).astype(du_ref.dtype)
        dmix = do_t * gu
        dmix_b = dmix.astype(BF16)
        wst = wst_ref[...]
        dvn_chunks, db, dws = [], None, [None] * GM_GROUPS
        for c in range(R // GM_CHUNK):
            rows = slice(c * GM_CHUNK, (c + 1) * GM_CHUNK)
            dc, dcb, vc = dmix[rows, :], dmix_b[rows, :], vn[rows, :]
            db = dc if db is None else db + dc
            dvn_chunks.append(jnp.dot(wst, _stack_groups(dcb, masks), preferred_element_type=F32))
            for g, m in enumerate(masks):
                part = lax.dot_general(jnp.where(m, dcb, jnp.zeros_like(dcb)), vc, (((1,), (1,)), ((), ())),
                                       preferred_element_type=F32)
                dws[g] = part if dws[g] is None else dws[g] + part
        dvn = jnp.concatenate(dvn_chunks, axis=0)
        lane = lax.broadcasted_iota(jnp.int32, (1, LANES), 1)
        db_groups = jnp.zeros((GM_CHUNK, LANES), F32)
        for g, m in enumerate(masks):
            total = jnp.sum(jnp.where(m, db, 0.0), axis=1, keepdims=True)
            db_groups = db_groups + jnp.where(lane == g, total, 0.0)
        _accumulate(db_ref, db_groups)
        i = pl.program_id(0)
        for g in range(GM_GROUPS):
            @pl.when(i == 0)
            def _(g=g):
                dws_ref[g] = dws[g]

            @pl.when(i > 0)
            def _(g=g):
                dws_ref[g] += dws[g]
        _accumulate(dg_ref, jnp.sum(dvn * gv * r, axis=0, keepdims=True))
        gd = gain_v * dvn
        dgv_in = r * gd - gv * (r * r * r) * jnp.mean(gv * gd, axis=-1, keepdims=True)
        dv_ref[...] = (dgv_in * dgv).astype(dv_ref.dtype)

    u_spec, v_spec, ws_spec, bias_spec = _gm_specs(S, R)
    row, vec = _row_spec(R, GM_WIDTH), _vec_spec(GM_WIDTH)
    dws_spec = pl.BlockSpec((GM_GROUPS, GM_CHUNK, GM_CHUNK), lambda i: (0, 0, 0))
    return pl.pallas_call(
        body, name=name, grid=(S // R,), in_specs=[u_spec, v_spec, row, vec, ws_spec, ws_spec, bias_spec],
        out_specs=[row, row, vec, dws_spec, pl.BlockSpec((GM_CHUNK, LANES), lambda i: (0, 0))],
        out_shape=[jax.ShapeDtypeStruct((S, GM_WIDTH), BF16)] * 2
        + [jax.ShapeDtypeStruct((1, GM_WIDTH), F32), jax.ShapeDtypeStruct((GM_GROUPS, GM_CHUNK, GM_CHUNK), F32),
           jax.ShapeDtypeStruct((GM_CHUNK, LANES), F32)],
        compiler_params=_params(1))(rest, rest, do, gain, ws_cat, wst_cat, bias)


GATE_ROWS = 1024
GATE_COLS = 256
GATE_BLOCKS = D_MODEL // GATE_COLS


def _gate_spec(tr, k):
    return pl.BlockSpec((tr, GATE_COLS), lambda i, j: (i, GATE_BLOCKS * k + j))


def _merge_fwd(gates, branches, *, name):
    S = gates.shape[0]
    tr = min(GATE_ROWS, S)

    def body(g0, g1, g2, b0, b1, b2, o_ref):
        acc = None
        for g_ref, b_ref in ((g0, b0), (g1, b1), (g2, b2)):
            term = jax.nn.sigmoid(g_ref[...].astype(F32)) * b_ref[...].astype(F32)
            acc = term if acc is None else acc + term
        o_ref[...] = acc.astype(o_ref.dtype)

    tile = pl.BlockSpec((tr, GATE_COLS), lambda i, j: (i, j))
    return pl.pallas_call(
        body, name=name, grid=(S // tr, GATE_BLOCKS),
        in_specs=[_gate_spec(tr, k) for k in range(N_BRANCH)] + [tile] * N_BRANCH, out_specs=tile,
        out_shape=jax.ShapeDtypeStruct((S, D_MODEL), BF16), compiler_params=_params(2))(gates, gates, gates, *branches)


def _merge_bwd(gates, branches, dmerged, *, name):
    S = gates.shape[0]
    tr = min(GATE_ROWS, S)

    def body(g0, g1, g2, b0, b1, b2, dm_ref, dg0, dg1, dg2, db0, db1, db2):
        dm = dm_ref[...].astype(F32)
        for g_ref, b_ref, dg_ref, db_ref in ((g0, b0, dg0, db0), (g1, b1, dg1, db1), (g2, b2, dg2, db2)):
            s = jax.nn.sigmoid(g_ref[...].astype(F32))
            db_ref[...] = (dm * s).astype(db_ref.dtype)
            dg_ref[...] = (dm * b_ref[...].astype(F32) * s * (1.0 - s)).astype(dg_ref.dtype)

    tile = pl.BlockSpec((tr, GATE_COLS), lambda i, j: (i, j))
    return pl.pallas_call(
        body, name=name, grid=(S // tr, GATE_BLOCKS),
        in_specs=[_gate_spec(tr, k) for k in range(N_BRANCH)] + [tile] * (N_BRANCH + 1), out_specs=[tile] * (2 * N_BRANCH),
        out_shape=[jax.ShapeDtypeStruct((S, D_MODEL), BF16)] * (2 * N_BRANCH),
        compiler_params=_params(2))(gates, gates, gates, *branches, dmerged)


TILE_BYTES = 24 * 1024 * 1024


BF16_ROWS = 16


def _tile_rows(rows, cols, n_arrays):
    padded = -(-cols // LANES) * LANES
    cap = max(BF16_ROWS, TILE_BYTES // (2 * n_arrays * padded * 4))
    best = None
    for tr in range(BF16_ROWS, min(rows, cap) + 1, BF16_ROWS):
        if rows % tr == 0:
            best = tr
    assert best is not None, (rows, cols)
    return best


def _sum_slots(stack, *, name):
    n, R, C = stack.shape
    tr = _tile_rows(R, C, n + 1)

    def body(s_ref, o_ref):
        acc = s_ref[0].astype(F32)
        for k in range(1, n):
            acc = acc + s_ref[k].astype(F32)
        o_ref[...] = acc

    return pl.pallas_call(
        body, name=name, grid=(R // tr,), in_specs=[pl.BlockSpec((n, tr, C), lambda i: (0, i, 0))],
        out_specs=_row_spec(tr, C), out_shape=jax.ShapeDtypeStruct((R, C), F32), compiler_params=_params(1))(stack)


def _add_own_half(parts, received, core, *, name):
    n, R, C = parts.shape
    half = R // 2
    tr = _tile_rows(half, C, 3)
    steps = half // tr

    def body(core_ref, own_ref, got_ref, o_ref):
        o_ref[...] = (own_ref[...] + got_ref[...]).astype(o_ref.dtype)

    tile = pl.BlockSpec((None, tr, C), lambda d, i, core_ref: (d, i, 0))
    own = pl.BlockSpec((None, tr, C), lambda d, i, core_ref: (d, core_ref[0] * steps + i, 0))
    return pl.pallas_call(
        body, name=name, out_shape=jax.ShapeDtypeStruct((n, half, C), BF16),
        grid_spec=pltpu.PrefetchScalarGridSpec(num_scalar_prefetch=1, grid=(n, steps), in_specs=[own, tile], out_specs=tile),
        compiler_params=_params(2))(core, parts, received)


def _adamw_math(w, m, v, g):
    m_new = ADAM_B1 * m + (1.0 - ADAM_B1) * g
    v_new = ADAM_B2 * v + (1.0 - ADAM_B2) * jnp.square(g)
    m_hat = m_new / (1.0 - ADAM_B1 ** ADAM_STEP)
    v_hat = v_new / (1.0 - ADAM_B2 ** ADAM_STEP)
    return -ADAM_LR * (m_hat / (jnp.sqrt(v_hat) + ADAM_EPS) + ADAM_WD * w), m_new, v_new


def _adamw_halves(w, m, v, mine, theirs, core, *, name):
    L, r, C = w.shape
    tr = _tile_rows(r // 2, C, 9)
    steps = r // 2 // tr

    def body(core_ref, w_ref, m_ref, v_ref, mine_ref, theirs_ref, go_ref, d_ref, mo_ref, vo_ref):
        in_my_half = pl.program_id(1) // steps == core_ref[0]
        g = jnp.where(in_my_half, mine_ref[...], theirs_ref[...])
        go_ref[...] = g
        d_ref[...], mo_ref[...], vo_ref[...] = _adamw_math(w_ref[...], m_ref[...], v_ref[...], g)

    row = pl.BlockSpec((None, tr, C), lambda l, i, core_ref: (l, i, 0))
    half = pl.BlockSpec((None, tr, C), lambda l, i, core_ref: (l, i % steps, 0))
    return pl.pallas_call(
        body, name=name, out_shape=[jax.ShapeDtypeStruct((L, r, C), F32)] * 4,
        grid_spec=pltpu.PrefetchScalarGridSpec(
            num_scalar_prefetch=1, grid=(L, r // tr), in_specs=[row, row, row, half, half], out_specs=[row] * 4),
        compiler_params=_params(2))(core, w, m, v, mine, theirs)


HBM_SPEC = pl.BlockSpec(memory_space=pl.ANY)


def _position():
    return lax.axis_index("x"), lax.axis_index("y"), lax.axis_index("c")


def _other_chips(x, y):
    return [(1 - x, y), (x, 1 - y), (1 - x, 1 - y)]


def _chip_exchange(arrays, *, scatter, name):
    n = len(arrays)

    def body(*refs):
        copies = _chip_copies(refs[:n], refs[n:2 * n], *refs[2 * n:], scatter=scatter)
        for cp in copies:
            cp.start()
        for cp in copies:
            cp.wait()

    return pl.pallas_call(
        body, name=name, in_specs=[HBM_SPEC] * n, out_specs=[HBM_SPEC] * n, out_shape=_chip_exchange_shapes(arrays, scatter),
        scratch_shapes=_chip_exchange_semaphores(n))(*arrays)


def _chip_exchange_shapes(arrays, scatter):
    return [jax.ShapeDtypeStruct(a.shape if scatter else (N_CHIPS,) + a.shape, a.dtype) for a in arrays]


def _chip_exchange_semaphores(n):
    return [pltpu.SemaphoreType.DMA((3 * n,)), pltpu.SemaphoreType.DMA((3 * n,)), pltpu.SemaphoreType.DMA((n,))]


def _chip_copies(ins, outs, send_sems, recv_sems, local_sems, *, scatter):
    x, y, c = _position()
    me = 2 * x + y
    copies = []
    for a in range(len(ins)):
        own = ins[a].at[me] if scatter else ins[a]
        copies.append(pltpu.make_async_copy(own, outs[a].at[me], local_sems.at[a]))
        for k, (px, py) in enumerate(_other_chips(x, y)):
            src = ins[a].at[2 * px + py] if scatter else ins[a]
            copies.append(pltpu.make_async_remote_copy(
                src_ref=src, dst_ref=outs[a].at[me], send_sem=send_sems.at[3 * a + k],
                recv_sem=recv_sems.at[3 * a + k], device_id=(px, py, c), device_id_type=MESH))
    return copies


def _sibling_swap(arrays, *, name):
    n = len(arrays)

    def body(*refs):
        ins, outs = refs[:n], refs[n:2 * n]
        send_sems, recv_sems = refs[2 * n:]
        x, y, c = _position()
        copies = []
        for a in range(n):
            cp = pltpu.make_async_remote_copy(
                src_ref=ins[a], dst_ref=outs[a], send_sem=send_sems.at[a], recv_sem=recv_sems.at[a],
                device_id=(x, y, 1 - c), device_id_type=MESH)
            cp.start()
            copies.append(cp)
        for cp in copies:
            cp.wait()

    return pl.pallas_call(
        body, name=name, in_specs=[HBM_SPEC] * n, out_specs=[HBM_SPEC] * n,
        out_shape=[jax.ShapeDtypeStruct(a.shape, a.dtype) for a in arrays],
        scratch_shapes=[pltpu.SemaphoreType.DMA((n,)), pltpu.SemaphoreType.DMA((n,))],
    )(*arrays)


def _sibling_other_half(arrays, *, name):
    n = len(arrays)

    def body(*refs):
        ins, outs = refs[:n], refs[n:2 * n]
        send_sems, recv_sems = refs[2 * n:]
        x, y, c = _position()
        copies = []
        for a in range(n):
            half = ins[a].shape[1] // 2
            theirs = ins[a].at[:, pl.ds(pl.multiple_of((1 - c) * half, BF16_ROWS), half), :]
            cp = pltpu.make_async_remote_copy(
                src_ref=theirs, dst_ref=outs[a], send_sem=send_sems.at[a], recv_sem=recv_sems.at[a],
                device_id=(x, y, 1 - c), device_id_type=MESH)
            cp.start()
            copies.append(cp)
        for cp in copies:
            cp.wait()

    return pl.pallas_call(
        body, name=name, in_specs=[HBM_SPEC] * n, out_specs=[HBM_SPEC] * n,
        out_shape=[jax.ShapeDtypeStruct((a.shape[0], a.shape[1] // 2, a.shape[2]), a.dtype) for a in arrays],
        scratch_shapes=[pltpu.SemaphoreType.DMA((n,)), pltpu.SemaphoreType.DMA((n,))],
    )(*arrays)


def _small_update(grads, ws, ms, vs, *, name):
    n = len(grads)

    def body(*refs):
        g_in, w_in, m_in, v_in = (refs[k * n:(k + 1) * n] for k in range(4))
        g_out, d_out, m_out, v_out = (refs[(4 + k) * n:(5 + k) * n] for k in range(4))
        from_sibling, chip_sums = refs[8 * n:9 * n], refs[9 * n:10 * n]
        sibling_send, sibling_recv, chip_send, chip_recv = refs[10 * n:]
        x, y, c = _position()
        me = 2 * x + y
        swaps = [pltpu.make_async_remote_copy(
            src_ref=g_in[a], dst_ref=from_sibling[a], send_sem=sibling_send.at[a], recv_sem=sibling_recv.at[a],
            device_id=(x, y, 1 - c), device_id_type=MESH) for a in range(n)]
        for cp in swaps:
            cp.start()
        for cp in swaps:
            cp.wait()
        for a in range(n):
            chip_sums[a][me] = g_in[a][...] + from_sibling[a][...]
        sends = [pltpu.make_async_remote_copy(
            src_ref=chip_sums[a].at[me], dst_ref=chip_sums[a].at[me], send_sem=chip_send.at[3 * a + k],
            recv_sem=chip_recv.at[3 * a + k], device_id=(px, py, c), device_id_type=MESH)
            for a in range(n) for k, (px, py) in enumerate(_other_chips(x, y))]
        for cp in sends:
            cp.start()
        for cp in sends:
            cp.wait()
        for a in range(n):
            g = chip_sums[a][0]
            for s in range(1, N_CHIPS):
                g = g + chip_sums[a][s]
            g_out[a][...] = g
            d_out[a][...], m_out[a][...], v_out[a][...] = _adamw_math(w_in[a][...], m_in[a][...], v_in[a][...], g)

    vmem = pl.BlockSpec(memory_space=pltpu.VMEM)
    shapes = [jax.ShapeDtypeStruct(g.shape, F32) for g in grads]
    outs = pl.pallas_call(
        body, name=name, in_specs=[vmem] * (4 * n), out_specs=[vmem] * (4 * n), out_shape=shapes * 4,
        scratch_shapes=[pltpu.VMEM(g.shape, F32) for g in grads] + [pltpu.VMEM((N_CHIPS,) + g.shape, F32) for g in grads]
        + [pltpu.SemaphoreType.DMA((n,)), pltpu.SemaphoreType.DMA((n,)),
           pltpu.SemaphoreType.DMA((3 * n,)), pltpu.SemaphoreType.DMA((3 * n,))],
        compiler_params=pltpu.CompilerParams(vmem_limit_bytes=VMEM_LIMIT),
    )(*grads, *ws, *ms, *vs)
    return outs[:n], outs[n:2 * n], outs[2 * n:3 * n], outs[3 * n:]


def _relu2(p):
    return p, jnp.square(jnp.maximum(p, 0.0))


def _relu2_grad(p, a):
    return (p * (2.0 * jnp.maximum(a.astype(F32), 0.0)),)


def _mixer_constants(w_pool, w_spatial, b_spatial):
    eye = jnp.eye(len(POOL_WINDOWS), dtype=F32)
    w_bd = (eye[:, None, :, None] * w_pool[:, :, None, :]).reshape(POOL_WIDTH, POOL_WIDTH).astype(BF16)
    causal = jnp.tril(jnp.ones((GM_CHUNK, GM_CHUNK), dtype=bool))
    ws = jnp.where(causal[None], w_spatial, 0.0).astype(BF16)
    ws_cat = ws.transpose(1, 0, 2).reshape(GM_CHUNK, GM_GROUPS * GM_CHUNK)
    wst_cat = ws.transpose(2, 0, 1).reshape(GM_CHUNK, GM_GROUPS * GM_CHUNK)
    bias = jnp.repeat(b_spatial.T, GM_GROUP_DIM, axis=1)
    return w_bd, ws_cat, wst_cat, bias


def _local_step(x, target, half_shards, small, core):
    L = small["g_mix_pre"].shape[0]
    vec = lambda name, l: small[name][l][None, :]
    consts = [_mixer_constants(small["w_pool"][l], small["w_spatial"][l], small["b_spatial"][l]) for l in range(L)]
    core_index = core.astype(jnp.int32).reshape(1)
    layer_halves = lambda l: [half_shards[n][l] for n in BIG_WEIGHTS]

    def full_weights(gathered):
        theirs = _sibling_swap(gathered, name="swap_weight_halves")
        return {n: _full_weight(n, a, b, core) for n, a, b in zip(BIG_WEIGHTS, gathered, theirs)}

    weights = [full_weights(_chip_exchange(layer_halves(0), scatter=False, name="gather_weights"))]
    saved = []
    h = _rms_fwd(x, vec("g_mix_pre", 0), name="rms_in")
    for l in range(L):
        w_bd, ws_cat, wst_cat, bias = consts[l]
        proj = lambda n, off, dtype, name: _matmul(h, weights[l]["w_in"], tb=True, n=n, bn=PROJ_BLOCK,
                                                   b_col_off=off // PROJ_BLOCK, out_dtypes=(dtype,), name=name)
        qkv = proj(QKV_WIDTH, 0, BF16, "proj_qkv")
        rest = proj(MIX_WIDTH, QKV_WIDTH, F32, "proj_mix")
        gates = proj(GATE_WIDTH, QKV_WIDTH + MIX_WIDTH, BF16, "proj_gates")
        if l + 1 < L:
            o_sb, tot, first, *gathered = _sba_fwd(qkv, cargo=layer_halves(l + 1), name="sba_fwd_gather")
            weights.append(full_weights(gathered))
        else:
            o_sb, tot, first = _sba_fwd(qkv, name="sba_fwd")
        o_pool = _pool_fwd(rest, w_bd, vec("pool_scale", l), name="pool_fwd")
        o_gm = _gm_fwd(rest, vec("gm_gain", l), ws_cat, bias, name="gm_fwd")
        branches = (_matmul(o_sb, weights[l]["w_br_sb"], out_dtypes=(BF16,), name="br_sb"),
                    _matmul(o_pool, weights[l]["w_br_pool"], out_dtypes=(BF16,), name="br_pool"),
                    _matmul(o_gm, weights[l]["w_br_gm"], out_dtypes=(BF16,), name="br_gm"))
        merged = _merge_fwd(gates, branches, name="merge_fwd")
        y = _matmul(merged, weights[l]["w_out"], name="out_proj")
        x1, h2 = _resid_rms(x, y, vec("g_mix_post", l), vec("g_ff_pre", l), name="resid_mix")
        a, r = _matmul(h2, weights[l]["w_ff_in"], out_dtypes=(BF16, BF16), epilogue=_relu2, name="ff_in")
        ff = _matmul(r, weights[l]["w_ff_out"], name="ff_out")
        g_next = vec("g_mix_pre", l + 1) if l + 1 < L else None
        x2, h_next = _resid_rms(x1, ff, vec("g_ff_post", l), g_next, name="resid_ff" if l + 1 < L else "resid_last")
        saved.append(dict(x=x, h=h, qkv=qkv, rest=rest, gates=gates, o_sb=o_sb, tot=tot, first=first, o_pool=o_pool,
                          o_gm=o_gm, branches=branches, merged=merged, y=y, x1=x1, h2=h2, a=a, r=r, ff=ff))
        x, h = x2, h_next

    dx2, loss = _loss_head(x, target, name="loss_head")
    gb = {k: [None] * L for k in ("w_in", "w_br_sb", "w_br_pool", "w_br_gm", "w_out", "w_ff_in", "w_ff_out")}
    gs = {k: [None] * L for k in ("w_pool", "pool_scale", "gm_gain", "w_spatial", "b_spatial", "g_mix_pre",
                                  "g_mix_post", "g_ff_pre", "g_ff_post")}
    d_ff, gs["g_ff_post"][L - 1] = _rms_bwd(saved[-1]["ff"], vec("g_ff_post", L - 1), dx2, name="rms_bwd_last")
    received = [None] * L
    chip_sums = ()
    for l in reversed(range(L)):
        s = saved[l]
        w_bd, ws_cat, wst_cat, bias = consts[l]
        da = _matmul(d_ff, weights[l]["w_ff_out"], tb=True, out_dtypes=(BF16,), extras=(s["a"],), epilogue=_relu2_grad,
                     name="ff_out_dx")
        gb["w_ff_out"][l] = _matmul(s["r"], d_ff, ta=True, name="ff_out_dw")
        dh2 = _matmul(da, weights[l]["w_ff_in"], tb=True, name="ff_in_dx")
        gb["w_ff_in"][l] = _matmul(s["h2"], da, ta=True, name="ff_in_dw")
        dx1, gs["g_ff_pre"][l], dy, gs["g_mix_post"][l] = _rms_bwd_chain(
            s["x1"], vec("g_ff_pre", l), dh2, dx2, s["y"], vec("g_mix_post", l), name="rms_bwd_mid")
        dmerged = _matmul(dy, weights[l]["w_out"], tb=True, out_dtypes=(BF16,), name="out_proj_dx")
        gb["w_out"][l] = _matmul(s["merged"], dy, ta=True, name="out_proj_dw")
        dg0, dg1, dg2, db_sb, db_pool, db_gm = _merge_bwd(s["gates"], s["branches"], dmerged, name="merge_bwd")
        do_sb = _matmul(db_sb, weights[l]["w_br_sb"], tb=True, out_dtypes=(BF16,), name="br_sb_dx")
        gb["w_br_sb"][l] = _matmul(s["o_sb"], db_sb, ta=True, name="br_sb_dw")
        do_pool = _matmul(db_pool, weights[l]["w_br_pool"], tb=True, name="br_pool_dx")
        gb["w_br_pool"][l] = _matmul(s["o_pool"], db_pool, ta=True, name="br_pool_dw")
        do_gm = _matmul(db_gm, weights[l]["w_br_gm"], tb=True, name="br_gm_dx")
        gb["w_br_gm"][l] = _matmul(s["o_gm"], db_gm, ta=True, name="br_gm_dw")
        if chip_sums:
            dq, dk, dv, *received[l + 1] = _sba_bwd(s["qkv"], do_sb, s["tot"], s["first"], cargo=chip_sums,
                                                    name="sba_bwd_exchange")
        else:
            dq, dk, dv = _sba_bwd(s["qkv"], do_sb, s["tot"], s["first"], name="sba_bwd")
        dp, dw_bd, gs["pool_scale"][l] = _pool_bwd(s["rest"], do_pool, w_bd, vec("pool_scale", l), name="pool_bwd")
        du, dgv, gs["gm_gain"][l], dws, db = _gm_bwd(s["rest"], do_gm, vec("gm_gain", l), ws_cat, wst_cat, bias,
                                                      name="gm_bwd")
        gs["w_pool"][l] = jnp.stack([dw_bd[g * 64:(g + 1) * 64, g * 64:(g + 1) * 64] for g in range(len(POOL_WINDOWS))])
        gs["w_spatial"][l] = jnp.where(jnp.tril(jnp.ones((GM_CHUNK, GM_CHUNK), dtype=bool))[None], dws, 0.0)
        gs["b_spatial"][l] = db[:, :GM_GROUPS].T
        dproj = jnp.concatenate([dq, dk, dv, dp, du, dgv, dg0, dg1, dg2], axis=1)
        dh = _matmul(dproj, weights[l]["w_in"], bk=D_IN // 3, name="proj_dx")
        gb["w_in"][l] = _matmul(dproj, s["h"], ta=True, bm=PROJ_BLOCK, name="proj_dw")
        parts = [_parts_by_chip(n, gb[n][l]) for n in BIG_WEIGHTS]
        from_sibling = _sibling_other_half(parts, name="swap_grad_halves")
        chip_sums = [_add_own_half(p, f, core_index, name="sum_cores") for p, f in zip(parts, from_sibling)]
        if l == 0:
            received[0] = _chip_exchange(chip_sums, scatter=True, name="exchange_grads")
        if l > 0:
            dx2, gs["g_mix_pre"][l], d_ff, gs["g_ff_post"][l - 1] = _rms_bwd_chain(
                s["x"], vec("g_mix_pre", l), dh, dx1, saved[l - 1]["ff"], vec("g_ff_post", l - 1), name="rms_bwd_mid")
        else:
            dx2, gs["g_mix_pre"][l], _, _ = _rms_bwd_chain(s["x"], vec("g_mix_pre", l), dh, dx1, None, None,
                                                           name="rms_bwd_first")
    small_grads = {k: jnp.stack([g.reshape(small[k].shape[1:]) for g in v]) for k, v in gs.items()}
    return loss, dx2, received, small_grads


TRANSPOSED = ("w_in",)
COLUMN_SHARDED = ("w_br_sb", "w_br_pool", "w_br_gm", "w_ff_in")
ROW_SHARDED = ("w_in", "w_out", "w_ff_out")
BIG_WEIGHTS = ("w_in", "w_br_sb", "w_br_pool", "w_br_gm", "w_ff_in", "w_out", "w_ff_out")
SMALL_WEIGHTS = ("w_pool", "pool_scale", "gm_gain", "w_spatial", "b_spatial", "g_mix_pre", "g_mix_post", "g_ff_pre",
                 "g_ff_post")
WEIGHT_ORDER = ("w_in", "w_pool", "pool_scale", "gm_gain", "w_spatial", "b_spatial", "w_br_sb", "w_br_pool", "w_br_gm",
                "w_out", "g_mix_pre", "g_mix_post", "g_ff_pre", "g_ff_post", "w_ff_in", "w_ff_out")


def _full_weight(name, mine, theirs, core):
    g = jnp.stack([jnp.where(core == 0, mine, theirs), jnp.where(core == 0, theirs, mine)])
    half, cols = g.shape[2], g.shape[3]
    if name in COLUMN_SHARDED:
        return g.transpose(0, 2, 1, 3).reshape(2 * half, N_CHIPS * cols)
    return g.transpose(1, 0, 2, 3).reshape(N_CHIPS * 2 * half, cols)


def _parts_by_chip(name, grad):
    if name in COLUMN_SHARDED:
        r, c = grad.shape[0], grad.shape[1] // N_CHIPS
        return grad.reshape(r, N_CHIPS, c).transpose(1, 0, 2)
    return grad.reshape(N_CHIPS, grad.shape[0] // N_CHIPS, grad.shape[1])


def kernel(x, w_in, w_pool, pool_scale, gm_gain, w_spatial, b_spatial, w_br_sb, w_br_pool, w_br_gm, w_out, g_mix_pre, g_mix_post, g_ff_pre, g_ff_post, w_ff_in, w_ff_out, loss_target, m_w_in, m_w_pool, m_pool_scale, m_gm_gain, m_w_spatial, m_b_spatial, m_w_br_sb, m_w_br_pool, m_w_br_gm, m_w_out, m_g_mix_pre, m_g_mix_post, m_g_ff_pre, m_g_ff_post, m_w_ff_in, m_w_ff_out, v_w_in, v_w_pool, v_pool_scale, v_gm_gain, v_w_spatial, v_b_spatial, v_w_br_sb, v_w_br_pool, v_w_br_gm, v_w_out, v_g_mix_pre, v_g_mix_post, v_g_ff_pre, v_g_ff_post, v_w_ff_in, v_w_ff_out):
    given = dict(locals())
    w = {n: given[n] for n in WEIGHT_ORDER}
    m = {n: given["m_" + n] for n in WEIGHT_ORDER}
    v = {n: given["v_" + n] for n in WEIGHT_ORDER}
    L = w_in.shape[0]

    core = lax.axis_index("c")

    def my_rows(a):
        half = a.shape[1] // 2
        return lax.dynamic_slice_in_dim(a.astype(BF16), core * half, half, axis=1)

    view = lambda n, a: jnp.swapaxes(a, 1, 2) if n in TRANSPOSED else a
    half_shards = {n: my_rows(view(n, w[n])) for n in BIG_WEIGHTS}
    small = {n: w[n] for n in SMALL_WEIGHTS}
    loss, dx, received, small_grads = _local_step(x[0], loss_target[0], half_shards, small, core)

    core_index = core.astype(jnp.int32).reshape(1)
    reduced = []
    for k in range(len(BIG_WEIGHTS)):
        from_chips = jnp.concatenate([received[l][k] for l in range(L)], axis=1)
        reduced.append(_sum_slots(from_chips, name="sum_chips"))
    reduced_by_sibling = _sibling_swap(reduced, name="swap_reduced_halves")
    grads, deltas, new_m, new_v = {}, {}, {}, {}
    for n, mine, theirs in zip(BIG_WEIGHTS, reduced, reduced_by_sibling):
        by_layer = lambda a: a.reshape(L, -1, a.shape[-1])
        outs = _adamw_halves(view(n, w[n]), view(n, m[n]), view(n, v[n]), by_layer(mine), by_layer(theirs), core_index,
                             name="adamw_big")
        grads[n], deltas[n], new_m[n], new_v[n] = [view(n, o) for o in outs]

    flat = lambda a: a.reshape(-1, a.shape[-1])
    outs = _small_update(*[[flat(t[n]) for n in SMALL_WEIGHTS] for t in (small_grads, w, m, v)], name="small_update")
    for store, arrays in zip((grads, deltas, new_m, new_v), outs):
        store.update({n: a.reshape(w[n].shape) for n, a in zip(SMALL_WEIGHTS, arrays)})

    total_loss = lax.psum(loss[0, 0], ("x", "y", "c"))
    return (total_loss, dx[None], *[grads[n] for n in WEIGHT_ORDER], *[deltas[n] for n in WEIGHT_ORDER],
            *[new_m[n] for n in WEIGHT_ORDER], *[new_v[n] for n in WEIGHT_ORDER])
```

```python
import functools
import math

import jax
import jax.numpy as jnp
from jax import lax
from jax.experimental import pallas as pl
from jax.experimental.pallas import tpu as pltpu

F32 = jnp.float32
BF16 = jnp.bfloat16

D_MODEL = 1024
SB_HEADS = 8
SB_HEAD_DIM = 64
SB_WIDTH = SB_HEADS * SB_HEAD_DIM
POOL_WINDOWS = (2, 4, 8, 16)
POOL_GROUP_DIM = 64
POOL_WIDTH = 256
POOL_HALO = 16
GM_GROUPS = 4
GM_GROUP_DIM = 64
GM_WIDTH = 256
GM_CHUNK = 128
N_BRANCH = 3
D_FF = 4 * D_MODEL
RMS_EPS = 1e-6
QKV_WIDTH = 3 * SB_WIDTH
MIX_WIDTH = POOL_WIDTH + 2 * GM_WIDTH
GATE_WIDTH = N_BRANCH * D_MODEL
D_IN = QKV_WIDTH + MIX_WIDTH + GATE_WIDTH
PROJ_BLOCK = 768
LANES = 128
N_CHIPS = 4
N_DEV = 8

ADAM_LR = 0.001
ADAM_B1 = 0.9
ADAM_B2 = 0.999
ADAM_EPS = 1e-08
ADAM_WD = 0.01
ADAM_STEP = 10

VMEM_LIMIT = 56 * 1024 * 1024
MESH = pl.DeviceIdType.MESH


def _params(n_grid):
    return pltpu.CompilerParams(dimension_semantics=("arbitrary",) * n_grid, vmem_limit_bytes=VMEM_LIMIT)


def _bf(x):
    return x if x.dtype == BF16 else x.astype(BF16)


def _matmul(a, b, *, name, ta=False, tb=False, out_dtypes=(F32,), n=None, b_col_off=0, bm=1024, bn=1024, bk=2048,
            extras=(), epilogue=None):
    M, K = (a.shape[1], a.shape[0]) if ta else a.shape
    nb = b.shape[0] if tb else b.shape[1]
    n = nb if n is None else n
    bm, bn, bk = min(bm, M), min(bn, n), min(bk, K)
    assert M % bm == 0 and n % bn == 0 and K % bk == 0, (name, M, n, K, bm, bn, bk)
    assert (b.shape[1] if tb else b.shape[0]) == K, (name, a.shape, b.shape)
    nk = K // bk
    dims = (((0 if ta else 1,), (1 if tb else 0,)), ((), ()))
    n_out = len(out_dtypes)
    direct = nk > 1 and epilogue is None and out_dtypes == (F32,)
    use_acc = nk > 1 and not direct

    def body(*refs):
        a_ref, b_ref = refs[:2]
        extra_refs = refs[2:2 + len(extras)]
        out_refs = refs[2 + len(extras):2 + len(extras) + n_out]
        p = lax.dot_general(_bf(a_ref[...]), _bf(b_ref[...]), dims, preferred_element_type=F32)

        def finish(acc):
            outs = (acc,) if epilogue is None else epilogue(acc, *[r[...] for r in extra_refs])
            for r, o in zip(out_refs, outs):
                r[...] = o.astype(r.dtype)

        if nk == 1:
            finish(p)
            return
        k = pl.program_id(2)
        acc_ref = out_refs[0] if direct else refs[-1]

        @pl.when(k == 0)
        def _():
            acc_ref[...] = p

        @pl.when(k > 0)
        def _():
            acc_ref[...] += p

        if use_acc:
            @pl.when(k == nk - 1)
            def _():
                finish(acc_ref[...])

    a_spec = pl.BlockSpec((bk, bm), lambda i, j, k: (k, i)) if ta else pl.BlockSpec((bm, bk), lambda i, j, k: (i, k))
    if tb:
        b_spec = pl.BlockSpec((bn, bk), lambda i, j, k: (j + b_col_off, k))
    else:
        b_spec = pl.BlockSpec((bk, bn), lambda i, j, k: (k, j + b_col_off))
    tile = pl.BlockSpec((bm, bn), lambda i, j, k: (i, j))
    outs = pl.pallas_call(
        body, name=name, grid=(M // bm, n // bn, nk),
        in_specs=[a_spec, b_spec] + [tile] * len(extras),
        out_specs=[tile] * n_out,
        out_shape=[jax.ShapeDtypeStruct((M, n), d) for d in out_dtypes],
        scratch_shapes=[pltpu.VMEM((bm, bn), F32)] if use_acc else [],
        compiler_params=_params(3),
    )(a, b, *extras)
    return outs[0] if n_out == 1 else outs


ROW_TILE = 512


def _rows(S):
    tr = min(ROW_TILE, S)
    assert S % tr == 0
    return tr


def _rstd(x):
    return lax.rsqrt(jnp.mean(x * x, axis=-1, keepdims=True) + RMS_EPS)


def _rms_bwd_math(x, g, dy):
    r = _rstd(x)
    gd = g * dy
    dx = r * gd - x * (r * r * r) * jnp.mean(x * gd, axis=-1, keepdims=True)
    dg = jnp.sum(dy * x * r, axis=0, keepdims=True)
    return dx, dg


def _accumulate(ref, value):
    i = pl.program_id(0)

    @pl.when(i == 0)
    def _():
        ref[...] = value

    @pl.when(i > 0)
    def _():
        ref[...] += value


def _row_spec(tr, width):
    return pl.BlockSpec((tr, width), lambda i: (i, 0))


def _vec_spec(width):
    return pl.BlockSpec((1, width), lambda i: (0, 0))


def _rms_fwd(x, g, *, name):
    S, D = x.shape
    tr = _rows(S)

    def body(x_ref, g_ref, o_ref):
        xf = x_ref[...]
        o_ref[...] = (xf * _rstd(xf) * g_ref[...]).astype(o_ref.dtype)

    return pl.pallas_call(
        body, name=name, grid=(S // tr,), in_specs=[_row_spec(tr, D), _vec_spec(D)], out_specs=_row_spec(tr, D),
        out_shape=jax.ShapeDtypeStruct((S, D), BF16), compiler_params=_params(1))(x, g)


def _resid_rms(x, y, g_post, g_next, *, name):
    S, D = x.shape
    tr = _rows(S)
    with_next = g_next is not None

    def body(*refs):
        if with_next:
            x_ref, y_ref, gp_ref, gn_ref, xo_ref, ho_ref = refs
        else:
            x_ref, y_ref, gp_ref, xo_ref = refs
        yf = y_ref[...]
        xn = x_ref[...] + yf * _rstd(yf) * gp_ref[...]
        xo_ref[...] = xn
        if with_next:
            ho_ref[...] = (xn * _rstd(xn) * gn_ref[...]).astype(ho_ref.dtype)

    row, vec = _row_spec(tr, D), _vec_spec(D)
    ins = [x, y, g_post] + ([g_next] if with_next else [])
    outs = pl.pallas_call(
        body, name=name, grid=(S // tr,), in_specs=[row, row, vec] + ([vec] if with_next else []),
        out_specs=[row] + ([row] if with_next else []),
        out_shape=[jax.ShapeDtypeStruct((S, D), F32)] + ([jax.ShapeDtypeStruct((S, D), BF16)] if with_next else []),
        compiler_params=_params(1))(*ins)
    return (outs[0], outs[1]) if with_next else (outs[0], None)


def _rms_bwd(x, g, dy, *, name):
    S, D = x.shape
    tr = _rows(S)

    def body(x_ref, g_ref, dy_ref, dx_ref, dg_ref):
        dx, dg = _rms_bwd_math(x_ref[...], g_ref[...], dy_ref[...])
        dx_ref[...] = dx.astype(dx_ref.dtype)
        _accumulate(dg_ref, dg)

    row, vec = _row_spec(tr, D), _vec_spec(D)
    return pl.pallas_call(
        body, name=name, grid=(S // tr,), in_specs=[row, vec, row], out_specs=[row, vec],
        out_shape=[jax.ShapeDtypeStruct((S, D), BF16), jax.ShapeDtypeStruct((1, D), F32)],
        compiler_params=_params(1))(x, g, dy)


def _rms_bwd_chain(xa, ga, da, resid, xb, gb, *, name):
    S, D = xa.shape
    tr = _rows(S)
    chain = xb is not None

    def body(*refs):
        if chain:
            xa_ref, ga_ref, da_ref, rs_ref, xb_ref, gb_ref, dx_ref, dga_ref, dxb_ref, dgb_ref = refs
        else:
            xa_ref, ga_ref, da_ref, rs_ref, dx_ref, dga_ref = refs
        dxa, dga = _rms_bwd_math(xa_ref[...], ga_ref[...], da_ref[...])
        dx = rs_ref[...] + dxa
        dx_ref[...] = dx
        _accumulate(dga_ref, dga)
        if chain:
            dxb, dgb = _rms_bwd_math(xb_ref[...], gb_ref[...], dx)
            dxb_ref[...] = dxb.astype(dxb_ref.dtype)
            _accumulate(dgb_ref, dgb)

    row, vec = _row_spec(tr, D), _vec_spec(D)
    ins = [xa, ga, da, resid] + ([xb, gb] if chain else [])
    outs = pl.pallas_call(
        body, name=name, grid=(S // tr,), in_specs=[row, vec, row, row] + ([row, vec] if chain else []),
        out_specs=[row, vec] + ([row, vec] if chain else []),
        out_shape=[jax.ShapeDtypeStruct((S, D), F32), jax.ShapeDtypeStruct((1, D), F32)]
        + ([jax.ShapeDtypeStruct((S, D), BF16), jax.ShapeDtypeStruct((1, D), F32)] if chain else []),
        compiler_params=_params(1))(*ins)
    return tuple(outs) if chain else (outs[0], outs[1], None, None)


def _loss_head(y, target, *, name):
    S, D = y.shape
    tr = _rows(S)
    n_tiles = S // tr

    def body(y_ref, t_ref, dy_ref, loss_ref, acc_ref):
        err = y_ref[...] - t_ref[...]
        dy_ref[...] = err * (1.0 / D)
        _accumulate(acc_ref, jnp.sum(err * err, axis=0, keepdims=True))

        @pl.when(pl.program_id(0) == n_tiles - 1)
        def _():
            loss_ref[...] = jnp.sum(acc_ref[...], axis=1, keepdims=True) * (0.5 / D)

    row = _row_spec(tr, D)
    return pl.pallas_call(
        body, name=name, grid=(n_tiles,), in_specs=[row, row],
        out_specs=[row, pl.BlockSpec((1, 1), lambda i: (0, 0))],
        out_shape=[jax.ShapeDtypeStruct((S, D), F32), jax.ShapeDtypeStruct((1, 1), F32)],
        scratch_shapes=[pltpu.VMEM((1, D), F32)], compiler_params=_params(1))(y, target)


SB_TILE = 256
SB_PAIRS = SB_HEADS * SB_HEAD_DIM // LANES
SB_DEAD_LOG = -110.0


def _log_sigmoids(z):
    l1p = jnp.log(1.0 + jnp.exp(-jnp.abs(z)))
    return jnp.minimum(z, 0.0) - l1p, jnp.minimum(-z, 0.0) - l1p


def _split_bf16(x):
    hi = x.astype(BF16)
    lo = (x - hi.astype(F32)).astype(BF16)
    return jnp.concatenate([hi, lo], axis=1)


def _tri(T, cmp):
    j = lax.broadcasted_iota(jnp.int32, (T, T), 0)
    s = lax.broadcasted_iota(jnp.int32, (T, T), 1)
    m = jnp.where(cmp(j, s), 1.0, 0.0).astype(BF16)
    return jnp.concatenate([m, m], axis=0)


def _head_masks():
    lane = lax.broadcasted_iota(jnp.int32, (1, LANES), 1)
    return [lane < SB_HEAD_DIM, lane >= SB_HEAD_DIM]


def _cargo(refs, n_in, n_out, cargo, scatter):
    n = len(cargo)
    if not n:
        return refs, lambda first: None, lambda last: None
    ins = refs[n_in:n_in + n]
    outs = refs[n_in + n + n_out:n_in + n + n_out + n]
    sems = refs[len(refs) - 3:]
    own = refs[:n_in] + refs[n_in + n:n_in + n + n_out] + refs[n_in + n + n_out + n:len(refs) - 3]

    def start(first):
        @pl.when(first)
        def _():
            for cp in _chip_copies(ins, outs, *sems, scatter=scatter):
                cp.start()

    def finish(last):
        @pl.when(last)
        def _():
            for cp in _chip_copies(ins, outs, *sems, scatter=scatter):
                cp.wait()

    return own, start, finish


def _sba_fwd(qkv, *, name, cargo=()):
    S = qkv.shape[0]
    T = min(SB_TILE, S)
    nq = S // T
    scale = SB_HEAD_DIM ** -0.5

    def body(*refs):
        (q_ref, k_ref, v_ref, o_ref, t_ref, first_ref), start_cargo, finish_cargo = _cargo(refs, 3, 3, cargo, False)
        p, i = pl.program_id(0), pl.program_id(1)
        start_cargo(jnp.logical_and(p == 0, i == 0))
        row = lax.broadcasted_iota(jnp.int32, (T, T), 0)
        col = lax.broadcasted_iota(jnp.int32, (T, T), 1)
        strict = col < row
        after = _tri(T, lambda j, s: j > s)
        masks = _head_masks()
        q = q_ref[...] * scale
        qs = [jnp.where(hm, q, jnp.zeros_like(q)) for hm in masks]

        def walk(tiles, carry):
            values, logs = [], []
            for j, diag in tiles:
                rows = pl.ds(pl.multiple_of(j * T, T), T)
                kb = k_ref[rows, :]
                values.append(v_ref[rows, :])
                for qh in qs:
                    z = lax.dot_general(qh, kb, (((1,), (1,)), ((), ())), preferred_element_type=F32)
                    ls, ln = _log_sigmoids(z)
                    logs.append((ls, jnp.where(strict, ln, 0.0) if diag else ln))
            suffixes = [jnp.dot(_split_bf16(ln), after, preferred_element_type=F32) for _, ln in logs]
            for t, (_, diag) in enumerate(tiles):
                out = []
                for h, (C, acc) in enumerate(carry):
                    ls, ln = logs[2 * t + h]
                    a = jnp.exp(ls + suffixes[2 * t + h] + C)
                    if diag:
                        a = jnp.where(strict, a, 0.0)
                    acc = acc + jnp.dot(a.astype(BF16), values[t], preferred_element_type=F32)
                    out.append((C + jnp.sum(ln, axis=1, keepdims=True), acc))
                carry = tuple(out)
            return carry

        fresh = (jnp.zeros((T, 1), F32), jnp.zeros((T, LANES), F32))
        carry = lax.cond(i > 0, lambda: walk([(i, True), (i - 1, False)], (fresh, fresh)),
                         lambda: walk([(i, True)], (fresh, fresh)))

        def alive(state):
            j, ((C0, _), (C1, _)) = state
            return jnp.logical_and(j >= 0, jnp.max(jnp.maximum(C0, C1)) > SB_DEAD_LOG)

        def step(state):
            j, carry = state
            return j - 1, walk([(j, False)], carry)

        j, ((C0, acc0), (C1, acc1)) = lax.while_loop(alive, step, (i - 2, carry))
        t_ref[0] = jnp.broadcast_to(C0, (T, LANES))
        t_ref[1] = jnp.broadcast_to(C1, (T, LANES))
        first_ref[...] = jnp.full((8, LANES), jnp.maximum(j + 1, 0).astype(F32))
        o_ref[...] = jnp.where(masks[0], acc0, acc1).astype(o_ref.dtype)
        finish_cargo(jnp.logical_and(p == SB_PAIRS - 1, i == nq - 1))

    kv = lambda off: pl.BlockSpec((S, LANES), lambda p, i: (0, off + p))
    n = len(cargo)
    return pl.pallas_call(
        body, name=name, grid=(SB_PAIRS, nq),
        in_specs=[pl.BlockSpec((T, LANES), lambda p, i: (i, p)), kv(SB_PAIRS), kv(2 * SB_PAIRS)] + [HBM_SPEC] * n,
        out_specs=[pl.BlockSpec((T, LANES), lambda p, i: (i, p)), pl.BlockSpec((2, T, LANES), lambda p, i: (p, i, 0)),
                   pl.BlockSpec((None, None, 8, LANES), lambda p, i: (p, i, 0, 0))] + [HBM_SPEC] * n,
        out_shape=[jax.ShapeDtypeStruct((S, SB_WIDTH), BF16), jax.ShapeDtypeStruct((SB_HEADS, S, LANES), F32),
                   jax.ShapeDtypeStruct((SB_PAIRS, nq, 8, LANES), F32)] + _chip_exchange_shapes(cargo, False),
        scratch_shapes=_chip_exchange_semaphores(n) if n else [],
        compiler_params=_params(2))(qkv, qkv, qkv, *cargo)


def _sba_bwd(qkv, do, tot, first, *, name, cargo=()):
    S = qkv.shape[0]
    T = min(SB_TILE, S)
    nq = S // T
    scale = SB_HEAD_DIM ** -0.5

    def body(*refs):
        own, start_cargo, finish_cargo = _cargo(refs, 6, 3, cargo, True)
        q_ref, k_ref, v_ref, do_ref, t_ref, first_ref, dq_ref, dk_ref, dv_ref, dk_acc, dv_acc = own
        p, i = pl.program_id(0), pl.program_id(1)
        start_cargo(jnp.logical_and(p == 0, i == 0))

        @pl.when(i == 0)
        def _():
            dk_acc[...] = jnp.zeros_like(dk_acc)
            dv_acc[...] = jnp.zeros_like(dv_acc)

        row = lax.broadcasted_iota(jnp.int32, (T, T), 0)
        col = lax.broadcasted_iota(jnp.int32, (T, T), 1)
        strict = col < row
        upto = _tri(T, lambda j, s: j <= s)
        before = _tri(T, lambda j, s: j < s)
        masks = _head_masks()
        q, do_t = q_ref[...], do_ref[...]
        q = q * scale
        qs = [jnp.where(hm, q, jnp.zeros_like(q)) for hm in masks]
        dos = [jnp.where(hm, do_t, jnp.zeros_like(do_t)) for hm in masks]
        totals = [t_ref[h][:, 0:1] for h in range(2)]
        over_lanes = (((1,), (1,)), ((), ()))
        over_queries = (((0,), (0,)), ((), ()))

        def walk(tiles, carry):
            rows = [pl.ds(pl.multiple_of(j * T, T), T) for j, _ in tiles]
            keys = [k_ref[r, :] for r in rows]
            values = [v_ref[r, :] for r in rows]
            chains = [(t, h) for t in range(len(tiles)) for h in range(2)]
            logs, da = {}, {}
            for t, h in chains:
                z = lax.dot_general(qs[h], keys[t], over_lanes, preferred_element_type=F32)
                ls, ln = _log_sigmoids(z)
                logs[t, h] = (ls, jnp.where(strict, ln, 0.0) if tiles[t][1] else ln)
                da[t, h] = lax.dot_general(dos[h], values[t], over_lanes, preferred_element_type=F32)
            upto_sums = {c: jnp.dot(_split_bf16(logs[c][1]), upto, preferred_element_type=F32) for c in chains}
            a, g = {}, {}
            P = [c[0] for c in carry]
            for t, h in chains:
                ls, ln = logs[t, h]
                a_th = jnp.exp(ls + ((totals[h] - P[h]) - upto_sums[t, h]))
                a[t, h] = jnp.where(strict, a_th, 0.0) if tiles[t][1] else a_th
                g[t, h] = a[t, h] * da[t, h]
                P[h] = P[h] + jnp.sum(ln, axis=1, keepdims=True)
            before_sums = {c: jnp.dot(_split_bf16(g[c]), before, preferred_element_type=F32) for c in chains}
            G = [c[1] for c in carry]
            dq = [c[2] for c in carry]
            dz = {}
            for t, h in chains:
                beta = jnp.exp(logs[t, h][0])
                dz_th = g[t, h] * (1.0 - beta) - (G[h] + before_sums[t, h]) * beta
                dz[t, h] = (jnp.where(strict, dz_th, 0.0) if tiles[t][1] else dz_th).astype(BF16)
                G[h] = G[h] + jnp.sum(g[t, h], axis=1, keepdims=True)
            for t, h in chains:
                dq[h] = dq[h] + jnp.dot(dz[t, h], keys[t], preferred_element_type=F32)
            for t in range(len(tiles)):
                dk_acc[rows[t], :] += sum(
                    lax.dot_general(dz[t, h], qs[h], over_queries, preferred_element_type=F32) for h in range(2))
                dv_acc[rows[t], :] += sum(
                    lax.dot_general(a[t, h].astype(BF16), dos[h], over_queries, preferred_element_type=F32)
                    for h in range(2))
            return tuple((P[h], G[h], dq[h]) for h in range(2))

        zero = jnp.zeros((T, 1), F32)
        fresh = (zero, zero, jnp.zeros((T, LANES), F32))
        last_single = jnp.maximum(i - 1, 0)
        j0 = jnp.clip(jnp.max(first_ref[...]).astype(jnp.int32), 0, last_single)
        carry = lax.fori_loop(j0, last_single, lambda j, c: walk([(j, False)], c), (fresh, fresh))
        (_, _, dq0), (_, _, dq1) = lax.cond(i > 0, lambda: walk([(i - 1, False), (i, True)], carry),
                                            lambda: walk([(i, True)], carry))
        dq_ref[...] = (jnp.where(masks[0], dq0, dq1) * scale).astype(dq_ref.dtype)

        @pl.when(i == nq - 1)
        def _():
            dk_ref[...] = dk_acc[...].astype(dk_ref.dtype)
            dv_ref[...] = dv_acc[...].astype(dv_ref.dtype)

        finish_cargo(jnp.logical_and(p == SB_PAIRS - 1, i == nq - 1))

    kv = lambda off: pl.BlockSpec((S, LANES), lambda p, i: (0, off + p))
    tile = lambda off: pl.BlockSpec((T, LANES), lambda p, i: (i, off + p))
    n = len(cargo)
    return pl.pallas_call(
        body, name=name, grid=(SB_PAIRS, nq),
        in_specs=[tile(0), kv(SB_PAIRS), kv(2 * SB_PAIRS), tile(0), pl.BlockSpec((2, T, LANES), lambda p, i: (p, i, 0)),
                  pl.BlockSpec((None, None, 8, LANES), lambda p, i: (p, i, 0, 0))] + [HBM_SPEC] * n,
        out_specs=[tile(0), kv(0), kv(0)] + [HBM_SPEC] * n,
        out_shape=[jax.ShapeDtypeStruct((S, SB_WIDTH), BF16)] * 3 + _chip_exchange_shapes(cargo, True),
        scratch_shapes=[pltpu.VMEM((S, LANES), F32), pltpu.VMEM((S, LANES), F32)]
        + (_chip_exchange_semaphores(n) if n else []),
        compiler_params=_params(2))(qkv, qkv, qkv, do, tot, first, *cargo)


POOL_TILE = 512


def _by_group(lane, values):
    return jnp.where(lane < 64, values[0], jnp.where(lane < 128, values[1], jnp.where(lane < 192, values[2], values[3])))


def _pool_inv_count(first_row, n_rows):
    t = first_row + lax.broadcasted_iota(jnp.int32, (n_rows, POOL_WIDTH), 0)
    lane = lax.broadcasted_iota(jnp.int32, (n_rows, POOL_WIDTH), 1)
    window = _by_group(lane, POOL_WINDOWS)
    return 1.0 / jnp.clip(t + 1, 1, window).astype(F32), lane


def _pooled(ext, first_row, R):
    n = R + POOL_HALO
    s2 = ext + pltpu.roll(ext, 1, 0)
    s4 = s2 + pltpu.roll(s2, 2, 0)
    s8 = s4 + pltpu.roll(s4, 4, 0)
    s16 = s8 + pltpu.roll(s8, 8, 0)
    inv, lane = _pool_inv_count(first_row - POOL_HALO, n)
    pooled = _by_group(lane, (s2, s4, s8, s16)) * inv - ext
    return pooled[POOL_HALO:, :]


def _pool_specs(S, R, col):
    per = R // POOL_HALO
    tile = pl.BlockSpec((R, POOL_WIDTH), lambda i: (i, col))
    prev = pl.BlockSpec((POOL_HALO, POOL_WIDTH), lambda i: (jnp.maximum(i * per - 1, 0), col))
    return tile, prev


def _pool_fwd(rest, w_bd, scale, *, name):
    S = rest.shape[0]
    R = min(POOL_TILE, S)

    def body(p_ref, prev_ref, w_ref, s_ref, o_ref, ext_ref):
        i = pl.program_id(0)
        ext_ref[:POOL_HALO, :] = jnp.where(i > 0, prev_ref[...], 0.0)
        ext_ref[POOL_HALO:, :] = p_ref[...]
        pooled = _pooled(ext_ref[...], i * R, R)
        mixed = jnp.dot(pooled.astype(BF16), w_ref[...], preferred_element_type=F32)
        o_ref[...] = (mixed * s_ref[...]).astype(o_ref.dtype)

    tile, prev = _pool_specs(S, R, 0)
    return pl.pallas_call(
        body, name=name, grid=(S // R,),
        in_specs=[tile, prev, pl.BlockSpec((POOL_WIDTH, POOL_WIDTH), lambda i: (0, 0)), _vec_spec(POOL_WIDTH)],
        out_specs=_row_spec(R, POOL_WIDTH), out_shape=jax.ShapeDtypeStruct((S, POOL_WIDTH), BF16),
        scratch_shapes=[pltpu.VMEM((R + POOL_HALO, POOL_WIDTH), F32)], compiler_params=_params(1))(rest, rest, w_bd, scale)


def _pool_bwd(rest, do, w_bd, scale, *, name):
    S = rest.shape[0]
    R = min(POOL_TILE, S)
    n_tiles = S // R
    per = R // POOL_HALO
    n = R + POOL_HALO

    def body(p_ref, prev_ref, do_ref, nxt_ref, w_ref, s_ref, dp_ref, dw_ref, ds_ref, ext_ref, dext_ref):
        i = pl.program_id(0)
        ext_ref[:POOL_HALO, :] = jnp.where(i > 0, prev_ref[...], 0.0)
        ext_ref[POOL_HALO:, :] = p_ref[...]
        pooled = _pooled(ext_ref[...], i * R, R).astype(BF16)
        w = w_ref[...]
        mixed = jnp.dot(pooled, w, preferred_element_type=F32)
        do_t = do_ref[...]
        _accumulate(ds_ref, jnp.sum(do_t * mixed, axis=0, keepdims=True))
        dext_ref[:R, :] = do_t
        dext_ref[R:, :] = jnp.where(i < n_tiles - 1, nxt_ref[...], 0.0)
        dmixed = (dext_ref[...] * s_ref[...]).astype(BF16)
        dpooled = lax.dot_general(dmixed, w, (((1,), (1,)), ((), ())), preferred_element_type=F32)
        _accumulate(dw_ref, lax.dot_general(pooled, dmixed[:R, :], (((0,), (0,)), ((), ())), preferred_element_type=F32))
        inv, lane = _pool_inv_count(i * R, n)
        u = dpooled * inv
        f2 = u + pltpu.roll(u, n - 1, 0)
        f4 = f2 + pltpu.roll(f2, n - 2, 0)
        f8 = f4 + pltpu.roll(f4, n - 4, 0)
        f16 = f8 + pltpu.roll(f8, n - 8, 0)
        dp = _by_group(lane, (f2, f4, f8, f16)) - dpooled
        dp_ref[...] = dp[:R, :].astype(dp_ref.dtype)

    tile, prev = _pool_specs(S, R, 0)
    nxt = pl.BlockSpec((POOL_HALO, POOL_WIDTH), lambda i: (jnp.minimum((i + 1) * per, S // POOL_HALO - 1), 0))
    full = pl.BlockSpec((POOL_WIDTH, POOL_WIDTH), lambda i: (0, 0))
    return pl.pallas_call(
        body, name=name, grid=(n_tiles,),
        in_specs=[tile, prev, _row_spec(R, POOL_WIDTH), nxt, full, _vec_spec(POOL_WIDTH)],
        out_specs=[_row_spec(R, POOL_WIDTH), full, _vec_spec(POOL_WIDTH)],
        out_shape=[jax.ShapeDtypeStruct((S, POOL_WIDTH), BF16), jax.ShapeDtypeStruct((POOL_WIDTH, POOL_WIDTH), F32),
                   jax.ShapeDtypeStruct((1, POOL_WIDTH), F32)],
        scratch_shapes=[pltpu.VMEM((n, POOL_WIDTH), F32), pltpu.VMEM((n, POOL_WIDTH), F32)],
        compiler_params=_params(1))(rest, rest, do, do, w_bd, scale)


GM_TILE = 512
GELU_C = math.sqrt(2.0 / math.pi)
GELU_A = 0.044715


def _gelu(x):
    return 0.5 * x * (1.0 + jnp.tanh(GELU_C * (x + GELU_A * x * x * x)))


def _gelu_and_grad(x):
    t = jnp.tanh(GELU_C * (x + GELU_A * x * x * x))
    y = 0.5 * x * (1.0 + t)
    dy = 0.5 * (1.0 + t) + 0.5 * x * (1.0 - t * t) * (GELU_C * (1.0 + 3.0 * GELU_A * x * x))
    return y, dy


def _group_lane_masks():
    lane = lax.broadcasted_iota(jnp.int32, (1, GM_WIDTH), 1)
    return [(lane >= g * GM_GROUP_DIM) & (lane < (g + 1) * GM_GROUP_DIM) for g in range(GM_GROUPS)]


def _stack_groups(x, masks):
    return jnp.concatenate([jnp.where(m, x, jnp.zeros_like(x)) for m in masks], axis=0)


def _gm_mixed(vn, ws_cat, bias, masks, R):
    chunks = []
    for c in range(R // GM_CHUNK):
        vc = vn[c * GM_CHUNK:(c + 1) * GM_CHUNK, :]
        chunks.append(jnp.dot(ws_cat, _stack_groups(vc, masks), preferred_element_type=F32) + bias)
    return jnp.concatenate(chunks, axis=0)


def _gm_specs(S, R):
    u = pl.BlockSpec((R, GM_WIDTH), lambda i: (i, 1))
    v = pl.BlockSpec((R, GM_WIDTH), lambda i: (i, 2))
    ws = pl.BlockSpec((GM_CHUNK, GM_GROUPS * GM_CHUNK), lambda i: (0, 0))
    bias = pl.BlockSpec((GM_CHUNK, GM_WIDTH), lambda i: (0, 0))
    return u, v, ws, bias


def _gm_fwd(rest, gain, ws_cat, bias, *, name):
    S = rest.shape[0]
    R = min(GM_TILE, S)

    def body(u_ref, v_ref, g_ref, ws_ref, b_ref, o_ref):
        gv = _gelu(v_ref[...])
        vn = (gv * _rstd(gv) * g_ref[...]).astype(BF16)
        mixed = _gm_mixed(vn, ws_ref[...], b_ref[...], _group_lane_masks(), R)
        o_ref[...] = (_gelu(u_ref[...]) * mixed).astype(o_ref.dtype)

    u_spec, v_spec, ws_spec, bias_spec = _gm_specs(S, R)
    return pl.pallas_call(
        body, name=name, grid=(S // R,), in_specs=[u_spec, v_spec, _vec_spec(GM_WIDTH), ws_spec, bias_spec],
        out_specs=_row_spec(R, GM_WIDTH), out_shape=jax.ShapeDtypeStruct((S, GM_WIDTH), BF16),
        compiler_params=_params(1))(rest, rest, gain, ws_cat, bias)


def _gm_bwd(rest, do, gain, ws_cat, wst_cat, bias, *, name):
    S = rest.shape[0]
    R = min(GM_TILE, S)

    def body(u_ref, v_ref, do_ref, g_ref, ws_ref, wst_ref, b_ref, du_ref, dv_ref, dg_ref, dws_ref, db_ref):
        masks = _group_lane_masks()
        gain_v = g_ref[...]
        gu, dgu = _gelu_and_grad(u_ref[...])
        gv, dgv = _gelu_and_grad(v_ref[...])
        r = _rstd(gv)
        vn = (gv * r * gain_v).astype(BF16)
        mixed = _gm_mixed(vn, ws_ref[...], b_ref[...], masks, R)
        do_t = do_ref[...]
        du_ref[...] = (do_t * mixed * dgu).astype(du_ref.dtype)
        dmix = do_t * gu
        dmix_b = dmix.astype(BF16)
        wst = wst_ref[...]
        dvn_chunks, db, dws = [], None, [None] * GM_GROUPS
        for c in range(R // GM_CHUNK):
            rows = slice(c * GM_CHUNK, (c + 1) * GM_CHUNK)
            dc, dcb, vc = dmix[rows, :], dmix_b[rows, :], vn[rows, :]
            db = dc if db is None else db + dc
            dvn_chunks.append(jnp.dot(wst, _stack_groups(dcb, masks), preferred_element_type=F32))
            for g, m in enumerate(masks):
                part = lax.dot_general(jnp.where(m, dcb, jnp.zeros_like(dcb)), vc, (((1,), (1,)), ((), ())),
                                       preferred_element_type=F32)
                dws[g] = part if dws[g] is None else dws[g] + part
        dvn = jnp.concatenate(dvn_chunks, axis=0)
        lane = lax.broadcasted_iota(jnp.int32, (1, LANES), 1)
        db_groups = jnp.zeros((GM_CHUNK, LANES), F32)
        for g, m in enumerate(masks):
            total = jnp.sum(jnp.where(m, db, 0.0), axis=1, keepdims=True)
            db_groups = db_groups + jnp.where(lane == g, total, 0.0)
        _accumulate(db_ref, db_groups)
        i = pl.program_id(0)
        for g in range(GM_GROUPS):
            @pl.when(i == 0)
            def _(g=g):
                dws_ref[g] = dws[g]

            @pl.when(i > 0)
            def _(g=g):
                dws_ref[g] += dws[g]
        _accumulate(dg_ref, jnp.sum(dvn * gv * r, axis=0, keepdims=True))
        gd = gain_v * dvn
        dgv_in = r * gd - gv * (r * r * r) * jnp.mean(gv * gd, axis=-1, keepdims=True)
        dv_ref[...] = (dgv_in * dgv).astype(dv_ref.dtype)

    u_spec, v_spec, ws_spec, bias_spec = _gm_specs(S, R)
    row, vec = _row_spec(R, GM_WIDTH), _vec_spec(GM_WIDTH)
    dws_spec = pl.BlockSpec((GM_GROUPS, GM_CHUNK, GM_CHUNK), lambda i: (0, 0, 0))
    return pl.pallas_call(
        body, name=name, grid=(S // R,), in_specs=[u_spec, v_spec, row, vec, ws_spec, ws_spec, bias_spec],
        out_specs=[row, row, vec, dws_spec, pl.BlockSpec((GM_CHUNK, LANES), lambda i: (0, 0))],
        out_shape=[jax.ShapeDtypeStruct((S, GM_WIDTH), BF16)] * 2
        + [jax.ShapeDtypeStruct((1, GM_WIDTH), F32), jax.ShapeDtypeStruct((GM_GROUPS, GM_CHUNK, GM_CHUNK), F32),
           jax.ShapeDtypeStruct((GM_CHUNK, LANES), F32)],
        compiler_params=_params(1))(rest, rest, do, gain, ws_cat, wst_cat, bias)


GATE_ROWS = 1024
GATE_COLS = 256
GATE_BLOCKS = D_MODEL // GATE_COLS


def _gate_spec(tr, k):
    return pl.BlockSpec((tr, GATE_COLS), lambda i, j: (i, GATE_BLOCKS * k + j))


def _merge_fwd(gates, branches, *, name):
    S = gates.shape[0]
    tr = min(GATE_ROWS, S)

    def body(g0, g1, g2, b0, b1, b2, o_ref):
        acc = None
        for g_ref, b_ref in ((g0, b0), (g1, b1), (g2, b2)):
            term = jax.nn.sigmoid(g_ref[...].astype(F32)) * b_ref[...].astype(F32)
            acc = term if acc is None else acc + term
        o_ref[...] = acc.astype(o_ref.dtype)

    tile = pl.BlockSpec((tr, GATE_COLS), lambda i, j: (i, j))
    return pl.pallas_call(
        body, name=name, grid=(S // tr, GATE_BLOCKS),
        in_specs=[_gate_spec(tr, k) for k in range(N_BRANCH)] + [tile] * N_BRANCH, out_specs=tile,
        out_shape=jax.ShapeDtypeStruct((S, D_MODEL), BF16), compiler_params=_params(2))(gates, gates, gates, *branches)


def _merge_bwd(gates, branches, dmerged, *, name):
    S = gates.shape[0]
    tr = min(GATE_ROWS, S)

    def body(g0, g1, g2, b0, b1, b2, dm_ref, dg0, dg1, dg2, db0, db1, db2):
        dm = dm_ref[...].astype(F32)
        for g_ref, b_ref, dg_ref, db_ref in ((g0, b0, dg0, db0), (g1, b1, dg1, db1), (g2, b2, dg2, db2)):
            s = jax.nn.sigmoid(g_ref[...].astype(F32))
            db_ref[...] = (dm * s).astype(db_ref.dtype)
            dg_ref[...] = (dm * b_ref[...].astype(F32) * s * (1.0 - s)).astype(dg_ref.dtype)

    tile = pl.BlockSpec((tr, GATE_COLS), lambda i, j: (i, j))
    return pl.pallas_call(
        body, name=name, grid=(S // tr, GATE_BLOCKS),
        in_specs=[_gate_spec(tr, k) for k in range(N_BRANCH)] + [tile] * (N_BRANCH + 1), out_specs=[tile] * (2 * N_BRANCH),
        out_shape=[jax.ShapeDtypeStruct((S, D_MODEL), BF16)] * (2 * N_BRANCH),
        compiler_params=_params(2))(gates, gates, gates, *branches, dmerged)


TILE_BYTES = 24 * 1024 * 1024


BF16_ROWS = 16


def _tile_rows(rows, cols, n_arrays):
    padded = -(-cols // LANES) * LANES
    cap = max(BF16_ROWS, TILE_BYTES // (2 * n_arrays * padded * 4))
    best = None
    for tr in range(BF16_ROWS, min(rows, cap) + 1, BF16_ROWS):
        if rows % tr == 0:
            best = tr
    assert best is not None, (rows, cols)
    return best


def _sum_slots(stack, *, name):
    n, R, C = stack.shape
    tr = _tile_rows(R, C, n + 1)

    def body(s_ref, o_ref):
        acc = s_ref[0].astype(F32)
        for k in range(1, n):
            acc = acc + s_ref[k].astype(F32)
        o_ref[...] = acc

    return pl.pallas_call(
        body, name=name, grid=(R // tr,), in_specs=[pl.BlockSpec((n, tr, C), lambda i: (0, i, 0))],
        out_specs=_row_spec(tr, C), out_shape=jax.ShapeDtypeStruct((R, C), F32), compiler_params=_params(1))(stack)


def _add_own_half(parts, received, core, *, name):
    n, R, C = parts.shape
    half = R // 2
    tr = _tile_rows(half, C, 3)
    steps = half // tr

    def body(core_ref, own_ref, got_ref, o_ref):
        o_ref[...] = (own_ref[...] + got_ref[...]).astype(o_ref.dtype)

    tile = pl.BlockSpec((None, tr, C), lambda d, i, core_ref: (d, i, 0))
    own = pl.BlockSpec((None, tr, C), lambda d, i, core_ref: (d, core_ref[0] * steps + i, 0))
    return pl.pallas_call(
        body, name=name, out_shape=jax.ShapeDtypeStruct((n, half, C), BF16),
        grid_spec=pltpu.PrefetchScalarGridSpec(num_scalar_prefetch=1, grid=(n, steps), in_specs=[own, tile], out_specs=tile),
        compiler_params=_params(2))(core, parts, received)


def _adamw_math(w, m, v, g):
    m_new = ADAM_B1 * m + (1.0 - ADAM_B1) * g
    v_new = ADAM_B2 * v + (1.0 - ADAM_B2) * jnp.square(g)
    m_hat = m_new / (1.0 - ADAM_B1 ** ADAM_STEP)
    v_hat = v_new / (1.0 - ADAM_B2 ** ADAM_STEP)
    return -ADAM_LR * (m_hat / (jnp.sqrt(v_hat) + ADAM_EPS) + ADAM_WD * w), m_new, v_new


def _adamw_halves(w, m, v, mine, theirs, core, *, name):
    L, r, C = w.shape
    tr = _tile_rows(r // 2, C, 9)
    steps = r // 2 // tr

    def body(core_ref, w_ref, m_ref, v_ref, mine_ref, theirs_ref, go_ref, d_ref, mo_ref, vo_ref):
        in_my_half = pl.program_id(1) // steps == core_ref[0]
        g = jnp.where(in_my_half, mine_ref[...], theirs_ref[...])
        go_ref[...] = g
        d_ref[...], mo_ref[...], vo_ref[...] = _adamw_math(w_ref[...], m_ref[...], v_ref[...], g)

    row = pl.BlockSpec((None, tr, C), lambda l, i, core_ref: (l, i, 0))
    half = pl.BlockSpec((None, tr, C), lambda l, i, core_ref: (l, i % steps, 0))
    return pl.pallas_call(
        body, name=name, out_shape=[jax.ShapeDtypeStruct((L, r, C), F32)] * 4,
        grid_spec=pltpu.PrefetchScalarGridSpec(
            num_scalar_prefetch=1, grid=(L, r // tr), in_specs=[row, row, row, half, half], out_specs=[row] * 4),
        compiler_params=_params(2))(core, w, m, v, mine, theirs)


HBM_SPEC = pl.BlockSpec(memory_space=pl.ANY)


def _position():
    return lax.axis_index("x"), lax.axis_index("y"), lax.axis_index("c")


def _other_chips(x, y):
    return [(1 - x, y), (x, 1 - y), (1 - x, 1 - y)]


def _chip_exchange(arrays, *, scatter, name):
    n = len(arrays)

    def body(*refs):
        copies = _chip_copies(refs[:n], refs[n:2 * n], *refs[2 * n:], scatter=scatter)
        for cp in copies:
            cp.start()
        for cp in copies:
            cp.wait()

    return pl.pallas_call(
        body, name=name, in_specs=[HBM_SPEC] * n, out_specs=[HBM_SPEC] * n, out_shape=_chip_exchange_shapes(arrays, scatter),
        scratch_shapes=_chip_exchange_semaphores(n))(*arrays)


def _chip_exchange_shapes(arrays, scatter):
    return [jax.ShapeDtypeStruct(a.shape if scatter else (N_CHIPS,) + a.shape, a.dtype) for a in arrays]


def _chip_exchange_semaphores(n):
    return [pltpu.SemaphoreType.DMA((3 * n,)), pltpu.SemaphoreType.DMA((3 * n,)), pltpu.SemaphoreType.DMA((n,))]


def _chip_copies(ins, outs, send_sems, recv_sems, local_sems, *, scatter):
    x, y, c = _position()
    me = 2 * x + y
    copies = []
    for a in range(len(ins)):
        own = ins[a].at[me] if scatter else ins[a]
        copies.append(pltpu.make_async_copy(own, outs[a].at[me], local_sems.at[a]))
        for k, (px, py) in enumerate(_other_chips(x, y)):
            src = ins[a].at[2 * px + py] if scatter else ins[a]
            copies.append(pltpu.make_async_remote_copy(
                src_ref=src, dst_ref=outs[a].at[me], send_sem=send_sems.at[3 * a + k],
                recv_sem=recv_sems.at[3 * a + k], device_id=(px, py, c), device_id_type=MESH))
    return copies


def _sibling_swap(arrays, *, name):
    n = len(arrays)

    def body(*refs):
        ins, outs = refs[:n], refs[n:2 * n]
        send_sems, recv_sems = refs[2 * n:]
        x, y, c = _position()
        copies = []
        for a in range(n):
            cp = pltpu.make_async_remote_copy(
                src_ref=ins[a], dst_ref=outs[a], send_sem=send_sems.at[a], recv_sem=recv_sems.at[a],
                device_id=(x, y, 1 - c), device_id_type=MESH)
            cp.start()
            copies.append(cp)
        for cp in copies:
            cp.wait()

    return pl.pallas_call(
        body, name=name, in_specs=[HBM_SPEC] * n, out_specs=[HBM_SPEC] * n,
        out_shape=[jax.ShapeDtypeStruct(a.shape, a.dtype) for a in arrays],
        scratch_shapes=[pltpu.SemaphoreType.DMA((n,)), pltpu.SemaphoreType.DMA((n,))],
    )(*arrays)


def _sibling_other_half(arrays, *, name):
    n = len(arrays)

    def body(*refs):
        ins, outs = refs[:n], refs[n:2 * n]
        send_sems, recv_sems = refs[2 * n:]
        x, y, c = _position()
        copies = []
        for a in range(n):
            half = ins[a].shape[1] // 2
            theirs = ins[a].at[:, pl.ds(pl.multiple_of((1 - c) * half, BF16_ROWS), half), :]
            cp = pltpu.make_async_remote_copy(
                src_ref=theirs, dst_ref=outs[a], send_sem=send_sems.at[a], recv_sem=recv_sems.at[a],
                device_id=(x, y, 1 - c), device_id_type=MESH)
            cp.start()
            copies.append(cp)
        for cp in copies:
            cp.wait()

    return pl.pallas_call(
        body, name=name, in_specs=[HBM_SPEC] * n, out_specs=[HBM_SPEC] * n,
        out_shape=[jax.ShapeDtypeStruct((a.shape[0], a.shape[1] // 2, a.shape[2]), a.dtype) for a in arrays],
        scratch_shapes=[pltpu.SemaphoreType.DMA((n,)), pltpu.SemaphoreType.DMA((n,))],
    )(*arrays)


def _small_update(grads, ws, ms, vs, *, name):
    n, L = len(ws), ws[0].shape[0]
    pieces = [g for per_layer in grads for g in per_layer]
    np_ = len(pieces)

    def body(*refs):
        g_in, refs = refs[:np_], refs[np_:]
        w_in, m_in, v_in, g_out, d_out, m_out, v_out = (refs[k * n:(k + 1) * n] for k in range(7))
        from_sibling, chip_sums = refs[7 * n:7 * n + np_], refs[7 * n + np_:7 * n + 2 * np_]
        sibling_send, sibling_recv, chip_send, chip_recv = refs[7 * n + 2 * np_:]
        x, y, c = _position()
        me = 2 * x + y
        swaps = [pltpu.make_async_remote_copy(
            src_ref=g_in[p], dst_ref=from_sibling[p], send_sem=sibling_send.at[p], recv_sem=sibling_recv.at[p],
            device_id=(x, y, 1 - c), device_id_type=MESH) for p in range(np_)]
        for cp in swaps:
            cp.start()
        for cp in swaps:
            cp.wait()
        for p in range(np_):
            chip_sums[p][me] = g_in[p][...] + from_sibling[p][...]
        sends = [pltpu.make_async_remote_copy(
            src_ref=chip_sums[p].at[me], dst_ref=chip_sums[p].at[me], send_sem=chip_send.at[3 * p + k],
            recv_sem=chip_recv.at[3 * p + k], device_id=(px, py, c), device_id_type=MESH)
            for p in range(np_) for k, (px, py) in enumerate(_other_chips(x, y))]
        for cp in sends:
            cp.start()
        for cp in sends:
            cp.wait()
        for a in range(n):
            for l in range(L):
                sums = chip_sums[a * L + l]
                g = sums[0]
                for s in range(1, N_CHIPS):
                    g = g + sums[s]
                g_out[a][l] = g
                d_out[a][l], m_out[a][l], v_out[a][l] = _adamw_math(w_in[a][l], m_in[a][l], v_in[a][l], g)

    vmem = pl.BlockSpec(memory_space=pltpu.VMEM)
    shapes = [jax.ShapeDtypeStruct(w.shape, F32) for w in ws]
    outs = pl.pallas_call(
        body, name=name, in_specs=[vmem] * (np_ + 3 * n), out_specs=[vmem] * (4 * n), out_shape=shapes * 4,
        scratch_shapes=[pltpu.VMEM(g.shape, F32) for g in pieces] + [pltpu.VMEM((N_CHIPS,) + g.shape, F32) for g in pieces]
        + [pltpu.SemaphoreType.DMA((np_,)), pltpu.SemaphoreType.DMA((np_,)),
           pltpu.SemaphoreType.DMA((3 * np_,)), pltpu.SemaphoreType.DMA((3 * np_,))],
        compiler_params=pltpu.CompilerParams(vmem_limit_bytes=VMEM_LIMIT),
    )(*pieces, *ws, *ms, *vs)
    return outs[:n], outs[n:2 * n], outs[2 * n:3 * n], outs[3 * n:]


def _relu2(p):
    return p, jnp.square(jnp.maximum(p, 0.0))


def _relu2_grad(p, a):
    return (p * (2.0 * jnp.maximum(a.astype(F32), 0.0)),)


def _mixer_constants(w_pool, w_spatial, b_spatial):
    eye = jnp.eye(len(POOL_WINDOWS), dtype=F32)
    w_bd = (eye[:, None, :, None] * w_pool[:, :, None, :]).reshape(POOL_WIDTH, POOL_WIDTH).astype(BF16)
    causal = jnp.tril(jnp.ones((GM_CHUNK, GM_CHUNK), dtype=bool))
    ws = jnp.where(causal[None], w_spatial, 0.0).astype(BF16)
    ws_cat = ws.transpose(1, 0, 2).reshape(GM_CHUNK, GM_GROUPS * GM_CHUNK)
    wst_cat = ws.transpose(2, 0, 1).reshape(GM_CHUNK, GM_GROUPS * GM_CHUNK)
    bias = jnp.repeat(b_spatial.T, GM_GROUP_DIM, axis=1)
    return w_bd, ws_cat, wst_cat, bias


def _local_step(x, target, half_shards, small, core):
    L = small["g_mix_pre"].shape[0]
    vec = lambda name, l: small[name][l][None, :]
    consts = [_mixer_constants(small["w_pool"][l], small["w_spatial"][l], small["b_spatial"][l]) for l in range(L)]
    core_index = core.astype(jnp.int32).reshape(1)
    first_used, used_later = BIG_WEIGHTS[:1], BIG_WEIGHTS[1:]
    weights = [{} for _ in range(L)]

    def finish_gather(wanted, gathered):
        theirs = _sibling_swap(gathered, name="swap_weight_halves")
        for (n, ll), a, b in zip(wanted, gathered, theirs):
            weights[ll][n] = _full_weight(n, a, b, core)

    def core_sums(names, l):
        parts = [_parts_by_chip(n, gb[n][l]) for n in names]
        from_sibling = _sibling_other_half(parts, name="swap_grad_halves")
        return [(n, l, _add_own_half(p, f, core_index, name="sum_cores")) for n, p, f in zip(names, parts, from_sibling)]

    wanted = [(n, 0) for n in first_used]
    finish_gather(wanted, _chip_exchange([half_shards[n][ll] for n, ll in wanted], scatter=False, name="gather_weights"))
    saved = []
    h = _rms_fwd(x, vec("g_mix_pre", 0), name="rms_in")
    for l in range(L):
        w_bd, ws_cat, wst_cat, bias = consts[l]
        proj = lambda n, off, dtype, name: _matmul(h, weights[l]["w_in"], tb=True, n=n, bn=PROJ_BLOCK,
                                                   b_col_off=off // PROJ_BLOCK, out_dtypes=(dtype,), name=name)
        qkv = proj(QKV_WIDTH, 0, BF16, "proj_qkv")
        rest = proj(MIX_WIDTH, QKV_WIDTH, F32, "proj_mix")
        gates = proj(GATE_WIDTH, QKV_WIDTH + MIX_WIDTH, BF16, "proj_gates")
        wanted = [(n, l) for n in used_later] + [(n, l + 1) for n in first_used if l + 1 < L]
        o_sb, tot, first, *gathered = _sba_fwd(qkv, cargo=[half_shards[n][ll] for n, ll in wanted], name="sba_fwd_gather")
        finish_gather(wanted, gathered)
        o_pool = _pool_fwd(rest, w_bd, vec("pool_scale", l), name="pool_fwd")
        o_gm = _gm_fwd(rest, vec("gm_gain", l), ws_cat, bias, name="gm_fwd")
        branches = (_matmul(o_sb, weights[l]["w_br_sb"], out_dtypes=(BF16,), name="br_sb"),
                    _matmul(o_pool, weights[l]["w_br_pool"], out_dtypes=(BF16,), name="br_pool"),
                    _matmul(o_gm, weights[l]["w_br_gm"], out_dtypes=(BF16,), name="br_gm"))
        merged = _merge_fwd(gates, branches, name="merge_fwd")
        y = _matmul(merged, weights[l]["w_out"], name="out_proj")
        x1, h2 = _resid_rms(x, y, vec("g_mix_post", l), vec("g_ff_pre", l), name="resid_mix")
        a, r = _matmul(h2, weights[l]["w_ff_in"], out_dtypes=(BF16, BF16), epilogue=_relu2, name="ff_in")
        ff = _matmul(r, weights[l]["w_ff_out"], name="ff_out")
        g_next = vec("g_mix_pre", l + 1) if l + 1 < L else None
        x2, h_next = _resid_rms(x1, ff, vec("g_ff_post", l), g_next, name="resid_ff" if l + 1 < L else "resid_last")
        saved.append(dict(x=x, h=h, qkv=qkv, rest=rest, gates=gates, o_sb=o_sb, tot=tot, first=first, o_pool=o_pool,
                          o_gm=o_gm, branches=branches, merged=merged, y=y, x1=x1, h2=h2, a=a, r=r, ff=ff))
        x, h = x2, h_next

    dx2, loss = _loss_head(x, target, name="loss_head")
    gb = {k: [None] * L for k in ("w_in", "w_br_sb", "w_br_pool", "w_br_gm", "w_out", "w_ff_in", "w_ff_out")}
    gs = {k: [None] * L for k in ("w_pool", "pool_scale", "gm_gain", "w_spatial", "b_spatial", "g_mix_pre",
                                  "g_mix_post", "g_ff_pre", "g_ff_post")}
    d_ff, gs["g_ff_post"][L - 1] = _rms_bwd(saved[-1]["ff"], vec("g_ff_post", L - 1), dx2, name="rms_bwd_last")
    received = [{} for _ in range(L)]
    waiting = []
    for l in reversed(range(L)):
        s = saved[l]
        w_bd, ws_cat, wst_cat, bias = consts[l]
        da = _matmul(d_ff, weights[l]["w_ff_out"], tb=True, out_dtypes=(BF16,), extras=(s["a"],), epilogue=_relu2_grad,
                     name="ff_out_dx")
        gb["w_ff_out"][l] = _matmul(s["r"], d_ff, ta=True, name="ff_out_dw")
        dh2 = _matmul(da, weights[l]["w_ff_in"], tb=True, name="ff_in_dx")
        gb["w_ff_in"][l] = _matmul(s["h2"], da, ta=True, name="ff_in_dw")
        dx1, gs["g_ff_pre"][l], dy, gs["g_mix_post"][l] = _rms_bwd_chain(
            s["x1"], vec("g_ff_pre", l), dh2, dx2, s["y"], vec("g_mix_post", l), name="rms_bwd_mid")
        dmerged = _matmul(dy, weights[l]["w_out"], tb=True, out_dtypes=(BF16,), name="out_proj_dx")
        gb["w_out"][l] = _matmul(s["merged"], dy, ta=True, name="out_proj_dw")
        dg0, dg1, dg2, db_sb, db_pool, db_gm = _merge_bwd(s["gates"], s["branches"], dmerged, name="merge_bwd")
        do_sb = _matmul(db_sb, weights[l]["w_br_sb"], tb=True, out_dtypes=(BF16,), name="br_sb_dx")
        gb["w_br_sb"][l] = _matmul(s["o_sb"], db_sb, ta=True, name="br_sb_dw")
        do_pool = _matmul(db_pool, weights[l]["w_br_pool"], tb=True, name="br_pool_dx")
        gb["w_br_pool"][l] = _matmul(s["o_pool"], db_pool, ta=True, name="br_pool_dw")
        do_gm = _matmul(db_gm, weights[l]["w_br_gm"], tb=True, name="br_gm_dx")
        gb["w_br_gm"][l] = _matmul(s["o_gm"], db_gm, ta=True, name="br_gm_dw")
        waiting += core_sums(used_later, l)
        dq, dk, dv, *arrived = _sba_bwd(s["qkv"], do_sb, s["tot"], s["first"], cargo=[c for _, _, c in waiting],
                                        name="sba_bwd_exchange")
        for (n, ll, _), got in zip(waiting, arrived):
            received[ll][n] = got
        dp, dw_bd, gs["pool_scale"][l] = _pool_bwd(s["rest"], do_pool, w_bd, vec("pool_scale", l), name="pool_bwd")
        du, dgv, gs["gm_gain"][l], dws, db = _gm_bwd(s["rest"], do_gm, vec("gm_gain", l), ws_cat, wst_cat, bias,
                                                      name="gm_bwd")
        gs["w_pool"][l] = jnp.stack([dw_bd[g * 64:(g + 1) * 64, g * 64:(g + 1) * 64] for g in range(len(POOL_WINDOWS))])
        gs["w_spatial"][l] = jnp.where(jnp.tril(jnp.ones((GM_CHUNK, GM_CHUNK), dtype=bool))[None], dws, 0.0)
        gs["b_spatial"][l] = db[:, :GM_GROUPS].T
        dproj = jnp.concatenate([dq, dk, dv, dp, du, dgv, dg0, dg1, dg2], axis=1)
        dh = _matmul(dproj, weights[l]["w_in"], bk=D_IN // 3, name="proj_dx")
        gb["w_in"][l] = _matmul(dproj, s["h"], ta=True, bm=PROJ_BLOCK, name="proj_dw")
        waiting = core_sums(first_used, l)
        if l == 0:
            for (n, ll, _), got in zip(waiting, _chip_exchange([c for _, _, c in waiting], scatter=True, name="exchange_grads")):
                received[ll][n] = got
        if l > 0:
            dx2, gs["g_mix_pre"][l], d_ff, gs["g_ff_post"][l - 1] = _rms_bwd_chain(
                s["x"], vec("g_mix_pre", l), dh, dx1, saved[l - 1]["ff"], vec("g_ff_post", l - 1), name="rms_bwd_mid")
        else:
            dx2, gs["g_mix_pre"][l], _, _ = _rms_bwd_chain(s["x"], vec("g_mix_pre", l), dh, dx1, None, None,
                                                           name="rms_bwd_first")
    return loss, dx2, received, gs


TRANSPOSED = ("w_in",)
COLUMN_SHARDED = ("w_br_sb", "w_br_pool", "w_br_gm", "w_ff_in")
ROW_SHARDED = ("w_in", "w_out", "w_ff_out")
BIG_WEIGHTS = ("w_in", "w_br_sb", "w_br_pool", "w_br_gm", "w_ff_in", "w_out", "w_ff_out")
SMALL_WEIGHTS = ("w_pool", "pool_scale", "gm_gain", "w_spatial", "b_spatial", "g_mix_pre", "g_mix_post", "g_ff_pre",
                 "g_ff_post")
WEIGHT_ORDER = ("w_in", "w_pool", "pool_scale", "gm_gain", "w_spatial", "b_spatial", "w_br_sb", "w_br_pool", "w_br_gm",
                "w_out", "g_mix_pre", "g_mix_post", "g_ff_pre", "g_ff_post", "w_ff_in", "w_ff_out")


def _full_weight(name, mine, theirs, core):
    g = jnp.stack([jnp.where(core == 0, mine, theirs), jnp.where(core == 0, theirs, mine)])
    half, cols = g.shape[2], g.shape[3]
    if name in COLUMN_SHARDED:
        return g.transpose(0, 2, 1, 3).reshape(2 * half, N_CHIPS * cols)
    return g.transpose(1, 0, 2, 3).reshape(N_CHIPS * 2 * half, cols)


def _parts_by_chip(name, grad):
    if name in COLUMN_SHARDED:
        r, c = grad.shape[0], grad.shape[1] // N_CHIPS
        return grad.reshape(r, N_CHIPS, c).transpose(1, 0, 2)
    return grad.reshape(N_CHIPS, grad.shape[0] // N_CHIPS, grad.shape[1])


def kernel(x, w_in, w_pool, pool_scale, gm_gain, w_spatial, b_spatial, w_br_sb, w_br_pool, w_br_gm, w_out, g_mix_pre, g_mix_post, g_ff_pre, g_ff_post, w_ff_in, w_ff_out, loss_target, m_w_in, m_w_pool, m_pool_scale, m_gm_gain, m_w_spatial, m_b_spatial, m_w_br_sb, m_w_br_pool, m_w_br_gm, m_w_out, m_g_mix_pre, m_g_mix_post, m_g_ff_pre, m_g_ff_post, m_w_ff_in, m_w_ff_out, v_w_in, v_w_pool, v_pool_scale, v_gm_gain, v_w_spatial, v_b_spatial, v_w_br_sb, v_w_br_pool, v_w_br_gm, v_w_out, v_g_mix_pre, v_g_mix_post, v_g_ff_pre, v_g_ff_post, v_w_ff_in, v_w_ff_out):
    given = dict(locals())
    w = {n: given[n] for n in WEIGHT_ORDER}
    m = {n: given["m_" + n] for n in WEIGHT_ORDER}
    v = {n: given["v_" + n] for n in WEIGHT_ORDER}
    L = w_in.shape[0]

    core = lax.axis_index("c")

    def my_rows(a):
        half = a.shape[1] // 2
        return lax.dynamic_slice_in_dim(a.astype(BF16), core * half, half, axis=1)

    view = lambda n, a: jnp.swapaxes(a, 1, 2) if n in TRANSPOSED else a
    half_shards = {n: my_rows(view(n, w[n])) for n in BIG_WEIGHTS}
    small = {n: w[n] for n in SMALL_WEIGHTS}
    loss, dx, received, small_grads = _local_step(x[0], loss_target[0], half_shards, small, core)

    core_index = core.astype(jnp.int32).reshape(1)
    reduced = []
    for n in BIG_WEIGHTS:
        from_chips = jnp.concatenate([received[l][n] for l in range(L)], axis=1)
        reduced.append(_sum_slots(from_chips, name="sum_chips"))
    reduced_by_sibling = _sibling_swap(reduced, name="swap_reduced_halves")
    grads, deltas, new_m, new_v = {}, {}, {}, {}
    for n, mine, theirs in zip(BIG_WEIGHTS, reduced, reduced_by_sibling):
        by_layer = lambda a: a.reshape(L, -1, a.shape[-1])
        outs = _adamw_halves(view(n, w[n]), view(n, m[n]), view(n, v[n]), by_layer(mine), by_layer(theirs), core_index,
                             name="adamw_big")
        grads[n], deltas[n], new_m[n], new_v[n] = [view(n, o) for o in outs]

    by_layer = lambda a: a.reshape(L, -1, a.shape[-1])
    flat = lambda a: a.reshape(-1, a.shape[-1])
    outs = _small_update([[flat(g) for g in small_grads[n]] for n in SMALL_WEIGHTS],
                         *[[by_layer(t[n]) for n in SMALL_WEIGHTS] for t in (w, m, v)], name="small_update")
    for store, arrays in zip((grads, deltas, new_m, new_v), outs):
        store.update({n: a.reshape(w[n].shape) for n, a in zip(SMALL_WEIGHTS, arrays)})

    total_loss = lax.psum(loss[0, 0], ("x", "y", "c"))
    return (total_loss, dx[None], *[grads[n] for n in WEIGHT_ORDER], *[deltas[n] for n in WEIGHT_ORDER],
            *[new_m[n] for n in WEIGHT_ORDER], *[new_v[n] for n in WEIGHT_ORDER])
```

```python
import functools
import math

import jax
import jax.numpy as jnp
from jax import lax
from jax.experimental import pallas as pl
from jax.experimental.pallas import tpu as pltpu

F32 = jnp.float32
BF16 = jnp.bfloat16

D_MODEL = 1024
SB_HEADS = 8
SB_HEAD_DIM = 64
SB_WIDTH = SB_HEADS * SB_HEAD_DIM
POOL_WINDOWS = (2, 4, 8, 16)
POOL_GROUP_DIM = 64
POOL_WIDTH = 256
POOL_HALO = 16
GM_GROUPS = 4
GM_GROUP_DIM = 64
GM_WIDTH = 256
GM_CHUNK = 128
N_BRANCH = 3
D_FF = 4 * D_MODEL
RMS_EPS = 1e-6
QKV_WIDTH = 3 * SB_WIDTH
MIX_WIDTH = POOL_WIDTH + 2 * GM_WIDTH
GATE_WIDTH = N_BRANCH * D_MODEL
D_IN = QKV_WIDTH + MIX_WIDTH + GATE_WIDTH
PROJ_BLOCK = 768
LANES = 128
N_CHIPS = 4
N_DEV = 8

ADAM_LR = 0.001
ADAM_B1 = 0.9
ADAM_B2 = 0.999
ADAM_EPS = 1e-08
ADAM_WD = 0.01
ADAM_STEP = 10

VMEM_LIMIT = 56 * 1024 * 1024
MESH = pl.DeviceIdType.MESH


def _params(n_grid):
    return pltpu.CompilerParams(dimension_semantics=("arbitrary",) * n_grid, vmem_limit_bytes=VMEM_LIMIT)


def _bf(x):
    return x if x.dtype == BF16 else x.astype(BF16)


def _matmul(a, b, *, name, ta=False, tb=False, out_dtypes=(F32,), n=None, b_col_off=0, bm=1024, bn=1024, bk=2048,
            extras=(), epilogue=None):
    M, K = (a.shape[1], a.shape[0]) if ta else a.shape
    nb = b.shape[0] if tb else b.shape[1]
    n = nb if n is None else n
    bm, bn, bk = min(bm, M), min(bn, n), min(bk, K)
    assert M % bm == 0 and n % bn == 0 and K % bk == 0, (name, M, n, K, bm, bn, bk)
    assert (b.shape[1] if tb else b.shape[0]) == K, (name, a.shape, b.shape)
    nk = K // bk
    dims = (((0 if ta else 1,), (1 if tb else 0,)), ((), ()))
    n_out = len(out_dtypes)
    direct = nk > 1 and epilogue is None and out_dtypes == (F32,)
    use_acc = nk > 1 and not direct

    def body(*refs):
        a_ref, b_ref = refs[:2]
        extra_refs = refs[2:2 + len(extras)]
        out_refs = refs[2 + len(extras):2 + len(extras) + n_out]
        p = lax.dot_general(_bf(a_ref[...]), _bf(b_ref[...]), dims, preferred_element_type=F32)

        def finish(acc):
            outs = (acc,) if epilogue is None else epilogue(acc, *[r[...] for r in extra_refs])
            for r, o in zip(out_refs, outs):
                r[...] = o.astype(r.dtype)

        if nk == 1:
            finish(p)
            return
        k = pl.program_id(2)
        acc_ref = out_refs[0] if direct else refs[-1]

        @pl.when(k == 0)
        def _():
            acc_ref[...] = p

        @pl.when(k > 0)
        def _():
            acc_ref[...] += p

        if use_acc:
            @pl.when(k == nk - 1)
            def _():
                finish(acc_ref[...])

    a_spec = pl.BlockSpec((bk, bm), lambda i, j, k: (k, i)) if ta else pl.BlockSpec((bm, bk), lambda i, j, k: (i, k))
    if tb:
        b_spec = pl.BlockSpec((bn, bk), lambda i, j, k: (j + b_col_off, k))
    else:
        b_spec = pl.BlockSpec((bk, bn), lambda i, j, k: (k, j + b_col_off))
    tile = pl.BlockSpec((bm, bn), lambda i, j, k: (i, j))
    outs = pl.pallas_call(
        body, name=name, grid=(M // bm, n // bn, nk),
        in_specs=[a_spec, b_spec] + [tile] * len(extras),
        out_specs=[tile] * n_out,
        out_shape=[jax.ShapeDtypeStruct((M, n), d) for d in out_dtypes],
        scratch_shapes=[pltpu.VMEM((bm, bn), F32)] if use_acc else [],
        compiler_params=_params(3),
    )(a, b, *extras)
    return outs[0] if n_out == 1 else outs


ROW_TILE = 512


def _rows(S):
    tr = min(ROW_TILE, S)
    assert S % tr == 0
    return tr


def _rstd(x):
    return lax.rsqrt(jnp.mean(x * x, axis=-1, keepdims=True) + RMS_EPS)


def _rms_bwd_math(x, g, dy):
    r = _rstd(x)
    gd = g * dy
    dx = r * gd - x * (r * r * r) * jnp.mean(x * gd, axis=-1, keepdims=True)
    dg = jnp.sum(dy * x * r, axis=0, keepdims=True)
    return dx, dg


def _accumulate(ref, value):
    i = pl.program_id(0)

    @pl.when(i == 0)
    def _():
        ref[...] = value

    @pl.when(i > 0)
    def _():
        ref[...] += value


def _row_spec(tr, width):
    return pl.BlockSpec((tr, width), lambda i: (i, 0))


def _vec_spec(width):
    return pl.BlockSpec((1, width), lambda i: (0, 0))


def _rms_fwd(x, g, *, name):
    S, D = x.shape
    tr = _rows(S)

    def body(x_ref, g_ref, o_ref):
        xf = x_ref[...]
        o_ref[...] = (xf * _rstd(xf) * g_ref[...]).astype(o_ref.dtype)

    return pl.pallas_call(
        body, name=name, grid=(S // tr,), in_specs=[_row_spec(tr, D), _vec_spec(D)], out_specs=_row_spec(tr, D),
        out_shape=jax.ShapeDtypeStruct((S, D), BF16), compiler_params=_params(1))(x, g)


def _resid_rms(x, y, g_post, g_next, *, name):
    S, D = x.shape
    tr = _rows(S)
    with_next = g_next is not None

    def body(*refs):
        if with_next:
            x_ref, y_ref, gp_ref, gn_ref, xo_ref, ho_ref = refs
        else:
            x_ref, y_ref, gp_ref, xo_ref = refs
        yf = y_ref[...]
        xn = x_ref[...] + yf * _rstd(yf) * gp_ref[...]
        xo_ref[...] = xn
        if with_next:
            ho_ref[...] = (xn * _rstd(xn) * gn_ref[...]).astype(ho_ref.dtype)

    row, vec = _row_spec(tr, D), _vec_spec(D)
    ins = [x, y, g_post] + ([g_next] if with_next else [])
    outs = pl.pallas_call(
        body, name=name, grid=(S // tr,), in_specs=[row, row, vec] + ([vec] if with_next else []),
        out_specs=[row] + ([row] if with_next else []),
        out_shape=[jax.ShapeDtypeStruct((S, D), F32)] + ([jax.ShapeDtypeStruct((S, D), BF16)] if with_next else []),
        compiler_params=_params(1))(*ins)
    return (outs[0], outs[1]) if with_next else (outs[0], None)


def _rms_bwd(x, g, dy, *, name):
    S, D = x.shape
    tr = _rows(S)

    def body(x_ref, g_ref, dy_ref, dx_ref, dg_ref):
        dx, dg = _rms_bwd_math(x_ref[...], g_ref[...], dy_ref[...])
        dx_ref[...] = dx.astype(dx_ref.dtype)
        _accumulate(dg_ref, dg)

    row, vec = _row_spec(tr, D), _vec_spec(D)
    return pl.pallas_call(
        body, name=name, grid=(S // tr,), in_specs=[row, vec, row], out_specs=[row, vec],
        out_shape=[jax.ShapeDtypeStruct((S, D), BF16), jax.ShapeDtypeStruct((1, D), F32)],
        compiler_params=_params(1))(x, g, dy)


def _rms_bwd_chain(xa, ga, da, resid, xb, gb, *, name):
    S, D = xa.shape
    tr = _rows(S)
    chain = xb is not None

    def body(*refs):
        if chain:
            xa_ref, ga_ref, da_ref, rs_ref, xb_ref, gb_ref, dx_ref, dga_ref, dxb_ref, dgb_ref = refs
        else:
            xa_ref, ga_ref, da_ref, rs_ref, dx_ref, dga_ref = refs
        dxa, dga = _rms_bwd_math(xa_ref[...], ga_ref[...], da_ref[...])
        dx = rs_ref[...] + dxa
        dx_ref[...] = dx
        _accumulate(dga_ref, dga)
        if chain:
            dxb, dgb = _rms_bwd_math(xb_ref[...], gb_ref[...], dx)
            dxb_ref[...] = dxb.astype(dxb_ref.dtype)
            _accumulate(dgb_ref, dgb)

    row, vec = _row_spec(tr, D), _vec_spec(D)
    ins = [xa, ga, da, resid] + ([xb, gb] if chain else [])
    outs = pl.pallas_call(
        body, name=name, grid=(S // tr,), in_specs=[row, vec, row, row] + ([row, vec] if chain else []),
        out_specs=[row, vec] + ([row, vec] if chain else []),
        out_shape=[jax.ShapeDtypeStruct((S, D), F32), jax.ShapeDtypeStruct((1, D), F32)]
        + ([jax.ShapeDtypeStruct((S, D), BF16), jax.ShapeDtypeStruct((1, D), F32)] if chain else []),
        compiler_params=_params(1))(*ins)
    return tuple(outs) if chain else (outs[0], outs[1], None, None)


def _loss_head(y, target, *, name):
    S, D = y.shape
    tr = _rows(S)
    n_tiles = S // tr

    def body(y_ref, t_ref, dy_ref, loss_ref, acc_ref):
        err = y_ref[...] - t_ref[...]
        dy_ref[...] = err * (1.0 / D)
        _accumulate(acc_ref, jnp.sum(err * err, axis=0, keepdims=True))

        @pl.when(pl.program_id(0) == n_tiles - 1)
        def _():
            loss_ref[...] = jnp.sum(acc_ref[...], axis=1, keepdims=True) * (0.5 / D)

    row = _row_spec(tr, D)
    return pl.pallas_call(
        body, name=name, grid=(n_tiles,), in_specs=[row, row],
        out_specs=[row, pl.BlockSpec((1, 1), lambda i: (0, 0))],
        out_shape=[jax.ShapeDtypeStruct((S, D), F32), jax.ShapeDtypeStruct((1, 1), F32)],
        scratch_shapes=[pltpu.VMEM((1, D), F32)], compiler_params=_params(1))(y, target)


SB_TILE = 256
SB_PAIRS = SB_HEADS * SB_HEAD_DIM // LANES
SB_DEAD_LOG = -110.0


def _log_sigmoids(z):
    l1p = jnp.log(1.0 + jnp.exp(-jnp.abs(z)))
    return jnp.minimum(z, 0.0) - l1p, jnp.minimum(-z, 0.0) - l1p


def _split_bf16(x):
    hi = x.astype(BF16)
    lo = (x - hi.astype(F32)).astype(BF16)
    return jnp.concatenate([hi, lo], axis=1)


def _tri(T, cmp):
    j = lax.broadcasted_iota(jnp.int32, (T, T), 0)
    s = lax.broadcasted_iota(jnp.int32, (T, T), 1)
    m = jnp.where(cmp(j, s), 1.0, 0.0).astype(BF16)
    return jnp.concatenate([m, m], axis=0)


def _head_masks():
    lane = lax.broadcasted_iota(jnp.int32, (1, LANES), 1)
    return [lane < SB_HEAD_DIM, lane >= SB_HEAD_DIM]


def _cargo(refs, n_in, n_out, cargo, scatter):
    n = len(cargo)
    if not n:
        return refs, lambda first: None, lambda last: None
    ins = refs[n_in:n_in + n]
    outs = refs[n_in + n + n_out:n_in + n + n_out + n]
    sems = refs[len(refs) - 3:]
    own = refs[:n_in] + refs[n_in + n:n_in + n + n_out] + refs[n_in + n + n_out + n:len(refs) - 3]

    def start(first):
        @pl.when(first)
        def _():
            for cp in _chip_copies(ins, outs, *sems, scatter=scatter):
                cp.start()

    def finish(last):
        @pl.when(last)
        def _():
            for cp in _chip_copies(ins, outs, *sems, scatter=scatter):
                cp.wait()

    return own, start, finish


def _sba_fwd(qkv, *, name, cargo=()):
    S = qkv.shape[0]
    T = min(SB_TILE, S)
    nq = S // T
    scale = SB_HEAD_DIM ** -0.5

    def body(*refs):
        (q_ref, k_ref, v_ref, o_ref, t_ref, first_ref), start_cargo, finish_cargo = _cargo(refs, 3, 3, cargo, False)
        p, i = pl.program_id(0), pl.program_id(1)
        start_cargo(jnp.logical_and(p == 0, i == 0))
        row = lax.broadcasted_iota(jnp.int32, (T, T), 0)
        col = lax.broadcasted_iota(jnp.int32, (T, T), 1)
        strict = col < row
        after = _tri(T, lambda j, s: j > s)
        masks = _head_masks()
        q = q_ref[...] * scale
        qs = [jnp.where(hm, q, jnp.zeros_like(q)) for hm in masks]

        def walk(tiles, carry):
            values, logs = [], []
            for j, diag in tiles:
                rows = pl.ds(pl.multiple_of(j * T, T), T)
                kb = k_ref[rows, :]
                values.append(v_ref[rows, :])
                for qh in qs:
                    z = lax.dot_general(qh, kb, (((1,), (1,)), ((), ())), preferred_element_type=F32)
                    ls, ln = _log_sigmoids(z)
                    logs.append((ls, jnp.where(strict, ln, 0.0) if diag else ln))
            suffixes = [jnp.dot(_split_bf16(ln), after, preferred_element_type=F32) for _, ln in logs]
            for t, (_, diag) in enumerate(tiles):
                out = []
                for h, (C, acc) in enumerate(carry):
                    ls, ln = logs[2 * t + h]
                    a = jnp.exp(ls + suffixes[2 * t + h] + C)
                    if diag:
                        a = jnp.where(strict, a, 0.0)
                    acc = acc + jnp.dot(a.astype(BF16), values[t], preferred_element_type=F32)
                    out.append((C + jnp.sum(ln, axis=1, keepdims=True), acc))
                carry = tuple(out)
            return carry

        fresh = (jnp.zeros((T, 1), F32), jnp.zeros((T, LANES), F32))
        carry = lax.cond(i > 0, lambda: walk([(i, True), (i - 1, False)], (fresh, fresh)),
                         lambda: walk([(i, True)], (fresh, fresh)))

        def alive(state):
            j, ((C0, _), (C1, _)) = state
            return jnp.logical_and(j >= 0, jnp.max(jnp.maximum(C0, C1)) > SB_DEAD_LOG)

        def step(state):
            j, carry = state
            return j - 1, walk([(j, False)], carry)

        j, ((C0, acc0), (C1, acc1)) = lax.while_loop(alive, step, (i - 2, carry))
        t_ref[0] = jnp.broadcast_to(C0, (T, LANES))
        t_ref[1] = jnp.broadcast_to(C1, (T, LANES))
        first_ref[...] = jnp.full((8, LANES), jnp.maximum(j + 1, 0).astype(F32))
        o_ref[...] = jnp.where(masks[0], acc0, acc1).astype(o_ref.dtype)
        finish_cargo(jnp.logical_and(p == SB_PAIRS - 1, i == nq - 1))

    kv = lambda off: pl.BlockSpec((S, LANES), lambda p, i: (0, off + p))
    n = len(cargo)
    return pl.pallas_call(
        body, name=name, grid=(SB_PAIRS, nq),
        in_specs=[pl.BlockSpec((T, LANES), lambda p, i: (i, p)), kv(SB_PAIRS), kv(2 * SB_PAIRS)] + [HBM_SPEC] * n,
        out_specs=[pl.BlockSpec((T, LANES), lambda p, i: (i, p)), pl.BlockSpec((2, T, LANES), lambda p, i: (p, i, 0)),
                   pl.BlockSpec((None, None, 8, LANES), lambda p, i: (p, i, 0, 0))] + [HBM_SPEC] * n,
        out_shape=[jax.ShapeDtypeStruct((S, SB_WIDTH), BF16), jax.ShapeDtypeStruct((SB_HEADS, S, LANES), F32),
                   jax.ShapeDtypeStruct((SB_PAIRS, nq, 8, LANES), F32)] + _chip_exchange_shapes(cargo, False),
        scratch_shapes=_chip_exchange_semaphores(n) if n else [],
        compiler_params=_params(2))(qkv, qkv, qkv, *cargo)


def _sba_bwd(qkv, do, tot, first, *, name, cargo=()):
    S = qkv.shape[0]
    T = min(SB_TILE, S)
    nq = S // T
    scale = SB_HEAD_DIM ** -0.5

    def body(*refs):
        own, start_cargo, finish_cargo = _cargo(refs, 6, 3, cargo, True)
        q_ref, k_ref, v_ref, do_ref, t_ref, first_ref, dq_ref, dk_ref, dv_ref, dk_acc, dv_acc = own
        p, i = pl.program_id(0), pl.program_id(1)
        start_cargo(jnp.logical_and(p == 0, i == 0))

        @pl.when(i == 0)
        def _():
            dk_acc[...] = jnp.zeros_like(dk_acc)
            dv_acc[...] = jnp.zeros_like(dv_acc)

        row = lax.broadcasted_iota(jnp.int32, (T, T), 0)
        col = lax.broadcasted_iota(jnp.int32, (T, T), 1)
        strict = col < row
        upto = _tri(T, lambda j, s: j <= s)
        before = _tri(T, lambda j, s: j < s)
        masks = _head_masks()
        q, do_t = q_ref[...], do_ref[...]
        q = q * scale
        qs = [jnp.where(hm, q, jnp.zeros_like(q)) for hm in masks]
        dos = [jnp.where(hm, do_t, jnp.zeros_like(do_t)) for hm in masks]
        totals = [t_ref[h][:, 0:1] for h in range(2)]
        over_lanes = (((1,), (1,)), ((), ()))
        over_queries = (((0,), (0,)), ((), ()))

        def walk(tiles, carry):
            rows = [pl.ds(pl.multiple_of(j * T, T), T) for j, _ in tiles]
            keys = [k_ref[r, :] for r in rows]
            values = [v_ref[r, :] for r in rows]
            chains = [(t, h) for t in range(len(tiles)) for h in range(2)]
            logs, da = {}, {}
            for t, h in chains:
                z = lax.dot_general(qs[h], keys[t], over_lanes, preferred_element_type=F32)
                ls, ln = _log_sigmoids(z)
                logs[t, h] = (ls, jnp.where(strict, ln, 0.0) if tiles[t][1] else ln)
                da[t, h] = lax.dot_general(dos[h], values[t], over_lanes, preferred_element_type=F32)
            upto_sums = {c: jnp.dot(_split_bf16(logs[c][1]), upto, preferred_element_type=F32) for c in chains}
            a, g = {}, {}
            P = [c[0] for c in carry]
            for t, h in chains:
                ls, ln = logs[t, h]
                a_th = jnp.exp(ls + ((totals[h] - P[h]) - upto_sums[t, h]))
                a[t, h] = jnp.where(strict, a_th, 0.0) if tiles[t][1] else a_th
                g[t, h] = a[t, h] * da[t, h]
                P[h] = P[h] + jnp.sum(ln, axis=1, keepdims=True)
            before_sums = {c: jnp.dot(_split_bf16(g[c]), before, preferred_element_type=F32) for c in chains}
            G = [c[1] for c in carry]
            dq = [c[2] for c in carry]
            dz = {}
            for t, h in chains:
                beta = jnp.exp(logs[t, h][0])
                dz_th = g[t, h] * (1.0 - beta) - (G[h] + before_sums[t, h]) * beta
                dz[t, h] = (jnp.where(strict, dz_th, 0.0) if tiles[t][1] else dz_th).astype(BF16)
                G[h] = G[h] + jnp.sum(g[t, h], axis=1, keepdims=True)
            for t, h in chains:
                dq[h] = dq[h] + jnp.dot(dz[t, h], keys[t], preferred_element_type=F32)
            for t in range(len(tiles)):
                dk_acc[rows[t], :] += sum(
                    lax.dot_general(dz[t, h], qs[h], over_queries, preferred_element_type=F32) for h in range(2))
                dv_acc[rows[t], :] += sum(
                    lax.dot_general(a[t, h].astype(BF16), dos[h], over_queries, preferred_element_type=F32)
                    for h in range(2))
            return tuple((P[h], G[h], dq[h]) for h in range(2))

        zero = jnp.zeros((T, 1), F32)
        fresh = (zero, zero, jnp.zeros((T, LANES), F32))
        last_single = jnp.maximum(i - 1, 0)
        j0 = jnp.clip(jnp.max(first_ref[...]).astype(jnp.int32), 0, last_single)
        carry = lax.fori_loop(j0, last_single, lambda j, c: walk([(j, False)], c), (fresh, fresh))
        (_, _, dq0), (_, _, dq1) = lax.cond(i > 0, lambda: walk([(i - 1, False), (i, True)], carry),
                                            lambda: walk([(i, True)], carry))
        dq_ref[...] = (jnp.where(masks[0], dq0, dq1) * scale).astype(dq_ref.dtype)

        @pl.when(i == nq - 1)
        def _():
            dk_ref[...] = dk_acc[...].astype(dk_ref.dtype)
            dv_ref[...] = dv_acc[...].astype(dv_ref.dtype)

        finish_cargo(jnp.logical_and(p == SB_PAIRS - 1, i == nq - 1))

    kv = lambda off: pl.BlockSpec((S, LANES), lambda p, i: (0, off + p))
    tile = lambda off: pl.BlockSpec((T, LANES), lambda p, i: (i, off + p))
    n = len(cargo)
    return pl.pallas_call(
        body, name=name, grid=(SB_PAIRS, nq),
        in_specs=[tile(0), kv(SB_PAIRS), kv(2 * SB_PAIRS), tile(0), pl.BlockSpec((2, T, LANES), lambda p, i: (p, i, 0)),
                  pl.BlockSpec((None, None, 8, LANES), lambda p, i: (p, i, 0, 0))] + [HBM_SPEC] * n,
        out_specs=[tile(0), kv(0), kv(0)] + [HBM_SPEC] * n,
        out_shape=[jax.ShapeDtypeStruct((S, SB_WIDTH), BF16)] * 3 + _chip_exchange_shapes(cargo, True),
        scratch_shapes=[pltpu.VMEM((S, LANES), F32), pltpu.VMEM((S, LANES), F32)]
        + (_chip_exchange_semaphores(n) if n else []),
        compiler_params=_params(2))(qkv, qkv, qkv, do, tot, first, *cargo)


POOL_TILE = 512


def _by_group(lane, values):
    return jnp.where(lane < 64, values[0], jnp.where(lane < 128, values[1], jnp.where(lane < 192, values[2], values[3])))


def _pool_inv_count(first_row, n_rows):
    t = first_row + lax.broadcasted_iota(jnp.int32, (n_rows, POOL_WIDTH), 0)
    lane = lax.broadcasted_iota(jnp.int32, (n_rows, POOL_WIDTH), 1)
    window = _by_group(lane, POOL_WINDOWS)
    return 1.0 / jnp.clip(t + 1, 1, window).astype(F32), lane


def _pooled(ext, first_row, R):
    n = R + POOL_HALO
    s2 = ext + pltpu.roll(ext, 1, 0)
    s4 = s2 + pltpu.roll(s2, 2, 0)
    s8 = s4 + pltpu.roll(s4, 4, 0)
    s16 = s8 + pltpu.roll(s8, 8, 0)
    inv, lane = _pool_inv_count(first_row - POOL_HALO, n)
    pooled = _by_group(lane, (s2, s4, s8, s16)) * inv - ext
    return pooled[POOL_HALO:, :]


def _pool_specs(S, R, col):
    per = R // POOL_HALO
    tile = pl.BlockSpec((R, POOL_WIDTH), lambda i: (i, col))
    prev = pl.BlockSpec((POOL_HALO, POOL_WIDTH), lambda i: (jnp.maximum(i * per - 1, 0), col))
    return tile, prev


def _pool_fwd(rest, w_bd, scale, *, name):
    S = rest.shape[0]
    R = min(POOL_TILE, S)

    def body(p_ref, prev_ref, w_ref, s_ref, o_ref, ext_ref):
        i = pl.program_id(0)
        ext_ref[:POOL_HALO, :] = jnp.where(i > 0, prev_ref[...], 0.0)
        ext_ref[POOL_HALO:, :] = p_ref[...]
        pooled = _pooled(ext_ref[...], i * R, R)
        mixed = jnp.dot(pooled.astype(BF16), w_ref[...], preferred_element_type=F32)
        o_ref[...] = (mixed * s_ref[...]).astype(o_ref.dtype)

    tile, prev = _pool_specs(S, R, 0)
    return pl.pallas_call(
        body, name=name, grid=(S // R,),
        in_specs=[tile, prev, pl.BlockSpec((POOL_WIDTH, POOL_WIDTH), lambda i: (0, 0)), _vec_spec(POOL_WIDTH)],
        out_specs=_row_spec(R, POOL_WIDTH), out_shape=jax.ShapeDtypeStruct((S, POOL_WIDTH), BF16),
        scratch_shapes=[pltpu.VMEM((R + POOL_HALO, POOL_WIDTH), F32)], compiler_params=_params(1))(rest, rest, w_bd, scale)


def _pool_bwd(rest, do, w_bd, scale, *, name):
    S = rest.shape[0]
    R = min(POOL_TILE, S)
    n_tiles = S // R
    per = R // POOL_HALO
    n = R + POOL_HALO

    def body(p_ref, prev_ref, do_ref, nxt_ref, w_ref, s_ref, dp_ref, dw_ref, ds_ref, ext_ref, dext_ref):
        i = pl.program_id(0)
        ext_ref[:POOL_HALO, :] = jnp.where(i > 0, prev_ref[...], 0.0)
        ext_ref[POOL_HALO:, :] = p_ref[...]
        pooled = _pooled(ext_ref[...], i * R, R).astype(BF16)
        w = w_ref[...]
        mixed = jnp.dot(pooled, w, preferred_element_type=F32)
        do_t = do_ref[...]
        _accumulate(ds_ref, jnp.sum(do_t * mixed, axis=0, keepdims=True))
        dext_ref[:R, :] = do_t
        dext_ref[R:, :] = jnp.where(i < n_tiles - 1, nxt_ref[...], 0.0)
        dmixed = (dext_ref[...] * s_ref[...]).astype(BF16)
        dpooled = lax.dot_general(dmixed, w, (((1,), (1,)), ((), ())), preferred_element_type=F32)
        _accumulate(dw_ref, lax.dot_general(pooled, dmixed[:R, :], (((0,), (0,)), ((), ())), preferred_element_type=F32))
        inv, lane = _pool_inv_count(i * R, n)
        u = dpooled * inv
        f2 = u + pltpu.roll(u, n - 1, 0)
        f4 = f2 + pltpu.roll(f2, n - 2, 0)
        f8 = f4 + pltpu.roll(f4, n - 4, 0)
        f16 = f8 + pltpu.roll(f8, n - 8, 0)
        dp = _by_group(lane, (f2, f4, f8, f16)) - dpooled
        dp_ref[...] = dp[:R, :].astype(dp_ref.dtype)

    tile, prev = _pool_specs(S, R, 0)
    nxt = pl.BlockSpec((POOL_HALO, POOL_WIDTH), lambda i: (jnp.minimum((i + 1) * per, S // POOL_HALO - 1), 0))
    full = pl.BlockSpec((POOL_WIDTH, POOL_WIDTH), lambda i: (0, 0))
    return pl.pallas_call(
        body, name=name, grid=(n_tiles,),
        in_specs=[tile, prev, _row_spec(R, POOL_WIDTH), nxt, full, _vec_spec(POOL_WIDTH)],
        out_specs=[_row_spec(R, POOL_WIDTH), full, _vec_spec(POOL_WIDTH)],
        out_shape=[jax.ShapeDtypeStruct((S, POOL_WIDTH), BF16), jax.ShapeDtypeStruct((POOL_WIDTH, POOL_WIDTH), F32),
                   jax.ShapeDtypeStruct((1, POOL_WIDTH), F32)],
        scratch_shapes=[pltpu.VMEM((n, POOL_WIDTH), F32), pltpu.VMEM((n, POOL_WIDTH), F32)],
        compiler_params=_params(1))(rest, rest, do, do, w_bd, scale)


GM_TILE = 512
GELU_C = math.sqrt(2.0 / math.pi)
GELU_A = 0.044715


def _gelu(x):
    return 0.5 * x * (1.0 + jnp.tanh(GELU_C * (x + GELU_A * x * x * x)))


def _gelu_and_grad(x):
    t = jnp.tanh(GELU_C * (x + GELU_A * x * x * x))
    y = 0.5 * x * (1.0 + t)
    dy = 0.5 * (1.0 + t) + 0.5 * x * (1.0 - t * t) * (GELU_C * (1.0 + 3.0 * GELU_A * x * x))
    return y, dy


def _group_lane_masks():
    lane = lax.broadcasted_iota(jnp.int32, (1, GM_WIDTH), 1)
    return [(lane >= g * GM_GROUP_DIM) & (lane < (g + 1) * GM_GROUP_DIM) for g in range(GM_GROUPS)]


def _stack_groups(x, masks):
    return jnp.concatenate([jnp.where(m, x, jnp.zeros_like(x)) for m in masks], axis=0)


def _gm_mixed(vn, ws_cat, bias, masks, R):
    chunks = []
    for c in range(R // GM_CHUNK):
        vc = vn[c * GM_CHUNK:(c + 1) * GM_CHUNK, :]
        chunks.append(jnp.dot(ws_cat, _stack_groups(vc, masks), preferred_element_type=F32) + bias)
    return jnp.concatenate(chunks, axis=0)


def _gm_specs(S, R):
    u = pl.BlockSpec((R, GM_WIDTH), lambda i: (i, 1))
    v = pl.BlockSpec((R, GM_WIDTH), lambda i: (i, 2))
    ws = pl.BlockSpec((GM_CHUNK, GM_GROUPS * GM_CHUNK), lambda i: (0, 0))
    bias = pl.BlockSpec((GM_CHUNK, GM_WIDTH), lambda i: (0, 0))
    return u, v, ws, bias


def _gm_fwd(rest, gain, ws_cat, bias, *, name):
    S = rest.shape[0]
    R = min(GM_TILE, S)

    def body(u_ref, v_ref, g_ref, ws_ref, b_ref, o_ref):
        gv = _gelu(v_ref[...])
        vn = (gv * _rstd(gv) * g_ref[...]).astype(BF16)
        mixed = _gm_mixed(vn, ws_ref[...], b_ref[...], _group_lane_masks(), R)
        o_ref[...] = (_gelu(u_ref[...]) * mixed).astype(o_ref.dtype)

    u_spec, v_spec, ws_spec, bias_spec = _gm_specs(S, R)
    return pl.pallas_call(
        body, name=name, grid=(S // R,), in_specs=[u_spec, v_spec, _vec_spec(GM_WIDTH), ws_spec, bias_spec],
        out_specs=_row_spec(R, GM_WIDTH), out_shape=jax.ShapeDtypeStruct((S, GM_WIDTH), BF16),
        compiler_params=_params(1))(rest, rest, gain, ws_cat, bias)


def _gm_bwd(rest, do, gain, ws_cat, wst_cat, bias, *, name):
    S = rest.shape[0]
    R = min(GM_TILE, S)

    def body(u_ref, v_ref, do_ref, g_ref, ws_ref, wst_ref, b_ref, du_ref, dv_ref, dg_ref, dws_ref, db_ref):
        masks = _group_lane_masks()
        gain_v = g_ref[...]
        gu, dgu = _gelu_and_grad(u_ref[...])
        gv, dgv = _gelu_and_grad(v_ref[...])
        r = _rstd(gv)
        vn = (gv * r * gain_v).astype(BF16)
        mixed = _gm_mixed(vn, ws_ref[...], b_ref[...], masks, R)
        do_t = do_ref[...]
        du_ref[...] = (do_t * mixed * dgu).astype(du_ref.dtype)
        dmix = do_t * gu
        dmix_b = dmix.astype(BF16)
        wst = wst_ref[...]
        dvn_chunks, db, dws = [], None, [None] * GM_GROUPS
        for c in range(R // GM_CHUNK):
            rows = slice(c * GM_CHUNK, (c + 1) * GM_CHUNK)
            dc, dcb, vc = dmix[rows, :], dmix_b[rows, :], vn[rows, :]
            db = dc if db is None else db + dc
            dvn_chunks.append(jnp.dot(wst, _stack_groups(dcb, masks), preferred_element_type=F32))
            for g, m in enumerate(masks):
                part = lax.dot_general(jnp.where(m, dcb, jnp.zeros_like(dcb)), vc, (((1,), (1,)), ((), ())),
                                       preferred_element_type=F32)
                dws[g] = part if dws[g] is None else dws[g] + part
        dvn = jnp.concatenate(dvn_chunks, axis=0)
        lane = lax.broadcasted_iota(jnp.int32, (1, LANES), 1)
        db_groups = jnp.zeros((GM_CHUNK, LANES), F32)
        for g, m in enumerate(masks):
            total = jnp.sum(jnp.where(m, db, 0.0), axis=1, keepdims=True)
            db_groups = db_groups + jnp.where(lane == g, total, 0.0)
        _accumulate(db_ref, db_groups)
        i = pl.program_id(0)
        for g in range(GM_GROUPS):
            @pl.when(i == 0)
            def _(g=g):
                dws_ref[g] = dws[g]

            @pl.when(i > 0)
            def _(g=g):
                dws_ref[g] += dws[g]
        _accumulate(dg_ref, jnp.sum(dvn * gv * r, axis=0, keepdims=True))
        gd = gain_v * dvn
        dgv_in = r * gd - gv * (r * r * r) * jnp.mean(gv * gd, axis=-1, keepdims=True)
        dv_ref[...] = (dgv_in * dgv).astype(dv_ref.dtype)

    u_spec, v_spec, ws_spec, bias_spec = _gm_specs(S, R)
    row, vec = _row_spec(R, GM_WIDTH), _vec_spec(GM_WIDTH)
    dws_spec = pl.BlockSpec((GM_GROUPS, GM_CHUNK, GM_CHUNK), lambda i: (0, 0, 0))
    return pl.pallas_call(
        body, name=name, grid=(S // R,), in_specs=[u_spec, v_spec, row, vec, ws_spec, ws_spec, bias_spec],
        out_specs=[row, row, vec, dws_spec, pl.BlockSpec((GM_CHUNK, LANES), lambda i: (0, 0))],
        out_shape=[jax.ShapeDtypeStruct((S, GM_WIDTH), BF16)] * 2
        + [jax.ShapeDtypeStruct((1, GM_WIDTH), F32), jax.ShapeDtypeStruct((GM_GROUPS, GM_CHUNK, GM_CHUNK), F32),
           jax.ShapeDtypeStruct((GM_CHUNK, LANES), F32)],
        compiler_params=_params(1))(rest, rest, do, gain, ws_cat, wst_cat, bias)


GATE_ROWS = 1024
GATE_COLS = 256
GATE_BLOCKS = D_MODEL // GATE_COLS


def _gate_spec(tr, k):
    return pl.BlockSpec((tr, GATE_COLS), lambda i, j: (i, GATE_BLOCKS * k + j))


def _merge_fwd(gates, branches, *, name):
    S = gates.shape[0]
    tr = min(GATE_ROWS, S)

    def body(g0, g1, g2, b0, b1, b2, o_ref):
        acc = None
        for g_ref, b_ref in ((g0, b0), (g1, b1), (g2, b2)):
            term = jax.nn.sigmoid(g_ref[...].astype(F32)) * b_ref[...].astype(F32)
            acc = term if acc is None else acc + term
        o_ref[...] = acc.astype(o_ref.dtype)

    tile = pl.BlockSpec((tr, GATE_COLS), lambda i, j: (i, j))
    return pl.pallas_call(
        body, name=name, grid=(S // tr, GATE_BLOCKS),
        in_specs=[_gate_spec(tr, k) for k in range(N_BRANCH)] + [tile] * N_BRANCH, out_specs=tile,
        out_shape=jax.ShapeDtypeStruct((S, D_MODEL), BF16), compiler_params=_params(2))(gates, gates, gates, *branches)


def _merge_bwd(gates, branches, dmerged, *, name):
    S = gates.shape[0]
    tr = min(GATE_ROWS, S)

    def body(g0, g1, g2, b0, b1, b2, dm_ref, dg0, dg1, dg2, db0, db1, db2):
        dm = dm_ref[...].astype(F32)
        for g_ref, b_ref, dg_ref, db_ref in ((g0, b0, dg0, db0), (g1, b1, dg1, db1), (g2, b2, dg2, db2)):
            s = jax.nn.sigmoid(g_ref[...].astype(F32))
            db_ref[...] = (dm * s).astype(db_ref.dtype)
            dg_ref[...] = (dm * b_ref[...].astype(F32) * s * (1.0 - s)).astype(dg_ref.dtype)

    tile = pl.BlockSpec((tr, GATE_COLS), lambda i, j: (i, j))
    return pl.pallas_call(
        body, name=name, grid=(S // tr, GATE_BLOCKS),
        in_specs=[_gate_spec(tr, k) for k in range(N_BRANCH)] + [tile] * (N_BRANCH + 1), out_specs=[tile] * (2 * N_BRANCH),
        out_shape=[jax.ShapeDtypeStruct((S, D_MODEL), BF16)] * (2 * N_BRANCH),
        compiler_params=_params(2))(gates, gates, gates, *branches, dmerged)


TILE_BYTES = 24 * 1024 * 1024


BF16_ROWS = 16


def _tile_rows(rows, cols, n_arrays):
    padded = -(-cols // LANES) * LANES
    cap = max(BF16_ROWS, TILE_BYTES // (2 * n_arrays * padded * 4))
    best = None
    for tr in range(BF16_ROWS, min(rows, cap) + 1, BF16_ROWS):
        if rows % tr == 0:
            best = tr
    assert best is not None, (rows, cols)
    return best


def _sum_slots(stack, *, name):
    n, R, C = stack.shape
    tr = _tile_rows(R, C, n + 1)

    def body(s_ref, o_ref):
        acc = s_ref[0].astype(F32)
        for k in range(1, n):
            acc = acc + s_ref[k].astype(F32)
        o_ref[...] = acc

    return pl.pallas_call(
        body, name=name, grid=(R // tr,), in_specs=[pl.BlockSpec((n, tr, C), lambda i: (0, i, 0))],
        out_specs=_row_spec(tr, C), out_shape=jax.ShapeDtypeStruct((R, C), F32), compiler_params=_params(1))(stack)


def _add_own_half(parts, received, core, *, name):
    n, R, C = parts.shape
    half = R // 2
    tr = _tile_rows(half, C, 3)
    steps = half // tr

    def body(core_ref, own_ref, got_ref, o_ref):
        o_ref[...] = (own_ref[...] + got_ref[...]).astype(o_ref.dtype)

    tile = pl.BlockSpec((None, tr, C), lambda d, i, core_ref: (d, i, 0))
    own = pl.BlockSpec((None, tr, C), lambda d, i, core_ref: (d, core_ref[0] * steps + i, 0))
    return pl.pallas_call(
        body, name=name, out_shape=jax.ShapeDtypeStruct((n, half, C), BF16),
        grid_spec=pltpu.PrefetchScalarGridSpec(num_scalar_prefetch=1, grid=(n, steps), in_specs=[own, tile], out_specs=tile),
        compiler_params=_params(2))(core, parts, received)


def _adamw_math(w, m, v, g):
    m_new = ADAM_B1 * m + (1.0 - ADAM_B1) * g
    v_new = ADAM_B2 * v + (1.0 - ADAM_B2) * jnp.square(g)
    m_hat = m_new / (1.0 - ADAM_B1 ** ADAM_STEP)
    v_hat = v_new / (1.0 - ADAM_B2 ** ADAM_STEP)
    return -ADAM_LR * (m_hat / (jnp.sqrt(v_hat) + ADAM_EPS) + ADAM_WD * w), m_new, v_new


def _adamw_halves(w, m, v, mine, theirs, core, *, name):
    L, r, C = w.shape
    tr = _tile_rows(r // 2, C, 9)
    steps = r // 2 // tr

    def body(core_ref, w_ref, m_ref, v_ref, mine_ref, theirs_ref, go_ref, d_ref, mo_ref, vo_ref):
        in_my_half = pl.program_id(1) // steps == core_ref[0]
        g = jnp.where(in_my_half, mine_ref[...], theirs_ref[...])
        go_ref[...] = g
        d_ref[...], mo_ref[...], vo_ref[...] = _adamw_math(w_ref[...], m_ref[...], v_ref[...], g)

    row = pl.BlockSpec((None, tr, C), lambda l, i, core_ref: (l, i, 0))
    half = pl.BlockSpec((None, tr, C), lambda l, i, core_ref: (l, i % steps, 0))
    return pl.pallas_call(
        body, name=name, out_shape=[jax.ShapeDtypeStruct((L, r, C), F32)] * 4,
        grid_spec=pltpu.PrefetchScalarGridSpec(
            num_scalar_prefetch=1, grid=(L, r // tr), in_specs=[row, row, row, half, half], out_specs=[row] * 4),
        compiler_params=_params(2))(core, w, m, v, mine, theirs)


HBM_SPEC = pl.BlockSpec(memory_space=pl.ANY)


def _position():
    return lax.axis_index("x"), lax.axis_index("y"), lax.axis_index("c")


def _other_chips(x, y):
    return [(1 - x, y), (x, 1 - y), (1 - x, 1 - y)]


def _chip_exchange(arrays, *, scatter, name):
    n = len(arrays)

    def body(*refs):
        copies = _chip_copies(refs[:n], refs[n:2 * n], *refs[2 * n:], scatter=scatter)
        for cp in copies:
            cp.start()
        for cp in copies:
            cp.wait()

    return pl.pallas_call(
        body, name=name, in_specs=[HBM_SPEC] * n, out_specs=[HBM_SPEC] * n, out_shape=_chip_exchange_shapes(arrays, scatter),
        scratch_shapes=_chip_exchange_semaphores(n))(*arrays)


def _chip_exchange_shapes(arrays, scatter):
    return [jax.ShapeDtypeStruct(a.shape if scatter else (N_CHIPS, 2) + a.shape, a.dtype) for a in arrays]


def _chip_exchange_semaphores(n):
    return [pltpu.SemaphoreType.DMA((3 * n,)), pltpu.SemaphoreType.DMA((3 * n,)), pltpu.SemaphoreType.DMA((n,))]


def _chip_copies(ins, outs, send_sems, recv_sems, local_sems, *, scatter):
    x, y, c = _position()
    me = 2 * x + y
    copies = []
    for a in range(len(ins)):
        own = ins[a].at[me] if scatter else ins[a]
        slot = outs[a].at[me] if scatter else outs[a].at[me, c]
        copies.append(pltpu.make_async_copy(own, slot, local_sems.at[a]))
        for k, (px, py) in enumerate(_other_chips(x, y)):
            src = ins[a].at[2 * px + py] if scatter else ins[a]
            copies.append(pltpu.make_async_remote_copy(
                src_ref=src, dst_ref=slot, send_sem=send_sems.at[3 * a + k],
                recv_sem=recv_sems.at[3 * a + k], device_id=(px, py, c), device_id_type=MESH))
    return copies


def _sibling_fill(arrays, *, name):
    n = len(arrays)

    def body(*refs):
        ins, outs = refs[:n], refs[n:2 * n]
        send_sems, recv_sems = refs[2 * n:]
        x, y, c = _position()
        copies = []
        for a in range(n):
            cp = pltpu.make_async_remote_copy(
                src_ref=ins[a].at[:, c], dst_ref=outs[a].at[:, c], send_sem=send_sems.at[a], recv_sem=recv_sems.at[a],
                device_id=(x, y, 1 - c), device_id_type=MESH)
            cp.start()
            copies.append(cp)
        for cp in copies:
            cp.wait()

    return pl.pallas_call(
        body, name=name, in_specs=[HBM_SPEC] * n, out_specs=[HBM_SPEC] * n,
        out_shape=[jax.ShapeDtypeStruct(a.shape, a.dtype) for a in arrays],
        input_output_aliases={a: a for a in range(n)},
        scratch_shapes=[pltpu.SemaphoreType.DMA((n,)), pltpu.SemaphoreType.DMA((n,))],
    )(*arrays)


def _sibling_swap(arrays, *, name):
    n = len(arrays)

    def body(*refs):
        ins, outs = refs[:n], refs[n:2 * n]
        send_sems, recv_sems = refs[2 * n:]
        x, y, c = _position()
        copies = []
        for a in range(n):
            cp = pltpu.make_async_remote_copy(
                src_ref=ins[a], dst_ref=outs[a], send_sem=send_sems.at[a], recv_sem=recv_sems.at[a],
                device_id=(x, y, 1 - c), device_id_type=MESH)
            cp.start()
            copies.append(cp)
        for cp in copies:
            cp.wait()

    return pl.pallas_call(
        body, name=name, in_specs=[HBM_SPEC] * n, out_specs=[HBM_SPEC] * n,
        out_shape=[jax.ShapeDtypeStruct(a.shape, a.dtype) for a in arrays],
        scratch_shapes=[pltpu.SemaphoreType.DMA((n,)), pltpu.SemaphoreType.DMA((n,))],
    )(*arrays)


def _sibling_other_half(arrays, *, name):
    n = len(arrays)

    def body(*refs):
        ins, outs = refs[:n], refs[n:2 * n]
        send_sems, recv_sems = refs[2 * n:]
        x, y, c = _position()
        copies = []
        for a in range(n):
            half = ins[a].shape[1] // 2
            theirs = ins[a].at[:, pl.ds(pl.multiple_of((1 - c) * half, BF16_ROWS), half), :]
            cp = pltpu.make_async_remote_copy(
                src_ref=theirs, dst_ref=outs[a], send_sem=send_sems.at[a], recv_sem=recv_sems.at[a],
                device_id=(x, y, 1 - c), device_id_type=MESH)
            cp.start()
            copies.append(cp)
        for cp in copies:
            cp.wait()

    return pl.pallas_call(
        body, name=name, in_specs=[HBM_SPEC] * n, out_specs=[HBM_SPEC] * n,
        out_shape=[jax.ShapeDtypeStruct((a.shape[0], a.shape[1] // 2, a.shape[2]), a.dtype) for a in arrays],
        scratch_shapes=[pltpu.SemaphoreType.DMA((n,)), pltpu.SemaphoreType.DMA((n,))],
    )(*arrays)


def _small_update(grads, ws, ms, vs, *, name):
    n, L = len(ws), ws[0].shape[0]
    pieces = [g for per_layer in grads for g in per_layer]
    np_ = len(pieces)

    def body(*refs):
        g_in, refs = refs[:np_], refs[np_:]
        w_in, m_in, v_in, g_out, d_out, m_out, v_out = (refs[k * n:(k + 1) * n] for k in range(7))
        from_sibling, chip_sums = refs[7 * n:7 * n + np_], refs[7 * n + np_:7 * n + 2 * np_]
        sibling_send, sibling_recv, chip_send, chip_recv = refs[7 * n + 2 * np_:]
        x, y, c = _position()
        me = 2 * x + y
        swaps = [pltpu.make_async_remote_copy(
            src_ref=g_in[p], dst_ref=from_sibling[p], send_sem=sibling_send.at[p], recv_sem=sibling_recv.at[p],
            device_id=(x, y, 1 - c), device_id_type=MESH) for p in range(np_)]
        for cp in swaps:
            cp.start()
        for cp in swaps:
            cp.wait()
        for p in range(np_):
            chip_sums[p][me] = g_in[p][...] + from_sibling[p][...]
        sends = [pltpu.make_async_remote_copy(
            src_ref=chip_sums[p].at[me], dst_ref=chip_sums[p].at[me], send_sem=chip_send.at[3 * p + k],
            recv_sem=chip_recv.at[3 * p + k], device_id=(px, py, c), device_id_type=MESH)
            for p in range(np_) for k, (px, py) in enumerate(_other_chips(x, y))]
        for cp in sends:
            cp.start()
        for cp in sends:
            cp.wait()
        for a in range(n):
            for l in range(L):
                sums = chip_sums[a * L + l]
                g = sums[0]
                for s in range(1, N_CHIPS):
                    g = g + sums[s]
                g_out[a][l] = g
                d_out[a][l], m_out[a][l], v_out[a][l] = _adamw_math(w_in[a][l], m_in[a][l], v_in[a][l], g)

    vmem = pl.BlockSpec(memory_space=pltpu.VMEM)
    shapes = [jax.ShapeDtypeStruct(w.shape, F32) for w in ws]
    outs = pl.pallas_call(
        body, name=name, in_specs=[vmem] * (np_ + 3 * n), out_specs=[vmem] * (4 * n), out_shape=shapes * 4,
        scratch_shapes=[pltpu.VMEM(g.shape, F32) for g in pieces] + [pltpu.VMEM((N_CHIPS,) + g.shape, F32) for g in pieces]
        + [pltpu.SemaphoreType.DMA((np_,)), pltpu.SemaphoreType.DMA((np_,)),
           pltpu.SemaphoreType.DMA((3 * np_,)), pltpu.SemaphoreType.DMA((3 * np_,))],
        compiler_params=pltpu.CompilerParams(vmem_limit_bytes=VMEM_LIMIT),
    )(*pieces, *ws, *ms, *vs)
    return outs[:n], outs[n:2 * n], outs[2 * n:3 * n], outs[3 * n:]


def _relu2(p):
    return p, jnp.square(jnp.maximum(p, 0.0))


def _relu2_grad(p, a):
    return (p * (2.0 * jnp.maximum(a.astype(F32), 0.0)),)


def _mixer_constants(w_pool, w_spatial, b_spatial):
    eye = jnp.eye(len(POOL_WINDOWS), dtype=F32)
    w_bd = (eye[:, None, :, None] * w_pool[:, :, None, :]).reshape(POOL_WIDTH, POOL_WIDTH).astype(BF16)
    causal = jnp.tril(jnp.ones((GM_CHUNK, GM_CHUNK), dtype=bool))
    ws = jnp.where(causal[None], w_spatial, 0.0).astype(BF16)
    ws_cat = ws.transpose(1, 0, 2).reshape(GM_CHUNK, GM_GROUPS * GM_CHUNK)
    wst_cat = ws.transpose(2, 0, 1).reshape(GM_CHUNK, GM_GROUPS * GM_CHUNK)
    bias = jnp.repeat(b_spatial.T, GM_GROUP_DIM, axis=1)
    return w_bd, ws_cat, wst_cat, bias


def _local_step(x, target, half_shards, small, core):
    L = small["g_mix_pre"].shape[0]
    vec = lambda name, l: small[name][l][None, :]
    consts = [_mixer_constants(small["w_pool"][l], small["w_spatial"][l], small["b_spatial"][l]) for l in range(L)]
    core_index = core.astype(jnp.int32).reshape(1)
    first_used, used_later = BIG_WEIGHTS[:1], BIG_WEIGHTS[1:]
    weights = [{} for _ in range(L)]

    def finish_gather(wanted, gathered):
        for (n, ll), both in zip(wanted, _sibling_fill(gathered, name="swap_weight_halves")):
            weights[ll][n] = _full_weight(n, both)

    def core_sums(names, l):
        parts = [_parts_by_chip(n, gb[n][l]) for n in names]
        from_sibling = _sibling_other_half(parts, name="swap_grad_halves")
        return [(n, l, _add_own_half(p, f, core_index, name="sum_cores")) for n, p, f in zip(names, parts, from_sibling)]

    wanted = [(n, 0) for n in first_used]
    finish_gather(wanted, _chip_exchange([half_shards[n][ll] for n, ll in wanted], scatter=False, name="gather_weights"))
    saved = []
    h = _rms_fwd(x, vec("g_mix_pre", 0), name="rms_in")
    for l in range(L):
        w_bd, ws_cat, wst_cat, bias = consts[l]
        proj = lambda n, off, dtype, name: _matmul(h, weights[l]["w_in"], tb=True, n=n, bn=PROJ_BLOCK,
                                                   b_col_off=off // PROJ_BLOCK, out_dtypes=(dtype,), name=name)
        qkv = proj(QKV_WIDTH, 0, BF16, "proj_qkv")
        rest = proj(MIX_WIDTH, QKV_WIDTH, F32, "proj_mix")
        gates = proj(GATE_WIDTH, QKV_WIDTH + MIX_WIDTH, BF16, "proj_gates")
        wanted = [(n, l) for n in used_later] + [(n, l + 1) for n in first_used if l + 1 < L]
        o_sb, tot, first, *gathered = _sba_fwd(qkv, cargo=[half_shards[n][ll] for n, ll in wanted], name="sba_fwd_gather")
        finish_gather(wanted, gathered)
        o_pool = _pool_fwd(rest, w_bd, vec("pool_scale", l), name="pool_fwd")
        o_gm = _gm_fwd(rest, vec("gm_gain", l), ws_cat, bias, name="gm_fwd")
        branches = (_matmul(o_sb, weights[l]["w_br_sb"], out_dtypes=(BF16,), name="br_sb"),
                    _matmul(o_pool, weights[l]["w_br_pool"], out_dtypes=(BF16,), name="br_pool"),
                    _matmul(o_gm, weights[l]["w_br_gm"], out_dtypes=(BF16,), name="br_gm"))
        merged = _merge_fwd(gates, branches, name="merge_fwd")
        y = _matmul(merged, weights[l]["w_out"], name="out_proj")
        x1, h2 = _resid_rms(x, y, vec("g_mix_post", l), vec("g_ff_pre", l), name="resid_mix")
        a, r = _matmul(h2, weights[l]["w_ff_in"], out_dtypes=(BF16, BF16), epilogue=_relu2, name="ff_in")
        ff = _matmul(r, weights[l]["w_ff_out"], name="ff_out")
        g_next = vec("g_mix_pre", l + 1) if l + 1 < L else None
        x2, h_next = _resid_rms(x1, ff, vec("g_ff_post", l), g_next, name="resid_ff" if l + 1 < L else "resid_last")
        saved.append(dict(x=x, h=h, qkv=qkv, rest=rest, gates=gates, o_sb=o_sb, tot=tot, first=first, o_pool=o_pool,
                          o_gm=o_gm, branches=branches, merged=merged, y=y, x1=x1, h2=h2, a=a, r=r, ff=ff))
        x, h = x2, h_next

    dx2, loss = _loss_head(x, target, name="loss_head")
    gb = {k: [None] * L for k in ("w_in", "w_br_sb", "w_br_pool", "w_br_gm", "w_out", "w_ff_in", "w_ff_out")}
    gs = {k: [None] * L for k in ("w_pool", "pool_scale", "gm_gain", "w_spatial", "b_spatial", "g_mix_pre",
                                  "g_mix_post", "g_ff_pre", "g_ff_post")}
    d_ff, gs["g_ff_post"][L - 1] = _rms_bwd(saved[-1]["ff"], vec("g_ff_post", L - 1), dx2, name="rms_bwd_last")
    received = [{} for _ in range(L)]
    waiting = []
    for l in reversed(range(L)):
        s = saved[l]
        w_bd, ws_cat, wst_cat, bias = consts[l]
        da = _matmul(d_ff, weights[l]["w_ff_out"], tb=True, out_dtypes=(BF16,), extras=(s["a"],), epilogue=_relu2_grad,
                     name="ff_out_dx")
        gb["w_ff_out"][l] = _matmul(s["r"], d_ff, ta=True, name="ff_out_dw")
        dh2 = _matmul(da, weights[l]["w_ff_in"], tb=True, name="ff_in_dx")
        gb["w_ff_in"][l] = _matmul(s["h2"], da, ta=True, name="ff_in_dw")
        dx1, gs["g_ff_pre"][l], dy, gs["g_mix_post"][l] = _rms_bwd_chain(
            s["x1"], vec("g_ff_pre", l), dh2, dx2, s["y"], vec("g_mix_post", l), name="rms_bwd_mid")
        dmerged = _matmul(dy, weights[l]["w_out"], tb=True, out_dtypes=(BF16,), name="out_proj_dx")
        gb["w_out"][l] = _matmul(s["merged"], dy, ta=True, name="out_proj_dw")
        dg0, dg1, dg2, db_sb, db_pool, db_gm = _merge_bwd(s["gates"], s["branches"], dmerged, name="merge_bwd")
        do_sb = _matmul(db_sb, weights[l]["w_br_sb"], tb=True, out_dtypes=(BF16,), name="br_sb_dx")
        gb["w_br_sb"][l] = _matmul(s["o_sb"], db_sb, ta=True, name="br_sb_dw")
        do_pool = _matmul(db_pool, weights[l]["w_br_pool"], tb=True, name="br_pool_dx")
        gb["w_br_pool"][l] = _matmul(s["o_pool"], db_pool, ta=True, name="br_pool_dw")
        do_gm = _matmul(db_gm, weights[l]["w_br_gm"], tb=True, name="br_gm_dx")
        gb["w_br_gm"][l] = _matmul(s["o_gm"], db_gm, ta=True, name="br_gm_dw")
        waiting += core_sums(used_later, l)
        dq, dk, dv, *arrived = _sba_bwd(s["qkv"], do_sb, s["tot"], s["first"], cargo=[c for _, _, c in waiting],
                                        name="sba_bwd_exchange")
        for (n, ll, _), got in zip(waiting, arrived):
            received[ll][n] = got
        dp, dw_bd, gs["pool_scale"][l] = _pool_bwd(s["rest"], do_pool, w_bd, vec("pool_scale", l), name="pool_bwd")
        du, dgv, gs["gm_gain"][l], dws, db = _gm_bwd(s["rest"], do_gm, vec("gm_gain", l), ws_cat, wst_cat, bias,
                                                      name="gm_bwd")
        gs["w_pool"][l] = jnp.stack([dw_bd[g * 64:(g + 1) * 64, g * 64:(g + 1) * 64] for g in range(len(POOL_WINDOWS))])
        gs["w_spatial"][l] = jnp.where(jnp.tril(jnp.ones((GM_CHUNK, GM_CHUNK), dtype=bool))[None], dws, 0.0)
        gs["b_spatial"][l] = db[:, :GM_GROUPS].T
        dproj = jnp.concatenate([dq, dk, dv, dp, du, dgv, dg0, dg1, dg2], axis=1)
        dh = _matmul(dproj, weights[l]["w_in"], bk=D_IN // 3, name="proj_dx")
        gb["w_in"][l] = _matmul(dproj, s["h"], ta=True, bm=PROJ_BLOCK, name="proj_dw")
        waiting = core_sums(first_used, l)
        if l == 0:
            for (n, ll, _), got in zip(waiting, _chip_exchange([c for _, _, c in waiting], scatter=True, name="exchange_grads")):
                received[ll][n] = got
        if l > 0:
            dx2, gs["g_mix_pre"][l], d_ff, gs["g_ff_post"][l - 1] = _rms_bwd_chain(
                s["x"], vec("g_mix_pre", l), dh, dx1, saved[l - 1]["ff"], vec("g_ff_post", l - 1), name="rms_bwd_mid")
        else:
            dx2, gs["g_mix_pre"][l], _, _ = _rms_bwd_chain(s["x"], vec("g_mix_pre", l), dh, dx1, None, None,
                                                           name="rms_bwd_first")
    return loss, dx2, received, gs


TRANSPOSED = ("w_in",)
COLUMN_SHARDED = ("w_br_sb", "w_br_pool", "w_br_gm", "w_ff_in")
ROW_SHARDED = ("w_in", "w_out", "w_ff_out")
BIG_WEIGHTS = ("w_in", "w_br_sb", "w_br_pool", "w_br_gm", "w_ff_in", "w_out", "w_ff_out")
SMALL_WEIGHTS = ("w_pool", "pool_scale", "gm_gain", "w_spatial", "b_spatial", "g_mix_pre", "g_mix_post", "g_ff_pre",
                 "g_ff_post")
WEIGHT_ORDER = ("w_in", "w_pool", "pool_scale", "gm_gain", "w_spatial", "b_spatial", "w_br_sb", "w_br_pool", "w_br_gm",
                "w_out", "g_mix_pre", "g_mix_post", "g_ff_pre", "g_ff_post", "w_ff_in", "w_ff_out")


def _full_weight(name, g):
    half, cols = g.shape[2], g.shape[3]
    if name in COLUMN_SHARDED:
        return g.transpose(1, 2, 0, 3).reshape(2 * half, N_CHIPS * cols)
    return g.reshape(N_CHIPS * 2 * half, cols)


def _parts_by_chip(name, grad):
    if name in COLUMN_SHARDED:
        r, c = grad.shape[0], grad.shape[1] // N_CHIPS
        return grad.reshape(r, N_CHIPS, c).transpose(1, 0, 2)
    return grad.reshape(N_CHIPS, grad.shape[0] // N_CHIPS, grad.shape[1])


def kernel(x, w_in, w_pool, pool_scale, gm_gain, w_spatial, b_spatial, w_br_sb, w_br_pool, w_br_gm, w_out, g_mix_pre, g_mix_post, g_ff_pre, g_ff_post, w_ff_in, w_ff_out, loss_target, m_w_in, m_w_pool, m_pool_scale, m_gm_gain, m_w_spatial, m_b_spatial, m_w_br_sb, m_w_br_pool, m_w_br_gm, m_w_out, m_g_mix_pre, m_g_mix_post, m_g_ff_pre, m_g_ff_post, m_w_ff_in, m_w_ff_out, v_w_in, v_w_pool, v_pool_scale, v_gm_gain, v_w_spatial, v_b_spatial, v_w_br_sb, v_w_br_pool, v_w_br_gm, v_w_out, v_g_mix_pre, v_g_mix_post, v_g_ff_pre, v_g_ff_post, v_w_ff_in, v_w_ff_out):
    given = dict(locals())
    w = {n: given[n] for n in WEIGHT_ORDER}
    m = {n: given["m_" + n] for n in WEIGHT_ORDER}
    v = {n: given["v_" + n] for n in WEIGHT_ORDER}
    L = w_in.shape[0]

    core = lax.axis_index("c")

    def my_rows(a):
        half = a.shape[1] // 2
        return lax.dynamic_slice_in_dim(a.astype(BF16), core * half, half, axis=1)

    view = lambda n, a: jnp.swapaxes(a, 1, 2) if n in TRANSPOSED else a
    half_shards = {n: my_rows(view(n, w[n])) for n in BIG_WEIGHTS}
    small = {n: w[n] for n in SMALL_WEIGHTS}
    loss, dx, received, small_grads = _local_step(x[0], loss_target[0], half_shards, small, core)

    core_index = core.astype(jnp.int32).reshape(1)
    reduced = []
    for n in BIG_WEIGHTS:
        from_chips = jnp.concatenate([received[l][n] for l in range(L)], axis=1)
        reduced.append(_sum_slots(from_chips, name="sum_chips"))
    reduced_by_sibling = _sibling_swap(reduced, name="swap_reduced_halves")
    grads, deltas, new_m, new_v = {}, {}, {}, {}
    for n, mine, theirs in zip(BIG_WEIGHTS, reduced, reduced_by_sibling):
        by_layer = lambda a: a.reshape(L, -1, a.shape[-1])
        outs = _adamw_halves(view(n, w[n]), view(n, m[n]), view(n, v[n]), by_layer(mine), by_layer(theirs), core_index,
                             name="adamw_big")
        grads[n], deltas[n], new_m[n], new_v[n] = [view(n, o) for o in outs]

    by_layer = lambda a: a.reshape(L, -1, a.shape[-1])
    flat = lambda a: a.reshape(-1, a.shape[-1])
    outs = _small_update([[flat(g) for g in small_grads[n]] for n in SMALL_WEIGHTS],
                         *[[by_layer(t[n]) for n in SMALL_WEIGHTS] for t in (w, m, v)], name="small_update")
    for store, arrays in zip((grads, deltas, new_m, new_v), outs):
        store.update({n: a.reshape(w[n].shape) for n, a in zip(SMALL_WEIGHTS, arrays)})

    total_loss = lax.psum(loss[0, 0], ("x", "y", "c"))
    return (total_loss, dx[None], *[grads[n] for n in WEIGHT_ORDER], *[deltas[n] for n in WEIGHT_ORDER],
            *[new_m[n] for n in WEIGHT_ORDER], *[new_v[n] for n in WEIGHT_ORDER])
```

```python
import functools
import math

import jax
import jax.numpy as jnp
from jax import lax
from jax.experimental import pallas as pl
from jax.experimental.pallas import tpu as pltpu

F32 = jnp.float32
BF16 = jnp.bfloat16

D_MODEL = 1024
SB_HEADS = 8
SB_HEAD_DIM = 64
SB_WIDTH = SB_HEADS * SB_HEAD_DIM
POOL_WINDOWS = (2, 4, 8, 16)
POOL_GROUP_DIM = 64
POOL_WIDTH = 256
POOL_HALO = 16
GM_GROUPS = 4
GM_GROUP_DIM = 64
GM_WIDTH = 256
GM_CHUNK = 128
N_BRANCH = 3
D_FF = 4 * D_MODEL
RMS_EPS = 1e-6
QKV_WIDTH = 3 * SB_WIDTH
MIX_WIDTH = POOL_WIDTH + 2 * GM_WIDTH
GATE_WIDTH = N_BRANCH * D_MODEL
D_IN = QKV_WIDTH + MIX_WIDTH + GATE_WIDTH
PROJ_BLOCK = 768
LANES = 128
N_CHIPS = 4
N_DEV = 8

ADAM_LR = 0.001
ADAM_B1 = 0.9
ADAM_B2 = 0.999
ADAM_EPS = 1e-08
ADAM_WD = 0.01
ADAM_STEP = 10

VMEM_LIMIT = 56 * 1024 * 1024
MESH = pl.DeviceIdType.MESH


def _params(n_grid):
    return pltpu.CompilerParams(dimension_semantics=("arbitrary",) * n_grid, vmem_limit_bytes=VMEM_LIMIT)


def _bf(x):
    return x if x.dtype == BF16 else x.astype(BF16)


def _matmul(a, b, *, name, ta=False, tb=False, out_dtypes=(F32,), n=None, b_col_off=0, bm=1024, bn=1024, bk=2048,
            extras=(), epilogue=None):
    M, K = (a.shape[1], a.shape[0]) if ta else a.shape
    nb = b.shape[0] if tb else b.shape[1]
    n = nb if n is None else n
    bm, bn, bk = min(bm, M), min(bn, n), min(bk, K)
    assert M % bm == 0 and n % bn == 0 and K % bk == 0, (name, M, n, K, bm, bn, bk)
    assert (b.shape[1] if tb else b.shape[0]) == K, (name, a.shape, b.shape)
    nk = K // bk
    dims = (((0 if ta else 1,), (1 if tb else 0,)), ((), ()))
    n_out = len(out_dtypes)
    direct = nk > 1 and epilogue is None and out_dtypes == (F32,)
    use_acc = nk > 1 and not direct

    def body(*refs):
        a_ref, b_ref = refs[:2]
        extra_refs = refs[2:2 + len(extras)]
        out_refs = refs[2 + len(extras):2 + len(extras) + n_out]
        p = lax.dot_general(_bf(a_ref[...]), _bf(b_ref[...]), dims, preferred_element_type=F32)

        def finish(acc):
            outs = (acc,) if epilogue is None else epilogue(acc, *[r[...] for r in extra_refs])
            for r, o in zip(out_refs, outs):
                r[...] = o.astype(r.dtype)

        if nk == 1:
            finish(p)
            return
        k = pl.program_id(2)
        acc_ref = out_refs[0] if direct else refs[-1]

        @pl.when(k == 0)
        def _():
            acc_ref[...] = p

        @pl.when(k > 0)
        def _():
            acc_ref[...] += p

        if use_acc:
            @pl.when(k == nk - 1)
            def _():
                finish(acc_ref[...])

    a_spec = pl.BlockSpec((bk, bm), lambda i, j, k: (k, i)) if ta else pl.BlockSpec((bm, bk), lambda i, j, k: (i, k))
    if tb:
        b_spec = pl.BlockSpec((bn, bk), lambda i, j, k: (j + b_col_off, k))
    else:
        b_spec = pl.BlockSpec((bk, bn), lambda i, j, k: (k, j + b_col_off))
    tile = pl.BlockSpec((bm, bn), lambda i, j, k: (i, j))
    outs = pl.pallas_call(
        body, name=name, grid=(M // bm, n // bn, nk),
        in_specs=[a_spec, b_spec] + [tile] * len(extras),
        out_specs=[tile] * n_out,
        out_shape=[jax.ShapeDtypeStruct((M, n), d) for d in out_dtypes],
        scratch_shapes=[pltpu.VMEM((bm, bn), F32)] if use_acc else [],
        compiler_params=_params(3),
    )(a, b, *extras)
    return outs[0] if n_out == 1 else outs


ROW_TILE = 512


def _rows(S):
    tr = min(ROW_TILE, S)
    assert S % tr == 0
    return tr


def _rstd(x):
    return lax.rsqrt(jnp.mean(x * x, axis=-1, keepdims=True) + RMS_EPS)


def _rms_bwd_math(x, g, dy):
    r = _rstd(x)
    gd = g * dy
    dx = r * gd - x * (r * r * r) * jnp.mean(x * gd, axis=-1, keepdims=True)
    dg = jnp.sum(dy * x * r, axis=0, keepdims=True)
    return dx, dg


def _accumulate(ref, value):
    i = pl.program_id(0)

    @pl.when(i == 0)
    def _():
        ref[...] = value

    @pl.when(i > 0)
    def _():
        ref[...] += value


def _row_spec(tr, width):
    return pl.BlockSpec((tr, width), lambda i: (i, 0))


def _vec_spec(width):
    return pl.BlockSpec((1, width), lambda i: (0, 0))


def _rms_fwd(x, g, *, name):
    S, D = x.shape
    tr = _rows(S)

    def body(x_ref, g_ref, o_ref):
        xf = x_ref[...]
        o_ref[...] = (xf * _rstd(xf) * g_ref[...]).astype(o_ref.dtype)

    return pl.pallas_call(
        body, name=name, grid=(S // tr,), in_specs=[_row_spec(tr, D), _vec_spec(D)], out_specs=_row_spec(tr, D),
        out_shape=jax.ShapeDtypeStruct((S, D), BF16), compiler_params=_params(1))(x, g)


def _resid_rms(x, y, g_post, g_next, *, name):
    S, D = x.shape
    tr = _rows(S)
    with_next = g_next is not None

    def body(*refs):
        if with_next:
            x_ref, y_ref, gp_ref, gn_ref, xo_ref, ho_ref = refs
        else:
            x_ref, y_ref, gp_ref, xo_ref = refs
        yf = y_ref[...]
        xn = x_ref[...] + yf * _rstd(yf) * gp_ref[...]
        xo_ref[...] = xn
        if with_next:
            ho_ref[...] = (xn * _rstd(xn) * gn_ref[...]).astype(ho_ref.dtype)

    row, vec = _row_spec(tr, D), _vec_spec(D)
    ins = [x, y, g_post] + ([g_next] if with_next else [])
    outs = pl.pallas_call(
        body, name=name, grid=(S // tr,), in_specs=[row, row, vec] + ([vec] if with_next else []),
        out_specs=[row] + ([row] if with_next else []),
        out_shape=[jax.ShapeDtypeStruct((S, D), F32)] + ([jax.ShapeDtypeStruct((S, D), BF16)] if with_next else []),
        compiler_params=_params(1))(*ins)
    return (outs[0], outs[1]) if with_next else (outs[0], None)


def _rms_bwd(x, g, dy, *, name):
    S, D = x.shape
    tr = _rows(S)

    def body(x_ref, g_ref, dy_ref, dx_ref, dg_ref):
        dx, dg = _rms_bwd_math(x_ref[...], g_ref[...], dy_ref[...])
        dx_ref[...] = dx.astype(dx_ref.dtype)
        _accumulate(dg_ref, dg)

    row, vec = _row_spec(tr, D), _vec_spec(D)
    return pl.pallas_call(
        body, name=name, grid=(S // tr,), in_specs=[row, vec, row], out_specs=[row, vec],
        out_shape=[jax.ShapeDtypeStruct((S, D), BF16), jax.ShapeDtypeStruct((1, D), F32)],
        compiler_params=_params(1))(x, g, dy)


def _rms_bwd_chain(xa, ga, da, resid, xb, gb, *, name):
    S, D = xa.shape
    tr = _rows(S)
    chain = xb is not None

    def body(*refs):
        if chain:
            xa_ref, ga_ref, da_ref, rs_ref, xb_ref, gb_ref, dx_ref, dga_ref, dxb_ref, dgb_ref = refs
        else:
            xa_ref, ga_ref, da_ref, rs_ref, dx_ref, dga_ref = refs
        dxa, dga = _rms_bwd_math(xa_ref[...], ga_ref[...], da_ref[...])
        dx = rs_ref[...] + dxa
        dx_ref[...] = dx
        _accumulate(dga_ref, dga)
        if chain:
            dxb, dgb = _rms_bwd_math(xb_ref[...], gb_ref[...], dx)
            dxb_ref[...] = dxb.astype(dxb_ref.dtype)
            _accumulate(dgb_ref, dgb)

    row, vec = _row_spec(tr, D), _vec_spec(D)
    ins = [xa, ga, da, resid] + ([xb, gb] if chain else [])
    outs = pl.pallas_call(
        body, name=name, grid=(S // tr,), in_specs=[row, vec, row, row] + ([row, vec] if chain else []),
        out_specs=[row, vec] + ([row, vec] if chain else []),
        out_shape=[jax.ShapeDtypeStruct((S, D), F32), jax.ShapeDtypeStruct((1, D), F32)]
        + ([jax.ShapeDtypeStruct((S, D), BF16), jax.ShapeDtypeStruct((1, D), F32)] if chain else []),
        compiler_params=_params(1))(*ins)
    return tuple(outs) if chain else (outs[0], outs[1], None, None)


def _loss_head(y, target, *, name):
    S, D = y.shape
    tr = _rows(S)
    n_tiles = S // tr

    def body(y_ref, t_ref, dy_ref, loss_ref, acc_ref):
        err = y_ref[...] - t_ref[...]
        dy_ref[...] = err * (1.0 / D)
        _accumulate(acc_ref, jnp.sum(err * err, axis=0, keepdims=True))

        @pl.when(pl.program_id(0) == n_tiles - 1)
        def _():
            loss_ref[...] = jnp.sum(acc_ref[...], axis=1, keepdims=True) * (0.5 / D)

    row = _row_spec(tr, D)
    return pl.pallas_call(
        body, name=name, grid=(n_tiles,), in_specs=[row, row],
        out_specs=[row, pl.BlockSpec((1, 1), lambda i: (0, 0))],
        out_shape=[jax.ShapeDtypeStruct((S, D), F32), jax.ShapeDtypeStruct((1, 1), F32)],
        scratch_shapes=[pltpu.VMEM((1, D), F32)], compiler_params=_params(1))(y, target)


SB_TILE = 256
SB_PAIRS = SB_HEADS * SB_HEAD_DIM // LANES
SB_DEAD_LOG = -110.0


def _log_sigmoids(z):
    ls = jnp.minimum(z, 0.0) - jnp.log(1.0 + jnp.exp(-jnp.abs(z)))
    return ls, ls - z


def _running_sums(x, tri2):
    hi = x.astype(BF16)
    lo = (x - hi.astype(F32)).astype(BF16)
    return jnp.dot(jnp.concatenate([hi, lo], axis=1), tri2, preferred_element_type=F32)


def _tri(T, cmp):
    j = lax.broadcasted_iota(jnp.int32, (T, T), 0)
    s = lax.broadcasted_iota(jnp.int32, (T, T), 1)
    m = jnp.where(cmp(j, s), 1.0, 0.0).astype(BF16)
    return jnp.concatenate([m, m], axis=0)


def _head_masks():
    lane = lax.broadcasted_iota(jnp.int32, (1, LANES), 1)
    return [lane < SB_HEAD_DIM, lane >= SB_HEAD_DIM]


def _cargo(refs, n_in, n_out, cargo, scatter):
    n = len(cargo)
    if not n:
        return refs, lambda first: None, lambda last: None
    ins = refs[n_in:n_in + n]
    outs = refs[n_in + n + n_out:n_in + n + n_out + n]
    sems = refs[len(refs) - 3:]
    own = refs[:n_in] + refs[n_in + n:n_in + n + n_out] + refs[n_in + n + n_out + n:len(refs) - 3]

    def start(first):
        @pl.when(first)
        def _():
            for cp in _chip_copies(ins, outs, *sems, scatter=scatter):
                cp.start()

    def finish(last):
        @pl.when(last)
        def _():
            for cp in _chip_copies(ins, outs, *sems, scatter=scatter):
                cp.wait()

    return own, start, finish


def _sba_fwd(qkv, *, name, cargo=()):
    S = qkv.shape[0]
    T = min(SB_TILE, S)
    nq = S // T
    scale = SB_HEAD_DIM ** -0.5

    def body(*refs):
        (q_ref, k_ref, v_ref, o_ref, t_ref, first_ref), start_cargo, finish_cargo = _cargo(refs, 3, 3, cargo, False)
        p, i = pl.program_id(0), pl.program_id(1)
        start_cargo(jnp.logical_and(p == 0, i == 0))
        row = lax.broadcasted_iota(jnp.int32, (T, T), 0)
        col = lax.broadcasted_iota(jnp.int32, (T, T), 1)
        strict = col < row
        after = _tri(T, lambda j, s: j > s)
        masks = _head_masks()
        q = q_ref[...] * scale
        qs = [jnp.where(hm, q, jnp.zeros_like(q)) for hm in masks]

        def walk(tiles, carry):
            values, zs = [], []
            for j, diag in tiles:
                rows = pl.ds(pl.multiple_of(j * T, T), T)
                kb = k_ref[rows, :]
                values.append(v_ref[rows, :])
                zs += [(lax.dot_general(qh, kb, (((1,), (1,)), ((), ())), preferred_element_type=F32), diag) for qh in qs]
            logs, suffixes = [], []
            for z, diag in zs:
                ls, ln = _log_sigmoids(z)
                ln = jnp.where(strict, ln, 0.0) if diag else ln
                logs.append((ls, ln))
                suffixes.append(_running_sums(ln, after))
            for t, (_, diag) in enumerate(tiles):
                out = []
                for h, (C, acc) in enumerate(carry):
                    ls, ln = logs[2 * t + h]
                    a = jnp.exp(ls + suffixes[2 * t + h] + C)
                    if diag:
                        a = jnp.where(strict, a, 0.0)
                    acc = acc + jnp.dot(a.astype(BF16), values[t], preferred_element_type=F32)
                    out.append((C + jnp.sum(ln, axis=1, keepdims=True), acc))
                carry = tuple(out)
            return carry

        fresh = (jnp.zeros((T, 1), F32), jnp.zeros((T, LANES), F32))
        carry = lax.cond(i > 0, lambda: walk([(i, True), (i - 1, False)], (fresh, fresh)),
                         lambda: walk([(i, True)], (fresh, fresh)))

        def alive(state):
            j, ((C0, _), (C1, _)) = state
            return jnp.logical_and(j >= 0, jnp.max(jnp.maximum(C0, C1)) > SB_DEAD_LOG)

        def step(state):
            j, carry = state
            return j - 1, walk([(j, False)], carry)

        j, ((C0, acc0), (C1, acc1)) = lax.while_loop(alive, step, (i - 2, carry))
        t_ref[0] = jnp.broadcast_to(C0, (T, LANES))
        t_ref[1] = jnp.broadcast_to(C1, (T, LANES))
        first_ref[...] = jnp.full((8, LANES), jnp.maximum(j + 1, 0).astype(F32))
        o_ref[...] = jnp.where(masks[0], acc0, acc1).astype(o_ref.dtype)
        finish_cargo(jnp.logical_and(p == SB_PAIRS - 1, i == nq - 1))

    kv = lambda off: pl.BlockSpec((S, LANES), lambda p, i: (0, off + p))
    n = len(cargo)
    return pl.pallas_call(
        body, name=name, grid=(SB_PAIRS, nq),
        in_specs=[pl.BlockSpec((T, LANES), lambda p, i: (i, p)), kv(SB_PAIRS), kv(2 * SB_PAIRS)] + [HBM_SPEC] * n,
        out_specs=[pl.BlockSpec((T, LANES), lambda p, i: (i, p)), pl.BlockSpec((2, T, LANES), lambda p, i: (p, i, 0)),
                   pl.BlockSpec((None, None, 8, LANES), lambda p, i: (p, i, 0, 0))] + [HBM_SPEC] * n,
        out_shape=[jax.ShapeDtypeStruct((S, SB_WIDTH), BF16), jax.ShapeDtypeStruct((SB_HEADS, S, LANES), F32),
                   jax.ShapeDtypeStruct((SB_PAIRS, nq, 8, LANES), F32)] + _chip_exchange_shapes(cargo, False),
        scratch_shapes=_chip_exchange_semaphores(n) if n else [],
        compiler_params=_params(2))(qkv, qkv, qkv, *cargo)


def _sba_bwd(qkv, do, tot, first, *, name, cargo=()):
    S = qkv.shape[0]
    T = min(SB_TILE, S)
    nq = S // T
    scale = SB_HEAD_DIM ** -0.5

    def body(*refs):
        own, start_cargo, finish_cargo = _cargo(refs, 6, 3, cargo, True)
        q_ref, k_ref, v_ref, do_ref, t_ref, first_ref, dq_ref, dk_ref, dv_ref, dk_acc, dv_acc = own
        p, i = pl.program_id(0), pl.program_id(1)
        start_cargo(jnp.logical_and(p == 0, i == 0))

        @pl.when(i == 0)
        def _():
            dk_acc[...] = jnp.zeros_like(dk_acc)
            dv_acc[...] = jnp.zeros_like(dv_acc)

        row = lax.broadcasted_iota(jnp.int32, (T, T), 0)
        col = lax.broadcasted_iota(jnp.int32, (T, T), 1)
        strict = col < row
        upto = _tri(T, lambda j, s: j <= s)
        before = _tri(T, lambda j, s: j < s)
        masks = _head_masks()
        q, do_t = q_ref[...], do_ref[...]
        q = q * scale
        qs = [jnp.where(hm, q, jnp.zeros_like(q)) for hm in masks]
        dos = [jnp.where(hm, do_t, jnp.zeros_like(do_t)) for hm in masks]
        totals = [t_ref[h][:, 0:1] for h in range(2)]
        over_lanes = (((1,), (1,)), ((), ()))
        over_queries = (((0,), (0,)), ((), ()))

        def walk(tiles, carry):
            rows = [pl.ds(pl.multiple_of(j * T, T), T) for j, _ in tiles]
            keys = [k_ref[r, :] for r in rows]
            values = [v_ref[r, :] for r in rows]
            chains = [(t, h) for t in range(len(tiles)) for h in range(2)]
            logs, da = {}, {}
            for t, h in chains:
                z = lax.dot_general(qs[h], keys[t], over_lanes, preferred_element_type=F32)
                ls, ln = _log_sigmoids(z)
                logs[t, h] = (ls, jnp.where(strict, ln, 0.0) if tiles[t][1] else ln)
                da[t, h] = lax.dot_general(dos[h], values[t], over_lanes, preferred_element_type=F32)
            upto_sums = {c: _running_sums(logs[c][1], upto) for c in chains}
            a, g = {}, {}
            P = [c[0] for c in carry]
            for t, h in chains:
                ls, ln = logs[t, h]
                a_th = jnp.exp(ls + ((totals[h] - P[h]) - upto_sums[t, h]))
                a[t, h] = jnp.where(strict, a_th, 0.0) if tiles[t][1] else a_th
                g[t, h] = a[t, h] * da[t, h]
                P[h] = P[h] + jnp.sum(ln, axis=1, keepdims=True)
            before_sums = {c: _running_sums(g[c], before) for c in chains}
            G = [c[1] for c in carry]
            dq = [c[2] for c in carry]
            dz = {}
            for t, h in chains:
                beta = jnp.exp(logs[t, h][0])
                dz_th = g[t, h] * (1.0 - beta) - (G[h] + before_sums[t, h]) * beta
                dz[t, h] = (jnp.where(strict, dz_th, 0.0) if tiles[t][1] else dz_th).astype(BF16)
                G[h] = G[h] + jnp.sum(g[t, h], axis=1, keepdims=True)
            for t, h in chains:
                dq[h] = dq[h] + jnp.dot(dz[t, h], keys[t], preferred_element_type=F32)
            for t in range(len(tiles)):
                dk_acc[rows[t], :] += sum(
                    lax.dot_general(dz[t, h], qs[h], over_queries, preferred_element_type=F32) for h in range(2))
                dv_acc[rows[t], :] += sum(
                    lax.dot_general(a[t, h].astype(BF16), dos[h], over_queries, preferred_element_type=F32)
                    for h in range(2))
            return tuple((P[h], G[h], dq[h]) for h in range(2))

        zero = jnp.zeros((T, 1), F32)
        fresh = (zero, zero, jnp.zeros((T, LANES), F32))
        last_single = jnp.maximum(i - 1, 0)
        j0 = jnp.clip(jnp.max(first_ref[...]).astype(jnp.int32), 0, last_single)
        carry = lax.fori_loop(j0, last_single, lambda j, c: walk([(j, False)], c), (fresh, fresh))
        (_, _, dq0), (_, _, dq1) = lax.cond(i > 0, lambda: walk([(i - 1, False), (i, True)], carry),
                                            lambda: walk([(i, True)], carry))
        dq_ref[...] = (jnp.where(masks[0], dq0, dq1) * scale).astype(dq_ref.dtype)

        @pl.when(i == nq - 1)
        def _():
            dk_ref[...] = dk_acc[...].astype(dk_ref.dtype)
            dv_ref[...] = dv_acc[...].astype(dv_ref.dtype)

        finish_cargo(jnp.logical_and(p == SB_PAIRS - 1, i == nq - 1))

    kv = lambda off: pl.BlockSpec((S, LANES), lambda p, i: (0, off + p))
    tile = lambda off: pl.BlockSpec((T, LANES), lambda p, i: (i, off + p))
    n = len(cargo)
    return pl.pallas_call(
        body, name=name, grid=(SB_PAIRS, nq),
        in_specs=[tile(0), kv(SB_PAIRS), kv(2 * SB_PAIRS), tile(0), pl.BlockSpec((2, T, LANES), lambda p, i: (p, i, 0)),
                  pl.BlockSpec((None, None, 8, LANES), lambda p, i: (p, i, 0, 0))] + [HBM_SPEC] * n,
        out_specs=[tile(0), kv(0), kv(0)] + [HBM_SPEC] * n,
        out_shape=[jax.ShapeDtypeStruct((S, SB_WIDTH), BF16)] * 3 + _chip_exchange_shapes(cargo, True),
        scratch_shapes=[pltpu.VMEM((S, LANES), F32), pltpu.VMEM((S, LANES), F32)]
        + (_chip_exchange_semaphores(n) if n else []),
        compiler_params=_params(2))(qkv, qkv, qkv, do, tot, first, *cargo)


POOL_TILE = 512


def _by_group(lane, values):
    return jnp.where(lane < 64, values[0], jnp.where(lane < 128, values[1], jnp.where(lane < 192, values[2], values[3])))


def _pool_inv_count(first_row, n_rows):
    t = first_row + lax.broadcasted_iota(jnp.int32, (n_rows, POOL_WIDTH), 0)
    lane = lax.broadcasted_iota(jnp.int32, (n_rows, POOL_WIDTH), 1)
    window = _by_group(lane, POOL_WINDOWS)
    return 1.0 / jnp.clip(t + 1, 1, window).astype(F32), lane


def _pooled(ext, first_row, R):
    n = R + POOL_HALO
    s2 = ext + pltpu.roll(ext, 1, 0)
    s4 = s2 + pltpu.roll(s2, 2, 0)
    s8 = s4 + pltpu.roll(s4, 4, 0)
    s16 = s8 + pltpu.roll(s8, 8, 0)
    inv, lane = _pool_inv_count(first_row - POOL_HALO, n)
    pooled = _by_group(lane, (s2, s4, s8, s16)) * inv - ext
    return pooled[POOL_HALO:, :]


def _pool_specs(S, R, col):
    per = R // POOL_HALO
    tile = pl.BlockSpec((R, POOL_WIDTH), lambda i: (i, col))
    prev = pl.BlockSpec((POOL_HALO, POOL_WIDTH), lambda i: (jnp.maximum(i * per - 1, 0), col))
    return tile, prev


def _pool_fwd(rest, w_bd, scale, *, name):
    S = rest.shape[0]
    R = min(POOL_TILE, S)

    def body(p_ref, prev_ref, w_ref, s_ref, o_ref, ext_ref):
        i = pl.program_id(0)
        ext_ref[:POOL_HALO, :] = jnp.where(i > 0, prev_ref[...], 0.0)
        ext_ref[POOL_HALO:, :] = p_ref[...]
        pooled = _pooled(ext_ref[...], i * R, R)
        mixed = jnp.dot(pooled.astype(BF16), w_ref[...], preferred_element_type=F32)
        o_ref[...] = (mixed * s_ref[...]).astype(o_ref.dtype)

    tile, prev = _pool_specs(S, R, 0)
    return pl.pallas_call(
        body, name=name, grid=(S // R,),
        in_specs=[tile, prev, pl.BlockSpec((POOL_WIDTH, POOL_WIDTH), lambda i: (0, 0)), _vec_spec(POOL_WIDTH)],
        out_specs=_row_spec(R, POOL_WIDTH), out_shape=jax.ShapeDtypeStruct((S, POOL_WIDTH), BF16),
        scratch_shapes=[pltpu.VMEM((R + POOL_HALO, POOL_WIDTH), F32)], compiler_params=_params(1))(rest, rest, w_bd, scale)


def _pool_bwd(rest, do, w_bd, scale, *, name):
    S = rest.shape[0]
    R = min(POOL_TILE, S)
    n_tiles = S // R
    per = R // POOL_HALO
    n = R + POOL_HALO

    def body(p_ref, prev_ref, do_ref, nxt_ref, w_ref, s_ref, dp_ref, dw_ref, ds_ref, ext_ref, dext_ref):
        i = pl.program_id(0)
        ext_ref[:POOL_HALO, :] = jnp.where(i > 0, prev_ref[...], 0.0)
        ext_ref[POOL_HALO:, :] = p_ref[...]
        pooled = _pooled(ext_ref[...], i * R, R).astype(BF16)
        w = w_ref[...]
        mixed = jnp.dot(pooled, w, preferred_element_type=F32)
        do_t = do_ref[...]
        _accumulate(ds_ref, jnp.sum(do_t * mixed, axis=0, keepdims=True))
        dext_ref[:R, :] = do_t
        dext_ref[R:, :] = jnp.where(i < n_tiles - 1, nxt_ref[...], 0.0)
        dmixed = (dext_ref[...] * s_ref[...]).astype(BF16)
        dpooled = lax.dot_general(dmixed, w, (((1,), (1,)), ((), ())), preferred_element_type=F32)
        _accumulate(dw_ref, lax.dot_general(pooled, dmixed[:R, :], (((0,), (0,)), ((), ())), preferred_element_type=F32))
        inv, lane = _pool_inv_count(i * R, n)
        u = dpooled * inv
        f2 = u + pltpu.roll(u, n - 1, 0)
        f4 = f2 + pltpu.roll(f2, n - 2, 0)
        f8 = f4 + pltpu.roll(f4, n - 4, 0)
        f16 = f8 + pltpu.roll(f8, n - 8, 0)
        dp = _by_group(lane, (f2, f4, f8, f16)) - dpooled
        dp_ref[...] = dp[:R, :].astype(dp_ref.dtype)

    tile, prev = _pool_specs(S, R, 0)
    nxt = pl.BlockSpec((POOL_HALO, POOL_WIDTH), lambda i: (jnp.minimum((i + 1) * per, S // POOL_HALO - 1), 0))
    full = pl.BlockSpec((POOL_WIDTH, POOL_WIDTH), lambda i: (0, 0))
    return pl.pallas_call(
        body, name=name, grid=(n_tiles,),
        in_specs=[tile, prev, _row_spec(R, POOL_WIDTH), nxt, full, _vec_spec(POOL_WIDTH)],
        out_specs=[_row_spec(R, POOL_WIDTH), full, _vec_spec(POOL_WIDTH)],
        out_shape=[jax.ShapeDtypeStruct((S, POOL_WIDTH), BF16), jax.ShapeDtypeStruct((POOL_WIDTH, POOL_WIDTH), F32),
                   jax.ShapeDtypeStruct((1, POOL_WIDTH), F32)],
        scratch_shapes=[pltpu.VMEM((n, POOL_WIDTH), F32), pltpu.VMEM((n, POOL_WIDTH), F32)],
        compiler_params=_params(1))(rest, rest, do, do, w_bd, scale)


GM_TILE = 512
GELU_C = math.sqrt(2.0 / math.pi)
GELU_A = 0.044715


def _gelu(x):
    return 0.5 * x * (1.0 + jnp.tanh(GELU_C * (x + GELU_A * x * x * x)))


def _gelu_and_grad(x):
    t = jnp.tanh(GELU_C * (x + GELU_A * x * x * x))
    y = 0.5 * x * (1.0 + t)
    dy = 0.5 * (1.0 + t) + 0.5 * x * (1.0 - t * t) * (GELU_C * (1.0 + 3.0 * GELU_A * x * x))
    return y, dy


def _group_lane_masks():
    lane = lax.broadcasted_iota(jnp.int32, (1, GM_WIDTH), 1)
    return [(lane >= g * GM_GROUP_DIM) & (lane < (g + 1) * GM_GROUP_DIM) for g in range(GM_GROUPS)]


def _stack_groups(x, masks):
    return jnp.concatenate([jnp.where(m, x, jnp.zeros_like(x)) for m in masks], axis=0)


def _gm_mixed(vn, ws_cat, bias, masks, R):
    chunks = []
    for c in range(R // GM_CHUNK):
        vc = vn[c * GM_CHUNK:(c + 1) * GM_CHUNK, :]
        chunks.append(jnp.dot(ws_cat, _stack_groups(vc, masks), preferred_element_type=F32) + bias)
    return jnp.concatenate(chunks, axis=0)


def _gm_specs(S, R):
    u = pl.BlockSpec((R, GM_WIDTH), lambda i: (i, 1))
    v = pl.BlockSpec((R, GM_WIDTH), lambda i: (i, 2))
    ws = pl.BlockSpec((GM_CHUNK, GM_GROUPS * GM_CHUNK), lambda i: (0, 0))
    bias = pl.BlockSpec((GM_CHUNK, GM_WIDTH), lambda i: (0, 0))
    return u, v, ws, bias


def _gm_fwd(rest, gain, ws_cat, bias, *, name):
    S = rest.shape[0]
    R = min(GM_TILE, S)

    def body(u_ref, v_ref, g_ref, ws_ref, b_ref, o_ref):
        gv = _gelu(v_ref[...])
        vn = (gv * _rstd(gv) * g_ref[...]).astype(BF16)
        mixed = _gm_mixed(vn, ws_ref[...], b_ref[...], _group_lane_masks(), R)
        o_ref[...] = (_gelu(u_ref[...]) * mixed).astype(o_ref.dtype)

    u_spec, v_spec, ws_spec, bias_spec = _gm_specs(S, R)
    return pl.pallas_call(
        body, name=name, grid=(S // R,), in_specs=[u_spec, v_spec, _vec_spec(GM_WIDTH), ws_spec, bias_spec],
        out_specs=_row_spec(R, GM_WIDTH), out_shape=jax.ShapeDtypeStruct((S, GM_WIDTH), BF16),
        compiler_params=_params(1))(rest, rest, gain, ws_cat, bias)


def _gm_bwd(rest, do, gain, ws_cat, wst_cat, bias, *, name):
    S = rest.shape[0]
    R = min(GM_TILE, S)

    def body(u_ref, v_ref, do_ref, g_ref, ws_ref, wst_ref, b_ref, du_ref, dv_ref, dg_ref, dws_ref, db_ref):
        masks = _group_lane_masks()
        gain_v = g_ref[...]
        gu, dgu = _gelu_and_grad(u_ref[...])
        gv, dgv = _gelu_and_grad(v_ref[...])
        r = _rstd(gv)
        vn = (gv * r * gain_v).astype(BF16)
        mixed = _gm_mixed(vn, ws_ref[...], b_ref[...], masks, R)
        do_t = do_ref[...]
        du_ref[...] = (do_t * mixed * dgu).astype(du_ref.dtype)
        dmix = do_t * gu
        dmix_b = dmix.astype(BF16)
        wst = wst_ref[...]
        dvn_chunks, db, dws = [], None, [None] * GM_GROUPS
        for c in range(R // GM_CHUNK):
            rows = slice(c * GM_CHUNK, (c + 1) * GM_CHUNK)
            dc, dcb, vc = dmix[rows, :], dmix_b[rows, :], vn[rows, :]
            db = dc if db is None else db + dc
            dvn_chunks.append(jnp.dot(wst, _stack_groups(dcb, masks), preferred_element_type=F32))
            for g, m in enumerate(masks):
                part = lax.dot_general(jnp.where(m, dcb, jnp.zeros_like(dcb)), vc, (((1,), (1,)), ((), ())),
                                       preferred_element_type=F32)
                dws[g] = part if dws[g] is None else dws[g] + part
        dvn = jnp.concatenate(dvn_chunks, axis=0)
        lane = lax.broadcasted_iota(jnp.int32, (1, LANES), 1)
        db_groups = jnp.zeros((GM_CHUNK, LANES), F32)
        for g, m in enumerate(masks):
            total = jnp.sum(jnp.where(m, db, 0.0), axis=1, keepdims=True)
            db_groups = db_groups + jnp.where(lane == g, total, 0.0)
        _accumulate(db_ref, db_groups)
        i = pl.program_id(0)
        for g in range(GM_GROUPS):
            @pl.when(i == 0)
            def _(g=g):
                dws_ref[g] = dws[g]

            @pl.when(i > 0)
            def _(g=g):
                dws_ref[g] += dws[g]
        _accumulate(dg_ref, jnp.sum(dvn * gv * r, axis=0, keepdims=True))
        gd = gain_v * dvn
        dgv_in = r * gd - gv * (r * r * r) * jnp.mean(gv * gd, axis=-1, keepdims=True)
        dv_ref[...] = (dgv_in * dgv).astype(dv_ref.dtype)

    u_spec, v_spec, ws_spec, bias_spec = _gm_specs(S, R)
    row, vec = _row_spec(R, GM_WIDTH), _vec_spec(GM_WIDTH)
    dws_spec = pl.BlockSpec((GM_GROUPS, GM_CHUNK, GM_CHUNK), lambda i: (0, 0, 0))
    return pl.pallas_call(
        body, name=name, grid=(S // R,), in_specs=[u_spec, v_spec, row, vec, ws_spec, ws_spec, bias_spec],
        out_specs=[row, row, vec, dws_spec, pl.BlockSpec((GM_CHUNK, LANES), lambda i: (0, 0))],
        out_shape=[jax.ShapeDtypeStruct((S, GM_WIDTH), BF16)] * 2
        + [jax.ShapeDtypeStruct((1, GM_WIDTH), F32), jax.ShapeDtypeStruct((GM_GROUPS, GM_CHUNK, GM_CHUNK), F32),
           jax.ShapeDtypeStruct((GM_CHUNK, LANES), F32)],
        compiler_params=_params(1))(rest, rest, do, gain, ws_cat, wst_cat, bias)


GATE_ROWS = 1024
GATE_COLS = 512
GATE_BLOCKS = D_MODEL // GATE_COLS


def _gate_spec(tr, k):
    return pl.BlockSpec((tr, GATE_COLS), lambda i, j: (i, GATE_BLOCKS * k + j))


def _merge_fwd(gates, branches, *, name):
    S = gates.shape[0]
    tr = min(GATE_ROWS, S)

    def body(g0, g1, g2, b0, b1, b2, o_ref):
        acc = None
        for g_ref, b_ref in ((g0, b0), (g1, b1), (g2, b2)):
            term = jax.nn.sigmoid(g_ref[...].astype(F32)) * b_ref[...].astype(F32)
            acc = term if acc is None else acc + term
        o_ref[...] = acc.astype(o_ref.dtype)

    tile = pl.BlockSpec((tr, GATE_COLS), lambda i, j: (i, j))
    return pl.pallas_call(
        body, name=name, grid=(S // tr, GATE_BLOCKS),
        in_specs=[_gate_spec(tr, k) for k in range(N_BRANCH)] + [tile] * N_BRANCH, out_specs=tile,
        out_shape=jax.ShapeDtypeStruct((S, D_MODEL), BF16), compiler_params=_params(2))(gates, gates, gates, *branches)


def _merge_bwd(gates, branches, dmerged, *, name):
    S = gates.shape[0]
    tr = min(GATE_ROWS, S)

    def body(g0, g1, g2, b0, b1, b2, dm_ref, dg0, dg1, dg2, db0, db1, db2):
        dm = dm_ref[...].astype(F32)
        for g_ref, b_ref, dg_ref, db_ref in ((g0, b0, dg0, db0), (g1, b1, dg1, db1), (g2, b2, dg2, db2)):
            s = jax.nn.sigmoid(g_ref[...].astype(F32))
            db_ref[...] = (dm * s).astype(db_ref.dtype)
            dg_ref[...] = (dm * b_ref[...].astype(F32) * s * (1.0 - s)).astype(dg_ref.dtype)

    tile = pl.BlockSpec((tr, GATE_COLS), lambda i, j: (i, j))
    return pl.pallas_call(
        body, name=name, grid=(S // tr, GATE_BLOCKS),
        in_specs=[_gate_spec(tr, k) for k in range(N_BRANCH)] + [tile] * (N_BRANCH + 1), out_specs=[tile] * (2 * N_BRANCH),
        out_shape=[jax.ShapeDtypeStruct((S, D_MODEL), BF16)] * (2 * N_BRANCH),
        compiler_params=_params(2))(gates, gates, gates, *branches, dmerged)


TILE_BYTES = 24 * 1024 * 1024


BF16_ROWS = 16


def _tile_rows(rows, cols, n_arrays):
    padded = -(-cols // LANES) * LANES
    cap = max(BF16_ROWS, TILE_BYTES // (2 * n_arrays * padded * 4))
    best = None
    for tr in range(BF16_ROWS, min(rows, cap) + 1, BF16_ROWS):
        if rows % tr == 0:
            best = tr
    assert best is not None, (rows, cols)
    return best


def _sum_slots(stack, *, name):
    n, R, C = stack.shape
    tr = _tile_rows(R, C, n + 1)

    def body(s_ref, o_ref):
        acc = s_ref[0].astype(F32)
        for k in range(1, n):
            acc = acc + s_ref[k].astype(F32)
        o_ref[...] = acc

    return pl.pallas_call(
        body, name=name, grid=(R // tr,), in_specs=[pl.BlockSpec((n, tr, C), lambda i: (0, i, 0))],
        out_specs=_row_spec(tr, C), out_shape=jax.ShapeDtypeStruct((R, C), F32), compiler_params=_params(1))(stack)


def _add_own_half(parts, received, core, *, name):
    n, R, C = parts.shape
    half = R // 2
    tr = _tile_rows(half, C, 3)
    steps = half // tr

    def body(core_ref, own_ref, got_ref, o_ref):
        o_ref[...] = (own_ref[...] + got_ref[...]).astype(o_ref.dtype)

    tile = pl.BlockSpec((None, tr, C), lambda d, i, core_ref: (d, i, 0))
    own = pl.BlockSpec((None, tr, C), lambda d, i, core_ref: (d, core_ref[0] * steps + i, 0))
    return pl.pallas_call(
        body, name=name, out_shape=jax.ShapeDtypeStruct((n, half, C), BF16),
        grid_spec=pltpu.PrefetchScalarGridSpec(num_scalar_prefetch=1, grid=(n, steps), in_specs=[own, tile], out_specs=tile),
        compiler_params=_params(2))(core, parts, received)


def _adamw_math(w, m, v, g):
    m_new = ADAM_B1 * m + (1.0 - ADAM_B1) * g
    v_new = ADAM_B2 * v + (1.0 - ADAM_B2) * jnp.square(g)
    m_hat = m_new / (1.0 - ADAM_B1 ** ADAM_STEP)
    v_hat = v_new / (1.0 - ADAM_B2 ** ADAM_STEP)
    return -ADAM_LR * (m_hat / (jnp.sqrt(v_hat) + ADAM_EPS) + ADAM_WD * w), m_new, v_new


def _adamw_halves(w, m, v, mine, theirs, core, *, name):
    L, r, C = w.shape
    tr = _tile_rows(r // 2, C, 9)
    steps = r // 2 // tr

    def body(core_ref, w_ref, m_ref, v_ref, mine_ref, theirs_ref, go_ref, d_ref, mo_ref, vo_ref):
        in_my_half = pl.program_id(1) // steps == core_ref[0]
        g = jnp.where(in_my_half, mine_ref[...], theirs_ref[...])
        go_ref[...] = g
        d_ref[...], mo_ref[...], vo_ref[...] = _adamw_math(w_ref[...], m_ref[...], v_ref[...], g)

    row = pl.BlockSpec((None, tr, C), lambda l, i, core_ref: (l, i, 0))
    half = pl.BlockSpec((None, tr, C), lambda l, i, core_ref: (l, i % steps, 0))
    return pl.pallas_call(
        body, name=name, out_shape=[jax.ShapeDtypeStruct((L, r, C), F32)] * 4,
        grid_spec=pltpu.PrefetchScalarGridSpec(
            num_scalar_prefetch=1, grid=(L, r // tr), in_specs=[row, row, row, half, half], out_specs=[row] * 4),
        compiler_params=_params(2))(core, w, m, v, mine, theirs)


HBM_SPEC = pl.BlockSpec(memory_space=pl.ANY)


def _position():
    return lax.axis_index("x"), lax.axis_index("y"), lax.axis_index("c")


def _other_chips(x, y):
    return [(1 - x, y), (x, 1 - y), (1 - x, 1 - y)]


def _chip_exchange(arrays, *, scatter, name):
    n = len(arrays)

    def body(*refs):
        copies = _chip_copies(refs[:n], refs[n:2 * n], *refs[2 * n:], scatter=scatter)
        for cp in copies:
            cp.start()
        for cp in copies:
            cp.wait()

    return pl.pallas_call(
        body, name=name, in_specs=[HBM_SPEC] * n, out_specs=[HBM_SPEC] * n, out_shape=_chip_exchange_shapes(arrays, scatter),
        scratch_shapes=_chip_exchange_semaphores(n))(*arrays)


def _chip_exchange_shapes(arrays, scatter):
    return [jax.ShapeDtypeStruct(a.shape if scatter else (N_CHIPS, 2) + a.shape, a.dtype) for a in arrays]


def _chip_exchange_semaphores(n):
    return [pltpu.SemaphoreType.DMA((3 * n,)), pltpu.SemaphoreType.DMA((3 * n,)), pltpu.SemaphoreType.DMA((n,))]


def _chip_copies(ins, outs, send_sems, recv_sems, local_sems, *, scatter):
    x, y, c = _position()
    me = 2 * x + y
    copies = []
    for a in range(len(ins)):
        own = ins[a].at[me] if scatter else ins[a]
        slot = outs[a].at[me] if scatter else outs[a].at[me, c]
        copies.append(pltpu.make_async_copy(own, slot, local_sems.at[a]))
        for k, (px, py) in enumerate(_other_chips(x, y)):
            src = ins[a].at[2 * px + py] if scatter else ins[a]
            copies.append(pltpu.make_async_remote_copy(
                src_ref=src, dst_ref=slot, send_sem=send_sems.at[3 * a + k],
                recv_sem=recv_sems.at[3 * a + k], device_id=(px, py, c), device_id_type=MESH))
    return copies


def _sibling_fill(arrays, *, name):
    n = len(arrays)

    def body(*refs):
        ins, outs = refs[:n], refs[n:2 * n]
        send_sems, recv_sems = refs[2 * n:]
        x, y, c = _position()
        copies = []
        for a in range(n):
            cp = pltpu.make_async_remote_copy(
                src_ref=ins[a].at[:, c], dst_ref=outs[a].at[:, c], send_sem=send_sems.at[a], recv_sem=recv_sems.at[a],
                device_id=(x, y, 1 - c), device_id_type=MESH)
            cp.start()
            copies.append(cp)
        for cp in copies:
            cp.wait()

    return pl.pallas_call(
        body, name=name, in_specs=[HBM_SPEC] * n, out_specs=[HBM_SPEC] * n,
        out_shape=[jax.ShapeDtypeStruct(a.shape, a.dtype) for a in arrays],
        input_output_aliases={a: a for a in range(n)},
        scratch_shapes=[pltpu.SemaphoreType.DMA((n,)), pltpu.SemaphoreType.DMA((n,))],
    )(*arrays)


def _sibling_swap(arrays, *, name):
    n = len(arrays)

    def body(*refs):
        ins, outs = refs[:n], refs[n:2 * n]
        send_sems, recv_sems = refs[2 * n:]
        x, y, c = _position()
        copies = []
        for a in range(n):
            cp = pltpu.make_async_remote_copy(
                src_ref=ins[a], dst_ref=outs[a], send_sem=send_sems.at[a], recv_sem=recv_sems.at[a],
                device_id=(x, y, 1 - c), device_id_type=MESH)
            cp.start()
            copies.append(cp)
        for cp in copies:
            cp.wait()

    return pl.pallas_call(
        body, name=name, in_specs=[HBM_SPEC] * n, out_specs=[HBM_SPEC] * n,
        out_shape=[jax.ShapeDtypeStruct(a.shape, a.dtype) for a in arrays],
        scratch_shapes=[pltpu.SemaphoreType.DMA((n,)), pltpu.SemaphoreType.DMA((n,))],
    )(*arrays)


def _sibling_other_half(arrays, *, name):
    n = len(arrays)

    def body(*refs):
        ins, outs = refs[:n], refs[n:2 * n]
        send_sems, recv_sems = refs[2 * n:]
        x, y, c = _position()
        copies = []
        for a in range(n):
            half = ins[a].shape[1] // 2
            theirs = ins[a].at[:, pl.ds(pl.multiple_of((1 - c) * half, BF16_ROWS), half), :]
            cp = pltpu.make_async_remote_copy(
                src_ref=theirs, dst_ref=outs[a], send_sem=send_sems.at[a], recv_sem=recv_sems.at[a],
                device_id=(x, y, 1 - c), device_id_type=MESH)
            cp.start()
            copies.append(cp)
        for cp in copies:
            cp.wait()

    return pl.pallas_call(
        body, name=name, in_specs=[HBM_SPEC] * n, out_specs=[HBM_SPEC] * n,
        out_shape=[jax.ShapeDtypeStruct((a.shape[0], a.shape[1] // 2, a.shape[2]), a.dtype) for a in arrays],
        scratch_shapes=[pltpu.SemaphoreType.DMA((n,)), pltpu.SemaphoreType.DMA((n,))],
    )(*arrays)


def _small_update(grads, ws, ms, vs, *, name):
    n, L = len(ws), ws[0].shape[0]
    pieces = [g for per_layer in grads for g in per_layer]
    np_ = len(pieces)

    def body(*refs):
        g_in, refs = refs[:np_], refs[np_:]
        w_in, m_in, v_in, g_out, d_out, m_out, v_out = (refs[k * n:(k + 1) * n] for k in range(7))
        from_sibling, chip_sums = refs[7 * n:7 * n + np_], refs[7 * n + np_:7 * n + 2 * np_]
        sibling_send, sibling_recv, chip_send, chip_recv = refs[7 * n + 2 * np_:]
        x, y, c = _position()
        me = 2 * x + y
        swaps = [pltpu.make_async_remote_copy(
            src_ref=g_in[p], dst_ref=from_sibling[p], send_sem=sibling_send.at[p], recv_sem=sibling_recv.at[p],
            device_id=(x, y, 1 - c), device_id_type=MESH) for p in range(np_)]
        for cp in swaps:
            cp.start()
        for cp in swaps:
            cp.wait()
        for p in range(np_):
            chip_sums[p][me] = g_in[p][...] + from_sibling[p][...]
        sends = [pltpu.make_async_remote_copy(
            src_ref=chip_sums[p].at[me], dst_ref=chip_sums[p].at[me], send_sem=chip_send.at[3 * p + k],
            recv_sem=chip_recv.at[3 * p + k], device_id=(px, py, c), device_id_type=MESH)
            for p in range(np_) for k, (px, py) in enumerate(_other_chips(x, y))]
        for cp in sends:
            cp.start()
        for cp in sends:
            cp.wait()
        for a in range(n):
            for l in range(L):
                sums = chip_sums[a * L + l]
                g = sums[0]
                for s in range(1, N_CHIPS):
                    g = g + sums[s]
                g_out[a][l] = g
                d_out[a][l], m_out[a][l], v_out[a][l] = _adamw_math(w_in[a][l], m_in[a][l], v_in[a][l], g)

    vmem = pl.BlockSpec(memory_space=pltpu.VMEM)
    shapes = [jax.ShapeDtypeStruct(w.shape, F32) for w in ws]
    outs = pl.pallas_call(
        body, name=name, in_specs=[vmem] * (np_ + 3 * n), out_specs=[vmem] * (4 * n), out_shape=shapes * 4,
        scratch_shapes=[pltpu.VMEM(g.shape, F32) for g in pieces] + [pltpu.VMEM((N_CHIPS,) + g.shape, F32) for g in pieces]
        + [pltpu.SemaphoreType.DMA((np_,)), pltpu.SemaphoreType.DMA((np_,)),
           pltpu.SemaphoreType.DMA((3 * np_,)), pltpu.SemaphoreType.DMA((3 * np_,))],
        compiler_params=pltpu.CompilerParams(vmem_limit_bytes=VMEM_LIMIT),
    )(*pieces, *ws, *ms, *vs)
    return outs[:n], outs[n:2 * n], outs[2 * n:3 * n], outs[3 * n:]


def _relu2(p):
    return p, jnp.square(jnp.maximum(p, 0.0))


def _relu2_grad(p, a):
    return (p * (2.0 * jnp.maximum(a.astype(F32), 0.0)),)


def _mixer_constants(w_pool, w_spatial, b_spatial):
    eye = jnp.eye(len(POOL_WINDOWS), dtype=F32)
    w_bd = (eye[:, None, :, None] * w_pool[:, :, None, :]).reshape(POOL_WIDTH, POOL_WIDTH).astype(BF16)
    causal = jnp.tril(jnp.ones((GM_CHUNK, GM_CHUNK), dtype=bool))
    ws = jnp.where(causal[None], w_spatial, 0.0).astype(BF16)
    ws_cat = ws.transpose(1, 0, 2).reshape(GM_CHUNK, GM_GROUPS * GM_CHUNK)
    wst_cat = ws.transpose(2, 0, 1).reshape(GM_CHUNK, GM_GROUPS * GM_CHUNK)
    bias = jnp.repeat(b_spatial.T, GM_GROUP_DIM, axis=1)
    return w_bd, ws_cat, wst_cat, bias


def _local_step(x, target, half_shards, small, core):
    L = small["g_mix_pre"].shape[0]
    vec = lambda name, l: small[name][l][None, :]
    consts = [_mixer_constants(small["w_pool"][l], small["w_spatial"][l], small["b_spatial"][l]) for l in range(L)]
    core_index = core.astype(jnp.int32).reshape(1)
    first_used, used_later = BIG_WEIGHTS[:1], BIG_WEIGHTS[1:]
    weights = [{} for _ in range(L)]

    def finish_gather(wanted, gathered):
        for (n, ll), both in zip(wanted, _sibling_fill(gathered, name="swap_weight_halves")):
            weights[ll][n] = _full_weight(n, both)

    def core_sums(names, l):
        parts = [_parts_by_chip(n, gb[n][l]) for n in names]
        from_sibling = _sibling_other_half(parts, name="swap_grad_halves")
        return [(n, l, _add_own_half(p, f, core_index, name="sum_cores")) for n, p, f in zip(names, parts, from_sibling)]

    wanted = [(n, 0) for n in first_used]
    finish_gather(wanted, _chip_exchange([half_shards[n][ll] for n, ll in wanted], scatter=False, name="gather_weights"))
    saved = []
    h = _rms_fwd(x, vec("g_mix_pre", 0), name="rms_in")
    for l in range(L):
        w_bd, ws_cat, wst_cat, bias = consts[l]
        proj = lambda n, off, dtype, name: _matmul(h, weights[l]["w_in"], tb=True, n=n, bn=PROJ_BLOCK,
                                                   b_col_off=off // PROJ_BLOCK, out_dtypes=(dtype,), name=name)
        qkv = proj(QKV_WIDTH, 0, BF16, "proj_qkv")
        rest = proj(MIX_WIDTH, QKV_WIDTH, F32, "proj_mix")
        gates = proj(GATE_WIDTH, QKV_WIDTH + MIX_WIDTH, BF16, "proj_gates")
        wanted = [(n, l) for n in used_later] + [(n, l + 1) for n in first_used if l + 1 < L]
        o_sb, tot, first, *gathered = _sba_fwd(qkv, cargo=[half_shards[n][ll] for n, ll in wanted], name="sba_fwd_gather")
        finish_gather(wanted, gathered)
        o_pool = _pool_fwd(rest, w_bd, vec("pool_scale", l), name="pool_fwd")
        o_gm = _gm_fwd(rest, vec("gm_gain", l), ws_cat, bias, name="gm_fwd")
        branches = (_matmul(o_sb, weights[l]["w_br_sb"], out_dtypes=(BF16,), name="br_sb"),
                    _matmul(o_pool, weights[l]["w_br_pool"], out_dtypes=(BF16,), name="br_pool"),
                    _matmul(o_gm, weights[l]["w_br_gm"], out_dtypes=(BF16,), name="br_gm"))
        merged = _merge_fwd(gates, branches, name="merge_fwd")
        y = _matmul(merged, weights[l]["w_out"], name="out_proj")
        x1, h2 = _resid_rms(x, y, vec("g_mix_post", l), vec("g_ff_pre", l), name="resid_mix")
        a, r = _matmul(h2, weights[l]["w_ff_in"], out_dtypes=(BF16, BF16), epilogue=_relu2, name="ff_in")
        ff = _matmul(r, weights[l]["w_ff_out"], name="ff_out")
        g_next = vec("g_mix_pre", l + 1) if l + 1 < L else None
        x2, h_next = _resid_rms(x1, ff, vec("g_ff_post", l), g_next, name="resid_ff" if l + 1 < L else "resid_last")
        saved.append(dict(x=x, h=h, qkv=qkv, rest=rest, gates=gates, o_sb=o_sb, tot=tot, first=first, o_pool=o_pool,
                          o_gm=o_gm, branches=branches, merged=merged, y=y, x1=x1, h2=h2, a=a, r=r, ff=ff))
        x, h = x2, h_next

    dx2, loss = _loss_head(x, target, name="loss_head")
    gb = {k: [None] * L for k in ("w_in", "w_br_sb", "w_br_pool", "w_br_gm", "w_out", "w_ff_in", "w_ff_out")}
    gs = {k: [None] * L for k in ("w_pool", "pool_scale", "gm_gain", "w_spatial", "b_spatial", "g_mix_pre",
                                  "g_mix_post", "g_ff_pre", "g_ff_post")}
    d_ff, gs["g_ff_post"][L - 1] = _rms_bwd(saved[-1]["ff"], vec("g_ff_post", L - 1), dx2, name="rms_bwd_last")
    received = [{} for _ in range(L)]
    waiting = []
    for l in reversed(range(L)):
        s = saved[l]
        w_bd, ws_cat, wst_cat, bias = consts[l]
        da = _matmul(d_ff, weights[l]["w_ff_out"], tb=True, out_dtypes=(BF16,), extras=(s["a"],), epilogue=_relu2_grad,
                     name="ff_out_dx")
        gb["w_ff_out"][l] = _matmul(s["r"], d_ff, ta=True, name="ff_out_dw")
        dh2 = _matmul(da, weights[l]["w_ff_in"], tb=True, name="ff_in_dx")
        gb["w_ff_in"][l] = _matmul(s["h2"], da, ta=True, name="ff_in_dw")
        dx1, gs["g_ff_pre"][l], dy, gs["g_mix_post"][l] = _rms_bwd_chain(
            s["x1"], vec("g_ff_pre", l), dh2, dx2, s["y"], vec("g_mix_post", l), name="rms_bwd_mid")
        dmerged = _matmul(dy, weights[l]["w_out"], tb=True, out_dtypes=(BF16,), name="out_proj_dx")
        gb["w_out"][l] = _matmul(s["merged"], dy, ta=True, name="out_proj_dw")
        dg0, dg1, dg2, db_sb, db_pool, db_gm = _merge_bwd(s["gates"], s["branches"], dmerged, name="merge_bwd")
        do_sb = _matmul(db_sb, weights[l]["w_br_sb"], tb=True, out_dtypes=(BF16,), name="br_sb_dx")
        gb["w_br_sb"][l] = _matmul(s["o_sb"], db_sb, ta=True, name="br_sb_dw")
        do_pool = _matmul(db_pool, weights[l]["w_br_pool"], tb=True, name="br_pool_dx")
        gb["w_br_pool"][l] = _matmul(s["o_pool"], db_pool, ta=True, name="br_pool_dw")
        do_gm = _matmul(db_gm, weights[l]["w_br_gm"], tb=True, name="br_gm_dx")
        gb["w_br_gm"][l] = _matmul(s["o_gm"], db_gm, ta=True, name="br_gm_dw")
        waiting += core_sums(used_later, l)
        dq, dk, dv, *arrived = _sba_bwd(s["qkv"], do_sb, s["tot"], s["first"], cargo=[c for _, _, c in waiting],
                                        name="sba_bwd_exchange")
        for (n, ll, _), got in zip(waiting, arrived):
            received[ll][n] = got
        dp, dw_bd, gs["pool_scale"][l] = _pool_bwd(s["rest"], do_pool, w_bd, vec("pool_scale", l), name="pool_bwd")
        du, dgv, gs["gm_gain"][l], dws, db = _gm_bwd(s["rest"], do_gm, vec("gm_gain", l), ws_cat, wst_cat, bias,
                                                      name="gm_bwd")
        gs["w_pool"][l] = jnp.stack([dw_bd[g * 64:(g + 1) * 64, g * 64:(g + 1) * 64] for g in range(len(POOL_WINDOWS))])
        gs["w_spatial"][l] = jnp.where(jnp.tril(jnp.ones((GM_CHUNK, GM_CHUNK), dtype=bool))[None], dws, 0.0)
        gs["b_spatial"][l] = db[:, :GM_GROUPS].T
        dproj = jnp.concatenate([dq, dk, dv, dp, du, dgv, dg0, dg1, dg2], axis=1)
        dh = _matmul(dproj, weights[l]["w_in"], bk=D_IN // 3, name="proj_dx")
        gb["w_in"][l] = _matmul(dproj, s["h"], ta=True, bm=PROJ_BLOCK, name="proj_dw")
        waiting = core_sums(first_used, l)
        if l == 0:
            for (n, ll, _), got in zip(waiting, _chip_exchange([c for _, _, c in waiting], scatter=True, name="exchange_grads")):
                received[ll][n] = got
        if l > 0:
            dx2, gs["g_mix_pre"][l], d_ff, gs["g_ff_post"][l - 1] = _rms_bwd_chain(
                s["x"], vec("g_mix_pre", l), dh, dx1, saved[l - 1]["ff"], vec("g_ff_post", l - 1), name="rms_bwd_mid")
        else:
            dx2, gs["g_mix_pre"][l], _, _ = _rms_bwd_chain(s["x"], vec("g_mix_pre", l), dh, dx1, None, None,
                                                           name="rms_bwd_first")
    return loss, dx2, received, gs


TRANSPOSED = ("w_in",)
COLUMN_SHARDED = ("w_br_sb", "w_br_pool", "w_br_gm", "w_ff_in")
ROW_SHARDED = ("w_in", "w_out", "w_ff_out")
BIG_WEIGHTS = ("w_in", "w_br_sb", "w_br_pool", "w_br_gm", "w_ff_in", "w_out", "w_ff_out")
SMALL_WEIGHTS = ("w_pool", "pool_scale", "gm_gain", "w_spatial", "b_spatial", "g_mix_pre", "g_mix_post", "g_ff_pre",
                 "g_ff_post")
WEIGHT_ORDER = ("w_in", "w_pool", "pool_scale", "gm_gain", "w_spatial", "b_spatial", "w_br_sb", "w_br_pool", "w_br_gm",
                "w_out", "g_mix_pre", "g_mix_post", "g_ff_pre", "g_ff_post", "w_ff_in", "w_ff_out")


def _full_weight(name, g):
    half, cols = g.shape[2], g.shape[3]
    if name in COLUMN_SHARDED:
        return g.transpose(1, 2, 0, 3).reshape(2 * half, N_CHIPS * cols)
    return g.reshape(N_CHIPS * 2 * half, cols)


def _parts_by_chip(name, grad):
    if name in COLUMN_SHARDED:
        r, c = grad.shape[0], grad.shape[1] // N_CHIPS
        return grad.reshape(r, N_CHIPS, c).transpose(1, 0, 2)
    return grad.reshape(N_CHIPS, grad.shape[0] // N_CHIPS, grad.shape[1])


def kernel(x, w_in, w_pool, pool_scale, gm_gain, w_spatial, b_spatial, w_br_sb, w_br_pool, w_br_gm, w_out, g_mix_pre, g_mix_post, g_ff_pre, g_ff_post, w_ff_in, w_ff_out, loss_target, m_w_in, m_w_pool, m_pool_scale, m_gm_gain, m_w_spatial, m_b_spatial, m_w_br_sb, m_w_br_pool, m_w_br_gm, m_w_out, m_g_mix_pre, m_g_mix_post, m_g_ff_pre, m_g_ff_post, m_w_ff_in, m_w_ff_out, v_w_in, v_w_pool, v_pool_scale, v_gm_gain, v_w_spatial, v_b_spatial, v_w_br_sb, v_w_br_pool, v_w_br_gm, v_w_out, v_g_mix_pre, v_g_mix_post, v_g_ff_pre, v_g_ff_post, v_w_ff_in, v_w_ff_out):
    given = dict(locals())
    w = {n: given[n] for n in WEIGHT_ORDER}
    m = {n: given["m_" + n] for n in WEIGHT_ORDER}
    v = {n: given["v_" + n] for n in WEIGHT_ORDER}
    L = w_in.shape[0]

    core = lax.axis_index("c")

    def my_rows(a):
        half = a.shape[1] // 2
        return lax.dynamic_slice_in_dim(a.astype(BF16), core * half, half, axis=1)

    view = lambda n, a: jnp.swapaxes(a, 1, 2) if n in TRANSPOSED else a
    half_shards = {n: my_rows(view(n, w[n])) for n in BIG_WEIGHTS}
    small = {n: w[n] for n in SMALL_WEIGHTS}
    loss, dx, received, small_grads = _local_step(x[0], loss_target[0], half_shards, small, core)

    core_index = core.astype(jnp.int32).reshape(1)
    reduced = []
    for n in BIG_WEIGHTS:
        from_chips = jnp.concatenate([received[l][n] for l in range(L)], axis=1)
        reduced.append(_sum_slots(from_chips, name="sum_chips"))
    reduced_by_sibling = _sibling_swap(reduced, name="swap_reduced_halves")
    grads, deltas, new_m, new_v = {}, {}, {}, {}
    for n, mine, theirs in zip(BIG_WEIGHTS, reduced, reduced_by_sibling):
        by_layer = lambda a: a.reshape(L, -1, a.shape[-1])
        outs = _adamw_halves(view(n, w[n]), view(n, m[n]), view(n, v[n]), by_layer(mine), by_layer(theirs), core_index,
                             name="adamw_big")
        grads[n], deltas[n], new_m[n], new_v[n] = [view(n, o) for o in outs]

    by_layer = lambda a: a.reshape(L, -1, a.shape[-1])
    flat = lambda a: a.reshape(-1, a.shape[-1])
    outs = _small_update([[flat(g) for g in small_grads[n]] for n in SMALL_WEIGHTS],
                         *[[by_layer(t[n]) for n in SMALL_WEIGHTS] for t in (w, m, v)], name="small_update")
    for store, arrays in zip((grads, deltas, new_m, new_v), outs):
        store.update({n: a.reshape(w[n].shape) for n, a in zip(SMALL_WEIGHTS, arrays)})

    total_loss = lax.psum(loss[0, 0], ("x", "y", "c"))
    return (total_loss, dx[None], *[grads[n] for n in WEIGHT_ORDER], *[deltas[n] for n in WEIGHT_ORDER],
            *[new_m[n] for n in WEIGHT_ORDER], *[new_v[n] for n in WEIGHT_ORDER])
```

```python
import functools
import math

import jax
import jax.numpy as jnp
from jax import lax
from jax.experimental import pallas as pl
from jax.experimental.pallas import tpu as pltpu

F32 = jnp.float32
BF16 = jnp.bfloat16

D_MODEL = 1024
SB_HEADS = 8
SB_HEAD_DIM = 64
SB_WIDTH = SB_HEADS * SB_HEAD_DIM
POOL_WINDOWS = (2, 4, 8, 16)
POOL_GROUP_DIM = 64
POOL_WIDTH = 256
POOL_HALO = 16
GM_GROUPS = 4
GM_GROUP_DIM = 64
GM_WIDTH = 256
GM_CHUNK = 128
N_BRANCH = 3
D_FF = 4 * D_MODEL
RMS_EPS = 1e-6
QKV_WIDTH = 3 * SB_WIDTH
MIX_WIDTH = POOL_WIDTH + 2 * GM_WIDTH
GATE_WIDTH = N_BRANCH * D_MODEL
D_IN = QKV_WIDTH + MIX_WIDTH + GATE_WIDTH
PROJ_BLOCK = 768
LANES = 128
N_CHIPS = 4
N_DEV = 8

ADAM_LR = 0.001
ADAM_B1 = 0.9
ADAM_B2 = 0.999
ADAM_EPS = 1e-08
ADAM_WD = 0.01
ADAM_STEP = 10

VMEM_LIMIT = 56 * 1024 * 1024
MESH = pl.DeviceIdType.MESH


def _params(n_grid):
    return pltpu.CompilerParams(dimension_semantics=("arbitrary",) * n_grid, vmem_limit_bytes=VMEM_LIMIT)


def _bf(x):
    return x if x.dtype == BF16 else x.astype(BF16)


def _matmul(a, b, *, name, ta=False, tb=False, out_dtypes=(F32,), n=None, b_col_off=0, bm=1024, bn=1024, bk=2048,
            extras=(), epilogue=None):
    M, K = (a.shape[1], a.shape[0]) if ta else a.shape
    nb = b.shape[0] if tb else b.shape[1]
    n = nb if n is None else n
    bm, bn, bk = min(bm, M), min(bn, n), min(bk, K)
    assert M % bm == 0 and n % bn == 0 and K % bk == 0, (name, M, n, K, bm, bn, bk)
    assert (b.shape[1] if tb else b.shape[0]) == K, (name, a.shape, b.shape)
    nk = K // bk
    dims = (((0 if ta else 1,), (1 if tb else 0,)), ((), ()))
    n_out = len(out_dtypes)
    direct = nk > 1 and epilogue is None and out_dtypes == (F32,)
    use_acc = nk > 1 and not direct

    def body(*refs):
        a_ref, b_ref = refs[:2]
        extra_refs = refs[2:2 + len(extras)]
        out_refs = refs[2 + len(extras):2 + len(extras) + n_out]
        p = lax.dot_general(_bf(a_ref[...]), _bf(b_ref[...]), dims, preferred_element_type=F32)

        def finish(acc):
            outs = (acc,) if epilogue is None else epilogue(acc, *[r[...] for r in extra_refs])
            for r, o in zip(out_refs, outs):
                r[...] = o.astype(r.dtype)

        if nk == 1:
            finish(p)
            return
        k = pl.program_id(2)
        acc_ref = out_refs[0] if direct else refs[-1]

        @pl.when(k == 0)
        def _():
            acc_ref[...] = p

        @pl.when(k > 0)
        def _():
            acc_ref[...] += p

        if use_acc:
            @pl.when(k == nk - 1)
            def _():
                finish(acc_ref[...])

    a_spec = pl.BlockSpec((bk, bm), lambda i, j, k: (k, i)) if ta else pl.BlockSpec((bm, bk), lambda i, j, k: (i, k))
    if tb:
        b_spec = pl.BlockSpec((bn, bk), lambda i, j, k: (j + b_col_off, k))
    else:
        b_spec = pl.BlockSpec((bk, bn), lambda i, j, k: (k, j + b_col_off))
    tile = pl.BlockSpec((bm, bn), lambda i, j, k: (i, j))
    outs = pl.pallas_call(
        body, name=name, grid=(M // bm, n // bn, nk),
        in_specs=[a_spec, b_spec] + [tile] * len(extras),
        out_specs=[tile] * n_out,
        out_shape=[jax.ShapeDtypeStruct((M, n), d) for d in out_dtypes],
        scratch_shapes=[pltpu.VMEM((bm, bn), F32)] if use_acc else [],
        compiler_params=_params(3),
    )(a, b, *extras)
    return outs[0] if n_out == 1 else outs


ROW_TILE = 512


def _rows(S):
    tr = min(ROW_TILE, S)
    assert S % tr == 0
    return tr


def _rstd(x):
    return lax.rsqrt(jnp.mean(x * x, axis=-1, keepdims=True) + RMS_EPS)


def _rms_bwd_math(x, g, dy):
    r = _rstd(x)
    gd = g * dy
    dx = r * gd - x * (r * r * r) * jnp.mean(x * gd, axis=-1, keepdims=True)
    dg = jnp.sum(dy * x * r, axis=0, keepdims=True)
    return dx, dg


def _accumulate(ref, value):
    i = pl.program_id(0)

    @pl.when(i == 0)
    def _():
        ref[...] = value

    @pl.when(i > 0)
    def _():
        ref[...] += value


def _row_spec(tr, width):
    return pl.BlockSpec((tr, width), lambda i: (i, 0))


def _vec_spec(width):
    return pl.BlockSpec((1, width), lambda i: (0, 0))


def _rms_fwd(x, g, *, name):
    S, D = x.shape
    tr = _rows(S)

    def body(x_ref, g_ref, o_ref):
        xf = x_ref[...]
        o_ref[...] = (xf * _rstd(xf) * g_ref[...]).astype(o_ref.dtype)

    return pl.pallas_call(
        body, name=name, grid=(S // tr,), in_specs=[_row_spec(tr, D), _vec_spec(D)], out_specs=_row_spec(tr, D),
        out_shape=jax.ShapeDtypeStruct((S, D), BF16), compiler_params=_params(1))(x, g)


def _resid_rms(x, y, g_post, g_next, *, name):
    S, D = x.shape
    tr = _rows(S)
    with_next = g_next is not None

    def body(*refs):
        if with_next:
            x_ref, y_ref, gp_ref, gn_ref, xo_ref, ho_ref = refs
        else:
            x_ref, y_ref, gp_ref, xo_ref = refs
        yf = y_ref[...]
        xn = x_ref[...] + yf * _rstd(yf) * gp_ref[...]
        xo_ref[...] = xn
        if with_next:
            ho_ref[...] = (xn * _rstd(xn) * gn_ref[...]).astype(ho_ref.dtype)

    row, vec = _row_spec(tr, D), _vec_spec(D)
    ins = [x, y, g_post] + ([g_next] if with_next else [])
    outs = pl.pallas_call(
        body, name=name, grid=(S // tr,), in_specs=[row, row, vec] + ([vec] if with_next else []),
        out_specs=[row] + ([row] if with_next else []),
        out_shape=[jax.ShapeDtypeStruct((S, D), F32)] + ([jax.ShapeDtypeStruct((S, D), BF16)] if with_next else []),
        compiler_params=_params(1))(*ins)
    return (outs[0], outs[1]) if with_next else (outs[0], None)


def _rms_bwd(x, g, dy, *, name):
    S, D = x.shape
    tr = _rows(S)

    def body(x_ref, g_ref, dy_ref, dx_ref, dg_ref):
        dx, dg = _rms_bwd_math(x_ref[...], g_ref[...], dy_ref[...])
        dx_ref[...] = dx.astype(dx_ref.dtype)
        _accumulate(dg_ref, dg)

    row, vec = _row_spec(tr, D), _vec_spec(D)
    return pl.pallas_call(
        body, name=name, grid=(S // tr,), in_specs=[row, vec, row], out_specs=[row, vec],
        out_shape=[jax.ShapeDtypeStruct((S, D), BF16), jax.ShapeDtypeStruct((1, D), F32)],
        compiler_params=_params(1))(x, g, dy)


def _rms_bwd_chain(xa, ga, da, resid, xb, gb, *, name):
    S, D = xa.shape
    tr = _rows(S)
    chain = xb is not None

    def body(*refs):
        if chain:
            xa_ref, ga_ref, da_ref, rs_ref, xb_ref, gb_ref, dx_ref, dga_ref, dxb_ref, dgb_ref = refs
        else:
            xa_ref, ga_ref, da_ref, rs_ref, dx_ref, dga_ref = refs
        dxa, dga = _rms_bwd_math(xa_ref[...], ga_ref[...], da_ref[...])
        dx = rs_ref[...] + dxa
        dx_ref[...] = dx
        _accumulate(dga_ref, dga)
        if chain:
            dxb, dgb = _rms_bwd_math(xb_ref[...], gb_ref[...], dx)
            dxb_ref[...] = dxb.astype(dxb_ref.dtype)
            _accumulate(dgb_ref, dgb)

    row, vec = _row_spec(tr, D), _vec_spec(D)
    ins = [xa, ga, da, resid] + ([xb, gb] if chain else [])
    outs = pl.pallas_call(
        body, name=name, grid=(S // tr,), in_specs=[row, vec, row, row] + ([row, vec] if chain else []),
        out_specs=[row, vec] + ([row, vec] if chain else []),
        out_shape=[jax.ShapeDtypeStruct((S, D), F32), jax.ShapeDtypeStruct((1, D), F32)]
        + ([jax.ShapeDtypeStruct((S, D), BF16), jax.ShapeDtypeStruct((1, D), F32)] if chain else []),
        compiler_params=_params(1))(*ins)
    return tuple(outs) if chain else (outs[0], outs[1], None, None)


def _loss_head(y, target, *, name):
    S, D = y.shape
    tr = _rows(S)
    n_tiles = S // tr

    def body(y_ref, t_ref, dy_ref, loss_ref, acc_ref):
        err = y_ref[...] - t_ref[...]
        dy_ref[...] = err * (1.0 / D)
        _accumulate(acc_ref, jnp.sum(err * err, axis=0, keepdims=True))

        @pl.when(pl.program_id(0) == n_tiles - 1)
        def _():
            loss_ref[...] = jnp.sum(acc_ref[...], axis=1, keepdims=True) * (0.5 / D)

    row = _row_spec(tr, D)
    return pl.pallas_call(
        body, name=name, grid=(n_tiles,), in_specs=[row, row],
        out_specs=[row, pl.BlockSpec((1, 1), lambda i: (0, 0))],
        out_shape=[jax.ShapeDtypeStruct((S, D), F32), jax.ShapeDtypeStruct((1, 1), F32)],
        scratch_shapes=[pltpu.VMEM((1, D), F32)], compiler_params=_params(1))(y, target)


SB_TILE = 256
SB_PAIRS = SB_HEADS * SB_HEAD_DIM // LANES
SB_DEAD_LOG = -110.0


def _log_sigmoids(z):
    ls = jnp.minimum(z, 0.0) - jnp.log(1.0 + jnp.exp(-jnp.abs(z)))
    return ls, ls - z


def _running_sums(x, tri2):
    hi = x.astype(BF16)
    lo = (x - hi.astype(F32)).astype(BF16)
    return jnp.dot(jnp.concatenate([hi, lo], axis=1), tri2, preferred_element_type=F32)


def _tri(T, cmp):
    j = lax.broadcasted_iota(jnp.int32, (T, T), 0)
    s = lax.broadcasted_iota(jnp.int32, (T, T), 1)
    m = jnp.where(cmp(j, s), 1.0, 0.0).astype(BF16)
    return jnp.concatenate([m, m], axis=0)


def _head_masks():
    lane = lax.broadcasted_iota(jnp.int32, (1, LANES), 1)
    return [lane < SB_HEAD_DIM, lane >= SB_HEAD_DIM]


def _cargo(refs, n_in, n_out, cargo, scatter):
    n = len(cargo)
    if not n:
        return refs, lambda first: None, lambda last: None
    ins = refs[n_in:n_in + n]
    outs = refs[n_in + n + n_out:n_in + n + n_out + n]
    sems = refs[len(refs) - 3:]
    own = refs[:n_in] + refs[n_in + n:n_in + n + n_out] + refs[n_in + n + n_out + n:len(refs) - 3]

    def start(first):
        @pl.when(first)
        def _():
            for cp in _chip_copies(ins, outs, *sems, scatter=scatter):
                cp.start()

    def finish(last):
        @pl.when(last)
        def _():
            for cp in _chip_copies(ins, outs, *sems, scatter=scatter):
                cp.wait()

    return own, start, finish


def _sba_fwd(qkv, *, name, cargo=()):
    S = qkv.shape[0]
    T = min(SB_TILE, S)
    nq = S // T
    scale = SB_HEAD_DIM ** -0.5

    def body(*refs):
        (q_ref, k_ref, v_ref, o_ref, t_ref, first_ref), start_cargo, finish_cargo = _cargo(refs, 3, 3, cargo, False)
        p, i = pl.program_id(0), pl.program_id(1)
        start_cargo(jnp.logical_and(p == 0, i == 0))
        row = lax.broadcasted_iota(jnp.int32, (T, T), 0)
        col = lax.broadcasted_iota(jnp.int32, (T, T), 1)
        strict = col < row
        after = _tri(T, lambda j, s: j > s)
        masks = _head_masks()
        q = q_ref[...] * scale
        qs = [jnp.where(hm, q, jnp.zeros_like(q)) for hm in masks]

        def walk(tiles, carry):
            values, zs = [], []
            for j, diag in tiles:
                rows = pl.ds(pl.multiple_of(j * T, T), T)
                kb = k_ref[rows, :]
                values.append(v_ref[rows, :])
                zs += [(lax.dot_general(qh, kb, (((1,), (1,)), ((), ())), preferred_element_type=F32), diag) for qh in qs]
            logs, suffixes = [], []
            for z, diag in zs:
                ls, ln = _log_sigmoids(z)
                ln = jnp.where(strict, ln, 0.0) if diag else ln
                logs.append((ls, ln))
                suffixes.append(_running_sums(ln, after))
            for t, (_, diag) in enumerate(tiles):
                out = []
                for h, (C, acc) in enumerate(carry):
                    ls, ln = logs[2 * t + h]
                    a = jnp.exp(ls + suffixes[2 * t + h] + C)
                    if diag:
                        a = jnp.where(strict, a, 0.0)
                    acc = acc + jnp.dot(a.astype(BF16), values[t], preferred_element_type=F32)
                    out.append((C + jnp.sum(ln, axis=1, keepdims=True), acc))
                carry = tuple(out)
            return carry

        fresh = (jnp.zeros((T, 1), F32), jnp.zeros((T, LANES), F32))
        carry = lax.cond(i > 0, lambda: walk([(i, True), (i - 1, False)], (fresh, fresh)),
                         lambda: walk([(i, True)], (fresh, fresh)))

        def alive(state):
            j, ((C0, _), (C1, _)) = state
            return jnp.logical_and(j >= 0, jnp.max(jnp.maximum(C0, C1)) > SB_DEAD_LOG)

        def step(state):
            j, carry = state
            return j - 1, walk([(j, False)], carry)

        j, ((C0, acc0), (C1, acc1)) = lax.while_loop(alive, step, (i - 2, carry))
        t_ref[0] = jnp.broadcast_to(C0, (T, LANES))
        t_ref[1] = jnp.broadcast_to(C1, (T, LANES))
        first_ref[...] = jnp.full((8, LANES), jnp.maximum(j + 1, 0).astype(F32))
        o_ref[...] = jnp.where(masks[0], acc0, acc1).astype(o_ref.dtype)
        finish_cargo(jnp.logical_and(p == SB_PAIRS - 1, i == nq - 1))

    kv = lambda off: pl.BlockSpec((S, LANES), lambda p, i: (0, off + p))
    n = len(cargo)
    return pl.pallas_call(
        body, name=name, grid=(SB_PAIRS, nq),
        in_specs=[pl.BlockSpec((T, LANES), lambda p, i: (i, p)), kv(SB_PAIRS), kv(2 * SB_PAIRS)] + [HBM_SPEC] * n,
        out_specs=[pl.BlockSpec((T, LANES), lambda p, i: (i, p)), pl.BlockSpec((2, T, LANES), lambda p, i: (p, i, 0)),
                   pl.BlockSpec((None, None, 8, LANES), lambda p, i: (p, i, 0, 0))] + [HBM_SPEC] * n,
        out_shape=[jax.ShapeDtypeStruct((S, SB_WIDTH), BF16), jax.ShapeDtypeStruct((SB_HEADS, S, LANES), F32),
                   jax.ShapeDtypeStruct((SB_PAIRS, nq, 8, LANES), F32)] + _chip_exchange_shapes(cargo, False),
        scratch_shapes=_chip_exchange_semaphores(n) if n else [],
        compiler_params=_params(2))(qkv, qkv, qkv, *cargo)


def _sba_bwd(qkv, do, tot, first, *, name, cargo=()):
    S = qkv.shape[0]
    T = min(SB_TILE, S)
    nq = S // T
    scale = SB_HEAD_DIM ** -0.5

    def body(*refs):
        own, start_cargo, finish_cargo = _cargo(refs, 6, 3, cargo, True)
        q_ref, k_ref, v_ref, do_ref, t_ref, first_ref, dq_ref, dk_ref, dv_ref, dk_acc, dv_acc = own
        p, i = pl.program_id(0), pl.program_id(1)
        start_cargo(jnp.logical_and(p == 0, i == 0))

        @pl.when(i == 0)
        def _():
            dk_acc[...] = jnp.zeros_like(dk_acc)
            dv_acc[...] = jnp.zeros_like(dv_acc)

        row = lax.broadcasted_iota(jnp.int32, (T, T), 0)
        col = lax.broadcasted_iota(jnp.int32, (T, T), 1)
        strict = col < row
        upto = _tri(T, lambda j, s: j <= s)
        before = _tri(T, lambda j, s: j < s)
        masks = _head_masks()
        q, do_t = q_ref[...], do_ref[...]
        q = q * scale
        qs = [jnp.where(hm, q, jnp.zeros_like(q)) for hm in masks]
        dos = [jnp.where(hm, do_t, jnp.zeros_like(do_t)) for hm in masks]
        totals = [t_ref[h][:, 0:1] for h in range(2)]
        over_lanes = (((1,), (1,)), ((), ()))
        over_queries = (((0,), (0,)), ((), ()))

        def walk(tiles, carry):
            rows = [pl.ds(pl.multiple_of(j * T, T), T) for j, _ in tiles]
            keys = [k_ref[r, :] for r in rows]
            values = [v_ref[r, :] for r in rows]
            chains = [(t, h) for t in range(len(tiles)) for h in range(2)]
            logs, da = {}, {}
            for t, h in chains:
                z = lax.dot_general(qs[h], keys[t], over_lanes, preferred_element_type=F32)
                ls, ln = _log_sigmoids(z)
                logs[t, h] = (ls, jnp.where(strict, ln, 0.0) if tiles[t][1] else ln)
                da[t, h] = lax.dot_general(dos[h], values[t], over_lanes, preferred_element_type=F32)
            upto_sums = {c: _running_sums(logs[c][1], upto) for c in chains}
            a, g = {}, {}
            P = [c[0] for c in carry]
            for t, h in chains:
                ls, ln = logs[t, h]
                a_th = jnp.exp(ls + ((totals[h] - P[h]) - upto_sums[t, h]))
                a[t, h] = jnp.where(strict, a_th, 0.0) if tiles[t][1] else a_th
                g[t, h] = a[t, h] * da[t, h]
                P[h] = P[h] + jnp.sum(ln, axis=1, keepdims=True)
            before_sums = {c: _running_sums(g[c], before) for c in chains}
            G = [c[1] for c in carry]
            dq = [c[2] for c in carry]
            dz = {}
            for t, h in chains:
                beta = jnp.exp(logs[t, h][0])
                dz_th = g[t, h] * (1.0 - beta) - (G[h] + before_sums[t, h]) * beta
                dz[t, h] = (jnp.where(strict, dz_th, 0.0) if tiles[t][1] else dz_th).astype(BF16)
                G[h] = G[h] + jnp.sum(g[t, h], axis=1, keepdims=True)
            for t, h in chains:
                dq[h] = dq[h] + jnp.dot(dz[t, h], keys[t], preferred_element_type=F32)
            for t in range(len(tiles)):
                dk_acc[rows[t], :] += sum(
                    lax.dot_general(dz[t, h], qs[h], over_queries, preferred_element_type=F32) for h in range(2))
                dv_acc[rows[t], :] += sum(
                    lax.dot_general(a[t, h].astype(BF16), dos[h], over_queries, preferred_element_type=F32)
                    for h in range(2))
            return tuple((P[h], G[h], dq[h]) for h in range(2))

        zero = jnp.zeros((T, 1), F32)
        fresh = (zero, zero, jnp.zeros((T, LANES), F32))
        last_single = jnp.maximum(i - 1, 0)
        j0 = jnp.clip(jnp.max(first_ref[...]).astype(jnp.int32), 0, last_single)
        carry = lax.fori_loop(j0, last_single, lambda j, c: walk([(j, False)], c), (fresh, fresh))
        (_, _, dq0), (_, _, dq1) = lax.cond(i > 0, lambda: walk([(i - 1, False), (i, True)], carry),
                                            lambda: walk([(i, True)], carry))
        dq_ref[...] = (jnp.where(masks[0], dq0, dq1) * scale).astype(dq_ref.dtype)

        @pl.when(i == nq - 1)
        def _():
            dk_ref[...] = dk_acc[...].astype(dk_ref.dtype)
            dv_ref[...] = dv_acc[...].astype(dv_ref.dtype)

        finish_cargo(jnp.logical_and(p == SB_PAIRS - 1, i == nq - 1))

    kv = lambda off: pl.BlockSpec((S, LANES), lambda p, i: (0, off + p))
    tile = lambda off: pl.BlockSpec((T, LANES), lambda p, i: (i, off + p))
    n = len(cargo)
    return pl.pallas_call(
        body, name=name, grid=(SB_PAIRS, nq),
        in_specs=[tile(0), kv(SB_PAIRS), kv(2 * SB_PAIRS), tile(0), pl.BlockSpec((2, T, LANES), lambda p, i: (p, i, 0)),
                  pl.BlockSpec((None, None, 8, LANES), lambda p, i: (p, i, 0, 0))] + [HBM_SPEC] * n,
        out_specs=[tile(0), kv(0), kv(0)] + [HBM_SPEC] * n,
        out_shape=[jax.ShapeDtypeStruct((S, SB_WIDTH), BF16)] * 3 + _chip_exchange_shapes(cargo, True),
        scratch_shapes=[pltpu.VMEM((S, LANES), F32), pltpu.VMEM((S, LANES), F32)]
        + (_chip_exchange_semaphores(n) if n else []),
        compiler_params=_params(2))(qkv, qkv, qkv, do, tot, first, *cargo)


POOL_TILE = 1024


def _by_group(lane, values):
    return jnp.where(lane < 64, values[0], jnp.where(lane < 128, values[1], jnp.where(lane < 192, values[2], values[3])))


def _pool_inv_count(first_row, n_rows):
    t = first_row + lax.broadcasted_iota(jnp.int32, (n_rows, POOL_WIDTH), 0)
    lane = lax.broadcasted_iota(jnp.int32, (n_rows, POOL_WIDTH), 1)
    window = _by_group(lane, POOL_WINDOWS)
    return 1.0 / jnp.clip(t + 1, 1, window).astype(F32), lane


def _pooled(ext, first_row, R):
    n = R + POOL_HALO
    s2 = ext + pltpu.roll(ext, 1, 0)
    s4 = s2 + pltpu.roll(s2, 2, 0)
    s8 = s4 + pltpu.roll(s4, 4, 0)
    s16 = s8 + pltpu.roll(s8, 8, 0)
    inv, lane = _pool_inv_count(first_row - POOL_HALO, n)
    pooled = _by_group(lane, (s2, s4, s8, s16)) * inv - ext
    return pooled[POOL_HALO:, :]


def _pool_specs(S, R, col):
    per = R // POOL_HALO
    tile = pl.BlockSpec((R, POOL_WIDTH), lambda i: (i, col))
    prev = pl.BlockSpec((POOL_HALO, POOL_WIDTH), lambda i: (jnp.maximum(i * per - 1, 0), col))
    return tile, prev


def _pool_fwd(rest, w_bd, scale, *, name):
    S = rest.shape[0]
    R = min(POOL_TILE, S)

    def body(p_ref, prev_ref, w_ref, s_ref, o_ref, ext_ref):
        i = pl.program_id(0)
        ext_ref[:POOL_HALO, :] = jnp.where(i > 0, prev_ref[...], 0.0)
        ext_ref[POOL_HALO:, :] = p_ref[...]
        pooled = _pooled(ext_ref[...], i * R, R)
        mixed = jnp.dot(pooled.astype(BF16), w_ref[...], preferred_element_type=F32)
        o_ref[...] = (mixed * s_ref[...]).astype(o_ref.dtype)

    tile, prev = _pool_specs(S, R, 0)
    return pl.pallas_call(
        body, name=name, grid=(S // R,),
        in_specs=[tile, prev, pl.BlockSpec((POOL_WIDTH, POOL_WIDTH), lambda i: (0, 0)), _vec_spec(POOL_WIDTH)],
        out_specs=_row_spec(R, POOL_WIDTH), out_shape=jax.ShapeDtypeStruct((S, POOL_WIDTH), BF16),
        scratch_shapes=[pltpu.VMEM((R + POOL_HALO, POOL_WIDTH), F32)], compiler_params=_params(1))(rest, rest, w_bd, scale)


def _pool_bwd(rest, do, w_bd, scale, *, name):
    S = rest.shape[0]
    R = min(POOL_TILE, S)
    n_tiles = S // R
    per = R // POOL_HALO
    n = R + POOL_HALO

    def body(p_ref, prev_ref, do_ref, nxt_ref, w_ref, s_ref, dp_ref, dw_ref, ds_ref, ext_ref, dext_ref):
        i = pl.program_id(0)
        ext_ref[:POOL_HALO, :] = jnp.where(i > 0, prev_ref[...], 0.0)
        ext_ref[POOL_HALO:, :] = p_ref[...]
        pooled = _pooled(ext_ref[...], i * R, R).astype(BF16)
        w = w_ref[...]
        mixed = jnp.dot(pooled, w, preferred_element_type=F32)
        do_t = do_ref[...]
        _accumulate(ds_ref, jnp.sum(do_t * mixed, axis=0, keepdims=True))
        dext_ref[:R, :] = do_t
        dext_ref[R:, :] = jnp.where(i < n_tiles - 1, nxt_ref[...], 0.0)
        dmixed = (dext_ref[...] * s_ref[...]).astype(BF16)
        dpooled = lax.dot_general(dmixed, w, (((1,), (1,)), ((), ())), preferred_element_type=F32)
        _accumulate(dw_ref, lax.dot_general(pooled, dmixed[:R, :], (((0,), (0,)), ((), ())), preferred_element_type=F32))
        inv, lane = _pool_inv_count(i * R, n)
        u = dpooled * inv
        f2 = u + pltpu.roll(u, n - 1, 0)
        f4 = f2 + pltpu.roll(f2, n - 2, 0)
        f8 = f4 + pltpu.roll(f4, n - 4, 0)
        f16 = f8 + pltpu.roll(f8, n - 8, 0)
        dp = _by_group(lane, (f2, f4, f8, f16)) - dpooled
        dp_ref[...] = dp[:R, :].astype(dp_ref.dtype)

    tile, prev = _pool_specs(S, R, 0)
    nxt = pl.BlockSpec((POOL_HALO, POOL_WIDTH), lambda i: (jnp.minimum((i + 1) * per, S // POOL_HALO - 1), 0))
    full = pl.BlockSpec((POOL_WIDTH, POOL_WIDTH), lambda i: (0, 0))
    return pl.pallas_call(
        body, name=name, grid=(n_tiles,),
        in_specs=[tile, prev, _row_spec(R, POOL_WIDTH), nxt, full, _vec_spec(POOL_WIDTH)],
        out_specs=[_row_spec(R, POOL_WIDTH), full, _vec_spec(POOL_WIDTH)],
        out_shape=[jax.ShapeDtypeStruct((S, POOL_WIDTH), BF16), jax.ShapeDtypeStruct((POOL_WIDTH, POOL_WIDTH), F32),
                   jax.ShapeDtypeStruct((1, POOL_WIDTH), F32)],
        scratch_shapes=[pltpu.VMEM((n, POOL_WIDTH), F32), pltpu.VMEM((n, POOL_WIDTH), F32)],
        compiler_params=_params(1))(rest, rest, do, do, w_bd, scale)


GM_TILE = 1024
GELU_C = math.sqrt(2.0 / math.pi)
GELU_A = 0.044715


def _gelu(x):
    return 0.5 * x * (1.0 + jnp.tanh(GELU_C * (x + GELU_A * x * x * x)))


def _gelu_and_grad(x):
    t = jnp.tanh(GELU_C * (x + GELU_A * x * x * x))
    y = 0.5 * x * (1.0 + t)
    dy = 0.5 * (1.0 + t) + 0.5 * x * (1.0 - t * t) * (GELU_C * (1.0 + 3.0 * GELU_A * x * x))
    return y, dy


def _group_lane_masks():
    lane = lax.broadcasted_iota(jnp.int32, (1, GM_WIDTH), 1)
    return [(lane >= g * GM_GROUP_DIM) & (lane < (g + 1) * GM_GROUP_DIM) for g in range(GM_GROUPS)]


def _stack_groups(x, masks):
    return jnp.concatenate([jnp.where(m, x, jnp.zeros_like(x)) for m in masks], axis=0)


def _gm_mixed(vn, ws_cat, bias, masks, R):
    chunks = []
    for c in range(R // GM_CHUNK):
        vc = vn[c * GM_CHUNK:(c + 1) * GM_CHUNK, :]
        chunks.append(jnp.dot(ws_cat, _stack_groups(vc, masks), preferred_element_type=F32) + bias)
    return jnp.concatenate(chunks, axis=0)


def _gm_specs(S, R):
    u = pl.BlockSpec((R, GM_WIDTH), lambda i: (i, 1))
    v = pl.BlockSpec((R, GM_WIDTH), lambda i: (i, 2))
    ws = pl.BlockSpec((GM_CHUNK, GM_GROUPS * GM_CHUNK), lambda i: (0, 0))
    bias = pl.BlockSpec((GM_CHUNK, GM_WIDTH), lambda i: (0, 0))
    return u, v, ws, bias


def _gm_fwd(rest, gain, ws_cat, bias, *, name):
    S = rest.shape[0]
    R = min(GM_TILE, S)

    def body(u_ref, v_ref, g_ref, ws_ref, b_ref, o_ref):
        gv = _gelu(v_ref[...])
        vn = (gv * _rstd(gv) * g_ref[...]).astype(BF16)
        mixed = _gm_mixed(vn, ws_ref[...], b_ref[...], _group_lane_masks(), R)
        o_ref[...] = (_gelu(u_ref[...]) * mixed).astype(o_ref.dtype)

    u_spec, v_spec, ws_spec, bias_spec = _gm_specs(S, R)
    return pl.pallas_call(
        body, name=name, grid=(S // R,), in_specs=[u_spec, v_spec, _vec_spec(GM_WIDTH), ws_spec, bias_spec],
        out_specs=_row_spec(R, GM_WIDTH), out_shape=jax.ShapeDtypeStruct((S, GM_WIDTH), BF16),
        compiler_params=_params(1))(rest, rest, gain, ws_cat, bias)


def _gm_bwd(rest, do, gain, ws_cat, wst_cat, bias, *, name):
    S = rest.shape[0]
    R = min(GM_TILE, S)

    def body(u_ref, v_ref, do_ref, g_ref, ws_ref, wst_ref, b_ref, du_ref, dv_ref, dg_ref, dws_ref, db_ref):
        masks = _group_lane_masks()
        gain_v = g_ref[...]
        gu, dgu = _gelu_and_grad(u_ref[...])
        gv, dgv = _gelu_and_grad(v_ref[...])
        r = _rstd(gv)
        vn = (gv * r * gain_v).astype(BF16)
        mixed = _gm_mixed(vn, ws_ref[...], b_ref[...], masks, R)
        do_t = do_ref[...]
        du_ref[...] = (do_t * mixed * dgu).astype(du_ref.dtype)
        dmix = do_t * gu
        dmix_b = dmix.astype(BF16)
        wst = wst_ref[...]
        dvn_chunks, db, dws = [], None, [None] * GM_GROUPS
        for c in range(R // GM_CHUNK):
            rows = slice(c * GM_CHUNK, (c + 1) * GM_CHUNK)
            dc, dcb, vc = dmix[rows, :], dmix_b[rows, :], vn[rows, :]
            db = dc if db is None else db + dc
            dvn_chunks.append(jnp.dot(wst, _stack_groups(dcb, masks), preferred_element_type=F32))
            for g, m in enumerate(masks):
                part = lax.dot_general(jnp.where(m, dcb, jnp.zeros_like(dcb)), vc, (((1,), (1,)), ((), ())),
                                       preferred_element_type=F32)
                dws[g] = part if dws[g] is None else dws[g] + part
        dvn = jnp.concatenate(dvn_chunks, axis=0)
        lane = lax.broadcasted_iota(jnp.int32, (1, LANES), 1)
        db_groups = jnp.zeros((GM_CHUNK, LANES), F32)
        for g, m in enumerate(masks):
            total = jnp.sum(jnp.where(m, db, 0.0), axis=1, keepdims=True)
            db_groups = db_groups + jnp.where(lane == g, total, 0.0)
        _accumulate(db_ref, db_groups)
        i = pl.program_id(0)
        for g in range(GM_GROUPS):
            @pl.when(i == 0)
            def _(g=g):
                dws_ref[g] = dws[g]

            @pl.when(i > 0)
            def _(g=g):
                dws_ref[g] += dws[g]
        _accumulate(dg_ref, jnp.sum(dvn * gv * r, axis=0, keepdims=True))
        gd = gain_v * dvn
        dgv_in = r * gd - gv * (r * r * r) * jnp.mean(gv * gd, axis=-1, keepdims=True)
        dv_ref[...] = (dgv_in * dgv).astype(dv_ref.dtype)

    u_spec, v_spec, ws_spec, bias_spec = _gm_specs(S, R)
    row, vec = _row_spec(R, GM_WIDTH), _vec_spec(GM_WIDTH)
    dws_spec = pl.BlockSpec((GM_GROUPS, GM_CHUNK, GM_CHUNK), lambda i: (0, 0, 0))
    return pl.pallas_call(
        body, name=name, grid=(S // R,), in_specs=[u_spec, v_spec, row, vec, ws_spec, ws_spec, bias_spec],
        out_specs=[row, row, vec, dws_spec, pl.BlockSpec((GM_CHUNK, LANES), lambda i: (0, 0))],
        out_shape=[jax.ShapeDtypeStruct((S, GM_WIDTH), BF16)] * 2
        + [jax.ShapeDtypeStruct((1, GM_WIDTH), F32), jax.ShapeDtypeStruct((GM_GROUPS, GM_CHUNK, GM_CHUNK), F32),
           jax.ShapeDtypeStruct((GM_CHUNK, LANES), F32)],
        compiler_params=_params(1))(rest, rest, do, gain, ws_cat, wst_cat, bias)


GATE_ROWS = 1024
GATE_COLS = 512
GATE_BLOCKS = D_MODEL // GATE_COLS


def _gate_spec(tr, k):
    return pl.BlockSpec((tr, GATE_COLS), lambda i, j: (i, GATE_BLOCKS * k + j))


def _merge_fwd(gates, branches, *, name):
    S = gates.shape[0]
    tr = min(GATE_ROWS, S)

    def body(g0, g1, g2, b0, b1, b2, o_ref):
        acc = None
        for g_ref, b_ref in ((g0, b0), (g1, b1), (g2, b2)):
            term = jax.nn.sigmoid(g_ref[...].astype(F32)) * b_ref[...].astype(F32)
            acc = term if acc is None else acc + term
        o_ref[...] = acc.astype(o_ref.dtype)

    tile = pl.BlockSpec((tr, GATE_COLS), lambda i, j: (i, j))
    return pl.pallas_call(
        body, name=name, grid=(S // tr, GATE_BLOCKS),
        in_specs=[_gate_spec(tr, k) for k in range(N_BRANCH)] + [tile] * N_BRANCH, out_specs=tile,
        out_shape=jax.ShapeDtypeStruct((S, D_MODEL), BF16), compiler_params=_params(2))(gates, gates, gates, *branches)


def _merge_bwd(gates, branches, dmerged, *, name):
    S = gates.shape[0]
    tr = min(GATE_ROWS, S)

    def body(g0, g1, g2, b0, b1, b2, dm_ref, dg0, dg1, dg2, db0, db1, db2):
        dm = dm_ref[...].astype(F32)
        for g_ref, b_ref, dg_ref, db_ref in ((g0, b0, dg0, db0), (g1, b1, dg1, db1), (g2, b2, dg2, db2)):
            s = jax.nn.sigmoid(g_ref[...].astype(F32))
            db_ref[...] = (dm * s).astype(db_ref.dtype)
            dg_ref[...] = (dm * b_ref[...].astype(F32) * s * (1.0 - s)).astype(dg_ref.dtype)

    tile = pl.BlockSpec((tr, GATE_COLS), lambda i, j: (i, j))
    return pl.pallas_call(
        body, name=name, grid=(S // tr, GATE_BLOCKS),
        in_specs=[_gate_spec(tr, k) for k in range(N_BRANCH)] + [tile] * (N_BRANCH + 1), out_specs=[tile] * (2 * N_BRANCH),
        out_shape=[jax.ShapeDtypeStruct((S, D_MODEL), BF16)] * (2 * N_BRANCH),
        compiler_params=_params(2))(gates, gates, gates, *branches, dmerged)


TILE_BYTES = 24 * 1024 * 1024


BF16_ROWS = 16


def _tile_rows(rows, cols, n_arrays):
    padded = -(-cols // LANES) * LANES
    cap = max(BF16_ROWS, TILE_BYTES // (2 * n_arrays * padded * 4))
    best = None
    for tr in range(BF16_ROWS, min(rows, cap) + 1, BF16_ROWS):
        if rows % tr == 0:
            best = tr
    assert best is not None, (rows, cols)
    return best


def _sum_slots(stack, *, name):
    n, R, C = stack.shape
    tr = _tile_rows(R, C, n + 1)

    def body(s_ref, o_ref):
        acc = s_ref[0].astype(F32)
        for k in range(1, n):
            acc = acc + s_ref[k].astype(F32)
        o_ref[...] = acc

    return pl.pallas_call(
        body, name=name, grid=(R // tr,), in_specs=[pl.BlockSpec((n, tr, C), lambda i: (0, i, 0))],
        out_specs=_row_spec(tr, C), out_shape=jax.ShapeDtypeStruct((R, C), F32), compiler_params=_params(1))(stack)


def _add_own_half(parts, received, core, *, name):
    n, R, C = parts.shape
    half = R // 2
    tr = _tile_rows(half, C, 3)
    steps = half // tr

    def body(core_ref, own_ref, got_ref, o_ref):
        o_ref[...] = (own_ref[...] + got_ref[...]).astype(o_ref.dtype)

    tile = pl.BlockSpec((None, tr, C), lambda d, i, core_ref: (d, i, 0))
    own = pl.BlockSpec((None, tr, C), lambda d, i, core_ref: (d, core_ref[0] * steps + i, 0))
    return pl.pallas_call(
        body, name=name, out_shape=jax.ShapeDtypeStruct((n, half, C), BF16),
        grid_spec=pltpu.PrefetchScalarGridSpec(num_scalar_prefetch=1, grid=(n, steps), in_specs=[own, tile], out_specs=tile),
        compiler_params=_params(2))(core, parts, received)


def _adamw_math(w, m, v, g):
    m_new = ADAM_B1 * m + (1.0 - ADAM_B1) * g
    v_new = ADAM_B2 * v + (1.0 - ADAM_B2) * jnp.square(g)
    m_hat = m_new / (1.0 - ADAM_B1 ** ADAM_STEP)
    v_hat = v_new / (1.0 - ADAM_B2 ** ADAM_STEP)
    return -ADAM_LR * (m_hat / (jnp.sqrt(v_hat) + ADAM_EPS) + ADAM_WD * w), m_new, v_new


def _adamw_halves(w, m, v, mine, theirs, core, *, name):
    L, r, C = w.shape
    tr = _tile_rows(r // 2, C, 9)
    steps = r // 2 // tr

    def body(core_ref, w_ref, m_ref, v_ref, mine_ref, theirs_ref, go_ref, d_ref, mo_ref, vo_ref):
        in_my_half = pl.program_id(1) // steps == core_ref[0]
        g = jnp.where(in_my_half, mine_ref[...], theirs_ref[...])
        go_ref[...] = g
        d_ref[...], mo_ref[...], vo_ref[...] = _adamw_math(w_ref[...], m_ref[...], v_ref[...], g)

    row = pl.BlockSpec((None, tr, C), lambda l, i, core_ref: (l, i, 0))
    half = pl.BlockSpec((None, tr, C), lambda l, i, core_ref: (l, i % steps, 0))
    return pl.pallas_call(
        body, name=name, out_shape=[jax.ShapeDtypeStruct((L, r, C), F32)] * 4,
        grid_spec=pltpu.PrefetchScalarGridSpec(
            num_scalar_prefetch=1, grid=(L, r // tr), in_specs=[row, row, row, half, half], out_specs=[row] * 4),
        compiler_params=_params(2))(core, w, m, v, mine, theirs)


HBM_SPEC = pl.BlockSpec(memory_space=pl.ANY)


def _position():
    return lax.axis_index("x"), lax.axis_index("y"), lax.axis_index("c")


def _other_chips(x, y):
    return [(1 - x, y), (x, 1 - y), (1 - x, 1 - y)]


def _chip_exchange(arrays, *, scatter, name):
    n = len(arrays)

    def body(*refs):
        copies = _chip_copies(refs[:n], refs[n:2 * n], *refs[2 * n:], scatter=scatter)
        for cp in copies:
            cp.start()
        for cp in copies:
            cp.wait()

    return pl.pallas_call(
        body, name=name, in_specs=[HBM_SPEC] * n, out_specs=[HBM_SPEC] * n, out_shape=_chip_exchange_shapes(arrays, scatter),
        scratch_shapes=_chip_exchange_semaphores(n))(*arrays)


def _chip_exchange_shapes(arrays, scatter):
    return [jax.ShapeDtypeStruct(a.shape if scatter else (N_CHIPS, 2) + a.shape, a.dtype) for a in arrays]


def _chip_exchange_semaphores(n):
    return [pltpu.SemaphoreType.DMA((3 * n,)), pltpu.SemaphoreType.DMA((3 * n,)), pltpu.SemaphoreType.DMA((n,))]


def _chip_copies(ins, outs, send_sems, recv_sems, local_sems, *, scatter):
    x, y, c = _position()
    me = 2 * x + y
    copies = []
    for a in range(len(ins)):
        own = ins[a].at[me] if scatter else ins[a]
        slot = outs[a].at[me] if scatter else outs[a].at[me, c]
        copies.append(pltpu.make_async_copy(own, slot, local_sems.at[a]))
        for k, (px, py) in enumerate(_other_chips(x, y)):
            src = ins[a].at[2 * px + py] if scatter else ins[a]
            copies.append(pltpu.make_async_remote_copy(
                src_ref=src, dst_ref=slot, send_sem=send_sems.at[3 * a + k],
                recv_sem=recv_sems.at[3 * a + k], device_id=(px, py, c), device_id_type=MESH))
    return copies


def _sibling_fill(arrays, *, name):
    n = len(arrays)

    def body(*refs):
        ins, outs = refs[:n], refs[n:2 * n]
        send_sems, recv_sems = refs[2 * n:]
        x, y, c = _position()
        copies = []
        for a in range(n):
            cp = pltpu.make_async_remote_copy(
                src_ref=ins[a].at[:, c], dst_ref=outs[a].at[:, c], send_sem=send_sems.at[a], recv_sem=recv_sems.at[a],
                device_id=(x, y, 1 - c), device_id_type=MESH)
            cp.start()
            copies.append(cp)
        for cp in copies:
            cp.wait()

    return pl.pallas_call(
        body, name=name, in_specs=[HBM_SPEC] * n, out_specs=[HBM_SPEC] * n,
        out_shape=[jax.ShapeDtypeStruct(a.shape, a.dtype) for a in arrays],
        input_output_aliases={a: a for a in range(n)},
        scratch_shapes=[pltpu.SemaphoreType.DMA((n,)), pltpu.SemaphoreType.DMA((n,))],
    )(*arrays)


def _sibling_swap(arrays, *, name):
    n = len(arrays)

    def body(*refs):
        ins, outs = refs[:n], refs[n:2 * n]
        send_sems, recv_sems = refs[2 * n:]
        x, y, c = _position()
        copies = []
        for a in range(n):
            cp = pltpu.make_async_remote_copy(
                src_ref=ins[a], dst_ref=outs[a], send_sem=send_sems.at[a], recv_sem=recv_sems.at[a],
                device_id=(x, y, 1 - c), device_id_type=MESH)
            cp.start()
            copies.append(cp)
        for cp in copies:
            cp.wait()

    return pl.pallas_call(
        body, name=name, in_specs=[HBM_SPEC] * n, out_specs=[HBM_SPEC] * n,
        out_shape=[jax.ShapeDtypeStruct(a.shape, a.dtype) for a in arrays],
        scratch_shapes=[pltpu.SemaphoreType.DMA((n,)), pltpu.SemaphoreType.DMA((n,))],
    )(*arrays)


def _sibling_other_half(arrays, *, name):
    n = len(arrays)

    def body(*refs):
        ins, outs = refs[:n], refs[n:2 * n]
        send_sems, recv_sems = refs[2 * n:]
        x, y, c = _position()
        copies = []
        for a in range(n):
            half = ins[a].shape[1] // 2
            theirs = ins[a].at[:, pl.ds(pl.multiple_of((1 - c) * half, BF16_ROWS), half), :]
            cp = pltpu.make_async_remote_copy(
                src_ref=theirs, dst_ref=outs[a], send_sem=send_sems.at[a], recv_sem=recv_sems.at[a],
                device_id=(x, y, 1 - c), device_id_type=MESH)
            cp.start()
            copies.append(cp)
        for cp in copies:
            cp.wait()

    return pl.pallas_call(
        body, name=name, in_specs=[HBM_SPEC] * n, out_specs=[HBM_SPEC] * n,
        out_shape=[jax.ShapeDtypeStruct((a.shape[0], a.shape[1] // 2, a.shape[2]), a.dtype) for a in arrays],
        scratch_shapes=[pltpu.SemaphoreType.DMA((n,)), pltpu.SemaphoreType.DMA((n,))],
    )(*arrays)


def _small_update(grads, ws, ms, vs, *, name):
    n, L = len(ws), ws[0].shape[0]
    pieces = [g for per_layer in grads for g in per_layer]
    np_ = len(pieces)

    def body(*refs):
        g_in, refs = refs[:np_], refs[np_:]
        w_in, m_in, v_in, g_out, d_out, m_out, v_out = (refs[k * n:(k + 1) * n] for k in range(7))
        from_sibling, chip_sums = refs[7 * n:7 * n + np_], refs[7 * n + np_:7 * n + 2 * np_]
        sibling_send, sibling_recv, chip_send, chip_recv = refs[7 * n + 2 * np_:]
        x, y, c = _position()
        me = 2 * x + y
        swaps = [pltpu.make_async_remote_copy(
            src_ref=g_in[p], dst_ref=from_sibling[p], send_sem=sibling_send.at[p], recv_sem=sibling_recv.at[p],
            device_id=(x, y, 1 - c), device_id_type=MESH) for p in range(np_)]
        for cp in swaps:
            cp.start()
        for cp in swaps:
            cp.wait()
        for p in range(np_):
            chip_sums[p][me] = g_in[p][...] + from_sibling[p][...]
        sends = [pltpu.make_async_remote_copy(
            src_ref=chip_sums[p].at[me], dst_ref=chip_sums[p].at[me], send_sem=chip_send.at[3 * p + k],
            recv_sem=chip_recv.at[3 * p + k], device_id=(px, py, c), device_id_type=MESH)
            for p in range(np_) for k, (px, py) in enumerate(_other_chips(x, y))]
        for cp in sends:
            cp.start()
        for cp in sends:
            cp.wait()
        for a in range(n):
            for l in range(L):
                sums = chip_sums[a * L + l]
                g = sums[0]
                for s in range(1, N_CHIPS):
                    g = g + sums[s]
                g_out[a][l] = g
                d_out[a][l], m_out[a][l], v_out[a][l] = _adamw_math(w_in[a][l], m_in[a][l], v_in[a][l], g)

    vmem = pl.BlockSpec(memory_space=pltpu.VMEM)
    shapes = [jax.ShapeDtypeStruct(w.shape, F32) for w in ws]
    outs = pl.pallas_call(
        body, name=name, in_specs=[vmem] * (np_ + 3 * n), out_specs=[vmem] * (4 * n), out_shape=shapes * 4,
        scratch_shapes=[pltpu.VMEM(g.shape, F32) for g in pieces] + [pltpu.VMEM((N_CHIPS,) + g.shape, F32) for g in pieces]
        + [pltpu.SemaphoreType.DMA((np_,)), pltpu.SemaphoreType.DMA((np_,)),
           pltpu.SemaphoreType.DMA((3 * np_,)), pltpu.SemaphoreType.DMA((3 * np_,))],
        compiler_params=pltpu.CompilerParams(vmem_limit_bytes=VMEM_LIMIT),
    )(*pieces, *ws, *ms, *vs)
    return outs[:n], outs[n:2 * n], outs[2 * n:3 * n], outs[3 * n:]


def _relu2(p):
    return p, jnp.square(jnp.maximum(p, 0.0))


def _relu2_grad(p, a):
    return (p * (2.0 * jnp.maximum(a.astype(F32), 0.0)),)


def _mixer_constants(w_pool, w_spatial, b_spatial):
    eye = jnp.eye(len(POOL_WINDOWS), dtype=F32)
    w_bd = (eye[:, None, :, None] * w_pool[:, :, None, :]).reshape(POOL_WIDTH, POOL_WIDTH).astype(BF16)
    causal = jnp.tril(jnp.ones((GM_CHUNK, GM_CHUNK), dtype=bool))
    ws = jnp.where(causal[None], w_spatial, 0.0).astype(BF16)
    ws_cat = ws.transpose(1, 0, 2).reshape(GM_CHUNK, GM_GROUPS * GM_CHUNK)
    wst_cat = ws.transpose(2, 0, 1).reshape(GM_CHUNK, GM_GROUPS * GM_CHUNK)
    bias = jnp.repeat(b_spatial.T, GM_GROUP_DIM, axis=1)
    return w_bd, ws_cat, wst_cat, bias


def _local_step(x, target, half_shards, small, core):
    L = small["g_mix_pre"].shape[0]
    vec = lambda name, l: small[name][l][None, :]
    consts = [_mixer_constants(small["w_pool"][l], small["w_spatial"][l], small["b_spatial"][l]) for l in range(L)]
    core_index = core.astype(jnp.int32).reshape(1)
    first_used, used_later = BIG_WEIGHTS[:1], BIG_WEIGHTS[1:]
    weights = [{} for _ in range(L)]

    def finish_gather(wanted, gathered):
        for (n, ll), both in zip(wanted, _sibling_fill(gathered, name="swap_weight_halves")):
            weights[ll][n] = _full_weight(n, both)

    def core_sums(names, l):
        parts = [_parts_by_chip(n, gb[n][l]) for n in names]
        from_sibling = _sibling_other_half(parts, name="swap_grad_halves")
        return [(n, l, _add_own_half(p, f, core_index, name="sum_cores")) for n, p, f in zip(names, parts, from_sibling)]

    wanted = [(n, 0) for n in first_used]
    finish_gather(wanted, _chip_exchange([half_shards[n][ll] for n, ll in wanted], scatter=False, name="gather_weights"))
    saved = []
    h = _rms_fwd(x, vec("g_mix_pre", 0), name="rms_in")
    for l in range(L):
        w_bd, ws_cat, wst_cat, bias = consts[l]
        proj = lambda n, off, dtype, name: _matmul(h, weights[l]["w_in"], tb=True, n=n, bn=PROJ_BLOCK,
                                                   b_col_off=off // PROJ_BLOCK, out_dtypes=(dtype,), name=name)
        qkv = proj(QKV_WIDTH, 0, BF16, "proj_qkv")
        rest = proj(MIX_WIDTH, QKV_WIDTH, F32, "proj_mix")
        gates = proj(GATE_WIDTH, QKV_WIDTH + MIX_WIDTH, BF16, "proj_gates")
        wanted = [(n, l) for n in used_later] + [(n, l + 1) for n in first_used if l + 1 < L]
        o_sb, tot, first, *gathered = _sba_fwd(qkv, cargo=[half_shards[n][ll] for n, ll in wanted], name="sba_fwd_gather")
        finish_gather(wanted, gathered)
        o_pool = _pool_fwd(rest, w_bd, vec("pool_scale", l), name="pool_fwd")
        o_gm = _gm_fwd(rest, vec("gm_gain", l), ws_cat, bias, name="gm_fwd")
        branches = (_matmul(o_sb, weights[l]["w_br_sb"], out_dtypes=(BF16,), name="br_sb"),
                    _matmul(o_pool, weights[l]["w_br_pool"], out_dtypes=(BF16,), name="br_pool"),
                    _matmul(o_gm, weights[l]["w_br_gm"], out_dtypes=(BF16,), name="br_gm"))
        merged = _merge_fwd(gates, branches, name="merge_fwd")
        y = _matmul(merged, weights[l]["w_out"], name="out_proj")
        x1, h2 = _resid_rms(x, y, vec("g_mix_post", l), vec("g_ff_pre", l), name="resid_mix")
        a, r = _matmul(h2, weights[l]["w_ff_in"], out_dtypes=(BF16, BF16), epilogue=_relu2, name="ff_in")
        ff = _matmul(r, weights[l]["w_ff_out"], name="ff_out")
        g_next = vec("g_mix_pre", l + 1) if l + 1 < L else None
        x2, h_next = _resid_rms(x1, ff, vec("g_ff_post", l), g_next, name="resid_ff" if l + 1 < L else "resid_last")
        saved.append(dict(x=x, h=h, qkv=qkv, rest=rest, gates=gates, o_sb=o_sb, tot=tot, first=first, o_pool=o_pool,
                          o_gm=o_gm, branches=branches, merged=merged, y=y, x1=x1, h2=h2, a=a, r=r, ff=ff))
        x, h = x2, h_next

    dx2, loss = _loss_head(x, target, name="loss_head")
    gb = {k: [None] * L for k in ("w_in", "w_br_sb", "w_br_pool", "w_br_gm", "w_out", "w_ff_in", "w_ff_out")}
    gs = {k: [None] * L for k in ("w_pool", "pool_scale", "gm_gain", "w_spatial", "b_spatial", "g_mix_pre",
                                  "g_mix_post", "g_ff_pre", "g_ff_post")}
    d_ff, gs["g_ff_post"][L - 1] = _rms_bwd(saved[-1]["ff"], vec("g_ff_post", L - 1), dx2, name="rms_bwd_last")
    received = [{} for _ in range(L)]
    waiting = []
    for l in reversed(range(L)):
        s = saved[l]
        w_bd, ws_cat, wst_cat, bias = consts[l]
        da = _matmul(d_ff, weights[l]["w_ff_out"], tb=True, out_dtypes=(BF16,), extras=(s["a"],), epilogue=_relu2_grad,
                     name="ff_out_dx")
        gb["w_ff_out"][l] = _matmul(s["r"], d_ff, ta=True, name="ff_out_dw")
        dh2 = _matmul(da, weights[l]["w_ff_in"], tb=True, name="ff_in_dx")
        gb["w_ff_in"][l] = _matmul(s["h2"], da, ta=True, name="ff_in_dw")
        dx1, gs["g_ff_pre"][l], dy, gs["g_mix_post"][l] = _rms_bwd_chain(
            s["x1"], vec("g_ff_pre", l), dh2, dx2, s["y"], vec("g_mix_post", l), name="rms_bwd_mid")
        dmerged = _matmul(dy, weights[l]["w_out"], tb=True, out_dtypes=(BF16,), name="out_proj_dx")
        gb["w_out"][l] = _matmul(s["merged"], dy, ta=True, name="out_proj_dw")
        dg0, dg1, dg2, db_sb, db_pool, db_gm = _merge_bwd(s["gates"], s["branches"], dmerged, name="merge_bwd")
        do_sb = _matmul(db_sb, weights[l]["w_br_sb"], tb=True, out_dtypes=(BF16,), name="br_sb_dx")
        gb["w_br_sb"][l] = _matmul(s["o_sb"], db_sb, ta=True, name="br_sb_dw")
        do_pool = _matmul(db_pool, weights[l]["w_br_pool"], tb=True, name="br_pool_dx")
        gb["w_br_pool"][l] = _matmul(s["o_pool"], db_pool, ta=True, name="br_pool_dw")
        do_gm = _matmul(db_gm, weights[l]["w_br_gm"], tb=True, name="br_gm_dx")
        gb["w_br_gm"][l] = _matmul(s["o_gm"], db_gm, ta=True, name="br_gm_dw")
        waiting += core_sums(used_later, l)
        dq, dk, dv, *arrived = _sba_bwd(s["qkv"], do_sb, s["tot"], s["first"], cargo=[c for _, _, c in waiting],
                                        name="sba_bwd_exchange")
        for (n, ll, _), got in zip(waiting, arrived):
            received[ll][n] = got
        dp, dw_bd, gs["pool_scale"][l] = _pool_bwd(s["rest"], do_pool, w_bd, vec("pool_scale", l), name="pool_bwd")
        du, dgv, gs["gm_gain"][l], dws, db = _gm_bwd(s["rest"], do_gm, vec("gm_gain", l), ws_cat, wst_cat, bias,
                                                      name="gm_bwd")
        gs["w_pool"][l] = jnp.stack([dw_bd[g * 64:(g + 1) * 64, g * 64:(g + 1) * 64] for g in range(len(POOL_WINDOWS))])
        gs["w_spatial"][l] = jnp.where(jnp.tril(jnp.ones((GM_CHUNK, GM_CHUNK), dtype=bool))[None], dws, 0.0)
        gs["b_spatial"][l] = db[:, :GM_GROUPS].T
        dproj = jnp.concatenate([dq, dk, dv, dp, du, dgv, dg0, dg1, dg2], axis=1)
        dh = _matmul(dproj, weights[l]["w_in"], bk=D_IN // 3, name="proj_dx")
        gb["w_in"][l] = _matmul(dproj, s["h"], ta=True, bm=PROJ_BLOCK, name="proj_dw")
        waiting = core_sums(first_used, l)
        if l == 0:
            for (n, ll, _), got in zip(waiting, _chip_exchange([c for _, _, c in waiting], scatter=True, name="exchange_grads")):
                received[ll][n] = got
        if l > 0:
            dx2, gs["g_mix_pre"][l], d_ff, gs["g_ff_post"][l - 1] = _rms_bwd_chain(
                s["x"], vec("g_mix_pre", l), dh, dx1, saved[l - 1]["ff"], vec("g_ff_post", l - 1), name="rms_bwd_mid")
        else:
            dx2, gs["g_mix_pre"][l], _, _ = _rms_bwd_chain(s["x"], vec("g_mix_pre", l), dh, dx1, None, None,
                                                           name="rms_bwd_first")
    return loss, dx2, received, gs


TRANSPOSED = ("w_in",)
COLUMN_SHARDED = ("w_br_sb", "w_br_pool", "w_br_gm", "w_ff_in")
ROW_SHARDED = ("w_in", "w_out", "w_ff_out")
BIG_WEIGHTS = ("w_in", "w_br_sb", "w_br_pool", "w_br_gm", "w_ff_in", "w_out", "w_ff_out")
SMALL_WEIGHTS = ("w_pool", "pool_scale", "gm_gain", "w_spatial", "b_spatial", "g_mix_pre", "g_mix_post", "g_ff_pre",
                 "g_ff_post")
WEIGHT_ORDER = ("w_in", "w_pool", "pool_scale", "gm_gain", "w_spatial", "b_spatial", "w_br_sb", "w_br_pool", "w_br_gm",
                "w_out", "g_mix_pre", "g_mix_post", "g_ff_pre", "g_ff_post", "w_ff_in", "w_ff_out")


def _full_weight(name, g):
    half, cols = g.shape[2], g.shape[3]
    if name in COLUMN_SHARDED:
        return g.transpose(1, 2, 0, 3).reshape(2 * half, N_CHIPS * cols)
    return g.reshape(N_CHIPS * 2 * half, cols)


def _parts_by_chip(name, grad):
    if name in COLUMN_SHARDED:
        r, c = grad.shape[0], grad.shape[1] // N_CHIPS
        return grad.reshape(r, N_CHIPS, c).transpose(1, 0, 2)
    return grad.reshape(N_CHIPS, grad.shape[0] // N_CHIPS, grad.shape[1])


def kernel(x, w_in, w_pool, pool_scale, gm_gain, w_spatial, b_spatial, w_br_sb, w_br_pool, w_br_gm, w_out, g_mix_pre, g_mix_post, g_ff_pre, g_ff_post, w_ff_in, w_ff_out, loss_target, m_w_in, m_w_pool, m_pool_scale, m_gm_gain, m_w_spatial, m_b_spatial, m_w_br_sb, m_w_br_pool, m_w_br_gm, m_w_out, m_g_mix_pre, m_g_mix_post, m_g_ff_pre, m_g_ff_post, m_w_ff_in, m_w_ff_out, v_w_in, v_w_pool, v_pool_scale, v_gm_gain, v_w_spatial, v_b_spatial, v_w_br_sb, v_w_br_pool, v_w_br_gm, v_w_out, v_g_mix_pre, v_g_mix_post, v_g_ff_pre, v_g_ff_post, v_w_ff_in, v_w_ff_out):
    given = dict(locals())
    w = {n: given[n] for n in WEIGHT_ORDER}
    m = {n: given["m_" + n] for n in WEIGHT_ORDER}
    v = {n: given["v_" + n] for n in WEIGHT_ORDER}
    L = w_in.shape[0]

    core = lax.axis_index("c")

    def my_rows(a):
        half = a.shape[1] // 2
        return lax.dynamic_slice_in_dim(a.astype(BF16), core * half, half, axis=1)

    view = lambda n, a: jnp.swapaxes(a, 1, 2) if n in TRANSPOSED else a
    half_shards = {n: my_rows(view(n, w[n])) for n in BIG_WEIGHTS}
    small = {n: w[n] for n in SMALL_WEIGHTS}
    loss, dx, received, small_grads = _local_step(x[0], loss_target[0], half_shards, small, core)

    core_index = core.astype(jnp.int32).reshape(1)
    reduced = []
    for n in BIG_WEIGHTS:
        from_chips = jnp.concatenate([received[l][n] for l in range(L)], axis=1)
        reduced.append(_sum_slots(from_chips, name="sum_chips"))
    reduced_by_sibling = _sibling_swap(reduced, name="swap_reduced_halves")
    grads, deltas, new_m, new_v = {}, {}, {}, {}
    for n, mine, theirs in zip(BIG_WEIGHTS, reduced, reduced_by_sibling):
        by_layer = lambda a: a.reshape(L, -1, a.shape[-1])
        outs = _adamw_halves(view(n, w[n]), view(n, m[n]), view(n, v[n]), by_layer(mine), by_layer(theirs), core_index,
                             name="adamw_big")
        grads[n], deltas[n], new_m[n], new_v[n] = [view(n, o) for o in outs]

    by_layer = lambda a: a.reshape(L, -1, a.shape[-1])
    flat = lambda a: a.reshape(-1, a.shape[-1])
    outs = _small_update([[flat(g) for g in small_grads[n]] for n in SMALL_WEIGHTS],
                         *[[by_layer(t[n]) for n in SMALL_WEIGHTS] for t in (w, m, v)], name="small_update")
    for store, arrays in zip((grads, deltas, new_m, new_v), outs):
        store.update({n: a.reshape(w[n].shape) for n, a in zip(SMALL_WEIGHTS, arrays)})

    total_loss = lax.psum(loss[0, 0], ("x", "y", "c"))
    return (total_loss, dx[None], *[grads[n] for n in WEIGHT_ORDER], *[deltas[n] for n in WEIGHT_ORDER],
            *[new_m[n] for n in WEIGHT_ORDER], *[new_v[n] for n in WEIGHT_ORDER])
```

```python
import functools
import math

import jax
import jax.numpy as jnp
from jax import lax
from jax.experimental import pallas as pl
from jax.experimental.pallas import tpu as pltpu

F32 = jnp.float32
BF16 = jnp.bfloat16

D_MODEL = 1024
SB_HEADS = 8
SB_HEAD_DIM = 64
SB_WIDTH = SB_HEADS * SB_HEAD_DIM
POOL_WINDOWS = (2, 4, 8, 16)
POOL_GROUP_DIM = 64
POOL_WIDTH = 256
POOL_HALO = 16
GM_GROUPS = 4
GM_GROUP_DIM = 64
GM_WIDTH = 256
GM_CHUNK = 128
N_BRANCH = 3
D_FF = 4 * D_MODEL
RMS_EPS = 1e-6
QKV_WIDTH = 3 * SB_WIDTH
MIX_WIDTH = POOL_WIDTH + 2 * GM_WIDTH
GATE_WIDTH = N_BRANCH * D_MODEL
D_IN = QKV_WIDTH + MIX_WIDTH + GATE_WIDTH
PROJ_BLOCK = 768
LANES = 128
N_CHIPS = 4
N_DEV = 8

ADAM_LR = 0.001
ADAM_B1 = 0.9
ADAM_B2 = 0.999
ADAM_EPS = 1e-08
ADAM_WD = 0.01
ADAM_STEP = 10

VMEM_LIMIT = 56 * 1024 * 1024
MESH = pl.DeviceIdType.MESH


def _params(n_grid):
    return pltpu.CompilerParams(dimension_semantics=("arbitrary",) * n_grid, vmem_limit_bytes=VMEM_LIMIT)


def _bf(x):
    return x if x.dtype == BF16 else x.astype(BF16)


def _matmul(a, b, *, name, ta=False, tb=False, out_dtypes=(F32,), n=None, b_col_off=0, bm=1024, bn=1024, bk=2048,
            extras=(), epilogue=None):
    M, K = (a.shape[1], a.shape[0]) if ta else a.shape
    nb = b.shape[0] if tb else b.shape[1]
    n = nb if n is None else n
    bm, bn, bk = min(bm, M), min(bn, n), min(bk, K)
    assert M % bm == 0 and n % bn == 0 and K % bk == 0, (name, M, n, K, bm, bn, bk)
    assert (b.shape[1] if tb else b.shape[0]) == K, (name, a.shape, b.shape)
    nk = K // bk
    dims = (((0 if ta else 1,), (1 if tb else 0,)), ((), ()))
    n_out = len(out_dtypes)
    direct = nk > 1 and epilogue is None and out_dtypes == (F32,)
    use_acc = nk > 1 and not direct

    def body(*refs):
        a_ref, b_ref = refs[:2]
        extra_refs = refs[2:2 + len(extras)]
        out_refs = refs[2 + len(extras):2 + len(extras) + n_out]
        p = lax.dot_general(_bf(a_ref[...]), _bf(b_ref[...]), dims, preferred_element_type=F32)

        def finish(acc):
            outs = (acc,) if epilogue is None else epilogue(acc, *[r[...] for r in extra_refs])
            for r, o in zip(out_refs, outs):
                r[...] = o.astype(r.dtype)

        if nk == 1:
            finish(p)
            return
        k = pl.program_id(2)
        acc_ref = out_refs[0] if direct else refs[-1]

        @pl.when(k == 0)
        def _():
            acc_ref[...] = p

        @pl.when(k > 0)
        def _():
            acc_ref[...] += p

        if use_acc:
            @pl.when(k == nk - 1)
            def _():
                finish(acc_ref[...])

    a_spec = pl.BlockSpec((bk, bm), lambda i, j, k: (k, i)) if ta else pl.BlockSpec((bm, bk), lambda i, j, k: (i, k))
    if tb:
        b_spec = pl.BlockSpec((bn, bk), lambda i, j, k: (j + b_col_off, k))
    else:
        b_spec = pl.BlockSpec((bk, bn), lambda i, j, k: (k, j + b_col_off))
    tile = pl.BlockSpec((bm, bn), lambda i, j, k: (i, j))
    outs = pl.pallas_call(
        body, name=name, grid=(M // bm, n // bn, nk),
        in_specs=[a_spec, b_spec] + [tile] * len(extras),
        out_specs=[tile] * n_out,
        out_shape=[jax.ShapeDtypeStruct((M, n), d) for d in out_dtypes],
        scratch_shapes=[pltpu.VMEM((bm, bn), F32)] if use_acc else [],
        compiler_params=_params(3),
    )(a, b, *extras)
    return outs[0] if n_out == 1 else outs


ROW_TILE = 512
WIDE_ROW_TILE = 1024


def _rows(S, tile=ROW_TILE):
    tr = min(tile, S)
    assert S % tr == 0
    return tr


def _rstd(x):
    return lax.rsqrt(jnp.mean(x * x, axis=-1, keepdims=True) + RMS_EPS)


def _rms_bwd_math(x, g, dy):
    r = _rstd(x)
    gd = g * dy
    dx = r * gd - x * (r * r * r) * jnp.mean(x * gd, axis=-1, keepdims=True)
    dg = jnp.sum(dy * x * r, axis=0, keepdims=True)
    return dx, dg


def _accumulate(ref, value):
    i = pl.program_id(0)

    @pl.when(i == 0)
    def _():
        ref[...] = value

    @pl.when(i > 0)
    def _():
        ref[...] += value


def _row_spec(tr, width):
    return pl.BlockSpec((tr, width), lambda i: (i, 0))


def _vec_spec(width):
    return pl.BlockSpec((1, width), lambda i: (0, 0))


def _rms_fwd(x, g, *, name):
    S, D = x.shape
    tr = _rows(S, WIDE_ROW_TILE)

    def body(x_ref, g_ref, o_ref):
        xf = x_ref[...]
        o_ref[...] = (xf * _rstd(xf) * g_ref[...]).astype(o_ref.dtype)

    return pl.pallas_call(
        body, name=name, grid=(S // tr,), in_specs=[_row_spec(tr, D), _vec_spec(D)], out_specs=_row_spec(tr, D),
        out_shape=jax.ShapeDtypeStruct((S, D), BF16), compiler_params=_params(1))(x, g)


def _resid_rms(x, y, g_post, g_next, *, name):
    S, D = x.shape
    tr = _rows(S, WIDE_ROW_TILE)
    with_next = g_next is not None

    def body(*refs):
        if with_next:
            x_ref, y_ref, gp_ref, gn_ref, xo_ref, ho_ref = refs
        else:
            x_ref, y_ref, gp_ref, xo_ref = refs
        yf = y_ref[...]
        xn = x_ref[...] + yf * _rstd(yf) * gp_ref[...]
        xo_ref[...] = xn
        if with_next:
            ho_ref[...] = (xn * _rstd(xn) * gn_ref[...]).astype(ho_ref.dtype)

    row, vec = _row_spec(tr, D), _vec_spec(D)
    ins = [x, y, g_post] + ([g_next] if with_next else [])
    outs = pl.pallas_call(
        body, name=name, grid=(S // tr,), in_specs=[row, row, vec] + ([vec] if with_next else []),
        out_specs=[row] + ([row] if with_next else []),
        out_shape=[jax.ShapeDtypeStruct((S, D), F32)] + ([jax.ShapeDtypeStruct((S, D), BF16)] if with_next else []),
        compiler_params=_params(1))(*ins)
    return (outs[0], outs[1]) if with_next else (outs[0], None)


def _rms_bwd(x, g, dy, *, name):
    S, D = x.shape
    tr = _rows(S, WIDE_ROW_TILE)

    def body(x_ref, g_ref, dy_ref, dx_ref, dg_ref):
        dx, dg = _rms_bwd_math(x_ref[...], g_ref[...], dy_ref[...])
        dx_ref[...] = dx.astype(dx_ref.dtype)
        _accumulate(dg_ref, dg)

    row, vec = _row_spec(tr, D), _vec_spec(D)
    return pl.pallas_call(
        body, name=name, grid=(S // tr,), in_specs=[row, vec, row], out_specs=[row, vec],
        out_shape=[jax.ShapeDtypeStruct((S, D), BF16), jax.ShapeDtypeStruct((1, D), F32)],
        compiler_params=_params(1))(x, g, dy)


def _rms_bwd_chain(xa, ga, da, resid, xb, gb, *, name):
    S, D = xa.shape
    tr = _rows(S)
    chain = xb is not None

    def body(*refs):
        if chain:
            xa_ref, ga_ref, da_ref, rs_ref, xb_ref, gb_ref, dx_ref, dga_ref, dxb_ref, dgb_ref = refs
        else:
            xa_ref, ga_ref, da_ref, rs_ref, dx_ref, dga_ref = refs
        dxa, dga = _rms_bwd_math(xa_ref[...], ga_ref[...], da_ref[...])
        dx = rs_ref[...] + dxa
        dx_ref[...] = dx
        _accumulate(dga_ref, dga)
        if chain:
            dxb, dgb = _rms_bwd_math(xb_ref[...], gb_ref[...], dx)
            dxb_ref[...] = dxb.astype(dxb_ref.dtype)
            _accumulate(dgb_ref, dgb)

    row, vec = _row_spec(tr, D), _vec_spec(D)
    ins = [xa, ga, da, resid] + ([xb, gb] if chain else [])
    outs = pl.pallas_call(
        body, name=name, grid=(S // tr,), in_specs=[row, vec, row, row] + ([row, vec] if chain else []),
        out_specs=[row, vec] + ([row, vec] if chain else []),
        out_shape=[jax.ShapeDtypeStruct((S, D), F32), jax.ShapeDtypeStruct((1, D), F32)]
        + ([jax.ShapeDtypeStruct((S, D), BF16), jax.ShapeDtypeStruct((1, D), F32)] if chain else []),
        compiler_params=_params(1))(*ins)
    return tuple(outs) if chain else (outs[0], outs[1], None, None)


def _loss_head(y, target, *, name):
    S, D = y.shape
    tr = _rows(S, WIDE_ROW_TILE)
    n_tiles = S // tr

    def body(y_ref, t_ref, dy_ref, loss_ref, acc_ref):
        err = y_ref[...] - t_ref[...]
        dy_ref[...] = err * (1.0 / D)
        _accumulate(acc_ref, jnp.sum(err * err, axis=0, keepdims=True))

        @pl.when(pl.program_id(0) == n_tiles - 1)
        def _():
            loss_ref[...] = jnp.sum(acc_ref[...], axis=1, keepdims=True) * (0.5 / D)

    row = _row_spec(tr, D)
    return pl.pallas_call(
        body, name=name, grid=(n_tiles,), in_specs=[row, row],
        out_specs=[row, pl.BlockSpec((1, 1), lambda i: (0, 0))],
        out_shape=[jax.ShapeDtypeStruct((S, D), F32), jax.ShapeDtypeStruct((1, 1), F32)],
        scratch_shapes=[pltpu.VMEM((1, D), F32)], compiler_params=_params(1))(y, target)


SB_TILE = 256
SB_PAIRS = SB_HEADS * SB_HEAD_DIM // LANES
SB_DEAD_LOG = -110.0


def _log_sigmoids(z):
    ls = jnp.minimum(z, 0.0) - jnp.log(1.0 + jnp.exp(-jnp.abs(z)))
    return ls, ls - z


def _running_sums(x, tri2):
    hi = x.astype(BF16)
    lo = (x - hi.astype(F32)).astype(BF16)
    return jnp.dot(jnp.concatenate([hi, lo], axis=1), tri2, preferred_element_type=F32)


def _tri(T, cmp):
    j = lax.broadcasted_iota(jnp.int32, (T, T), 0)
    s = lax.broadcasted_iota(jnp.int32, (T, T), 1)
    m = jnp.where(cmp(j, s), 1.0, 0.0).astype(BF16)
    return jnp.concatenate([m, m], axis=0)


def _head_masks():
    lane = lax.broadcasted_iota(jnp.int32, (1, LANES), 1)
    return [lane < SB_HEAD_DIM, lane >= SB_HEAD_DIM]


def _cargo(refs, n_in, n_out, cargo, scatter):
    n = len(cargo)
    if not n:
        return refs, lambda first: None, lambda last: None
    ins = refs[n_in:n_in + n]
    outs = refs[n_in + n + n_out:n_in + n + n_out + n]
    sems = refs[len(refs) - 3:]
    own = refs[:n_in] + refs[n_in + n:n_in + n + n_out] + refs[n_in + n + n_out + n:len(refs) - 3]

    def start(first):
        @pl.when(first)
        def _():
            for cp in _chip_copies(ins, outs, *sems, scatter=scatter):
                cp.start()

    def finish(last):
        @pl.when(last)
        def _():
            for cp in _chip_copies(ins, outs, *sems, scatter=scatter):
                cp.wait()

    return own, start, finish


def _sba_fwd(qkv, *, name, cargo=()):
    S = qkv.shape[0]
    T = min(SB_TILE, S)
    nq = S // T
    scale = SB_HEAD_DIM ** -0.5

    def body(*refs):
        (q_ref, k_ref, v_ref, o_ref, t_ref, first_ref), start_cargo, finish_cargo = _cargo(refs, 3, 3, cargo, False)
        p, i = pl.program_id(0), pl.program_id(1)
        start_cargo(jnp.logical_and(p == 0, i == 0))
        row = lax.broadcasted_iota(jnp.int32, (T, T), 0)
        col = lax.broadcasted_iota(jnp.int32, (T, T), 1)
        strict = col < row
        after = _tri(T, lambda j, s: j > s)
        masks = _head_masks()
        q = q_ref[...] * scale
        qs = [jnp.where(hm, q, jnp.zeros_like(q)) for hm in masks]

        def walk(tiles, carry):
            values, zs = [], []
            for j, diag in tiles:
                rows = pl.ds(pl.multiple_of(j * T, T), T)
                kb = k_ref[rows, :]
                values.append(v_ref[rows, :])
                zs += [(lax.dot_general(qh, kb, (((1,), (1,)), ((), ())), preferred_element_type=F32), diag) for qh in qs]
            logs, suffixes = [], []
            for z, diag in zs:
                ls, ln = _log_sigmoids(z)
                ln = jnp.where(strict, ln, 0.0) if diag else ln
                logs.append((ls, ln))
                suffixes.append(_running_sums(ln, after))
            for t, (_, diag) in enumerate(tiles):
                out = []
                for h, (C, acc) in enumerate(carry):
                    ls, ln = logs[2 * t + h]
                    a = jnp.exp(ls + suffixes[2 * t + h] + C)
                    if diag:
                        a = jnp.where(strict, a, 0.0)
                    acc = acc + jnp.dot(a.astype(BF16), values[t], preferred_element_type=F32)
                    out.append((C + jnp.sum(ln, axis=1, keepdims=True), acc))
                carry = tuple(out)
            return carry

        fresh = (jnp.zeros((T, 1), F32), jnp.zeros((T, LANES), F32))
        carry = lax.cond(i > 0, lambda: walk([(i, True), (i - 1, False)], (fresh, fresh)),
                         lambda: walk([(i, True)], (fresh, fresh)))

        def alive(state):
            j, ((C0, _), (C1, _)) = state
            return jnp.logical_and(j >= 0, jnp.max(jnp.maximum(C0, C1)) > SB_DEAD_LOG)

        def step(state):
            j, carry = state
            return j - 1, walk([(j, False)], carry)

        j, ((C0, acc0), (C1, acc1)) = lax.while_loop(alive, step, (i - 2, carry))
        t_ref[0] = jnp.broadcast_to(C0, (T, LANES))
        t_ref[1] = jnp.broadcast_to(C1, (T, LANES))
        first_ref[...] = jnp.full((8, LANES), jnp.maximum(j + 1, 0).astype(F32))
        o_ref[...] = jnp.where(masks[0], acc0, acc1).astype(o_ref.dtype)
        finish_cargo(jnp.logical_and(p == SB_PAIRS - 1, i == nq - 1))

    kv = lambda off: pl.BlockSpec((S, LANES), lambda p, i: (0, off + p))
    n = len(cargo)
    return pl.pallas_call(
        body, name=name, grid=(SB_PAIRS, nq),
        in_specs=[pl.BlockSpec((T, LANES), lambda p, i: (i, p)), kv(SB_PAIRS), kv(2 * SB_PAIRS)] + [HBM_SPEC] * n,
        out_specs=[pl.BlockSpec((T, LANES), lambda p, i: (i, p)), pl.BlockSpec((2, T, LANES), lambda p, i: (p, i, 0)),
                   pl.BlockSpec((None, None, 8, LANES), lambda p, i: (p, i, 0, 0))] + [HBM_SPEC] * n,
        out_shape=[jax.ShapeDtypeStruct((S, SB_WIDTH), BF16), jax.ShapeDtypeStruct((SB_HEADS, S, LANES), F32),
                   jax.ShapeDtypeStruct((SB_PAIRS, nq, 8, LANES), F32)] + _chip_exchange_shapes(cargo, False),
        scratch_shapes=_chip_exchange_semaphores(n) if n else [],
        compiler_params=_params(2))(qkv, qkv, qkv, *cargo)


def _sba_bwd(qkv, do, tot, first, *, name, cargo=()):
    S = qkv.shape[0]
    T = min(SB_TILE, S)
    nq = S // T
    scale = SB_HEAD_DIM ** -0.5

    def body(*refs):
        own, start_cargo, finish_cargo = _cargo(refs, 6, 3, cargo, True)
        q_ref, k_ref, v_ref, do_ref, t_ref, first_ref, dq_ref, dk_ref, dv_ref, dk_acc, dv_acc = own
        p, i = pl.program_id(0), pl.program_id(1)
        start_cargo(jnp.logical_and(p == 0, i == 0))

        @pl.when(i == 0)
        def _():
            dk_acc[...] = jnp.zeros_like(dk_acc)
            dv_acc[...] = jnp.zeros_like(dv_acc)

        row = lax.broadcasted_iota(jnp.int32, (T, T), 0)
        col = lax.broadcasted_iota(jnp.int32, (T, T), 1)
        strict = col < row
        upto = _tri(T, lambda j, s: j <= s)
        before = _tri(T, lambda j, s: j < s)
        masks = _head_masks()
        q, do_t = q_ref[...], do_ref[...]
        q = q * scale
        qs = [jnp.where(hm, q, jnp.zeros_like(q)) for hm in masks]
        dos = [jnp.where(hm, do_t, jnp.zeros_like(do_t)) for hm in masks]
        totals = [t_ref[h][:, 0:1] for h in range(2)]
        over_lanes = (((1,), (1,)), ((), ()))
        over_queries = (((0,), (0,)), ((), ()))

        def walk(tiles, carry):
            rows = [pl.ds(pl.multiple_of(j * T, T), T) for j, _ in tiles]
            keys = [k_ref[r, :] for r in rows]
            values = [v_ref[r, :] for r in rows]
            chains = [(t, h) for t in range(len(tiles)) for h in range(2)]
            logs, da = {}, {}
            for t, h in chains:
                z = lax.dot_general(qs[h], keys[t], over_lanes, preferred_element_type=F32)
                ls, ln = _log_sigmoids(z)
                logs[t, h] = (ls, jnp.where(strict, ln, 0.0) if tiles[t][1] else ln)
                da[t, h] = lax.dot_general(dos[h], values[t], over_lanes, preferred_element_type=F32)
            upto_sums = {c: _running_sums(logs[c][1], upto) for c in chains}
            a, g = {}, {}
            P = [c[0] for c in carry]
            for t, h in chains:
                ls, ln = logs[t, h]
                a_th = jnp.exp(ls + ((totals[h] - P[h]) - upto_sums[t, h]))
                a[t, h] = jnp.where(strict, a_th, 0.0) if tiles[t][1] else a_th
                g[t, h] = a[t, h] * da[t, h]
                P[h] = P[h] + jnp.sum(ln, axis=1, keepdims=True)
            before_sums = {c: _running_sums(g[c], before) for c in chains}
            G = [c[1] for c in carry]
            dq = [c[2] for c in carry]
            dz = {}
            for t, h in chains:
                beta = jnp.exp(logs[t, h][0])
                dz_th = g[t, h] * (1.0 - beta) - (G[h] + before_sums[t, h]) * beta
                dz[t, h] = (jnp.where(strict, dz_th, 0.0) if tiles[t][1] else dz_th).astype(BF16)
                G[h] = G[h] + jnp.sum(g[t, h], axis=1, keepdims=True)
            for t, h in chains:
                dq[h] = dq[h] + jnp.dot(dz[t, h], keys[t], preferred_element_type=F32)
            for t in range(len(tiles)):
                dk_acc[rows[t], :] += sum(
                    lax.dot_general(dz[t, h], qs[h], over_queries, preferred_element_type=F32) for h in range(2))
                dv_acc[rows[t], :] += sum(
                    lax.dot_general(a[t, h].astype(BF16), dos[h], over_queries, preferred_element_type=F32)
                    for h in range(2))
            return tuple((P[h], G[h], dq[h]) for h in range(2))

        zero = jnp.zeros((T, 1), F32)
        fresh = (zero, zero, jnp.zeros((T, LANES), F32))
        last_single = jnp.maximum(i - 1, 0)
        j0 = jnp.clip(jnp.max(first_ref[...]).astype(jnp.int32), 0, last_single)
        carry = lax.fori_loop(j0, last_single, lambda j, c: walk([(j, False)], c), (fresh, fresh))
        (_, _, dq0), (_, _, dq1) = lax.cond(i > 0, lambda: walk([(i - 1, False), (i, True)], carry),
                                            lambda: walk([(i, True)], carry))
        dq_ref[...] = (jnp.where(masks[0], dq0, dq1) * scale).astype(dq_ref.dtype)

        @pl.when(i == nq - 1)
        def _():
            dk_ref[...] = dk_acc[...].astype(dk_ref.dtype)
            dv_ref[...] = dv_acc[...].astype(dv_ref.dtype)

        finish_cargo(jnp.logical_and(p == SB_PAIRS - 1, i == nq - 1))

    kv = lambda off: pl.BlockSpec((S, LANES), lambda p, i: (0, off + p))
    tile = lambda off: pl.BlockSpec((T, LANES), lambda p, i: (i, off + p))
    n = len(cargo)
    return pl.pallas_call(
        body, name=name, grid=(SB_PAIRS, nq),
        in_specs=[tile(0), kv(SB_PAIRS), kv(2 * SB_PAIRS), tile(0), pl.BlockSpec((2, T, LANES), lambda p, i: (p, i, 0)),
                  pl.BlockSpec((None, None, 8, LANES), lambda p, i: (p, i, 0, 0))] + [HBM_SPEC] * n,
        out_specs=[tile(0), kv(0), kv(0)] + [HBM_SPEC] * n,
        out_shape=[jax.ShapeDtypeStruct((S, SB_WIDTH), BF16)] * 3 + _chip_exchange_shapes(cargo, True),
        scratch_shapes=[pltpu.VMEM((S, LANES), F32), pltpu.VMEM((S, LANES), F32)]
        + (_chip_exchange_semaphores(n) if n else []),
        compiler_params=_params(2))(qkv, qkv, qkv, do, tot, first, *cargo)


POOL_TILE = 1024


def _by_group(lane, values):
    return jnp.where(lane < 64, values[0], jnp.where(lane < 128, values[1], jnp.where(lane < 192, values[2], values[3])))


def _pool_inv_count(first_row, n_rows):
    t = first_row + lax.broadcasted_iota(jnp.int32, (n_rows, POOL_WIDTH), 0)
    lane = lax.broadcasted_iota(jnp.int32, (n_rows, POOL_WIDTH), 1)
    window = _by_group(lane, POOL_WINDOWS)
    return 1.0 / jnp.clip(t + 1, 1, window).astype(F32), lane


def _pooled(ext, first_row, R):
    n = R + POOL_HALO
    s2 = ext + pltpu.roll(ext, 1, 0)
    s4 = s2 + pltpu.roll(s2, 2, 0)
    s8 = s4 + pltpu.roll(s4, 4, 0)
    s16 = s8 + pltpu.roll(s8, 8, 0)
    inv, lane = _pool_inv_count(first_row - POOL_HALO, n)
    pooled = _by_group(lane, (s2, s4, s8, s16)) * inv - ext
    return pooled[POOL_HALO:, :]


def _pool_specs(S, R, col):
    per = R // POOL_HALO
    tile = pl.BlockSpec((R, POOL_WIDTH), lambda i: (i, col))
    prev = pl.BlockSpec((POOL_HALO, POOL_WIDTH), lambda i: (jnp.maximum(i * per - 1, 0), col))
    return tile, prev


def _pool_fwd(rest, w_bd, scale, *, name):
    S = rest.shape[0]
    R = min(POOL_TILE, S)

    def body(p_ref, prev_ref, w_ref, s_ref, o_ref, ext_ref):
        i = pl.program_id(0)
        ext_ref[:POOL_HALO, :] = jnp.where(i > 0, prev_ref[...], 0.0)
        ext_ref[POOL_HALO:, :] = p_ref[...]
        pooled = _pooled(ext_ref[...], i * R, R)
        mixed = jnp.dot(pooled.astype(BF16), w_ref[...], preferred_element_type=F32)
        o_ref[...] = (mixed * s_ref[...]).astype(o_ref.dtype)

    tile, prev = _pool_specs(S, R, 0)
    return pl.pallas_call(
        body, name=name, grid=(S // R,),
        in_specs=[tile, prev, pl.BlockSpec((POOL_WIDTH, POOL_WIDTH), lambda i: (0, 0)), _vec_spec(POOL_WIDTH)],
        out_specs=_row_spec(R, POOL_WIDTH), out_shape=jax.ShapeDtypeStruct((S, POOL_WIDTH), BF16),
        scratch_shapes=[pltpu.VMEM((R + POOL_HALO, POOL_WIDTH), F32)], compiler_params=_params(1))(rest, rest, w_bd, scale)


def _pool_bwd(rest, do, w_bd, scale, *, name):
    S = rest.shape[0]
    R = min(POOL_TILE, S)
    n_tiles = S // R
    per = R // POOL_HALO
    n = R + POOL_HALO

    def body(p_ref, prev_ref, do_ref, nxt_ref, w_ref, s_ref, dp_ref, dw_ref, ds_ref, ext_ref, dext_ref):
        i = pl.program_id(0)
        ext_ref[:POOL_HALO, :] = jnp.where(i > 0, prev_ref[...], 0.0)
        ext_ref[POOL_HALO:, :] = p_ref[...]
        pooled = _pooled(ext_ref[...], i * R, R).astype(BF16)
        w = w_ref[...]
        mixed = jnp.dot(pooled, w, preferred_element_type=F32)
        do_t = do_ref[...]
        _accumulate(ds_ref, jnp.sum(do_t * mixed, axis=0, keepdims=True))
        dext_ref[:R, :] = do_t
        dext_ref[R:, :] = jnp.where(i < n_tiles - 1, nxt_ref[...], 0.0)
        dmixed = (dext_ref[...] * s_ref[...]).astype(BF16)
        dpooled = lax.dot_general(dmixed, w, (((1,), (1,)), ((), ())), preferred_element_type=F32)
        _accumulate(dw_ref, lax.dot_general(pooled, dmixed[:R, :], (((0,), (0,)), ((), ())), preferred_element_type=F32))
        inv, lane = _pool_inv_count(i * R, n)
        u = dpooled * inv
        f2 = u + pltpu.roll(u, n - 1, 0)
        f4 = f2 + pltpu.roll(f2, n - 2, 0)
        f8 = f4 + pltpu.roll(f4, n - 4, 0)
        f16 = f8 + pltpu.roll(f8, n - 8, 0)
        dp = _by_group(lane, (f2, f4, f8, f16)) - dpooled
        dp_ref[...] = dp[:R, :].astype(dp_ref.dtype)

    tile, prev = _pool_specs(S, R, 0)
    nxt = pl.BlockSpec((POOL_HALO, POOL_WIDTH), lambda i: (jnp.minimum((i + 1) * per, S // POOL_HALO - 1), 0))
    full = pl.BlockSpec((POOL_WIDTH, POOL_WIDTH), lambda i: (0, 0))
    return pl.pallas_call(
        body, name=name, grid=(n_tiles,),
        in_specs=[tile, prev, _row_spec(R, POOL_WIDTH), nxt, full, _vec_spec(POOL_WIDTH)],
        out_specs=[_row_spec(R, POOL_WIDTH), full, _vec_spec(POOL_WIDTH)],
        out_shape=[jax.ShapeDtypeStruct((S, POOL_WIDTH), BF16), jax.ShapeDtypeStruct((POOL_WIDTH, POOL_WIDTH), F32),
                   jax.ShapeDtypeStruct((1, POOL_WIDTH), F32)],
        scratch_shapes=[pltpu.VMEM((n, POOL_WIDTH), F32), pltpu.VMEM((n, POOL_WIDTH), F32)],
        compiler_params=_params(1))(rest, rest, do, do, w_bd, scale)


GM_TILE = 1024
GELU_C = math.sqrt(2.0 / math.pi)
GELU_A = 0.044715


def _gelu(x):
    return 0.5 * x * (1.0 + jnp.tanh(GELU_C * (x + GELU_A * x * x * x)))


def _gelu_and_grad(x):
    t = jnp.tanh(GELU_C * (x + GELU_A * x * x * x))
    y = 0.5 * x * (1.0 + t)
    dy = 0.5 * (1.0 + t) + 0.5 * x * (1.0 - t * t) * (GELU_C * (1.0 + 3.0 * GELU_A * x * x))
    return y, dy


def _group_lane_masks():
    lane = lax.broadcasted_iota(jnp.int32, (1, GM_WIDTH), 1)
    return [(lane >= g * GM_GROUP_DIM) & (lane < (g + 1) * GM_GROUP_DIM) for g in range(GM_GROUPS)]


def _stack_groups(x, masks):
    return jnp.concatenate([jnp.where(m, x, jnp.zeros_like(x)) for m in masks], axis=0)


def _gm_mixed(vn, ws_cat, bias, masks, R):
    chunks = []
    for c in range(R // GM_CHUNK):
        vc = vn[c * GM_CHUNK:(c + 1) * GM_CHUNK, :]
        chunks.append(jnp.dot(ws_cat, _stack_groups(vc, masks), preferred_element_type=F32) + bias)
    return jnp.concatenate(chunks, axis=0)


def _gm_specs(S, R):
    u = pl.BlockSpec((R, GM_WIDTH), lambda i: (i, 1))
    v = pl.BlockSpec((R, GM_WIDTH), lambda i: (i, 2))
    ws = pl.BlockSpec((GM_CHUNK, GM_GROUPS * GM_CHUNK), lambda i: (0, 0))
    bias = pl.BlockSpec((GM_CHUNK, GM_WIDTH), lambda i: (0, 0))
    return u, v, ws, bias


def _gm_fwd(rest, gain, ws_cat, bias, *, name):
    S = rest.shape[0]
    R = min(GM_TILE, S)

    def body(u_ref, v_ref, g_ref, ws_ref, b_ref, o_ref):
        gv = _gelu(v_ref[...])
        vn = (gv * _rstd(gv) * g_ref[...]).astype(BF16)
        mixed = _gm_mixed(vn, ws_ref[...], b_ref[...], _group_lane_masks(), R)
        o_ref[...] = (_gelu(u_ref[...]) * mixed).astype(o_ref.dtype)

    u_spec, v_spec, ws_spec, bias_spec = _gm_specs(S, R)
    return pl.pallas_call(
        body, name=name, grid=(S // R,), in_specs=[u_spec, v_spec, _vec_spec(GM_WIDTH), ws_spec, bias_spec],
        out_specs=_row_spec(R, GM_WIDTH), out_shape=jax.ShapeDtypeStruct((S, GM_WIDTH), BF16),
        compiler_params=_params(1))(rest, rest, gain, ws_cat, bias)


def _gm_bwd(rest, do, gain, ws_cat, wst_cat, bias, *, name):
    S = rest.shape[0]
    R = min(GM_TILE, S)

    def body(u_ref, v_ref, do_ref, g_ref, ws_ref, wst_ref, b_ref, du_ref, dv_ref, dg_ref, dws_ref, db_ref):
        masks = _group_lane_masks()
        gain_v = g_ref[...]
        gu, dgu = _gelu_and_grad(u_ref[...])
        gv, dgv = _gelu_and_grad(v_ref[...])
        r = _rstd(gv)
        vn = (gv * r * gain_v).astype(BF16)
        mixed = _gm_mixed(vn, ws_ref[...], b_ref[...], masks, R)
        do_t = do_ref[...]
        du_ref[...] = (do_t * mixed * dgu).astype(du_ref.dtype)
        dmix = do_t * gu
        dmix_b = dmix.astype(BF16)
        wst = wst_ref[...]
        dvn_chunks, db, dws = [], None, [None] * GM_GROUPS
        for c in range(R // GM_CHUNK):
            rows = slice(c * GM_CHUNK, (c + 1) * GM_CHUNK)
            dc, dcb, vc = dmix[rows, :], dmix_b[rows, :], vn[rows, :]
            db = dc if db is None else db + dc
            dvn_chunks.append(jnp.dot(wst, _stack_groups(dcb, masks), preferred_element_type=F32))
            for g, m in enumerate(masks):
                part = lax.dot_general(jnp.where(m, dcb, jnp.zeros_like(dcb)), vc, (((1,), (1,)), ((), ())),
                                       preferred_element_type=F32)
                dws[g] = part if dws[g] is None else dws[g] + part
        dvn = jnp.concatenate(dvn_chunks, axis=0)
        lane = lax.broadcasted_iota(jnp.int32, (1, LANES), 1)
        db_groups = jnp.zeros((GM_CHUNK, LANES), F32)
        for g, m in enumerate(masks):
            total = jnp.sum(jnp.where(m, db, 0.0), axis=1, keepdims=True)
            db_groups = db_groups + jnp.where(lane == g, total, 0.0)
        _accumulate(db_ref, db_groups)
        i = pl.program_id(0)
        for g in range(GM_GROUPS):
            @pl.when(i == 0)
            def _(g=g):
                dws_ref[g] = dws[g]

            @pl.when(i > 0)
            def _(g=g):
                dws_ref[g] += dws[g]
        _accumulate(dg_ref, jnp.sum(dvn * gv * r, axis=0, keepdims=True))
        gd = gain_v * dvn
        dgv_in = r * gd - gv * (r * r * r) * jnp.mean(gv * gd, axis=-1, keepdims=True)
        dv_ref[...] = (dgv_in * dgv).astype(dv_ref.dtype)

    u_spec, v_spec, ws_spec, bias_spec = _gm_specs(S, R)
    row, vec = _row_spec(R, GM_WIDTH), _vec_spec(GM_WIDTH)
    dws_spec = pl.BlockSpec((GM_GROUPS, GM_CHUNK, GM_CHUNK), lambda i: (0, 0, 0))
    return pl.pallas_call(
        body, name=name, grid=(S // R,), in_specs=[u_spec, v_spec, row, vec, ws_spec, ws_spec, bias_spec],
        out_specs=[row, row, vec, dws_spec, pl.BlockSpec((GM_CHUNK, LANES), lambda i: (0, 0))],
        out_shape=[jax.ShapeDtypeStruct((S, GM_WIDTH), BF16)] * 2
        + [jax.ShapeDtypeStruct((1, GM_WIDTH), F32), jax.ShapeDtypeStruct((GM_GROUPS, GM_CHUNK, GM_CHUNK), F32),
           jax.ShapeDtypeStruct((GM_CHUNK, LANES), F32)],
        compiler_params=_params(1))(rest, rest, do, gain, ws_cat, wst_cat, bias)


GATE_ROWS = 1024
GATE_COLS = 512
GATE_BLOCKS = D_MODEL // GATE_COLS


def _gate_spec(tr, k):
    return pl.BlockSpec((tr, GATE_COLS), lambda i, j: (i, GATE_BLOCKS * k + j))


def _merge_fwd(gates, branches, *, name):
    S = gates.shape[0]
    tr = min(GATE_ROWS, S)

    def body(g0, g1, g2, b0, b1, b2, o_ref):
        acc = None
        for g_ref, b_ref in ((g0, b0), (g1, b1), (g2, b2)):
            term = jax.nn.sigmoid(g_ref[...].astype(F32)) * b_ref[...].astype(F32)
            acc = term if acc is None else acc + term
        o_ref[...] = acc.astype(o_ref.dtype)

    tile = pl.BlockSpec((tr, GATE_COLS), lambda i, j: (i, j))
    return pl.pallas_call(
        body, name=name, grid=(S // tr, GATE_BLOCKS),
        in_specs=[_gate_spec(tr, k) for k in range(N_BRANCH)] + [tile] * N_BRANCH, out_specs=tile,
        out_shape=jax.ShapeDtypeStruct((S, D_MODEL), BF16), compiler_params=_params(2))(gates, gates, gates, *branches)


def _merge_bwd(gates, branches, dmerged, *, name):
    S = gates.shape[0]
    tr = min(GATE_ROWS, S)

    def body(g0, g1, g2, b0, b1, b2, dm_ref, dg0, dg1, dg2, db0, db1, db2):
        dm = dm_ref[...].astype(F32)
        for g_ref, b_ref, dg_ref, db_ref in ((g0, b0, dg0, db0), (g1, b1, dg1, db1), (g2, b2, dg2, db2)):
            s = jax.nn.sigmoid(g_ref[...].astype(F32))
            db_ref[...] = (dm * s).astype(db_ref.dtype)
            dg_ref[...] = (dm * b_ref[...].astype(F32) * s * (1.0 - s)).astype(dg_ref.dtype)

    tile = pl.BlockSpec((tr, GATE_COLS), lambda i, j: (i, j))
    return pl.pallas_call(
        body, name=name, grid=(S // tr, GATE_BLOCKS),
        in_specs=[_gate_spec(tr, k) for k in range(N_BRANCH)] + [tile] * (N_BRANCH + 1), out_specs=[tile] * (2 * N_BRANCH),
        out_shape=[jax.ShapeDtypeStruct((S, D_MODEL), BF16)] * (2 * N_BRANCH),
        compiler_params=_params(2))(gates, gates, gates, *branches, dmerged)


TILE_BYTES = 24 * 1024 * 1024


BF16_ROWS = 16


def _tile_rows(rows, cols, n_arrays):
    padded = -(-cols // LANES) * LANES
    cap = max(BF16_ROWS, TILE_BYTES // (2 * n_arrays * padded * 4))
    best = None
    for tr in range(BF16_ROWS, min(rows, cap) + 1, BF16_ROWS):
        if rows % tr == 0:
            best = tr
    assert best is not None, (rows, cols)
    return best


def _sum_slots(stack, *, name):
    n, R, C = stack.shape
    tr = _tile_rows(R, C, n + 1)

    def body(s_ref, o_ref):
        acc = s_ref[0].astype(F32)
        for k in range(1, n):
            acc = acc + s_ref[k].astype(F32)
        o_ref[...] = acc

    return pl.pallas_call(
        body, name=name, grid=(R // tr,), in_specs=[pl.BlockSpec((n, tr, C), lambda i: (0, i, 0))],
        out_specs=_row_spec(tr, C), out_shape=jax.ShapeDtypeStruct((R, C), F32), compiler_params=_params(1))(stack)


def _add_own_half(parts, received, core, *, name):
    n, R, C = parts.shape
    half = R // 2
    tr = _tile_rows(half, C, 3)
    steps = half // tr

    def body(core_ref, own_ref, got_ref, o_ref):
        o_ref[...] = (own_ref[...] + got_ref[...]).astype(o_ref.dtype)

    tile = pl.BlockSpec((None, tr, C), lambda d, i, core_ref: (d, i, 0))
    own = pl.BlockSpec((None, tr, C), lambda d, i, core_ref: (d, core_ref[0] * steps + i, 0))
    return pl.pallas_call(
        body, name=name, out_shape=jax.ShapeDtypeStruct((n, half, C), BF16),
        grid_spec=pltpu.PrefetchScalarGridSpec(num_scalar_prefetch=1, grid=(n, steps), in_specs=[own, tile], out_specs=tile),
        compiler_params=_params(2))(core, parts, received)


def _adamw_math(w, m, v, g):
    m_new = ADAM_B1 * m + (1.0 - ADAM_B1) * g
    v_new = ADAM_B2 * v + (1.0 - ADAM_B2) * jnp.square(g)
    m_hat = m_new / (1.0 - ADAM_B1 ** ADAM_STEP)
    v_hat = v_new / (1.0 - ADAM_B2 ** ADAM_STEP)
    return -ADAM_LR * (m_hat / (jnp.sqrt(v_hat) + ADAM_EPS) + ADAM_WD * w), m_new, v_new


def _adamw_halves(w, m, v, mine, theirs, core, *, name):
    L, r, C = w.shape
    tr = _tile_rows(r // 2, C, 9)
    steps = r // 2 // tr

    def body(core_ref, w_ref, m_ref, v_ref, mine_ref, theirs_ref, go_ref, d_ref, mo_ref, vo_ref):
        in_my_half = pl.program_id(1) // steps == core_ref[0]
        g = jnp.where(in_my_half, mine_ref[...], theirs_ref[...])
        go_ref[...] = g
        d_ref[...], mo_ref[...], vo_ref[...] = _adamw_math(w_ref[...], m_ref[...], v_ref[...], g)

    row = pl.BlockSpec((None, tr, C), lambda l, i, core_ref: (l, i, 0))
    half = pl.BlockSpec((None, tr, C), lambda l, i, core_ref: (l, i % steps, 0))
    return pl.pallas_call(
        body, name=name, out_shape=[jax.ShapeDtypeStruct((L, r, C), F32)] * 4,
        grid_spec=pltpu.PrefetchScalarGridSpec(
            num_scalar_prefetch=1, grid=(L, r // tr), in_specs=[row, row, row, half, half], out_specs=[row] * 4),
        compiler_params=_params(2))(core, w, m, v, mine, theirs)


HBM_SPEC = pl.BlockSpec(memory_space=pl.ANY)


def _position():
    return lax.axis_index("x"), lax.axis_index("y"), lax.axis_index("c")


def _other_chips(x, y):
    return [(1 - x, y), (x, 1 - y), (1 - x, 1 - y)]


def _chip_exchange(arrays, *, scatter, name):
    n = len(arrays)

    def body(*refs):
        copies = _chip_copies(refs[:n], refs[n:2 * n], *refs[2 * n:], scatter=scatter)
        for cp in copies:
            cp.start()
        for cp in copies:
            cp.wait()

    return pl.pallas_call(
        body, name=name, in_specs=[HBM_SPEC] * n, out_specs=[HBM_SPEC] * n, out_shape=_chip_exchange_shapes(arrays, scatter),
        scratch_shapes=_chip_exchange_semaphores(n))(*arrays)


def _chip_exchange_shapes(arrays, scatter):
    return [jax.ShapeDtypeStruct(a.shape if scatter else (N_CHIPS, 2) + a.shape, a.dtype) for a in arrays]


def _chip_exchange_semaphores(n):
    return [pltpu.SemaphoreType.DMA((3 * n,)), pltpu.SemaphoreType.DMA((3 * n,)), pltpu.SemaphoreType.DMA((n,))]


def _chip_copies(ins, outs, send_sems, recv_sems, local_sems, *, scatter):
    x, y, c = _position()
    me = 2 * x + y
    copies = []
    for a in range(len(ins)):
        own = ins[a].at[me] if scatter else ins[a]
        slot = outs[a].at[me] if scatter else outs[a].at[me, c]
        copies.append(pltpu.make_async_copy(own, slot, local_sems.at[a]))
        for k, (px, py) in enumerate(_other_chips(x, y)):
            src = ins[a].at[2 * px + py] if scatter else ins[a]
            copies.append(pltpu.make_async_remote_copy(
                src_ref=src, dst_ref=slot, send_sem=send_sems.at[3 * a + k],
                recv_sem=recv_sems.at[3 * a + k], device_id=(px, py, c), device_id_type=MESH))
    return copies


def _sibling_fill(arrays, *, name):
    n = len(arrays)

    def body(*refs):
        ins, outs = refs[:n], refs[n:2 * n]
        send_sems, recv_sems = refs[2 * n:]
        x, y, c = _position()
        copies = []
        for a in range(n):
            cp = pltpu.make_async_remote_copy(
                src_ref=ins[a].at[:, c], dst_ref=outs[a].at[:, c], send_sem=send_sems.at[a], recv_sem=recv_sems.at[a],
                device_id=(x, y, 1 - c), device_id_type=MESH)
            cp.start()
            copies.append(cp)
        for cp in copies:
            cp.wait()

    return pl.pallas_call(
        body, name=name, in_specs=[HBM_SPEC] * n, out_specs=[HBM_SPEC] * n,
        out_shape=[jax.ShapeDtypeStruct(a.shape, a.dtype) for a in arrays],
        input_output_aliases={a: a for a in range(n)},
        scratch_shapes=[pltpu.SemaphoreType.DMA((n,)), pltpu.SemaphoreType.DMA((n,))],
    )(*arrays)


def _sibling_swap(arrays, *, name):
    n = len(arrays)

    def body(*refs):
        ins, outs = refs[:n], refs[n:2 * n]
        send_sems, recv_sems = refs[2 * n:]
        x, y, c = _position()
        copies = []
        for a in range(n):
            cp = pltpu.make_async_remote_copy(
                src_ref=ins[a], dst_ref=outs[a], send_sem=send_sems.at[a], recv_sem=recv_sems.at[a],
                device_id=(x, y, 1 - c), device_id_type=MESH)
            cp.start()
            copies.append(cp)
        for cp in copies:
            cp.wait()

    return pl.pallas_call(
        body, name=name, in_specs=[HBM_SPEC] * n, out_specs=[HBM_SPEC] * n,
        out_shape=[jax.ShapeDtypeStruct(a.shape, a.dtype) for a in arrays],
        scratch_shapes=[pltpu.SemaphoreType.DMA((n,)), pltpu.SemaphoreType.DMA((n,))],
    )(*arrays)


def _sibling_other_half(arrays, *, name):
    n = len(arrays)

    def body(*refs):
        ins, outs = refs[:n], refs[n:2 * n]
        send_sems, recv_sems = refs[2 * n:]
        x, y, c = _position()
        copies = []
        for a in range(n):
            half = ins[a].shape[1] // 2
            theirs = ins[a].at[:, pl.ds(pl.multiple_of((1 - c) * half, BF16_ROWS), half), :]
            cp = pltpu.make_async_remote_copy(
                src_ref=theirs, dst_ref=outs[a], send_sem=send_sems.at[a], recv_sem=recv_sems.at[a],
                device_id=(x, y, 1 - c), device_id_type=MESH)
            cp.start()
            copies.append(cp)
        for cp in copies:
            cp.wait()

    return pl.pallas_call(
        body, name=name, in_specs=[HBM_SPEC] * n, out_specs=[HBM_SPEC] * n,
        out_shape=[jax.ShapeDtypeStruct((a.shape[0], a.shape[1] // 2, a.shape[2]), a.dtype) for a in arrays],
        scratch_shapes=[pltpu.SemaphoreType.DMA((n,)), pltpu.SemaphoreType.DMA((n,))],
    )(*arrays)


def _small_update(grads, ws, ms, vs, *, name):
    n, L = len(ws), ws[0].shape[0]
    pieces = [g for per_layer in grads for g in per_layer]
    np_ = len(pieces)

    def body(*refs):
        g_in, refs = refs[:np_], refs[np_:]
        w_in, m_in, v_in, g_out, d_out, m_out, v_out = (refs[k * n:(k + 1) * n] for k in range(7))
        from_sibling, chip_sums = refs[7 * n:7 * n + np_], refs[7 * n + np_:7 * n + 2 * np_]
        sibling_send, sibling_recv, chip_send, chip_recv = refs[7 * n + 2 * np_:]
        x, y, c = _position()
        me = 2 * x + y
        swaps = [pltpu.make_async_remote_copy(
            src_ref=g_in[p], dst_ref=from_sibling[p], send_sem=sibling_send.at[p], recv_sem=sibling_recv.at[p],
            device_id=(x, y, 1 - c), device_id_type=MESH) for p in range(np_)]
        for cp in swaps:
            cp.start()
        for cp in swaps:
            cp.wait()
        for p in range(np_):
            chip_sums[p][me] = g_in[p][...] + from_sibling[p][...]
        sends = [pltpu.make_async_remote_copy(
            src_ref=chip_sums[p].at[me], dst_ref=chip_sums[p].at[me], send_sem=chip_send.at[3 * p + k],
            recv_sem=chip_recv.at[3 * p + k], device_id=(px, py, c), device_id_type=MESH)
            for p in range(np_) for k, (px, py) in enumerate(_other_chips(x, y))]
        for cp in sends:
            cp.start()
        for cp in sends:
            cp.wait()
        for a in range(n):
            for l in range(L):
                sums = chip_sums[a * L + l]
                g = sums[0]
                for s in range(1, N_CHIPS):
                    g = g + sums[s]
                g_out[a][l] = g
                d_out[a][l], m_out[a][l], v_out[a][l] = _adamw_math(w_in[a][l], m_in[a][l], v_in[a][l], g)

    vmem = pl.BlockSpec(memory_space=pltpu.VMEM)
    shapes = [jax.ShapeDtypeStruct(w.shape, F32) for w in ws]
    outs = pl.pallas_call(
        body, name=name, in_specs=[vmem] * (np_ + 3 * n), out_specs=[vmem] * (4 * n), out_shape=shapes * 4,
        scratch_shapes=[pltpu.VMEM(g.shape, F32) for g in pieces] + [pltpu.VMEM((N_CHIPS,) + g.shape, F32) for g in pieces]
        + [pltpu.SemaphoreType.DMA((np_,)), pltpu.SemaphoreType.DMA((np_,)),
           pltpu.SemaphoreType.DMA((3 * np_,)), pltpu.SemaphoreType.DMA((3 * np_,))],
        compiler_params=pltpu.CompilerParams(vmem_limit_bytes=VMEM_LIMIT),
    )(*pieces, *ws, *ms, *vs)
    return outs[:n], outs[n:2 * n], outs[2 * n:3 * n], outs[3 * n:]


def _relu2(p):
    return p, jnp.square(jnp.maximum(p, 0.0))


def _relu2_grad(p, a):
    return (p * (2.0 * jnp.maximum(a.astype(F32), 0.0)),)


def _mixer_constants(w_pool, w_spatial, b_spatial):
    eye = jnp.eye(len(POOL_WINDOWS), dtype=F32)
    w_bd = (eye[:, None, :, None] * w_pool[:, :, None, :]).reshape(POOL_WIDTH, POOL_WIDTH).astype(BF16)
    causal = jnp.tril(jnp.ones((GM_CHUNK, GM_CHUNK), dtype=bool))
    ws = jnp.where(causal[None], w_spatial, 0.0).astype(BF16)
    ws_cat = ws.transpose(1, 0, 2).reshape(GM_CHUNK, GM_GROUPS * GM_CHUNK)
    wst_cat = ws.transpose(2, 0, 1).reshape(GM_CHUNK, GM_GROUPS * GM_CHUNK)
    bias = jnp.repeat(b_spatial.T, GM_GROUP_DIM, axis=1)
    return w_bd, ws_cat, wst_cat, bias


def _local_step(x, target, half_shards, small, core):
    L = small["g_mix_pre"].shape[0]
    vec = lambda name, l: small[name][l][None, :]
    consts = [_mixer_constants(small["w_pool"][l], small["w_spatial"][l], small["b_spatial"][l]) for l in range(L)]
    core_index = core.astype(jnp.int32).reshape(1)
    first_used, used_later = BIG_WEIGHTS[:1], BIG_WEIGHTS[1:]
    weights = [{} for _ in range(L)]

    def finish_gather(wanted, gathered):
        for (n, ll), both in zip(wanted, _sibling_fill(gathered, name="swap_weight_halves")):
            weights[ll][n] = _full_weight(n, both)

    def core_sums(names, l):
        parts = [_parts_by_chip(n, gb[n][l]) for n in names]
        from_sibling = _sibling_other_half(parts, name="swap_grad_halves")
        return [(n, l, _add_own_half(p, f, core_index, name="sum_cores")) for n, p, f in zip(names, parts, from_sibling)]

    wanted = [(n, 0) for n in first_used]
    finish_gather(wanted, _chip_exchange([half_shards[n][ll] for n, ll in wanted], scatter=False, name="gather_weights"))
    saved = []
    h = _rms_fwd(x, vec("g_mix_pre", 0), name="rms_in")
    for l in range(L):
        w_bd, ws_cat, wst_cat, bias = consts[l]
        proj = lambda n, off, dtype, name: _matmul(h, weights[l]["w_in"], tb=True, n=n, bn=PROJ_BLOCK,
                                                   b_col_off=off // PROJ_BLOCK, out_dtypes=(dtype,), name=name)
        qkv = proj(QKV_WIDTH, 0, BF16, "proj_qkv")
        rest = proj(MIX_WIDTH, QKV_WIDTH, F32, "proj_mix")
        gates = proj(GATE_WIDTH, QKV_WIDTH + MIX_WIDTH, BF16, "proj_gates")
        wanted = [(n, l) for n in used_later] + [(n, l + 1) for n in first_used if l + 1 < L]
        o_sb, tot, first, *gathered = _sba_fwd(qkv, cargo=[half_shards[n][ll] for n, ll in wanted], name="sba_fwd_gather")
        finish_gather(wanted, gathered)
        o_pool = _pool_fwd(rest, w_bd, vec("pool_scale", l), name="pool_fwd")
        o_gm = _gm_fwd(rest, vec("gm_gain", l), ws_cat, bias, name="gm_fwd")
        branches = (_matmul(o_sb, weights[l]["w_br_sb"], out_dtypes=(BF16,), name="br_sb"),
                    _matmul(o_pool, weights[l]["w_br_pool"], out_dtypes=(BF16,), name="br_pool"),
                    _matmul(o_gm, weights[l]["w_br_gm"], out_dtypes=(BF16,), name="br_gm"))
        merged = _merge_fwd(gates, branches, name="merge_fwd")
        y = _matmul(merged, weights[l]["w_out"], name="out_proj")
        x1, h2 = _resid_rms(x, y, vec("g_mix_post", l), vec("g_ff_pre", l), name="resid_mix")
        a, r = _matmul(h2, weights[l]["w_ff_in"], out_dtypes=(BF16, BF16), epilogue=_relu2, name="ff_in")
        ff = _matmul(r, weights[l]["w_ff_out"], name="ff_out")
        g_next = vec("g_mix_pre", l + 1) if l + 1 < L else None
        x2, h_next = _resid_rms(x1, ff, vec("g_ff_post", l), g_next, name="resid_ff" if l + 1 < L else "resid_last")
        saved.append(dict(x=x, h=h, qkv=qkv, rest=rest, gates=gates, o_sb=o_sb, tot=tot, first=first, o_pool=o_pool,
                          o_gm=o_gm, branches=branches, merged=merged, y=y, x1=x1, h2=h2, a=a, r=r, ff=ff))
        x, h = x2, h_next

    dx2, loss = _loss_head(x, target, name="loss_head")
    gb = {k: [None] * L for k in ("w_in", "w_br_sb", "w_br_pool", "w_br_gm", "w_out", "w_ff_in", "w_ff_out")}
    gs = {k: [None] * L for k in ("w_pool", "pool_scale", "gm_gain", "w_spatial", "b_spatial", "g_mix_pre",
                                  "g_mix_post", "g_ff_pre", "g_ff_post")}
    d_ff, gs["g_ff_post"][L - 1] = _rms_bwd(saved[-1]["ff"], vec("g_ff_post", L - 1), dx2, name="rms_bwd_last")
    received = [{} for _ in range(L)]
    waiting = []
    for l in reversed(range(L)):
        s = saved[l]
        w_bd, ws_cat, wst_cat, bias = consts[l]
        da = _matmul(d_ff, weights[l]["w_ff_out"], tb=True, out_dtypes=(BF16,), extras=(s["a"],), epilogue=_relu2_grad,
                     name="ff_out_dx")
        gb["w_ff_out"][l] = _matmul(s["r"], d_ff, ta=True, name="ff_out_dw")
        dh2 = _matmul(da, weights[l]["w_ff_in"], tb=True, name="ff_in_dx")
        gb["w_ff_in"][l] = _matmul(s["h2"], da, ta=True, name="ff_in_dw")
        dx1, gs["g_ff_pre"][l], dy, gs["g_mix_post"][l] = _rms_bwd_chain(
            s["x1"], vec("g_ff_pre", l), dh2, dx2, s["y"], vec("g_mix_post", l), name="rms_bwd_mid")
        dmerged = _matmul(dy, weights[l]["w_out"], tb=True, out_dtypes=(BF16,), name="out_proj_dx")
        gb["w_out"][l] = _matmul(s["merged"], dy, ta=True, name="out_proj_dw")
        dg0, dg1, dg2, db_sb, db_pool, db_gm = _merge_bwd(s["gates"], s["branches"], dmerged, name="merge_bwd")
        do_sb = _matmul(db_sb, weights[l]["w_br_sb"], tb=True, out_dtypes=(BF16,), name="br_sb_dx")
        gb["w_br_sb"][l] = _matmul(s["o_sb"], db_sb, ta=True, name="br_sb_dw")
        do_pool = _matmul(db_pool, weights[l]["w_br_pool"], tb=True, name="br_pool_dx")
        gb["w_br_pool"][l] = _matmul(s["o_pool"], db_pool, ta=True, name="br_pool_dw")
        do_gm = _matmul(db_gm, weights[l]["w_br_gm"], tb=True, name="br_gm_dx")
        gb["w_br_gm"][l] = _matmul(s["o_gm"], db_gm, ta=True, name="br_gm_dw")
        waiting += core_sums(used_later, l)
        dq, dk, dv, *arrived = _sba_bwd(s["qkv"], do_sb, s["tot"], s["first"], cargo=[c for _, _, c in waiting],
                                        name="sba_bwd_exchange")
        for (n, ll, _), got in zip(waiting, arrived):
            received[ll][n] = got
        dp, dw_bd, gs["pool_scale"][l] = _pool_bwd(s["rest"], do_pool, w_bd, vec("pool_scale", l), name="pool_bwd")
        du, dgv, gs["gm_gain"][l], dws, db = _gm_bwd(s["rest"], do_gm, vec("gm_gain", l), ws_cat, wst_cat, bias,
                                                      name="gm_bwd")
        gs["w_pool"][l] = jnp.stack([dw_bd[g * 64:(g + 1) * 64, g * 64:(g + 1) * 64] for g in range(len(POOL_WINDOWS))])
        gs["w_spatial"][l] = jnp.where(jnp.tril(jnp.ones((GM_CHUNK, GM_CHUNK), dtype=bool))[None], dws, 0.0)
        gs["b_spatial"][l] = db[:, :GM_GROUPS].T
        dproj = jnp.concatenate([dq, dk, dv, dp, du, dgv, dg0, dg1, dg2], axis=1)
        dh = _matmul(dproj, weights[l]["w_in"], bk=D_IN // 3, name="proj_dx")
        gb["w_in"][l] = _matmul(dproj, s["h"], ta=True, bm=PROJ_BLOCK, name="proj_dw")
        waiting = core_sums(first_used, l)
        if l == 0:
            for (n, ll, _), got in zip(waiting, _chip_exchange([c for _, _, c in waiting], scatter=True, name="exchange_grads")):
                received[ll][n] = got
        if l > 0:
            dx2, gs["g_mix_pre"][l], d_ff, gs["g_ff_post"][l - 1] = _rms_bwd_chain(
                s["x"], vec("g_mix_pre", l), dh, dx1, saved[l - 1]["ff"], vec("g_ff_post", l - 1), name="rms_bwd_mid")
        else:
            dx2, gs["g_mix_pre"][l], _, _ = _rms_bwd_chain(s["x"], vec("g_mix_pre", l), dh, dx1, None, None,
                                                           name="rms_bwd_first")
    return loss, dx2, received, gs


TRANSPOSED = ("w_in",)
COLUMN_SHARDED = ("w_br_sb", "w_br_pool", "w_br_gm", "w_ff_in")
ROW_SHARDED = ("w_in", "w_out", "w_ff_out")
BIG_WEIGHTS = ("w_in", "w_br_sb", "w_br_pool", "w_br_gm", "w_ff_in", "w_out", "w_ff_out")
SMALL_WEIGHTS = ("w_pool", "pool_scale", "gm_gain", "w_spatial", "b_spatial", "g_mix_pre", "g_mix_post", "g_ff_pre",
                 "g_ff_post")
WEIGHT_ORDER = ("w_in", "w_pool", "pool_scale", "gm_gain", "w_spatial", "b_spatial", "w_br_sb", "w_br_pool", "w_br_gm",
                "w_out", "g_mix_pre", "g_mix_post", "g_ff_pre", "g_ff_post", "w_ff_in", "w_ff_out")


def _full_weight(name, g):
    half, cols = g.shape[2], g.shape[3]
    if name in COLUMN_SHARDED:
        return g.transpose(1, 2, 0, 3).reshape(2 * half, N_CHIPS * cols)
    return g.reshape(N_CHIPS * 2 * half, cols)


def _parts_by_chip(name, grad):
    if name in COLUMN_SHARDED:
        r, c = grad.shape[0], grad.shape[1] // N_CHIPS
        return grad.reshape(r, N_CHIPS, c).transpose(1, 0, 2)
    return grad.reshape(N_CHIPS, grad.shape[0] // N_CHIPS, grad.shape[1])


def kernel(x, w_in, w_pool, pool_scale, gm_gain, w_spatial, b_spatial, w_br_sb, w_br_pool, w_br_gm, w_out, g_mix_pre, g_mix_post, g_ff_pre, g_ff_post, w_ff_in, w_ff_out, loss_target, m_w_in, m_w_pool, m_pool_scale, m_gm_gain, m_w_spatial, m_b_spatial, m_w_br_sb, m_w_br_pool, m_w_br_gm, m_w_out, m_g_mix_pre, m_g_mix_post, m_g_ff_pre, m_g_ff_post, m_w_ff_in, m_w_ff_out, v_w_in, v_w_pool, v_pool_scale, v_gm_gain, v_w_spatial, v_b_spatial, v_w_br_sb, v_w_br_pool, v_w_br_gm, v_w_out, v_g_mix_pre, v_g_mix_post, v_g_ff_pre, v_g_ff_post, v_w_ff_in, v_w_ff_out):
    given = dict(locals())
    w = {n: given[n] for n in WEIGHT_ORDER}
    m = {n: given["m_" + n] for n in WEIGHT_ORDER}
    v = {n: given["v_" + n] for n in WEIGHT_ORDER}
    L = w_in.shape[0]

    core = lax.axis_index("c")

    def my_rows(a):
        half = a.shape[1] // 2
        return lax.dynamic_slice_in_dim(a.astype(BF16), core * half, half, axis=1)

    view = lambda n, a: jnp.swapaxes(a, 1, 2) if n in TRANSPOSED else a
    half_shards = {n: my_rows(view(n, w[n])) for n in BIG_WEIGHTS}
    small = {n: w[n] for n in SMALL_WEIGHTS}
    loss, dx, received, small_grads = _local_step(x[0], loss_target[0], half_shards, small, core)

    core_index = core.astype(jnp.int32).reshape(1)
    reduced = []
    for n in BIG_WEIGHTS:
        from_chips = jnp.concatenate([received[l][n] for l in range(L)], axis=1)
        reduced.append(_sum_slots(from_chips, name="sum_chips"))
    reduced_by_sibling = _sibling_swap(reduced, name="swap_reduced_halves")
    grads, deltas, new_m, new_v = {}, {}, {}, {}
    for n, mine, theirs in zip(BIG_WEIGHTS, reduced, reduced_by_sibling):
        by_layer = lambda a: a.reshape(L, -1, a.shape[-1])
        outs = _adamw_halves(view(n, w[n]), view(n, m[n]), view(n, v[n]), by_layer(mine), by_layer(theirs), core_index,
                             name="adamw_big")
        grads[n], deltas[n], new_m[n], new_v[n] = [view(n, o) for o in outs]

    by_layer = lambda a: a.reshape(L, -1, a.shape[-1])
    flat = lambda a: a.reshape(-1, a.shape[-1])
    outs = _small_update([[flat(g) for g in small_grads[n]] for n in SMALL_WEIGHTS],
                         *[[by_layer(t[n]) for n in SMALL_WEIGHTS] for t in (w, m, v)], name="small_update")
    for store, arrays in zip((grads, deltas, new_m, new_v), outs):
        store.update({n: a.reshape(w[n].shape) for n, a in zip(SMALL_WEIGHTS, arrays)})

    total_loss = lax.psum(loss[0, 0], ("x", "y", "c"))
    return (total_loss, dx[None], *[grads[n] for n in WEIGHT_ORDER], *[deltas[n] for n in WEIGHT_ORDER],
            *[new_m[n] for n in WEIGHT_ORDER], *[new_v[n] for n in WEIGHT_ORDER])
```

```python
import functools
import math

import jax
import jax.numpy as jnp
from jax import lax
from jax.experimental import pallas as pl
from jax.experimental.pallas import tpu as pltpu

F32 = jnp.float32
BF16 = jnp.bfloat16

D_MODEL = 1024
SB_HEADS = 8
SB_HEAD_DIM = 64
SB_WIDTH = SB_HEADS * SB_HEAD_DIM
POOL_WINDOWS = (2, 4, 8, 16)
POOL_GROUP_DIM = 64
POOL_WIDTH = 256
POOL_HALO = 16
GM_GROUPS = 4
GM_GROUP_DIM = 64
GM_WIDTH = 256
GM_CHUNK = 128
N_BRANCH = 3
D_FF = 4 * D_MODEL
RMS_EPS = 1e-6
QKV_WIDTH = 3 * SB_WIDTH
MIX_WIDTH = POOL_WIDTH + 2 * GM_WIDTH
GATE_WIDTH = N_BRANCH * D_MODEL
D_IN = QKV_WIDTH + MIX_WIDTH + GATE_WIDTH
PROJ_BLOCK = 768
LANES = 128
N_CHIPS = 4
N_DEV = 8

ADAM_LR = 0.001
ADAM_B1 = 0.9
ADAM_B2 = 0.999
ADAM_EPS = 1e-08
ADAM_WD = 0.01
ADAM_STEP = 10

VMEM_LIMIT = 56 * 1024 * 1024
MESH = pl.DeviceIdType.MESH


def _params(n_grid):
    return pltpu.CompilerParams(dimension_semantics=("arbitrary",) * n_grid, vmem_limit_bytes=VMEM_LIMIT)


def _bf(x):
    return x if x.dtype == BF16 else x.astype(BF16)


def _matmul(a, b, *, name, ta=False, tb=False, out_dtypes=(F32,), n=None, b_col_off=0, bm=1024, bn=1024, bk=2048,
            extras=(), epilogue=None):
    M, K = (a.shape[1], a.shape[0]) if ta else a.shape
    nb = b.shape[0] if tb else b.shape[1]
    n = nb if n is None else n
    bm, bn, bk = min(bm, M), min(bn, n), min(bk, K)
    assert M % bm == 0 and n % bn == 0 and K % bk == 0, (name, M, n, K, bm, bn, bk)
    assert (b.shape[1] if tb else b.shape[0]) == K, (name, a.shape, b.shape)
    nk = K // bk
    dims = (((0 if ta else 1,), (1 if tb else 0,)), ((), ()))
    n_out = len(out_dtypes)
    direct = nk > 1 and epilogue is None and out_dtypes == (F32,)
    use_acc = nk > 1 and not direct

    def body(*refs):
        a_ref, b_ref = refs[:2]
        extra_refs = refs[2:2 + len(extras)]
        out_refs = refs[2 + len(extras):2 + len(extras) + n_out]
        p = lax.dot_general(_bf(a_ref[...]), _bf(b_ref[...]), dims, preferred_element_type=F32)

        def finish(acc):
            outs = (acc,) if epilogue is None else epilogue(acc, *[r[...] for r in extra_refs])
            for r, o in zip(out_refs, outs):
                r[...] = o.astype(r.dtype)

        if nk == 1:
            finish(p)
            return
        k = pl.program_id(2)
        acc_ref = out_refs[0] if direct else refs[-1]

        @pl.when(k == 0)
        def _():
            acc_ref[...] = p

        @pl.when(k > 0)
        def _():
            acc_ref[...] += p

        if use_acc:
            @pl.when(k == nk - 1)
            def _():
                finish(acc_ref[...])

    a_spec = pl.BlockSpec((bk, bm), lambda i, j, k: (k, i)) if ta else pl.BlockSpec((bm, bk), lambda i, j, k: (i, k))
    if tb:
        b_spec = pl.BlockSpec((bn, bk), lambda i, j, k: (j + b_col_off, k))
    else:
        b_spec = pl.BlockSpec((bk, bn), lambda i, j, k: (k, j + b_col_off))
    tile = pl.BlockSpec((bm, bn), lambda i, j, k: (i, j))
    outs = pl.pallas_call(
        body, name=name, grid=(M // bm, n // bn, nk),
        in_specs=[a_spec, b_spec] + [tile] * len(extras),
        out_specs=[tile] * n_out,
        out_shape=[jax.ShapeDtypeStruct((M, n), d) for d in out_dtypes],
        scratch_shapes=[pltpu.VMEM((bm, bn), F32)] if use_acc else [],
        compiler_params=_params(3),
    )(a, b, *extras)
    return outs[0] if n_out == 1 else outs


ROW_TILE = 512
WIDE_ROW_TILE = 1024


def _rows(S, tile=ROW_TILE):
    tr = min(tile, S)
    assert S % tr == 0
    return tr


def _rstd(x):
    return lax.rsqrt(jnp.mean(x * x, axis=-1, keepdims=True) + RMS_EPS)


def _rms_bwd_math(x, g, dy):
    r = _rstd(x)
    gd = g * dy
    dx = r * gd - x * (r * r * r) * jnp.mean(x * gd, axis=-1, keepdims=True)
    dg = jnp.sum(dy * x * r, axis=0, keepdims=True)
    return dx, dg


def _accumulate(ref, value):
    i = pl.program_id(0)

    @pl.when(i == 0)
    def _():
        ref[...] = value

    @pl.when(i > 0)
    def _():
        ref[...] += value


def _row_spec(tr, width):
    return pl.BlockSpec((tr, width), lambda i: (i, 0))


def _vec_spec(width):
    return pl.BlockSpec((1, width), lambda i: (0, 0))


def _rms_fwd(x, g, *, name):
    S, D = x.shape
    tr = _rows(S, WIDE_ROW_TILE)

    def body(x_ref, g_ref, o_ref):
        xf = x_ref[...]
        o_ref[...] = (xf * _rstd(xf) * g_ref[...]).astype(o_ref.dtype)

    return pl.pallas_call(
        body, name=name, grid=(S // tr,), in_specs=[_row_spec(tr, D), _vec_spec(D)], out_specs=_row_spec(tr, D),
        out_shape=jax.ShapeDtypeStruct((S, D), BF16), compiler_params=_params(1))(x, g)


def _resid_rms(x, y, g_post, g_next, *, name):
    S, D = x.shape
    tr = _rows(S, WIDE_ROW_TILE)
    with_next = g_next is not None

    def body(*refs):
        if with_next:
            x_ref, y_ref, gp_ref, gn_ref, xo_ref, ho_ref = refs
        else:
            x_ref, y_ref, gp_ref, xo_ref = refs
        yf = y_ref[...]
        xn = x_ref[...] + yf * _rstd(yf) * gp_ref[...]
        xo_ref[...] = xn
        if with_next:
            ho_ref[...] = (xn * _rstd(xn) * gn_ref[...]).astype(ho_ref.dtype)

    row, vec = _row_spec(tr, D), _vec_spec(D)
    ins = [x, y, g_post] + ([g_next] if with_next else [])
    outs = pl.pallas_call(
        body, name=name, grid=(S // tr,), in_specs=[row, row, vec] + ([vec] if with_next else []),
        out_specs=[row] + ([row] if with_next else []),
        out_shape=[jax.ShapeDtypeStruct((S, D), F32)] + ([jax.ShapeDtypeStruct((S, D), BF16)] if with_next else []),
        compiler_params=_params(1))(*ins)
    return (outs[0], outs[1]) if with_next else (outs[0], None)


def _rms_bwd(x, g, dy, *, name):
    S, D = x.shape
    tr = _rows(S, WIDE_ROW_TILE)

    def body(x_ref, g_ref, dy_ref, dx_ref, dg_ref):
        dx, dg = _rms_bwd_math(x_ref[...], g_ref[...], dy_ref[...])
        dx_ref[...] = dx.astype(dx_ref.dtype)
        _accumulate(dg_ref, dg)

    row, vec = _row_spec(tr, D), _vec_spec(D)
    return pl.pallas_call(
        body, name=name, grid=(S // tr,), in_specs=[row, vec, row], out_specs=[row, vec],
        out_shape=[jax.ShapeDtypeStruct((S, D), BF16), jax.ShapeDtypeStruct((1, D), F32)],
        compiler_params=_params(1))(x, g, dy)


def _rms_bwd_chain(xa, ga, da, resid, xb, gb, *, name):
    S, D = xa.shape
    tr = _rows(S)
    chain = xb is not None

    def body(*refs):
        if chain:
            xa_ref, ga_ref, da_ref, rs_ref, xb_ref, gb_ref, dx_ref, dga_ref, dxb_ref, dgb_ref = refs
        else:
            xa_ref, ga_ref, da_ref, rs_ref, dx_ref, dga_ref = refs
        dxa, dga = _rms_bwd_math(xa_ref[...], ga_ref[...], da_ref[...])
        dx = rs_ref[...] + dxa
        dx_ref[...] = dx
        _accumulate(dga_ref, dga)
        if chain:
            dxb, dgb = _rms_bwd_math(xb_ref[...], gb_ref[...], dx)
            dxb_ref[...] = dxb.astype(dxb_ref.dtype)
            _accumulate(dgb_ref, dgb)

    row, vec = _row_spec(tr, D), _vec_spec(D)
    ins = [xa, ga, da, resid] + ([xb, gb] if chain else [])
    outs = pl.pallas_call(
        body, name=name, grid=(S // tr,), in_specs=[row, vec, row, row] + ([row, vec] if chain else []),
        out_specs=[row, vec] + ([row, vec] if chain else []),
        out_shape=[jax.ShapeDtypeStruct((S, D), F32), jax.ShapeDtypeStruct((1, D), F32)]
        + ([jax.ShapeDtypeStruct((S, D), BF16), jax.ShapeDtypeStruct((1, D), F32)] if chain else []),
        compiler_params=_params(1))(*ins)
    return tuple(outs) if chain else (outs[0], outs[1], None, None)


def _loss_head(y, target, *, name):
    S, D = y.shape
    tr = _rows(S, WIDE_ROW_TILE)
    n_tiles = S // tr

    def body(y_ref, t_ref, dy_ref, loss_ref, acc_ref):
        err = y_ref[...] - t_ref[...]
        dy_ref[...] = err * (1.0 / D)
        _accumulate(acc_ref, jnp.sum(err * err, axis=0, keepdims=True))

        @pl.when(pl.program_id(0) == n_tiles - 1)
        def _():
            loss_ref[...] = jnp.sum(acc_ref[...], axis=1, keepdims=True) * (0.5 / D)

    row = _row_spec(tr, D)
    return pl.pallas_call(
        body, name=name, grid=(n_tiles,), in_specs=[row, row],
        out_specs=[row, pl.BlockSpec((1, 1), lambda i: (0, 0))],
        out_shape=[jax.ShapeDtypeStruct((S, D), F32), jax.ShapeDtypeStruct((1, 1), F32)],
        scratch_shapes=[pltpu.VMEM((1, D), F32)], compiler_params=_params(1))(y, target)


SB_TILE = 256
SB_PAIRS = SB_HEADS * SB_HEAD_DIM // LANES
SB_DEAD_LOG = -110.0


def _log_sigmoids(z):
    ls = jnp.minimum(z, 0.0) - jnp.log(1.0 + jnp.exp(-jnp.abs(z)))
    return ls, ls - z


def _running_sums(x, tri2):
    hi = x.astype(BF16)
    lo = (x - hi.astype(F32)).astype(BF16)
    return jnp.dot(jnp.concatenate([hi, lo], axis=1), tri2, preferred_element_type=F32)


def _tri(T, cmp):
    j = lax.broadcasted_iota(jnp.int32, (T, T), 0)
    s = lax.broadcasted_iota(jnp.int32, (T, T), 1)
    m = jnp.where(cmp(j, s), 1.0, 0.0).astype(BF16)
    return jnp.concatenate([m, m], axis=0)


def _head_masks():
    lane = lax.broadcasted_iota(jnp.int32, (1, LANES), 1)
    return [lane < SB_HEAD_DIM, lane >= SB_HEAD_DIM]


def _cargo(refs, n_in, n_out, cargo, scatter):
    n = len(cargo)
    if not n:
        return refs, lambda first: None, lambda last: None
    ins = refs[n_in:n_in + n]
    outs = refs[n_in + n + n_out:n_in + n + n_out + n]
    sems = refs[len(refs) - 3:]
    own = refs[:n_in] + refs[n_in + n:n_in + n + n_out] + refs[n_in + n + n_out + n:len(refs) - 3]

    def start(first):
        @pl.when(first)
        def _():
            for cp in _chip_copies(ins, outs, *sems, scatter=scatter):
                cp.start()

    def finish(last):
        @pl.when(last)
        def _():
            for cp in _chip_copies(ins, outs, *sems, scatter=scatter):
                cp.wait()

    return own, start, finish


def _sba_fwd(qkv, *, name, cargo=()):
    S = qkv.shape[0]
    T = min(SB_TILE, S)
    nq = S // T
    scale = SB_HEAD_DIM ** -0.5

    def body(*refs):
        (q_ref, k_ref, v_ref, o_ref, t_ref, first_ref), start_cargo, finish_cargo = _cargo(refs, 3, 3, cargo, False)
        p, i = pl.program_id(0), pl.program_id(1)
        start_cargo(jnp.logical_and(p == 0, i == 0))
        row = lax.broadcasted_iota(jnp.int32, (T, T), 0)
        col = lax.broadcasted_iota(jnp.int32, (T, T), 1)
        strict = col < row
        after = _tri(T, lambda j, s: j > s)
        masks = _head_masks()
        q = q_ref[...] * scale
        qs = [jnp.where(hm, q, jnp.zeros_like(q)) for hm in masks]

        def walk(tiles, carry):
            values, zs = [], []
            for j, diag in tiles:
                rows = pl.ds(pl.multiple_of(j * T, T), T)
                kb = k_ref[rows, :]
                values.append(v_ref[rows, :])
                zs += [(lax.dot_general(qh, kb, (((1,), (1,)), ((), ())), preferred_element_type=F32), diag) for qh in qs]
            logs, suffixes = [], []
            for z, diag in zs:
                ls, ln = _log_sigmoids(z)
                ln = jnp.where(strict, ln, 0.0) if diag else ln
                logs.append((ls, ln))
                suffixes.append(_running_sums(ln, after))
            for t, (_, diag) in enumerate(tiles):
                out = []
                for h, (C, acc) in enumerate(carry):
                    ls, ln = logs[2 * t + h]
                    a = jnp.exp(ls + suffixes[2 * t + h] + C)
                    if diag:
                        a = jnp.where(strict, a, 0.0)
                    acc = acc + jnp.dot(a.astype(BF16), values[t], preferred_element_type=F32)
                    out.append((C + jnp.sum(ln, axis=1, keepdims=True), acc))
                carry = tuple(out)
            return carry

        fresh = (jnp.zeros((T, 1), F32), jnp.zeros((T, LANES), F32))
        carry = lax.cond(i > 0, lambda: walk([(i, True), (i - 1, False)], (fresh, fresh)),
                         lambda: walk([(i, True)], (fresh, fresh)))

        def alive(state):
            j, ((C0, _), (C1, _)) = state
            return jnp.logical_and(j >= 0, jnp.max(jnp.maximum(C0, C1)) > SB_DEAD_LOG)

        def step(state):
            j, carry = state
            return j - 1, walk([(j, False)], carry)

        j, ((C0, acc0), (C1, acc1)) = lax.while_loop(alive, step, (i - 2, carry))
        t_ref[0] = jnp.broadcast_to(C0, (T, LANES))
        t_ref[1] = jnp.broadcast_to(C1, (T, LANES))
        first_ref[...] = jnp.full((8, LANES), jnp.maximum(j + 1, 0).astype(F32))
        o_ref[...] = jnp.where(masks[0], acc0, acc1).astype(o_ref.dtype)
        finish_cargo(jnp.logical_and(p == SB_PAIRS - 1, i == nq - 1))

    kv = lambda off: pl.BlockSpec((S, LANES), lambda p, i: (0, off + p))
    n = len(cargo)
    return pl.pallas_call(
        body, name=name, grid=(SB_PAIRS, nq),
        in_specs=[pl.BlockSpec((T, LANES), lambda p, i: (i, p)), kv(SB_PAIRS), kv(2 * SB_PAIRS)] + [HBM_SPEC] * n,
        out_specs=[pl.BlockSpec((T, LANES), lambda p, i: (i, p)), pl.BlockSpec((2, T, LANES), lambda p, i: (p, i, 0)),
                   pl.BlockSpec((None, None, 8, LANES), lambda p, i: (p, i, 0, 0))] + [HBM_SPEC] * n,
        out_shape=[jax.ShapeDtypeStruct((S, SB_WIDTH), BF16), jax.ShapeDtypeStruct((SB_HEADS, S, LANES), F32),
                   jax.ShapeDtypeStruct((SB_PAIRS, nq, 8, LANES), F32)] + _chip_exchange_shapes(cargo, False),
        scratch_shapes=_chip_exchange_semaphores(n) if n else [],
        compiler_params=_params(2))(qkv, qkv, qkv, *cargo)


def _sba_bwd(qkv, do, tot, first, *, name, cargo=()):
    S = qkv.shape[0]
    T = min(SB_TILE, S)
    nq = S // T
    scale = SB_HEAD_DIM ** -0.5

    def body(*refs):
        own, start_cargo, finish_cargo = _cargo(refs, 6, 3, cargo, True)
        q_ref, k_ref, v_ref, do_ref, t_ref, first_ref, dq_ref, dk_ref, dv_ref, dk_acc, dv_acc = own
        p, i = pl.program_id(0), pl.program_id(1)
        start_cargo(jnp.logical_and(p == 0, i == 0))

        @pl.when(i == 0)
        def _():
            dk_acc[...] = jnp.zeros_like(dk_acc)
            dv_acc[...] = jnp.zeros_like(dv_acc)

        row = lax.broadcasted_iota(jnp.int32, (T, T), 0)
        col = lax.broadcasted_iota(jnp.int32, (T, T), 1)
        strict = col < row
        upto = _tri(T, lambda j, s: j <= s)
        before = _tri(T, lambda j, s: j < s)
        masks = _head_masks()
        q, do_t = q_ref[...], do_ref[...]
        q = q * scale
        qs = [jnp.where(hm, q, jnp.zeros_like(q)) for hm in masks]
        dos = [jnp.where(hm, do_t, jnp.zeros_like(do_t)) for hm in masks]
        totals = [t_ref[h][:, 0:1] for h in range(2)]
        over_lanes = (((1,), (1,)), ((), ()))
        over_queries = (((0,), (0,)), ((), ()))

        def walk(tiles, carry):
            rows = [pl.ds(pl.multiple_of(j * T, T), T) for j, _ in tiles]
            keys = [k_ref[r, :] for r in rows]
            values = [v_ref[r, :] for r in rows]
            chains = [(t, h) for t in range(len(tiles)) for h in range(2)]
            logs, da = {}, {}
            for t, h in chains:
                z = lax.dot_general(qs[h], keys[t], over_lanes, preferred_element_type=F32)
                ls, ln = _log_sigmoids(z)
                logs[t, h] = (ls, jnp.where(strict, ln, 0.0) if tiles[t][1] else ln)
                da[t, h] = lax.dot_general(dos[h], values[t], over_lanes, preferred_element_type=F32)
            upto_sums = {c: _running_sums(logs[c][1], upto) for c in chains}
            a, g = {}, {}
            P = [c[0] for c in carry]
            for t, h in chains:
                ls, ln = logs[t, h]
                a_th = jnp.exp(ls + ((totals[h] - P[h]) - upto_sums[t, h]))
                a[t, h] = jnp.where(strict, a_th, 0.0) if tiles[t][1] else a_th
                g[t, h] = a[t, h] * da[t, h]
                P[h] = P[h] + jnp.sum(ln, axis=1, keepdims=True)
            before_sums = {c: _running_sums(g[c], before) for c in chains}
            G = [c[1] for c in carry]
            dq = [c[2] for c in carry]
            dz = {}
            for t, h in chains:
                beta = jnp.exp(logs[t, h][0])
                dz_th = g[t, h] * (1.0 - beta) - (G[h] + before_sums[t, h]) * beta
                dz[t, h] = (jnp.where(strict, dz_th, 0.0) if tiles[t][1] else dz_th).astype(BF16)
                G[h] = G[h] + jnp.sum(g[t, h], axis=1, keepdims=True)
            for t, h in chains:
                dq[h] = dq[h] + jnp.dot(dz[t, h], keys[t], preferred_element_type=F32)
            for t in range(len(tiles)):
                dk_acc[rows[t], :] += sum(
                    lax.dot_general(dz[t, h], qs[h], over_queries, preferred_element_type=F32) for h in range(2))
                dv_acc[rows[t], :] += sum(
                    lax.dot_general(a[t, h].astype(BF16), dos[h], over_queries, preferred_element_type=F32)
                    for h in range(2))
            return tuple((P[h], G[h], dq[h]) for h in range(2))

        zero = jnp.zeros((T, 1), F32)
        fresh = (zero, zero, jnp.zeros((T, LANES), F32))
        last_single = jnp.maximum(i - 1, 0)
        j0 = jnp.clip(jnp.max(first_ref[...]).astype(jnp.int32), 0, last_single)
        carry = lax.fori_loop(j0, last_single, lambda j, c: walk([(j, False)], c), (fresh, fresh))
        (_, _, dq0), (_, _, dq1) = lax.cond(i > 0, lambda: walk([(i - 1, False), (i, True)], carry),
                                            lambda: walk([(i, True)], carry))
        dq_ref[...] = (jnp.where(masks[0], dq0, dq1) * scale).astype(dq_ref.dtype)

        @pl.when(i == nq - 1)
        def _():
            dk_ref[...] = dk_acc[...].astype(dk_ref.dtype)
            dv_ref[...] = dv_acc[...].astype(dv_ref.dtype)

        finish_cargo(jnp.logical_and(p == SB_PAIRS - 1, i == nq - 1))

    kv = lambda off: pl.BlockSpec((S, LANES), lambda p, i: (0, off + p))
    tile = lambda off: pl.BlockSpec((T, LANES), lambda p, i: (i, off + p))
    n = len(cargo)
    return pl.pallas_call(
        body, name=name, grid=(SB_PAIRS, nq),
        in_specs=[tile(0), kv(SB_PAIRS), kv(2 * SB_PAIRS), tile(0), pl.BlockSpec((2, T, LANES), lambda p, i: (p, i, 0)),
                  pl.BlockSpec((None, None, 8, LANES), lambda p, i: (p, i, 0, 0))] + [HBM_SPEC] * n,
        out_specs=[tile(0), kv(0), kv(0)] + [HBM_SPEC] * n,
        out_shape=[jax.ShapeDtypeStruct((S, SB_WIDTH), BF16)] * 3 + _chip_exchange_shapes(cargo, True),
        scratch_shapes=[pltpu.VMEM((S, LANES), F32), pltpu.VMEM((S, LANES), F32)]
        + (_chip_exchange_semaphores(n) if n else []),
        compiler_params=_params(2))(qkv, qkv, qkv, do, tot, first, *cargo)


POOL_TILE = 1024


def _by_group(lane, values):
    return jnp.where(lane < 64, values[0], jnp.where(lane < 128, values[1], jnp.where(lane < 192, values[2], values[3])))


def _pool_inv_count(first_row, n_rows):
    t = first_row + lax.broadcasted_iota(jnp.int32, (n_rows, POOL_WIDTH), 0)
    lane = lax.broadcasted_iota(jnp.int32, (n_rows, POOL_WIDTH), 1)
    window = _by_group(lane, POOL_WINDOWS)
    return 1.0 / jnp.clip(t + 1, 1, window).astype(F32), lane


def _pooled(ext, first_row, R):
    n = R + POOL_HALO
    s2 = ext + pltpu.roll(ext, 1, 0)
    s4 = s2 + pltpu.roll(s2, 2, 0)
    s8 = s4 + pltpu.roll(s4, 4, 0)
    s16 = s8 + pltpu.roll(s8, 8, 0)
    inv, lane = _pool_inv_count(first_row - POOL_HALO, n)
    pooled = _by_group(lane, (s2, s4, s8, s16)) * inv - ext
    return pooled[POOL_HALO:, :]


def _pool_specs(S, R, col):
    per = R // POOL_HALO
    tile = pl.BlockSpec((R, POOL_WIDTH), lambda i: (i, col))
    prev = pl.BlockSpec((POOL_HALO, POOL_WIDTH), lambda i: (jnp.maximum(i * per - 1, 0), col))
    return tile, prev


def _pool_fwd(rest, w_bd, scale, *, name):
    S = rest.shape[0]
    R = min(POOL_TILE, S)

    def body(p_ref, prev_ref, w_ref, s_ref, o_ref, ext_ref):
        i = pl.program_id(0)
        ext_ref[:POOL_HALO, :] = jnp.where(i > 0, prev_ref[...], 0.0)
        ext_ref[POOL_HALO:, :] = p_ref[...]
        pooled = _pooled(ext_ref[...], i * R, R)
        mixed = jnp.dot(pooled.astype(BF16), w_ref[...], preferred_element_type=F32)
        o_ref[...] = (mixed * s_ref[...]).astype(o_ref.dtype)

    tile, prev = _pool_specs(S, R, 0)
    return pl.pallas_call(
        body, name=name, grid=(S // R,),
        in_specs=[tile, prev, pl.BlockSpec((POOL_WIDTH, POOL_WIDTH), lambda i: (0, 0)), _vec_spec(POOL_WIDTH)],
        out_specs=_row_spec(R, POOL_WIDTH), out_shape=jax.ShapeDtypeStruct((S, POOL_WIDTH), BF16),
        scratch_shapes=[pltpu.VMEM((R + POOL_HALO, POOL_WIDTH), F32)], compiler_params=_params(1))(rest, rest, w_bd, scale)


def _pool_bwd(rest, do, w_bd, scale, *, name):
    S = rest.shape[0]
    R = min(POOL_TILE, S)
    n_tiles = S // R
    per = R // POOL_HALO
    n = R + POOL_HALO

    def body(p_ref, prev_ref, do_ref, nxt_ref, w_ref, s_ref, dp_ref, dw_ref, ds_ref, ext_ref, dext_ref):
        i = pl.program_id(0)
        ext_ref[:POOL_HALO, :] = jnp.where(i > 0, prev_ref[...], 0.0)
        ext_ref[POOL_HALO:, :] = p_ref[...]
        pooled = _pooled(ext_ref[...], i * R, R).astype(BF16)
        w = w_ref[...]
        mixed = jnp.dot(pooled, w, preferred_element_type=F32)
        do_t = do_ref[...]
        _accumulate(ds_ref, jnp.sum(do_t * mixed, axis=0, keepdims=True))
        dext_ref[:R, :] = do_t
        dext_ref[R:, :] = jnp.where(i < n_tiles - 1, nxt_ref[...], 0.0)
        dmixed = (dext_ref[...] * s_ref[...]).astype(BF16)
        dpooled = lax.dot_general(dmixed, w, (((1,), (1,)), ((), ())), preferred_element_type=F32)
        _accumulate(dw_ref, lax.dot_general(pooled, dmixed[:R, :], (((0,), (0,)), ((), ())), preferred_element_type=F32))
        inv, lane = _pool_inv_count(i * R, n)
        u = dpooled * inv
        f2 = u + pltpu.roll(u, n - 1, 0)
        f4 = f2 + pltpu.roll(f2, n - 2, 0)
        f8 = f4 + pltpu.roll(f4, n - 4, 0)
        f16 = f8 + pltpu.roll(f8, n - 8, 0)
        dp = _by_group(lane, (f2, f4, f8, f16)) - dpooled
        dp_ref[...] = dp[:R, :].astype(dp_ref.dtype)

    tile, prev = _pool_specs(S, R, 0)
    nxt = pl.BlockSpec((POOL_HALO, POOL_WIDTH), lambda i: (jnp.minimum((i + 1) * per, S // POOL_HALO - 1), 0))
    full = pl.BlockSpec((POOL_WIDTH, POOL_WIDTH), lambda i: (0, 0))
    return pl.pallas_call(
        body, name=name, grid=(n_tiles,),
        in_specs=[tile, prev, _row_spec(R, POOL_WIDTH), nxt, full, _vec_spec(POOL_WIDTH)],
        out_specs=[_row_spec(R, POOL_WIDTH), full, _vec_spec(POOL_WIDTH)],
        out_shape=[jax.ShapeDtypeStruct((S, POOL_WIDTH), BF16), jax.ShapeDtypeStruct((POOL_WIDTH, POOL_WIDTH), F32),
                   jax.ShapeDtypeStruct((1, POOL_WIDTH), F32)],
        scratch_shapes=[pltpu.VMEM((n, POOL_WIDTH), F32), pltpu.VMEM((n, POOL_WIDTH), F32)],
        compiler_params=_params(1))(rest, rest, do, do, w_bd, scale)


GM_TILE = 1024
GELU_C = math.sqrt(2.0 / math.pi)
GELU_A = 0.044715


def _gelu(x):
    return 0.5 * x * (1.0 + jnp.tanh(GELU_C * (x + GELU_A * x * x * x)))


def _gelu_and_grad(x):
    t = jnp.tanh(GELU_C * (x + GELU_A * x * x * x))
    y = 0.5 * x * (1.0 + t)
    dy = 0.5 * (1.0 + t) + 0.5 * x * (1.0 - t * t) * (GELU_C * (1.0 + 3.0 * GELU_A * x * x))
    return y, dy


def _group_lane_masks():
    lane = lax.broadcasted_iota(jnp.int32, (1, GM_WIDTH), 1)
    return [(lane >= g * GM_GROUP_DIM) & (lane < (g + 1) * GM_GROUP_DIM) for g in range(GM_GROUPS)]


def _stack_groups(x, masks):
    return jnp.concatenate([jnp.where(m, x, jnp.zeros_like(x)) for m in masks], axis=0)


def _gm_mixed(vn, ws_cat, bias, masks, R):
    chunks = []
    for c in range(R // GM_CHUNK):
        vc = vn[c * GM_CHUNK:(c + 1) * GM_CHUNK, :]
        chunks.append(jnp.dot(ws_cat, _stack_groups(vc, masks), preferred_element_type=F32) + bias)
    return jnp.concatenate(chunks, axis=0)


def _gm_specs(S, R):
    u = pl.BlockSpec((R, GM_WIDTH), lambda i: (i, 1))
    v = pl.BlockSpec((R, GM_WIDTH), lambda i: (i, 2))
    ws = pl.BlockSpec((GM_CHUNK, GM_GROUPS * GM_CHUNK), lambda i: (0, 0))
    bias = pl.BlockSpec((GM_CHUNK, GM_WIDTH), lambda i: (0, 0))
    return u, v, ws, bias


def _gm_fwd(rest, gain, ws_cat, bias, *, name):
    S = rest.shape[0]
    R = min(GM_TILE, S)

    def body(u_ref, v_ref, g_ref, ws_ref, b_ref, o_ref):
        gv = _gelu(v_ref[...])
        vn = (gv * _rstd(gv) * g_ref[...]).astype(BF16)
        mixed = _gm_mixed(vn, ws_ref[...], b_ref[...], _group_lane_masks(), R)
        o_ref[...] = (_gelu(u_ref[...]) * mixed).astype(o_ref.dtype)

    u_spec, v_spec, ws_spec, bias_spec = _gm_specs(S, R)
    return pl.pallas_call(
        body, name=name, grid=(S // R,), in_specs=[u_spec, v_spec, _vec_spec(GM_WIDTH), ws_spec, bias_spec],
        out_specs=_row_spec(R, GM_WIDTH), out_shape=jax.ShapeDtypeStruct((S, GM_WIDTH), BF16),
        compiler_params=_params(1))(rest, rest, gain, ws_cat, bias)


def _gm_bwd(rest, do, gain, ws_cat, wst_cat, bias, *, name):
    S = rest.shape[0]
    R = min(GM_TILE, S)

    def body(u_ref, v_ref, do_ref, g_ref, ws_ref, wst_ref, b_ref, du_ref, dv_ref, dg_ref, dws_ref, db_ref):
        masks = _group_lane_masks()
        gain_v = g_ref[...]
        gu, dgu = _gelu_and_grad(u_ref[...])
        gv, dgv = _gelu_and_grad(v_ref[...])
        r = _rstd(gv)
        vn = (gv * r * gain_v).astype(BF16)
        mixed = _gm_mixed(vn, ws_ref[...], b_ref[...], masks, R)
        do_t = do_ref[...]
        du_ref[...] = (do_t * mixed * dgu).astype(du_ref.dtype)
        dmix = do_t * gu
        dmix_b = dmix.astype(BF16)
        wst = wst_ref[...]
        dvn_chunks, db, dws = [], None, [None] * GM_GROUPS
        for c in range(R // GM_CHUNK):
            rows = slice(c * GM_CHUNK, (c + 1) * GM_CHUNK)
            dc, dcb, vc = dmix[rows, :], dmix_b[rows, :], vn[rows, :]
            db = dc if db is None else db + dc
            dvn_chunks.append(jnp.dot(wst, _stack_groups(dcb, masks), preferred_element_type=F32))
            for g, m in enumerate(masks):
                part = lax.dot_general(jnp.where(m, dcb, jnp.zeros_like(dcb)), vc, (((1,), (1,)), ((), ())),
                                       preferred_element_type=F32)
                dws[g] = part if dws[g] is None else dws[g] + part
        dvn = jnp.concatenate(dvn_chunks, axis=0)
        lane = lax.broadcasted_iota(jnp.int32, (1, LANES), 1)
        db_groups = jnp.zeros((GM_CHUNK, LANES), F32)
        for g, m in enumerate(masks):
            total = jnp.sum(jnp.where(m, db, 0.0), axis=1, keepdims=True)
            db_groups = db_groups + jnp.where(lane == g, total, 0.0)
        _accumulate(db_ref, db_groups)
        i = pl.program_id(0)
        for g in range(GM_GROUPS):
            @pl.when(i == 0)
            def _(g=g):
                dws_ref[g] = dws[g]

            @pl.when(i > 0)
            def _(g=g):
                dws_ref[g] += dws[g]
        _accumulate(dg_ref, jnp.sum(dvn * gv * r, axis=0, keepdims=True))
        gd = gain_v * dvn
        dgv_in = r * gd - gv * (r * r * r) * jnp.mean(gv * gd, axis=-1, keepdims=True)
        dv_ref[...] = (dgv_in * dgv).astype(dv_ref.dtype)

    u_spec, v_spec, ws_spec, bias_spec = _gm_specs(S, R)
    row, vec = _row_spec(R, GM_WIDTH), _vec_spec(GM_WIDTH)
    dws_spec = pl.BlockSpec((GM_GROUPS, GM_CHUNK, GM_CHUNK), lambda i: (0, 0, 0))
    return pl.pallas_call(
        body, name=name, grid=(S // R,), in_specs=[u_spec, v_spec, row, vec, ws_spec, ws_spec, bias_spec],
        out_specs=[row, row, vec, dws_spec, pl.BlockSpec((GM_CHUNK, LANES), lambda i: (0, 0))],
        out_shape=[jax.ShapeDtypeStruct((S, GM_WIDTH), BF16)] * 2
        + [jax.ShapeDtypeStruct((1, GM_WIDTH), F32), jax.ShapeDtypeStruct((GM_GROUPS, GM_CHUNK, GM_CHUNK), F32),
           jax.ShapeDtypeStruct((GM_CHUNK, LANES), F32)],
        compiler_params=_params(1))(rest, rest, do, gain, ws_cat, wst_cat, bias)


GATE_ROWS = 1024
GATE_COLS = 512
GATE_BLOCKS = D_MODEL // GATE_COLS


def _gate_spec(tr, k):
    return pl.BlockSpec((tr, GATE_COLS), lambda i, j: (i, GATE_BLOCKS * k + j))


def _merge_fwd(gates, branches, *, name):
    S = gates.shape[0]
    tr = min(GATE_ROWS, S)

    def body(g0, g1, g2, b0, b1, b2, o_ref):
        acc = None
        for g_ref, b_ref in ((g0, b0), (g1, b1), (g2, b2)):
            term = jax.nn.sigmoid(g_ref[...].astype(F32)) * b_ref[...].astype(F32)
            acc = term if acc is None else acc + term
        o_ref[...] = acc.astype(o_ref.dtype)

    tile = pl.BlockSpec((tr, GATE_COLS), lambda i, j: (i, j))
    return pl.pallas_call(
        body, name=name, grid=(S // tr, GATE_BLOCKS),
        in_specs=[_gate_spec(tr, k) for k in range(N_BRANCH)] + [tile] * N_BRANCH, out_specs=tile,
        out_shape=jax.ShapeDtypeStruct((S, D_MODEL), BF16), compiler_params=_params(2))(gates, gates, gates, *branches)


def _merge_bwd(gates, branches, dmerged, *, name):
    S = gates.shape[0]
    tr = min(GATE_ROWS, S)

    def body(g0, g1, g2, b0, b1, b2, dm_ref, dg0, dg1, dg2, db0, db1, db2):
        dm = dm_ref[...].astype(F32)
        for g_ref, b_ref, dg_ref, db_ref in ((g0, b0, dg0, db0), (g1, b1, dg1, db1), (g2, b2, dg2, db2)):
            s = jax.nn.sigmoid(g_ref[...].astype(F32))
            db_ref[...] = (dm * s).astype(db_ref.dtype)
            dg_ref[...] = (dm * b_ref[...].astype(F32) * s * (1.0 - s)).astype(dg_ref.dtype)

    tile = pl.BlockSpec((tr, GATE_COLS), lambda i, j: (i, j))
    return pl.pallas_call(
        body, name=name, grid=(S // tr, GATE_BLOCKS),
        in_specs=[_gate_spec(tr, k) for k in range(N_BRANCH)] + [tile] * (N_BRANCH + 1), out_specs=[tile] * (2 * N_BRANCH),
        out_shape=[jax.ShapeDtypeStruct((S, D_MODEL), BF16)] * (2 * N_BRANCH),
        compiler_params=_params(2))(gates, gates, gates, *branches, dmerged)


TILE_BYTES = 40 * 1024 * 1024


BF16_ROWS = 16


def _tile_rows(rows, cols, n_arrays):
    padded = -(-cols // LANES) * LANES
    cap = max(BF16_ROWS, TILE_BYTES // (2 * n_arrays * padded * 4))
    best = None
    for tr in range(BF16_ROWS, min(rows, cap) + 1, BF16_ROWS):
        if rows % tr == 0:
            best = tr
    assert best is not None, (rows, cols)
    return best


def _sum_slots(stack, *, name):
    n, R, C = stack.shape
    tr = _tile_rows(R, C, n + 1)

    def body(s_ref, o_ref):
        acc = s_ref[0].astype(F32)
        for k in range(1, n):
            acc = acc + s_ref[k].astype(F32)
        o_ref[...] = acc

    return pl.pallas_call(
        body, name=name, grid=(R // tr,), in_specs=[pl.BlockSpec((n, tr, C), lambda i: (0, i, 0))],
        out_specs=_row_spec(tr, C), out_shape=jax.ShapeDtypeStruct((R, C), F32), compiler_params=_params(1))(stack)


def _add_own_half(parts, received, core, *, name):
    n, R, C = parts.shape
    half = R // 2
    tr = _tile_rows(half, C, 3)
    steps = half // tr

    def body(core_ref, own_ref, got_ref, o_ref):
        o_ref[...] = (own_ref[...] + got_ref[...]).astype(o_ref.dtype)

    tile = pl.BlockSpec((None, tr, C), lambda d, i, core_ref: (d, i, 0))
    own = pl.BlockSpec((None, tr, C), lambda d, i, core_ref: (d, core_ref[0] * steps + i, 0))
    return pl.pallas_call(
        body, name=name, out_shape=jax.ShapeDtypeStruct((n, half, C), BF16),
        grid_spec=pltpu.PrefetchScalarGridSpec(num_scalar_prefetch=1, grid=(n, steps), in_specs=[own, tile], out_specs=tile),
        compiler_params=_params(2))(core, parts, received)


def _adamw_math(w, m, v, g):
    m_new = ADAM_B1 * m + (1.0 - ADAM_B1) * g
    v_new = ADAM_B2 * v + (1.0 - ADAM_B2) * jnp.square(g)
    m_hat = m_new / (1.0 - ADAM_B1 ** ADAM_STEP)
    v_hat = v_new / (1.0 - ADAM_B2 ** ADAM_STEP)
    return -ADAM_LR * (m_hat / (jnp.sqrt(v_hat) + ADAM_EPS) + ADAM_WD * w), m_new, v_new


def _adamw_halves(w, m, v, mine, theirs, core, *, name):
    L, r, C = w.shape
    tr = _tile_rows(r // 2, C, 9)
    steps = r // 2 // tr

    def body(core_ref, w_ref, m_ref, v_ref, mine_ref, theirs_ref, go_ref, d_ref, mo_ref, vo_ref):
        in_my_half = pl.program_id(1) // steps == core_ref[0]
        g = jnp.where(in_my_half, mine_ref[...], theirs_ref[...])
        go_ref[...] = g
        d_ref[...], mo_ref[...], vo_ref[...] = _adamw_math(w_ref[...], m_ref[...], v_ref[...], g)

    row = pl.BlockSpec((None, tr, C), lambda l, i, core_ref: (l, i, 0))
    half = pl.BlockSpec((None, tr, C), lambda l, i, core_ref: (l, i % steps, 0))
    return pl.pallas_call(
        body, name=name, out_shape=[jax.ShapeDtypeStruct((L, r, C), F32)] * 4,
        grid_spec=pltpu.PrefetchScalarGridSpec(
            num_scalar_prefetch=1, grid=(L, r // tr), in_specs=[row, row, row, half, half], out_specs=[row] * 4),
        compiler_params=_params(2))(core, w, m, v, mine, theirs)


HBM_SPEC = pl.BlockSpec(memory_space=pl.ANY)


def _position():
    return lax.axis_index("x"), lax.axis_index("y"), lax.axis_index("c")


def _other_chips(x, y):
    return [(1 - x, y), (x, 1 - y), (1 - x, 1 - y)]


def _chip_exchange(arrays, *, scatter, name):
    n = len(arrays)

    def body(*refs):
        copies = _chip_copies(refs[:n], refs[n:2 * n], *refs[2 * n:], scatter=scatter)
        for cp in copies:
            cp.start()
        for cp in copies:
            cp.wait()

    return pl.pallas_call(
        body, name=name, in_specs=[HBM_SPEC] * n, out_specs=[HBM_SPEC] * n, out_shape=_chip_exchange_shapes(arrays, scatter),
        scratch_shapes=_chip_exchange_semaphores(n))(*arrays)


def _chip_exchange_shapes(arrays, scatter):
    return [jax.ShapeDtypeStruct(a.shape if scatter else (N_CHIPS, 2) + a.shape, a.dtype) for a in arrays]


def _chip_exchange_semaphores(n):
    return [pltpu.SemaphoreType.DMA((3 * n,)), pltpu.SemaphoreType.DMA((3 * n,)), pltpu.SemaphoreType.DMA((n,))]


def _chip_copies(ins, outs, send_sems, recv_sems, local_sems, *, scatter):
    x, y, c = _position()
    me = 2 * x + y
    copies = []
    for a in range(len(ins)):
        own = ins[a].at[me] if scatter else ins[a]
        slot = outs[a].at[me] if scatter else outs[a].at[me, c]
        copies.append(pltpu.make_async_copy(own, slot, local_sems.at[a]))
        for k, (px, py) in enumerate(_other_chips(x, y)):
            src = ins[a].at[2 * px + py] if scatter else ins[a]
            copies.append(pltpu.make_async_remote_copy(
                src_ref=src, dst_ref=slot, send_sem=send_sems.at[3 * a + k],
                recv_sem=recv_sems.at[3 * a + k], device_id=(px, py, c), device_id_type=MESH))
    return copies


def _sibling_fill(arrays, *, name):
    n = len(arrays)

    def body(*refs):
        ins, outs = refs[:n], refs[n:2 * n]
        send_sems, recv_sems = refs[2 * n:]
        x, y, c = _position()
        copies = []
        for a in range(n):
            cp = pltpu.make_async_remote_copy(
                src_ref=ins[a].at[:, c], dst_ref=outs[a].at[:, c], send_sem=send_sems.at[a], recv_sem=recv_sems.at[a],
                device_id=(x, y, 1 - c), device_id_type=MESH)
            cp.start()
            copies.append(cp)
        for cp in copies:
            cp.wait()

    return pl.pallas_call(
        body, name=name, in_specs=[HBM_SPEC] * n, out_specs=[HBM_SPEC] * n,
        out_shape=[jax.ShapeDtypeStruct(a.shape, a.dtype) for a in arrays],
        input_output_aliases={a: a for a in range(n)},
        scratch_shapes=[pltpu.SemaphoreType.DMA((n,)), pltpu.SemaphoreType.DMA((n,))],
    )(*arrays)


def _sibling_swap(arrays, *, name):
    n = len(arrays)

    def body(*refs):
        ins, outs = refs[:n], refs[n:2 * n]
        send_sems, recv_sems = refs[2 * n:]
        x, y, c = _position()
        copies = []
        for a in range(n):
            cp = pltpu.make_async_remote_copy(
                src_ref=ins[a], dst_ref=outs[a], send_sem=send_sems.at[a], recv_sem=recv_sems.at[a],
                device_id=(x, y, 1 - c), device_id_type=MESH)
            cp.start()
            copies.append(cp)
        for cp in copies:
            cp.wait()

    return pl.pallas_call(
        body, name=name, in_specs=[HBM_SPEC] * n, out_specs=[HBM_SPEC] * n,
        out_shape=[jax.ShapeDtypeStruct(a.shape, a.dtype) for a in arrays],
        scratch_shapes=[pltpu.SemaphoreType.DMA((n,)), pltpu.SemaphoreType.DMA((n,))],
    )(*arrays)


def _sibling_other_half(arrays, *, name):
    n = len(arrays)

    def body(*refs):
        ins, outs = refs[:n], refs[n:2 * n]
        send_sems, recv_sems = refs[2 * n:]
        x, y, c = _position()
        copies = []
        for a in range(n):
            half = ins[a].shape[1] // 2
            theirs = ins[a].at[:, pl.ds(pl.multiple_of((1 - c) * half, BF16_ROWS), half), :]
            cp = pltpu.make_async_remote_copy(
                src_ref=theirs, dst_ref=outs[a], send_sem=send_sems.at[a], recv_sem=recv_sems.at[a],
                device_id=(x, y, 1 - c), device_id_type=MESH)
            cp.start()
            copies.append(cp)
        for cp in copies:
            cp.wait()

    return pl.pallas_call(
        body, name=name, in_specs=[HBM_SPEC] * n, out_specs=[HBM_SPEC] * n,
        out_shape=[jax.ShapeDtypeStruct((a.shape[0], a.shape[1] // 2, a.shape[2]), a.dtype) for a in arrays],
        scratch_shapes=[pltpu.SemaphoreType.DMA((n,)), pltpu.SemaphoreType.DMA((n,))],
    )(*arrays)


def _small_update(grads, ws, ms, vs, *, name):
    n, L = len(ws), ws[0].shape[0]
    pieces = [g for per_layer in grads for g in per_layer]
    np_ = len(pieces)

    def body(*refs):
        g_in, refs = refs[:np_], refs[np_:]
        w_in, m_in, v_in, g_out, d_out, m_out, v_out = (refs[k * n:(k + 1) * n] for k in range(7))
        from_sibling, chip_sums = refs[7 * n:7 * n + np_], refs[7 * n + np_:7 * n + 2 * np_]
        sibling_send, sibling_recv, chip_send, chip_recv = refs[7 * n + 2 * np_:]
        x, y, c = _position()
        me = 2 * x + y
        swaps = [pltpu.make_async_remote_copy(
            src_ref=g_in[p], dst_ref=from_sibling[p], send_sem=sibling_send.at[p], recv_sem=sibling_recv.at[p],
            device_id=(x, y, 1 - c), device_id_type=MESH) for p in range(np_)]
        for cp in swaps:
            cp.start()
        for cp in swaps:
            cp.wait()
        for p in range(np_):
            chip_sums[p][me] = g_in[p][...] + from_sibling[p][...]
        sends = [pltpu.make_async_remote_copy(
            src_ref=chip_sums[p].at[me], dst_ref=chip_sums[p].at[me], send_sem=chip_send.at[3 * p + k],
            recv_sem=chip_recv.at[3 * p + k], device_id=(px, py, c), device_id_type=MESH)
            for p in range(np_) for k, (px, py) in enumerate(_other_chips(x, y))]
        for cp in sends:
            cp.start()
        for cp in sends:
            cp.wait()
        for a in range(n):
            for l in range(L):
                sums = chip_sums[a * L + l]
                g = sums[0]
                for s in range(1, N_CHIPS):
                    g = g + sums[s]
                g_out[a][l] = g
                d_out[a][l], m_out[a][l], v_out[a][l] = _adamw_math(w_in[a][l], m_in[a][l], v_in[a][l], g)

    vmem = pl.BlockSpec(memory_space=pltpu.VMEM)
    shapes = [jax.ShapeDtypeStruct(w.shape, F32) for w in ws]
    outs = pl.pallas_call(
        body, name=name, in_specs=[vmem] * (np_ + 3 * n), out_specs=[vmem] * (4 * n), out_shape=shapes * 4,
        scratch_shapes=[pltpu.VMEM(g.shape, F32) for g in pieces] + [pltpu.VMEM((N_CHIPS,) + g.shape, F32) for g in pieces]
        + [pltpu.SemaphoreType.DMA((np_,)), pltpu.SemaphoreType.DMA((np_,)),
           pltpu.SemaphoreType.DMA((3 * np_,)), pltpu.SemaphoreType.DMA((3 * np_,))],
        compiler_params=pltpu.CompilerParams(vmem_limit_bytes=VMEM_LIMIT),
    )(*pieces, *ws, *ms, *vs)
    return outs[:n], outs[n:2 * n], outs[2 * n:3 * n], outs[3 * n:]


def _relu2(p):
    return p, jnp.square(jnp.maximum(p, 0.0))


def _relu2_grad(p, a):
    return (p * (2.0 * jnp.maximum(a.astype(F32), 0.0)),)


def _mixer_constants(w_pool, w_spatial, b_spatial):
    eye = jnp.eye(len(POOL_WINDOWS), dtype=F32)
    w_bd = (eye[:, None, :, None] * w_pool[:, :, None, :]).reshape(POOL_WIDTH, POOL_WIDTH).astype(BF16)
    causal = jnp.tril(jnp.ones((GM_CHUNK, GM_CHUNK), dtype=bool))
    ws = jnp.where(causal[None], w_spatial, 0.0).astype(BF16)
    ws_cat = ws.transpose(1, 0, 2).reshape(GM_CHUNK, GM_GROUPS * GM_CHUNK)
    wst_cat = ws.transpose(2, 0, 1).reshape(GM_CHUNK, GM_GROUPS * GM_CHUNK)
    bias = jnp.repeat(b_spatial.T, GM_GROUP_DIM, axis=1)
    return w_bd, ws_cat, wst_cat, bias


def _local_step(x, target, half_shards, small, core):
    L = small["g_mix_pre"].shape[0]
    vec = lambda name, l: small[name][l][None, :]
    consts = [_mixer_constants(small["w_pool"][l], small["w_spatial"][l], small["b_spatial"][l]) for l in range(L)]
    core_index = core.astype(jnp.int32).reshape(1)
    first_used, used_later = BIG_WEIGHTS[:1], BIG_WEIGHTS[1:]
    weights = [{} for _ in range(L)]

    def finish_gather(wanted, gathered):
        for (n, ll), both in zip(wanted, _sibling_fill(gathered, name="swap_weight_halves")):
            weights[ll][n] = _full_weight(n, both)

    def core_sums(names, l):
        parts = [_parts_by_chip(n, gb[n][l]) for n in names]
        from_sibling = _sibling_other_half(parts, name="swap_grad_halves")
        return [(n, l, _add_own_half(p, f, core_index, name="sum_cores")) for n, p, f in zip(names, parts, from_sibling)]

    wanted = [(n, 0) for n in first_used]
    finish_gather(wanted, _chip_exchange([half_shards[n][ll] for n, ll in wanted], scatter=False, name="gather_weights"))
    saved = []
    h = _rms_fwd(x, vec("g_mix_pre", 0), name="rms_in")
    for l in range(L):
        w_bd, ws_cat, wst_cat, bias = consts[l]
        proj = lambda n, off, dtype, name: _matmul(h, weights[l]["w_in"], tb=True, n=n, bn=PROJ_BLOCK,
                                                   b_col_off=off // PROJ_BLOCK, out_dtypes=(dtype,), name=name)
        qkv = proj(QKV_WIDTH, 0, BF16, "proj_qkv")
        rest = proj(MIX_WIDTH, QKV_WIDTH, F32, "proj_mix")
        gates = proj(GATE_WIDTH, QKV_WIDTH + MIX_WIDTH, BF16, "proj_gates")
        wanted = [(n, l) for n in used_later] + [(n, l + 1) for n in first_used if l + 1 < L]
        o_sb, tot, first, *gathered = _sba_fwd(qkv, cargo=[half_shards[n][ll] for n, ll in wanted], name="sba_fwd_gather")
        finish_gather(wanted, gathered)
        o_pool = _pool_fwd(rest, w_bd, vec("pool_scale", l), name="pool_fwd")
        o_gm = _gm_fwd(rest, vec("gm_gain", l), ws_cat, bias, name="gm_fwd")
        branches = (_matmul(o_sb, weights[l]["w_br_sb"], out_dtypes=(BF16,), name="br_sb"),
                    _matmul(o_pool, weights[l]["w_br_pool"], out_dtypes=(BF16,), name="br_pool"),
                    _matmul(o_gm, weights[l]["w_br_gm"], out_dtypes=(BF16,), name="br_gm"))
        merged = _merge_fwd(gates, branches, name="merge_fwd")
        y = _matmul(merged, weights[l]["w_out"], name="out_proj")
        x1, h2 = _resid_rms(x, y, vec("g_mix_post", l), vec("g_ff_pre", l), name="resid_mix")
        a, r = _matmul(h2, weights[l]["w_ff_in"], out_dtypes=(BF16, BF16), epilogue=_relu2, name="ff_in")
        ff = _matmul(r, weights[l]["w_ff_out"], name="ff_out")
        g_next = vec("g_mix_pre", l + 1) if l + 1 < L else None
        x2, h_next = _resid_rms(x1, ff, vec("g_ff_post", l), g_next, name="resid_ff" if l + 1 < L else "resid_last")
        saved.append(dict(x=x, h=h, qkv=qkv, rest=rest, gates=gates, o_sb=o_sb, tot=tot, first=first, o_pool=o_pool,
                          o_gm=o_gm, branches=branches, merged=merged, y=y, x1=x1, h2=h2, a=a, r=r, ff=ff))
        x, h = x2, h_next

    dx2, loss = _loss_head(x, target, name="loss_head")
    gb = {k: [None] * L for k in ("w_in", "w_br_sb", "w_br_pool", "w_br_gm", "w_out", "w_ff_in", "w_ff_out")}
    gs = {k: [None] * L for k in ("w_pool", "pool_scale", "gm_gain", "w_spatial", "b_spatial", "g_mix_pre",
                                  "g_mix_post", "g_ff_pre", "g_ff_post")}
    d_ff, gs["g_ff_post"][L - 1] = _rms_bwd(saved[-1]["ff"], vec("g_ff_post", L - 1), dx2, name="rms_bwd_last")
    received = [{} for _ in range(L)]
    waiting = []
    for l in reversed(range(L)):
        s = saved[l]
        w_bd, ws_cat, wst_cat, bias = consts[l]
        da = _matmul(d_ff, weights[l]["w_ff_out"], tb=True, out_dtypes=(BF16,), extras=(s["a"],), epilogue=_relu2_grad,
                     name="ff_out_dx")
        gb["w_ff_out"][l] = _matmul(s["r"], d_ff, ta=True, name="ff_out_dw")
        dh2 = _matmul(da, weights[l]["w_ff_in"], tb=True, out_dtypes=(BF16,), name="ff_in_dx")
        gb["w_ff_in"][l] = _matmul(s["h2"], da, ta=True, name="ff_in_dw")
        dx1, gs["g_ff_pre"][l], dy, gs["g_mix_post"][l] = _rms_bwd_chain(
            s["x1"], vec("g_ff_pre", l), dh2, dx2, s["y"], vec("g_mix_post", l), name="rms_bwd_mid")
        dmerged = _matmul(dy, weights[l]["w_out"], tb=True, out_dtypes=(BF16,), name="out_proj_dx")
        gb["w_out"][l] = _matmul(s["merged"], dy, ta=True, name="out_proj_dw")
        dg0, dg1, dg2, db_sb, db_pool, db_gm = _merge_bwd(s["gates"], s["branches"], dmerged, name="merge_bwd")
        do_sb = _matmul(db_sb, weights[l]["w_br_sb"], tb=True, out_dtypes=(BF16,), name="br_sb_dx")
        gb["w_br_sb"][l] = _matmul(s["o_sb"], db_sb, ta=True, name="br_sb_dw")
        do_pool = _matmul(db_pool, weights[l]["w_br_pool"], tb=True, name="br_pool_dx")
        gb["w_br_pool"][l] = _matmul(s["o_pool"], db_pool, ta=True, name="br_pool_dw")
        do_gm = _matmul(db_gm, weights[l]["w_br_gm"], tb=True, name="br_gm_dx")
        gb["w_br_gm"][l] = _matmul(s["o_gm"], db_gm, ta=True, name="br_gm_dw")
        waiting += core_sums(used_later, l)
        dq, dk, dv, *arrived = _sba_bwd(s["qkv"], do_sb, s["tot"], s["first"], cargo=[c for _, _, c in waiting],
                                        name="sba_bwd_exchange")
        for (n, ll, _), got in zip(waiting, arrived):
            received[ll][n] = got
        dp, dw_bd, gs["pool_scale"][l] = _pool_bwd(s["rest"], do_pool, w_bd, vec("pool_scale", l), name="pool_bwd")
        du, dgv, gs["gm_gain"][l], dws, db = _gm_bwd(s["rest"], do_gm, vec("gm_gain", l), ws_cat, wst_cat, bias,
                                                      name="gm_bwd")
        gs["w_pool"][l] = jnp.stack([dw_bd[g * 64:(g + 1) * 64, g * 64:(g + 1) * 64] for g in range(len(POOL_WINDOWS))])
        gs["w_spatial"][l] = jnp.where(jnp.tril(jnp.ones((GM_CHUNK, GM_CHUNK), dtype=bool))[None], dws, 0.0)
        gs["b_spatial"][l] = db[:, :GM_GROUPS].T
        dproj = jnp.concatenate([dq, dk, dv, dp, du, dgv, dg0, dg1, dg2], axis=1)
        dh = _matmul(dproj, weights[l]["w_in"], bk=D_IN // 3, out_dtypes=(BF16,), name="proj_dx")
        gb["w_in"][l] = _matmul(dproj, s["h"], ta=True, bm=PROJ_BLOCK, name="proj_dw")
        waiting = core_sums(first_used, l)
        if l == 0:
            for (n, ll, _), got in zip(waiting, _chip_exchange([c for _, _, c in waiting], scatter=True, name="exchange_grads")):
                received[ll][n] = got
        if l > 0:
            dx2, gs["g_mix_pre"][l], d_ff, gs["g_ff_post"][l - 1] = _rms_bwd_chain(
                s["x"], vec("g_mix_pre", l), dh, dx1, saved[l - 1]["ff"], vec("g_ff_post", l - 1), name="rms_bwd_mid")
        else:
            dx2, gs["g_mix_pre"][l], _, _ = _rms_bwd_chain(s["x"], vec("g_mix_pre", l), dh, dx1, None, None,
                                                           name="rms_bwd_first")
    return loss, dx2, received, gs


TRANSPOSED = ("w_in",)
COLUMN_SHARDED = ("w_br_sb", "w_br_pool", "w_br_gm", "w_ff_in")
ROW_SHARDED = ("w_in", "w_out", "w_ff_out")
BIG_WEIGHTS = ("w_in", "w_br_sb", "w_br_pool", "w_br_gm", "w_ff_in", "w_out", "w_ff_out")
SMALL_WEIGHTS = ("w_pool", "pool_scale", "gm_gain", "w_spatial", "b_spatial", "g_mix_pre", "g_mix_post", "g_ff_pre",
                 "g_ff_post")
WEIGHT_ORDER = ("w_in", "w_pool", "pool_scale", "gm_gain", "w_spatial", "b_spatial", "w_br_sb", "w_br_pool", "w_br_gm",
                "w_out", "g_mix_pre", "g_mix_post", "g_ff_pre", "g_ff_post", "w_ff_in", "w_ff_out")


def _full_weight(name, g):
    half, cols = g.shape[2], g.shape[3]
    if name in COLUMN_SHARDED:
        return g.transpose(1, 2, 0, 3).reshape(2 * half, N_CHIPS * cols)
    return g.reshape(N_CHIPS * 2 * half, cols)


def _parts_by_chip(name, grad):
    if name in COLUMN_SHARDED:
        r, c = grad.shape[0], grad.shape[1] // N_CHIPS
        return grad.reshape(r, N_CHIPS, c).transpose(1, 0, 2)
    return grad.reshape(N_CHIPS, grad.shape[0] // N_CHIPS, grad.shape[1])


def kernel(x, w_in, w_pool, pool_scale, gm_gain, w_spatial, b_spatial, w_br_sb, w_br_pool, w_br_gm, w_out, g_mix_pre, g_mix_post, g_ff_pre, g_ff_post, w_ff_in, w_ff_out, loss_target, m_w_in, m_w_pool, m_pool_scale, m_gm_gain, m_w_spatial, m_b_spatial, m_w_br_sb, m_w_br_pool, m_w_br_gm, m_w_out, m_g_mix_pre, m_g_mix_post, m_g_ff_pre, m_g_ff_post, m_w_ff_in, m_w_ff_out, v_w_in, v_w_pool, v_pool_scale, v_gm_gain, v_w_spatial, v_b_spatial, v_w_br_sb, v_w_br_pool, v_w_br_gm, v_w_out, v_g_mix_pre, v_g_mix_post, v_g_ff_pre, v_g_ff_post, v_w_ff_in, v_w_ff_out):
    given = dict(locals())
    w = {n: given[n] for n in WEIGHT_ORDER}
    m = {n: given["m_" + n] for n in WEIGHT_ORDER}
    v = {n: given["v_" + n] for n in WEIGHT_ORDER}
    L = w_in.shape[0]

    core = lax.axis_index("c")

    def my_rows(a):
        half = a.shape[1] // 2
        return lax.dynamic_slice_in_dim(a.astype(BF16), core * half, half, axis=1)

    view = lambda n, a: jnp.swapaxes(a, 1, 2) if n in TRANSPOSED else a
    half_shards = {n: my_rows(view(n, w[n])) for n in BIG_WEIGHTS}
    small = {n: w[n] for n in SMALL_WEIGHTS}
    loss, dx, received, small_grads = _local_step(x[0], loss_target[0], half_shards, small, core)

    core_index = core.astype(jnp.int32).reshape(1)
    reduced = []
    for n in BIG_WEIGHTS:
        from_chips = jnp.concatenate([received[l][n] for l in range(L)], axis=1)
        reduced.append(_sum_slots(from_chips, name="sum_chips"))
    reduced_by_sibling = _sibling_swap(reduced, name="swap_reduced_halves")
    grads, deltas, new_m, new_v = {}, {}, {}, {}
    for n, mine, theirs in zip(BIG_WEIGHTS, reduced, reduced_by_sibling):
        by_layer = lambda a: a.reshape(L, -1, a.shape[-1])
        outs = _adamw_halves(view(n, w[n]), view(n, m[n]), view(n, v[n]), by_layer(mine), by_layer(theirs), core_index,
                             name="adamw_big")
        grads[n], deltas[n], new_m[n], new_v[n] = [view(n, o) for o in outs]

    by_layer = lambda a: a.reshape(L, -1, a.shape[-1])
    flat = lambda a: a.reshape(-1, a.shape[-1])
    outs = _small_update([[flat(g) for g in small_grads[n]] for n in SMALL_WEIGHTS],
                         *[[by_layer(t[n]) for n in SMALL_WEIGHTS] for t in (w, m, v)], name="small_update")
    for store, arrays in zip((grads, deltas, new_m, new_v), outs):
        store.update({n: a.reshape(w[n].shape) for n, a in zip(SMALL_WEIGHTS, arrays)})

    total_loss = lax.psum(loss[0, 0], ("x", "y", "c"))
    return (total_loss, dx[None], *[grads[n] for n in WEIGHT_ORDER], *[deltas[n] for n in WEIGHT_ORDER],
            *[new_m[n] for n in WEIGHT_ORDER], *[new_v[n] for n in WEIGHT_ORDER])
```

```python
import functools
import math

import jax
import jax.numpy as jnp
from jax import lax
from jax.experimental import pallas as pl
from jax.experimental.pallas import tpu as pltpu

F32 = jnp.float32
BF16 = jnp.bfloat16

D_MODEL = 1024
SB_HEADS = 8
SB_HEAD_DIM = 64
SB_WIDTH = SB_HEADS * SB_HEAD_DIM
POOL_WINDOWS = (2, 4, 8, 16)
POOL_GROUP_DIM = 64
POOL_WIDTH = 256
POOL_HALO = 16
GM_GROUPS = 4
GM_GROUP_DIM = 64
GM_WIDTH = 256
GM_CHUNK = 128
N_BRANCH = 3
D_FF = 4 * D_MODEL
RMS_EPS = 1e-6
QKV_WIDTH = 3 * SB_WIDTH
MIX_WIDTH = POOL_WIDTH + 2 * GM_WIDTH
GATE_WIDTH = N_BRANCH * D_MODEL
D_IN = QKV_WIDTH + MIX_WIDTH + GATE_WIDTH
PROJ_BLOCK = 768
LANES = 128
N_CHIPS = 4
N_DEV = 8

ADAM_LR = 0.001
ADAM_B1 = 0.9
ADAM_B2 = 0.999
ADAM_EPS = 1e-08
ADAM_WD = 0.01
ADAM_STEP = 10

VMEM_LIMIT = 56 * 1024 * 1024
MESH = pl.DeviceIdType.MESH


def _params(n_grid):
    return pltpu.CompilerParams(dimension_semantics=("arbitrary",) * n_grid, vmem_limit_bytes=VMEM_LIMIT)


def _bf(x):
    return x if x.dtype == BF16 else x.astype(BF16)


def _matmul(a, b, *, name, ta=False, tb=False, out_dtypes=(F32,), n=None, b_col_off=0, bm=1024, bn=1024, bk=4096,
            extras=(), epilogue=None):
    M, K = (a.shape[1], a.shape[0]) if ta else a.shape
    nb = b.shape[0] if tb else b.shape[1]
    n = nb if n is None else n
    bm, bn, bk = min(bm, M), min(bn, n), min(bk, K)
    assert M % bm == 0 and n % bn == 0 and K % bk == 0, (name, M, n, K, bm, bn, bk)
    assert (b.shape[1] if tb else b.shape[0]) == K, (name, a.shape, b.shape)
    nk = K // bk
    dims = (((0 if ta else 1,), (1 if tb else 0,)), ((), ()))
    n_out = len(out_dtypes)
    direct = nk > 1 and epilogue is None and out_dtypes == (F32,)
    use_acc = nk > 1 and not direct

    def body(*refs):
        a_ref, b_ref = refs[:2]
        extra_refs = refs[2:2 + len(extras)]
        out_refs = refs[2 + len(extras):2 + len(extras) + n_out]
        p = lax.dot_general(_bf(a_ref[...]), _bf(b_ref[...]), dims, preferred_element_type=F32)

        def finish(acc):
            outs = (acc,) if epilogue is None else epilogue(acc, *[r[...] for r in extra_refs])
            for r, o in zip(out_refs, outs):
                r[...] = o.astype(r.dtype)

        if nk == 1:
            finish(p)
            return
        k = pl.program_id(2)
        acc_ref = out_refs[0] if direct else refs[-1]

        @pl.when(k == 0)
        def _():
            acc_ref[...] = p

        @pl.when(k > 0)
        def _():
            acc_ref[...] += p

        if use_acc:
            @pl.when(k == nk - 1)
            def _():
                finish(acc_ref[...])

    a_spec = pl.BlockSpec((bk, bm), lambda i, j, k: (k, i)) if ta else pl.BlockSpec((bm, bk), lambda i, j, k: (i, k))
    if tb:
        b_spec = pl.BlockSpec((bn, bk), lambda i, j, k: (j + b_col_off, k))
    else:
        b_spec = pl.BlockSpec((bk, bn), lambda i, j, k: (k, j + b_col_off))
    tile = pl.BlockSpec((bm, bn), lambda i, j, k: (i, j))
    outs = pl.pallas_call(
        body, name=name, grid=(M // bm, n // bn, nk),
        in_specs=[a_spec, b_spec] + [tile] * len(extras),
        out_specs=[tile] * n_out,
        out_shape=[jax.ShapeDtypeStruct((M, n), d) for d in out_dtypes],
        scratch_shapes=[pltpu.VMEM((bm, bn), F32)] if use_acc else [],
        compiler_params=_params(3),
    )(a, b, *extras)
    return outs[0] if n_out == 1 else outs


ROW_TILE = 512
WIDE_ROW_TILE = 1024


def _rows(S, tile=ROW_TILE):
    tr = min(tile, S)
    assert S % tr == 0
    return tr


def _rstd(x):
    return lax.rsqrt(jnp.mean(x * x, axis=-1, keepdims=True) + RMS_EPS)


def _rms_bwd_math(x, g, dy):
    r = _rstd(x)
    gd = g * dy
    dx = r * gd - x * (r * r * r) * jnp.mean(x * gd, axis=-1, keepdims=True)
    dg = jnp.sum(dy * x * r, axis=0, keepdims=True)
    return dx, dg


def _accumulate(ref, value):
    i = pl.program_id(0)

    @pl.when(i == 0)
    def _():
        ref[...] = value

    @pl.when(i > 0)
    def _():
        ref[...] += value


def _row_spec(tr, width):
    return pl.BlockSpec((tr, width), lambda i: (i, 0))


def _vec_spec(width):
    return pl.BlockSpec((1, width), lambda i: (0, 0))


def _rms_fwd(x, g, *, name):
    S, D = x.shape
    tr = _rows(S, WIDE_ROW_TILE)

    def body(x_ref, g_ref, o_ref):
        xf = x_ref[...]
        o_ref[...] = (xf * _rstd(xf) * g_ref[...]).astype(o_ref.dtype)

    return pl.pallas_call(
        body, name=name, grid=(S // tr,), in_specs=[_row_spec(tr, D), _vec_spec(D)], out_specs=_row_spec(tr, D),
        out_shape=jax.ShapeDtypeStruct((S, D), BF16), compiler_params=_params(1))(x, g)


def _resid_rms(x, y, g_post, g_next, *, name):
    S, D = x.shape
    tr = _rows(S, WIDE_ROW_TILE)
    with_next = g_next is not None

    def body(*refs):
        if with_next:
            x_ref, y_ref, gp_ref, gn_ref, xo_ref, ho_ref = refs
        else:
            x_ref, y_ref, gp_ref, xo_ref = refs
        yf = y_ref[...]
        xn = x_ref[...] + yf * _rstd(yf) * gp_ref[...]
        xo_ref[...] = xn
        if with_next:
            ho_ref[...] = (xn * _rstd(xn) * gn_ref[...]).astype(ho_ref.dtype)

    row, vec = _row_spec(tr, D), _vec_spec(D)
    ins = [x, y, g_post] + ([g_next] if with_next else [])
    outs = pl.pallas_call(
        body, name=name, grid=(S // tr,), in_specs=[row, row, vec] + ([vec] if with_next else []),
        out_specs=[row] + ([row] if with_next else []),
        out_shape=[jax.ShapeDtypeStruct((S, D), F32)] + ([jax.ShapeDtypeStruct((S, D), BF16)] if with_next else []),
        compiler_params=_params(1))(*ins)
    return (outs[0], outs[1]) if with_next else (outs[0], None)


def _rms_bwd(x, g, dy, *, name):
    S, D = x.shape
    tr = _rows(S, WIDE_ROW_TILE)

    def body(x_ref, g_ref, dy_ref, dx_ref, dg_ref):
        dx, dg = _rms_bwd_math(x_ref[...], g_ref[...], dy_ref[...])
        dx_ref[...] = dx.astype(dx_ref.dtype)
        _accumulate(dg_ref, dg)

    row, vec = _row_spec(tr, D), _vec_spec(D)
    return pl.pallas_call(
        body, name=name, grid=(S // tr,), in_specs=[row, vec, row], out_specs=[row, vec],
        out_shape=[jax.ShapeDtypeStruct((S, D), BF16), jax.ShapeDtypeStruct((1, D), F32)],
        compiler_params=_params(1))(x, g, dy)


def _rms_bwd_chain(xa, ga, da, resid, xb, gb, *, name):
    S, D = xa.shape
    tr = _rows(S)
    chain = xb is not None

    def body(*refs):
        if chain:
            xa_ref, ga_ref, da_ref, rs_ref, xb_ref, gb_ref, dx_ref, dga_ref, dxb_ref, dgb_ref = refs
        else:
            xa_ref, ga_ref, da_ref, rs_ref, dx_ref, dga_ref = refs
        dxa, dga = _rms_bwd_math(xa_ref[...], ga_ref[...], da_ref[...])
        dx = rs_ref[...] + dxa
        dx_ref[...] = dx
        _accumulate(dga_ref, dga)
        if chain:
            dxb, dgb = _rms_bwd_math(xb_ref[...], gb_ref[...], dx)
            dxb_ref[...] = dxb.astype(dxb_ref.dtype)
            _accumulate(dgb_ref, dgb)

    row, vec = _row_spec(tr, D), _vec_spec(D)
    ins = [xa, ga, da, resid] + ([xb, gb] if chain else [])
    outs = pl.pallas_call(
        body, name=name, grid=(S // tr,), in_specs=[row, vec, row, row] + ([row, vec] if chain else []),
        out_specs=[row, vec] + ([row, vec] if chain else []),
        out_shape=[jax.ShapeDtypeStruct((S, D), F32), jax.ShapeDtypeStruct((1, D), F32)]
        + ([jax.ShapeDtypeStruct((S, D), BF16), jax.ShapeDtypeStruct((1, D), F32)] if chain else []),
        compiler_params=_params(1))(*ins)
    return tuple(outs) if chain else (outs[0], outs[1], None, None)


def _loss_head(y, target, *, name):
    S, D = y.shape
    tr = _rows(S, WIDE_ROW_TILE)
    n_tiles = S // tr

    def body(y_ref, t_ref, dy_ref, loss_ref, acc_ref):
        err = y_ref[...] - t_ref[...]
        dy_ref[...] = err * (1.0 / D)
        _accumulate(acc_ref, jnp.sum(err * err, axis=0, keepdims=True))

        @pl.when(pl.program_id(0) == n_tiles - 1)
        def _():
            loss_ref[...] = jnp.sum(acc_ref[...], axis=1, keepdims=True) * (0.5 / D)

    row = _row_spec(tr, D)
    return pl.pallas_call(
        body, name=name, grid=(n_tiles,), in_specs=[row, row],
        out_specs=[row, pl.BlockSpec((1, 1), lambda i: (0, 0))],
        out_shape=[jax.ShapeDtypeStruct((S, D), F32), jax.ShapeDtypeStruct((1, 1), F32)],
        scratch_shapes=[pltpu.VMEM((1, D), F32)], compiler_params=_params(1))(y, target)


SB_TILE = 256
SB_PAIRS = SB_HEADS * SB_HEAD_DIM // LANES
SB_DEAD_LOG = -110.0


def _log_sigmoids(z):
    ls = jnp.minimum(z, 0.0) - jnp.log(1.0 + jnp.exp(-jnp.abs(z)))
    return ls, ls - z


def _running_sums(x, tri2):
    hi = x.astype(BF16)
    lo = (x - hi.astype(F32)).astype(BF16)
    return jnp.dot(jnp.concatenate([hi, lo], axis=1), tri2, preferred_element_type=F32)


def _tri(T, cmp):
    j = lax.broadcasted_iota(jnp.int32, (T, T), 0)
    s = lax.broadcasted_iota(jnp.int32, (T, T), 1)
    m = jnp.where(cmp(j, s), 1.0, 0.0).astype(BF16)
    return jnp.concatenate([m, m], axis=0)


def _head_masks():
    lane = lax.broadcasted_iota(jnp.int32, (1, LANES), 1)
    return [lane < SB_HEAD_DIM, lane >= SB_HEAD_DIM]


def _cargo(refs, n_in, n_out, cargo, scatter):
    n = len(cargo)
    if not n:
        return refs, lambda first: None, lambda last: None
    ins = refs[n_in:n_in + n]
    outs = refs[n_in + n + n_out:n_in + n + n_out + n]
    sems = refs[len(refs) - 3:]
    own = refs[:n_in] + refs[n_in + n:n_in + n + n_out] + refs[n_in + n + n_out + n:len(refs) - 3]

    def start(first):
        @pl.when(first)
        def _():
            for cp in _chip_copies(ins, outs, *sems, scatter=scatter):
                cp.start()

    def finish(last):
        @pl.when(last)
        def _():
            for cp in _chip_copies(ins, outs, *sems, scatter=scatter):
                cp.wait()

    return own, start, finish


def _sba_fwd(qkv, *, name, cargo=()):
    S = qkv.shape[0]
    T = min(SB_TILE, S)
    nq = S // T
    scale = SB_HEAD_DIM ** -0.5

    def body(*refs):
        (q_ref, k_ref, v_ref, o_ref, t_ref, first_ref), start_cargo, finish_cargo = _cargo(refs, 3, 3, cargo, False)
        p, i = pl.program_id(0), pl.program_id(1)
        start_cargo(jnp.logical_and(p == 0, i == 0))
        row = lax.broadcasted_iota(jnp.int32, (T, T), 0)
        col = lax.broadcasted_iota(jnp.int32, (T, T), 1)
        strict = col < row
        after = _tri(T, lambda j, s: j > s)
        masks = _head_masks()
        q = q_ref[...] * scale
        qs = [jnp.where(hm, q, jnp.zeros_like(q)) for hm in masks]

        def walk(tiles, carry):
            values, zs = [], []
            for j, diag in tiles:
                rows = pl.ds(pl.multiple_of(j * T, T), T)
                kb = k_ref[rows, :]
                values.append(v_ref[rows, :])
                zs += [(lax.dot_general(qh, kb, (((1,), (1,)), ((), ())), preferred_element_type=F32), diag) for qh in qs]
            logs, suffixes = [], []
            for z, diag in zs:
                ls, ln = _log_sigmoids(z)
                ln = jnp.where(strict, ln, 0.0) if diag else ln
                logs.append((ls, ln))
                suffixes.append(_running_sums(ln, after))
            for t, (_, diag) in enumerate(tiles):
                out = []
                for h, (C, acc) in enumerate(carry):
                    ls, ln = logs[2 * t + h]
                    a = jnp.exp(ls + suffixes[2 * t + h] + C)
                    if diag:
                        a = jnp.where(strict, a, 0.0)
                    acc = acc + jnp.dot(a.astype(BF16), values[t], preferred_element_type=F32)
                    out.append((C + jnp.sum(ln, axis=1, keepdims=True), acc))
                carry = tuple(out)
            return carry

        fresh = (jnp.zeros((T, 1), F32), jnp.zeros((T, LANES), F32))
        carry = lax.cond(i > 0, lambda: walk([(i, True), (i - 1, False)], (fresh, fresh)),
                         lambda: walk([(i, True)], (fresh, fresh)))

        def alive(state):
            j, ((C0, _), (C1, _)) = state
            return jnp.logical_and(j >= 0, jnp.max(jnp.maximum(C0, C1)) > SB_DEAD_LOG)

        def step(state):
            j, carry = state
            return j - 1, walk([(j, False)], carry)

        j, ((C0, acc0), (C1, acc1)) = lax.while_loop(alive, step, (i - 2, carry))
        t_ref[0] = jnp.broadcast_to(C0, (T, LANES))
        t_ref[1] = jnp.broadcast_to(C1, (T, LANES))
        first_ref[...] = jnp.full((8, LANES), jnp.maximum(j + 1, 0).astype(F32))
        o_ref[...] = jnp.where(masks[0], acc0, acc1).astype(o_ref.dtype)
        finish_cargo(jnp.logical_and(p == SB_PAIRS - 1, i == nq - 1))

    kv = lambda off: pl.BlockSpec((S, LANES), lambda p, i: (0, off + p))
    n = len(cargo)
    return pl.pallas_call(
        body, name=name, grid=(SB_PAIRS, nq),
        in_specs=[pl.BlockSpec((T, LANES), lambda p, i: (i, p)), kv(SB_PAIRS), kv(2 * SB_PAIRS)] + [HBM_SPEC] * n,
        out_specs=[pl.BlockSpec((T, LANES), lambda p, i: (i, p)), pl.BlockSpec((2, T, LANES), lambda p, i: (p, i, 0)),
                   pl.BlockSpec((None, None, 8, LANES), lambda p, i: (p, i, 0, 0))] + [HBM_SPEC] * n,
        out_shape=[jax.ShapeDtypeStruct((S, SB_WIDTH), BF16), jax.ShapeDtypeStruct((SB_HEADS, S, LANES), F32),
                   jax.ShapeDtypeStruct((SB_PAIRS, nq, 8, LANES), F32)] + _chip_exchange_shapes(cargo, False),
        scratch_shapes=_chip_exchange_semaphores(n) if n else [],
        compiler_params=_params(2))(qkv, qkv, qkv, *cargo)


def _sba_bwd(qkv, do, tot, first, *, name, cargo=()):
    S = qkv.shape[0]
    T = min(SB_TILE, S)
    nq = S // T
    scale = SB_HEAD_DIM ** -0.5

    def body(*refs):
        own, start_cargo, finish_cargo = _cargo(refs, 6, 3, cargo, True)
        q_ref, k_ref, v_ref, do_ref, t_ref, first_ref, dq_ref, dk_ref, dv_ref, dk_acc, dv_acc = own
        p, i = pl.program_id(0), pl.program_id(1)
        start_cargo(jnp.logical_and(p == 0, i == 0))

        @pl.when(i == 0)
        def _():
            dk_acc[...] = jnp.zeros_like(dk_acc)
            dv_acc[...] = jnp.zeros_like(dv_acc)

        row = lax.broadcasted_iota(jnp.int32, (T, T), 0)
        col = lax.broadcasted_iota(jnp.int32, (T, T), 1)
        strict = col < row
        upto = _tri(T, lambda j, s: j <= s)
        before = _tri(T, lambda j, s: j < s)
        masks = _head_masks()
        q, do_t = q_ref[...], do_ref[...]
        q = q * scale
        qs = [jnp.where(hm, q, jnp.zeros_like(q)) for hm in masks]
        dos = [jnp.where(hm, do_t, jnp.zeros_like(do_t)) for hm in masks]
        totals = [t_ref[h][:, 0:1] for h in range(2)]
        over_lanes = (((1,), (1,)), ((), ()))
        over_queries = (((0,), (0,)), ((), ()))

        def walk(tiles, carry):
            rows = [pl.ds(pl.multiple_of(j * T, T), T) for j, _ in tiles]
            keys = [k_ref[r, :] for r in rows]
            values = [v_ref[r, :] for r in rows]
            chains = [(t, h) for t in range(len(tiles)) for h in range(2)]
            logs, da = {}, {}
            for t, h in chains:
                z = lax.dot_general(qs[h], keys[t], over_lanes, preferred_element_type=F32)
                ls, ln = _log_sigmoids(z)
                logs[t, h] = (ls, jnp.where(strict, ln, 0.0) if tiles[t][1] else ln)
                da[t, h] = lax.dot_general(dos[h], values[t], over_lanes, preferred_element_type=F32)
            upto_sums = {c: _running_sums(logs[c][1], upto) for c in chains}
            a, g = {}, {}
            P = [c[0] for c in carry]
            for t, h in chains:
                ls, ln = logs[t, h]
                a_th = jnp.exp(ls + ((totals[h] - P[h]) - upto_sums[t, h]))
                a[t, h] = jnp.where(strict, a_th, 0.0) if tiles[t][1] else a_th
                g[t, h] = a[t, h] * da[t, h]
                P[h] = P[h] + jnp.sum(ln, axis=1, keepdims=True)
            before_sums = {c: _running_sums(g[c], before) for c in chains}
            G = [c[1] for c in carry]
            dq = [c[2] for c in carry]
            dz = {}
            for t, h in chains:
                beta = jnp.exp(logs[t, h][0])
                dz_th = g[t, h] * (1.0 - beta) - (G[h] + before_sums[t, h]) * beta
                dz[t, h] = (jnp.where(strict, dz_th, 0.0) if tiles[t][1] else dz_th).astype(BF16)
                G[h] = G[h] + jnp.sum(g[t, h], axis=1, keepdims=True)
            for t, h in chains:
                dq[h] = dq[h] + jnp.dot(dz[t, h], keys[t], preferred_element_type=F32)
            for t in range(len(tiles)):
                dk_acc[rows[t], :] += sum(
                    lax.dot_general(dz[t, h], qs[h], over_queries, preferred_element_type=F32) for h in range(2))
                dv_acc[rows[t], :] += sum(
                    lax.dot_general(a[t, h].astype(BF16), dos[h], over_queries, preferred_element_type=F32)
                    for h in range(2))
            return tuple((P[h], G[h], dq[h]) for h in range(2))

        zero = jnp.zeros((T, 1), F32)
        fresh = (zero, zero, jnp.zeros((T, LANES), F32))
        last_single = jnp.maximum(i - 1, 0)
        j0 = jnp.clip(jnp.max(first_ref[...]).astype(jnp.int32), 0, last_single)
        carry = lax.fori_loop(j0, last_single, lambda j, c: walk([(j, False)], c), (fresh, fresh))
        (_, _, dq0), (_, _, dq1) = lax.cond(i > 0, lambda: walk([(i - 1, False), (i, True)], carry),
                                            lambda: walk([(i, True)], carry))
        dq_ref[...] = (jnp.where(masks[0], dq0, dq1) * scale).astype(dq_ref.dtype)

        @pl.when(i == nq - 1)
        def _():
            dk_ref[...] = dk_acc[...].astype(dk_ref.dtype)
            dv_ref[...] = dv_acc[...].astype(dv_ref.dtype)

        finish_cargo(jnp.logical_and(p == SB_PAIRS - 1, i == nq - 1))

    kv = lambda off: pl.BlockSpec((S, LANES), lambda p, i: (0, off + p))
    tile = lambda off: pl.BlockSpec((T, LANES), lambda p, i: (i, off + p))
    n = len(cargo)
    return pl.pallas_call(
        body, name=name, grid=(SB_PAIRS, nq),
        in_specs=[tile(0), kv(SB_PAIRS), kv(2 * SB_PAIRS), tile(0), pl.BlockSpec((2, T, LANES), lambda p, i: (p, i, 0)),
                  pl.BlockSpec((None, None, 8, LANES), lambda p, i: (p, i, 0, 0))] + [HBM_SPEC] * n,
        out_specs=[tile(0), kv(0), kv(0)] + [HBM_SPEC] * n,
        out_shape=[jax.ShapeDtypeStruct((S, SB_WIDTH), BF16)] * 3 + _chip_exchange_shapes(cargo, True),
        scratch_shapes=[pltpu.VMEM((S, LANES), F32), pltpu.VMEM((S, LANES), F32)]
        + (_chip_exchange_semaphores(n) if n else []),
        compiler_params=_params(2))(qkv, qkv, qkv, do, tot, first, *cargo)


POOL_TILE = 1024


def _by_group(lane, values):
    return jnp.where(lane < 64, values[0], jnp.where(lane < 128, values[1], jnp.where(lane < 192, values[2], values[3])))


def _pool_inv_count(first_row, n_rows):
    t = first_row + lax.broadcasted_iota(jnp.int32, (n_rows, POOL_WIDTH), 0)
    lane = lax.broadcasted_iota(jnp.int32, (n_rows, POOL_WIDTH), 1)
    window = _by_group(lane, POOL_WINDOWS)
    return 1.0 / jnp.clip(t + 1, 1, window).astype(F32), lane


def _pooled(ext, first_row, R):
    n = R + POOL_HALO
    s2 = ext + pltpu.roll(ext, 1, 0)
    s4 = s2 + pltpu.roll(s2, 2, 0)
    s8 = s4 + pltpu.roll(s4, 4, 0)
    s16 = s8 + pltpu.roll(s8, 8, 0)
    inv, lane = _pool_inv_count(first_row - POOL_HALO, n)
    pooled = _by_group(lane, (s2, s4, s8, s16)) * inv - ext
    return pooled[POOL_HALO:, :]


def _pool_specs(S, R, col):
    per = R // POOL_HALO
    tile = pl.BlockSpec((R, POOL_WIDTH), lambda i: (i, col))
    prev = pl.BlockSpec((POOL_HALO, POOL_WIDTH), lambda i: (jnp.maximum(i * per - 1, 0), col))
    return tile, prev


def _pool_fwd(rest, w_bd, scale, *, name):
    S = rest.shape[0]
    R = min(POOL_TILE, S)

    def body(p_ref, prev_ref, w_ref, s_ref, o_ref, ext_ref):
        i = pl.program_id(0)
        ext_ref[:POOL_HALO, :] = jnp.where(i > 0, prev_ref[...], 0.0)
        ext_ref[POOL_HALO:, :] = p_ref[...]
        pooled = _pooled(ext_ref[...], i * R, R)
        mixed = jnp.dot(pooled.astype(BF16), w_ref[...], preferred_element_type=F32)
        o_ref[...] = (mixed * s_ref[...]).astype(o_ref.dtype)

    tile, prev = _pool_specs(S, R, 0)
    return pl.pallas_call(
        body, name=name, grid=(S // R,),
        in_specs=[tile, prev, pl.BlockSpec((POOL_WIDTH, POOL_WIDTH), lambda i: (0, 0)), _vec_spec(POOL_WIDTH)],
        out_specs=_row_spec(R, POOL_WIDTH), out_shape=jax.ShapeDtypeStruct((S, POOL_WIDTH), BF16),
        scratch_shapes=[pltpu.VMEM((R + POOL_HALO, POOL_WIDTH), F32)], compiler_params=_params(1))(rest, rest, w_bd, scale)


def _pool_bwd(rest, do, w_bd, scale, *, name):
    S = rest.shape[0]
    R = min(POOL_TILE, S)
    n_tiles = S // R
    per = R // POOL_HALO
    n = R + POOL_HALO

    def body(p_ref, prev_ref, do_ref, nxt_ref, w_ref, s_ref, dp_ref, dw_ref, ds_ref, ext_ref, dext_ref):
        i = pl.program_id(0)
        ext_ref[:POOL_HALO, :] = jnp.where(i > 0, prev_ref[...], 0.0)
        ext_ref[POOL_HALO:, :] = p_ref[...]
        pooled = _pooled(ext_ref[...], i * R, R).astype(BF16)
        w = w_ref[...]
        mixed = jnp.dot(pooled, w, preferred_element_type=F32)
        do_t = do_ref[...]
        _accumulate(ds_ref, jnp.sum(do_t * mixed, axis=0, keepdims=True))
        dext_ref[:R, :] = do_t
        dext_ref[R:, :] = jnp.where(i < n_tiles - 1, nxt_ref[...], 0.0)
        dmixed = (dext_ref[...] * s_ref[...]).astype(BF16)
        dpooled = lax.dot_general(dmixed, w, (((1,), (1,)), ((), ())), preferred_element_type=F32)
        _accumulate(dw_ref, lax.dot_general(pooled, dmixed[:R, :], (((0,), (0,)), ((), ())), preferred_element_type=F32))
        inv, lane = _pool_inv_count(i * R, n)
        u = dpooled * inv
        f2 = u + pltpu.roll(u, n - 1, 0)
        f4 = f2 + pltpu.roll(f2, n - 2, 0)
        f8 = f4 + pltpu.roll(f4, n - 4, 0)
        f16 = f8 + pltpu.roll(f8, n - 8, 0)
        dp = _by_group(lane, (f2, f4, f8, f16)) - dpooled
        dp_ref[...] = dp[:R, :].astype(dp_ref.dtype)

    tile, prev = _pool_specs(S, R, 0)
    nxt = pl.BlockSpec((POOL_HALO, POOL_WIDTH), lambda i: (jnp.minimum((i + 1) * per, S // POOL_HALO - 1), 0))
    full = pl.BlockSpec((POOL_WIDTH, POOL_WIDTH), lambda i: (0, 0))
    return pl.pallas_call(
        body, name=name, grid=(n_tiles,),
        in_specs=[tile, prev, _row_spec(R, POOL_WIDTH), nxt, full, _vec_spec(POOL_WIDTH)],
        out_specs=[_row_spec(R, POOL_WIDTH), full, _vec_spec(POOL_WIDTH)],
        out_shape=[jax.ShapeDtypeStruct((S, POOL_WIDTH), BF16), jax.ShapeDtypeStruct((POOL_WIDTH, POOL_WIDTH), F32),
                   jax.ShapeDtypeStruct((1, POOL_WIDTH), F32)],
        scratch_shapes=[pltpu.VMEM((n, POOL_WIDTH), F32), pltpu.VMEM((n, POOL_WIDTH), F32)],
        compiler_params=_params(1))(rest, rest, do, do, w_bd, scale)


GM_TILE = 1024
GELU_C = math.sqrt(2.0 / math.pi)
GELU_A = 0.044715


def _gelu(x):
    return 0.5 * x * (1.0 + jnp.tanh(GELU_C * (x + GELU_A * x * x * x)))


def _gelu_and_grad(x):
    t = jnp.tanh(GELU_C * (x + GELU_A * x * x * x))
    y = 0.5 * x * (1.0 + t)
    dy = 0.5 * (1.0 + t) + 0.5 * x * (1.0 - t * t) * (GELU_C * (1.0 + 3.0 * GELU_A * x * x))
    return y, dy


def _group_lane_masks():
    lane = lax.broadcasted_iota(jnp.int32, (1, GM_WIDTH), 1)
    return [(lane >= g * GM_GROUP_DIM) & (lane < (g + 1) * GM_GROUP_DIM) for g in range(GM_GROUPS)]


def _stack_groups(x, masks):
    return jnp.concatenate([jnp.where(m, x, jnp.zeros_like(x)) for m in masks], axis=0)


def _gm_mixed(vn, ws_cat, bias, masks, R):
    chunks = []
    for c in range(R // GM_CHUNK):
        vc = vn[c * GM_CHUNK:(c + 1) * GM_CHUNK, :]
        chunks.append(jnp.dot(ws_cat, _stack_groups(vc, masks), preferred_element_type=F32) + bias)
    return jnp.concatenate(chunks, axis=0)


def _gm_specs(S, R):
    u = pl.BlockSpec((R, GM_WIDTH), lambda i: (i, 1))
    v = pl.BlockSpec((R, GM_WIDTH), lambda i: (i, 2))
    ws = pl.BlockSpec((GM_CHUNK, GM_GROUPS * GM_CHUNK), lambda i: (0, 0))
    bias = pl.BlockSpec((GM_CHUNK, GM_WIDTH), lambda i: (0, 0))
    return u, v, ws, bias


def _gm_fwd(rest, gain, ws_cat, bias, *, name):
    S = rest.shape[0]
    R = min(GM_TILE, S)

    def body(u_ref, v_ref, g_ref, ws_ref, b_ref, o_ref):
        gv = _gelu(v_ref[...])
        vn = (gv * _rstd(gv) * g_ref[...]).astype(BF16)
        mixed = _gm_mixed(vn, ws_ref[...], b_ref[...], _group_lane_masks(), R)
        o_ref[...] = (_gelu(u_ref[...]) * mixed).astype(o_ref.dtype)

    u_spec, v_spec, ws_spec, bias_spec = _gm_specs(S, R)
    return pl.pallas_call(
        body, name=name, grid=(S // R,), in_specs=[u_spec, v_spec, _vec_spec(GM_WIDTH), ws_spec, bias_spec],
        out_specs=_row_spec(R, GM_WIDTH), out_shape=jax.ShapeDtypeStruct((S, GM_WIDTH), BF16),
        compiler_params=_params(1))(rest, rest, gain, ws_cat, bias)


def _gm_bwd(rest, do, gain, ws_cat, wst_cat, bias, *, name):
    S = rest.shape[0]
    R = min(GM_TILE, S)

    def body(u_ref, v_ref, do_ref, g_ref, ws_ref, wst_ref, b_ref, du_ref, dv_ref, dg_ref, dws_ref, db_ref):
        masks = _group_lane_masks()
        gain_v = g_ref[...]
        gu, dgu = _gelu_and_grad(u_ref[...])
        gv, dgv = _gelu_and_grad(v_ref[...])
        r = _rstd(gv)
        vn = (gv * r * gain_v).astype(BF16)
        mixed = _gm_mixed(vn, ws_ref[...], b_ref[...], masks, R)
        do_t = do_ref[...]
        du_ref[...] = (do_t * mixed * dgu).astype(du_ref.dtype)
        dmix = do_t * gu
        dmix_b = dmix.astype(BF16)
        wst = wst_ref[...]
        dvn_chunks, db, dws = [], None, [None] * GM_GROUPS
        for c in range(R // GM_CHUNK):
            rows = slice(c * GM_CHUNK, (c + 1) * GM_CHUNK)
            dc, dcb, vc = dmix[rows, :], dmix_b[rows, :], vn[rows, :]
            db = dc if db is None else db + dc
            dvn_chunks.append(jnp.dot(wst, _stack_groups(dcb, masks), preferred_element_type=F32))
            for g, m in enumerate(masks):
                part = lax.dot_general(jnp.where(m, dcb, jnp.zeros_like(dcb)), vc, (((1,), (1,)), ((), ())),
                                       preferred_element_type=F32)
                dws[g] = part if dws[g] is None else dws[g] + part
        dvn = jnp.concatenate(dvn_chunks, axis=0)
        lane = lax.broadcasted_iota(jnp.int32, (1, LANES), 1)
        db_groups = jnp.zeros((GM_CHUNK, LANES), F32)
        for g, m in enumerate(masks):
            total = jnp.sum(jnp.where(m, db, 0.0), axis=1, keepdims=True)
            db_groups = db_groups + jnp.where(lane == g, total, 0.0)
        _accumulate(db_ref, db_groups)
        i = pl.program_id(0)
        for g in range(GM_GROUPS):
            @pl.when(i == 0)
            def _(g=g):
                dws_ref[g] = dws[g]

            @pl.when(i > 0)
            def _(g=g):
                dws_ref[g] += dws[g]
        _accumulate(dg_ref, jnp.sum(dvn * gv * r, axis=0, keepdims=True))
        gd = gain_v * dvn
        dgv_in = r * gd - gv * (r * r * r) * jnp.mean(gv * gd, axis=-1, keepdims=True)
        dv_ref[...] = (dgv_in * dgv).astype(dv_ref.dtype)

    u_spec, v_spec, ws_spec, bias_spec = _gm_specs(S, R)
    row, vec = _row_spec(R, GM_WIDTH), _vec_spec(GM_WIDTH)
    dws_spec = pl.BlockSpec((GM_GROUPS, GM_CHUNK, GM_CHUNK), lambda i: (0, 0, 0))
    return pl.pallas_call(
        body, name=name, grid=(S // R,), in_specs=[u_spec, v_spec, row, vec, ws_spec, ws_spec, bias_spec],
        out_specs=[row, row, vec, dws_spec, pl.BlockSpec((GM_CHUNK, LANES), lambda i: (0, 0))],
        out_shape=[jax.ShapeDtypeStruct((S, GM_WIDTH), BF16)] * 2
        + [jax.ShapeDtypeStruct((1, GM_WIDTH), F32), jax.ShapeDtypeStruct((GM_GROUPS, GM_CHUNK, GM_CHUNK), F32),
           jax.ShapeDtypeStruct((GM_CHUNK, LANES), F32)],
        compiler_params=_params(1))(rest, rest, do, gain, ws_cat, wst_cat, bias)


GATE_ROWS = 1024
GATE_COLS = 512
GATE_BLOCKS = D_MODEL // GATE_COLS


def _gate_spec(tr, k):
    return pl.BlockSpec((tr, GATE_COLS), lambda i, j: (i, GATE_BLOCKS * k + j))


def _merge_fwd(gates, branches, *, name):
    S = gates.shape[0]
    tr = min(GATE_ROWS, S)

    def body(g0, g1, g2, b0, b1, b2, o_ref):
        acc = None
        for g_ref, b_ref in ((g0, b0), (g1, b1), (g2, b2)):
            term = jax.nn.sigmoid(g_ref[...].astype(F32)) * b_ref[...].astype(F32)
            acc = term if acc is None else acc + term
        o_ref[...] = acc.astype(o_ref.dtype)

    tile = pl.BlockSpec((tr, GATE_COLS), lambda i, j: (i, j))
    return pl.pallas_call(
        body, name=name, grid=(S // tr, GATE_BLOCKS),
        in_specs=[_gate_spec(tr, k) for k in range(N_BRANCH)] + [tile] * N_BRANCH, out_specs=tile,
        out_shape=jax.ShapeDtypeStruct((S, D_MODEL), BF16), compiler_params=_params(2))(gates, gates, gates, *branches)


def _merge_bwd(gates, branches, dmerged, *, name):
    S = gates.shape[0]
    tr = min(GATE_ROWS, S)

    def body(g0, g1, g2, b0, b1, b2, dm_ref, dg0, dg1, dg2, db0, db1, db2):
        dm = dm_ref[...].astype(F32)
        for g_ref, b_ref, dg_ref, db_ref in ((g0, b0, dg0, db0), (g1, b1, dg1, db1), (g2, b2, dg2, db2)):
            s = jax.nn.sigmoid(g_ref[...].astype(F32))
            db_ref[...] = (dm * s).astype(db_ref.dtype)
            dg_ref[...] = (dm * b_ref[...].astype(F32) * s * (1.0 - s)).astype(dg_ref.dtype)

    tile = pl.BlockSpec((tr, GATE_COLS), lambda i, j: (i, j))
    return pl.pallas_call(
        body, name=name, grid=(S // tr, GATE_BLOCKS),
        in_specs=[_gate_spec(tr, k) for k in range(N_BRANCH)] + [tile] * (N_BRANCH + 1), out_specs=[tile] * (2 * N_BRANCH),
        out_shape=[jax.ShapeDtypeStruct((S, D_MODEL), BF16)] * (2 * N_BRANCH),
        compiler_params=_params(2))(gates, gates, gates, *branches, dmerged)


TILE_BYTES = 40 * 1024 * 1024


BF16_ROWS = 16


def _tile_rows(rows, cols, n_arrays):
    padded = -(-cols // LANES) * LANES
    cap = max(BF16_ROWS, TILE_BYTES // (2 * n_arrays * padded * 4))
    best = None
    for tr in range(BF16_ROWS, min(rows, cap) + 1, BF16_ROWS):
        if rows % tr == 0:
            best = tr
    assert best is not None, (rows, cols)
    return best


def _sum_slots(stack, *, name):
    n, R, C = stack.shape
    tr = _tile_rows(R, C, n + 1)

    def body(s_ref, o_ref):
        acc = s_ref[0].astype(F32)
        for k in range(1, n):
            acc = acc + s_ref[k].astype(F32)
        o_ref[...] = acc

    return pl.pallas_call(
        body, name=name, grid=(R // tr,), in_specs=[pl.BlockSpec((n, tr, C), lambda i: (0, i, 0))],
        out_specs=_row_spec(tr, C), out_shape=jax.ShapeDtypeStruct((R, C), F32), compiler_params=_params(1))(stack)


def _add_own_half(parts, received, core, *, name):
    n, R, C = parts.shape
    half = R // 2
    tr = _tile_rows(half, C, 3)
    steps = half // tr

    def body(core_ref, own_ref, got_ref, o_ref):
        o_ref[...] = (own_ref[...] + got_ref[...]).astype(o_ref.dtype)

    tile = pl.BlockSpec((None, tr, C), lambda d, i, core_ref: (d, i, 0))
    own = pl.BlockSpec((None, tr, C), lambda d, i, core_ref: (d, core_ref[0] * steps + i, 0))
    return pl.pallas_call(
        body, name=name, out_shape=jax.ShapeDtypeStruct((n, half, C), BF16),
        grid_spec=pltpu.PrefetchScalarGridSpec(num_scalar_prefetch=1, grid=(n, steps), in_specs=[own, tile], out_specs=tile),
        compiler_params=_params(2))(core, parts, received)


def _adamw_math(w, m, v, g):
    m_new = ADAM_B1 * m + (1.0 - ADAM_B1) * g
    v_new = ADAM_B2 * v + (1.0 - ADAM_B2) * jnp.square(g)
    m_hat = m_new / (1.0 - ADAM_B1 ** ADAM_STEP)
    v_hat = v_new / (1.0 - ADAM_B2 ** ADAM_STEP)
    return -ADAM_LR * (m_hat / (jnp.sqrt(v_hat) + ADAM_EPS) + ADAM_WD * w), m_new, v_new


def _adamw_halves(w, m, v, mine, theirs, core, *, name):
    L, r, C = w.shape
    tr = _tile_rows(r // 2, C, 9)
    steps = r // 2 // tr

    def body(core_ref, w_ref, m_ref, v_ref, mine_ref, theirs_ref, go_ref, d_ref, mo_ref, vo_ref):
        in_my_half = pl.program_id(1) // steps == core_ref[0]
        g = jnp.where(in_my_half, mine_ref[...], theirs_ref[...])
        go_ref[...] = g
        d_ref[...], mo_ref[...], vo_ref[...] = _adamw_math(w_ref[...], m_ref[...], v_ref[...], g)

    row = pl.BlockSpec((None, tr, C), lambda l, i, core_ref: (l, i, 0))
    half = pl.BlockSpec((None, tr, C), lambda l, i, core_ref: (l, i % steps, 0))
    return pl.pallas_call(
        body, name=name, out_shape=[jax.ShapeDtypeStruct((L, r, C), F32)] * 4,
        grid_spec=pltpu.PrefetchScalarGridSpec(
            num_scalar_prefetch=1, grid=(L, r // tr), in_specs=[row, row, row, half, half], out_specs=[row] * 4),
        compiler_params=_params(2))(core, w, m, v, mine, theirs)


HBM_SPEC = pl.BlockSpec(memory_space=pl.ANY)


def _position():
    return lax.axis_index("x"), lax.axis_index("y"), lax.axis_index("c")


def _other_chips(x, y):
    return [(1 - x, y), (x, 1 - y), (1 - x, 1 - y)]


def _chip_exchange(arrays, *, scatter, name):
    n = len(arrays)

    def body(*refs):
        copies = _chip_copies(refs[:n], refs[n:2 * n], *refs[2 * n:], scatter=scatter)
        for cp in copies:
            cp.start()
        for cp in copies:
            cp.wait()

    return pl.pallas_call(
        body, name=name, in_specs=[HBM_SPEC] * n, out_specs=[HBM_SPEC] * n, out_shape=_chip_exchange_shapes(arrays, scatter),
        scratch_shapes=_chip_exchange_semaphores(n))(*arrays)


def _chip_exchange_shapes(arrays, scatter):
    return [jax.ShapeDtypeStruct(a.shape if scatter else (N_CHIPS, 2) + a.shape, a.dtype) for a in arrays]


def _chip_exchange_semaphores(n):
    return [pltpu.SemaphoreType.DMA((3 * n,)), pltpu.SemaphoreType.DMA((3 * n,)), pltpu.SemaphoreType.DMA((n,))]


def _chip_copies(ins, outs, send_sems, recv_sems, local_sems, *, scatter):
    x, y, c = _position()
    me = 2 * x + y
    copies = []
    for a in range(len(ins)):
        own = ins[a].at[me] if scatter else ins[a]
        slot = outs[a].at[me] if scatter else outs[a].at[me, c]
        copies.append(pltpu.make_async_copy(own, slot, local_sems.at[a]))
        for k, (px, py) in enumerate(_other_chips(x, y)):
            src = ins[a].at[2 * px + py] if scatter else ins[a]
            copies.append(pltpu.make_async_remote_copy(
                src_ref=src, dst_ref=slot, send_sem=send_sems.at[3 * a + k],
                recv_sem=recv_sems.at[3 * a + k], device_id=(px, py, c), device_id_type=MESH))
    return copies


def _sibling_fill(arrays, *, name):
    n = len(arrays)

    def body(*refs):
        ins, outs = refs[:n], refs[n:2 * n]
        send_sems, recv_sems = refs[2 * n:]
        x, y, c = _position()
        copies = []
        for a in range(n):
            cp = pltpu.make_async_remote_copy(
                src_ref=ins[a].at[:, c], dst_ref=outs[a].at[:, c], send_sem=send_sems.at[a], recv_sem=recv_sems.at[a],
                device_id=(x, y, 1 - c), device_id_type=MESH)
            cp.start()
            copies.append(cp)
        for cp in copies:
            cp.wait()

    return pl.pallas_call(
        body, name=name, in_specs=[HBM_SPEC] * n, out_specs=[HBM_SPEC] * n,
        out_shape=[jax.ShapeDtypeStruct(a.shape, a.dtype) for a in arrays],
        input_output_aliases={a: a for a in range(n)},
        scratch_shapes=[pltpu.SemaphoreType.DMA((n,)), pltpu.SemaphoreType.DMA((n,))],
    )(*arrays)


def _sibling_swap(arrays, *, name):
    n = len(arrays)

    def body(*refs):
        ins, outs = refs[:n], refs[n:2 * n]
        send_sems, recv_sems = refs[2 * n:]
        x, y, c = _position()
        copies = []
        for a in range(n):
            cp = pltpu.make_async_remote_copy(
                src_ref=ins[a], dst_ref=outs[a], send_sem=send_sems.at[a], recv_sem=recv_sems.at[a],
                device_id=(x, y, 1 - c), device_id_type=MESH)
            cp.start()
            copies.append(cp)
        for cp in copies:
            cp.wait()

    return pl.pallas_call(
        body, name=name, in_specs=[HBM_SPEC] * n, out_specs=[HBM_SPEC] * n,
        out_shape=[jax.ShapeDtypeStruct(a.shape, a.dtype) for a in arrays],
        scratch_shapes=[pltpu.SemaphoreType.DMA((n,)), pltpu.SemaphoreType.DMA((n,))],
    )(*arrays)


def _sibling_other_half(arrays, *, name):
    n = len(arrays)

    def body(*refs):
        ins, outs = refs[:n], refs[n:2 * n]
        send_sems, recv_sems = refs[2 * n:]
        x, y, c = _position()
        copies = []
        for a in range(n):
            half = ins[a].shape[1] // 2
            theirs = ins[a].at[:, pl.ds(pl.multiple_of((1 - c) * half, BF16_ROWS), half), :]
            cp = pltpu.make_async_remote_copy(
                src_ref=theirs, dst_ref=outs[a], send_sem=send_sems.at[a], recv_sem=recv_sems.at[a],
                device_id=(x, y, 1 - c), device_id_type=MESH)
            cp.start()
            copies.append(cp)
        for cp in copies:
            cp.wait()

    return pl.pallas_call(
        body, name=name, in_specs=[HBM_SPEC] * n, out_specs=[HBM_SPEC] * n,
        out_shape=[jax.ShapeDtypeStruct((a.shape[0], a.shape[1] // 2, a.shape[2]), a.dtype) for a in arrays],
        scratch_shapes=[pltpu.SemaphoreType.DMA((n,)), pltpu.SemaphoreType.DMA((n,))],
    )(*arrays)


def _small_update(grads, ws, ms, vs, *, name):
    n, L = len(ws), ws[0].shape[0]
    pieces = [g for per_layer in grads for g in per_layer]
    np_ = len(pieces)

    def body(*refs):
        g_in, refs = refs[:np_], refs[np_:]
        w_in, m_in, v_in, g_out, d_out, m_out, v_out = (refs[k * n:(k + 1) * n] for k in range(7))
        from_sibling, chip_sums = refs[7 * n:7 * n + np_], refs[7 * n + np_:7 * n + 2 * np_]
        sibling_send, sibling_recv, chip_send, chip_recv = refs[7 * n + 2 * np_:]
        x, y, c = _position()
        me = 2 * x + y
        swaps = [pltpu.make_async_remote_copy(
            src_ref=g_in[p], dst_ref=from_sibling[p], send_sem=sibling_send.at[p], recv_sem=sibling_recv.at[p],
            device_id=(x, y, 1 - c), device_id_type=MESH) for p in range(np_)]
        for cp in swaps:
            cp.start()
        for cp in swaps:
            cp.wait()
        for p in range(np_):
            chip_sums[p][me] = g_in[p][...] + from_sibling[p][...]
        sends = [pltpu.make_async_remote_copy(
            src_ref=chip_sums[p].at[me], dst_ref=chip_sums[p].at[me], send_sem=chip_send.at[3 * p + k],
            recv_sem=chip_recv.at[3 * p + k], device_id=(px, py, c), device_id_type=MESH)
            for p in range(np_) for k, (px, py) in enumerate(_other_chips(x, y))]
        for cp in sends:
            cp.start()
        for cp in sends:
            cp.wait()
        for a in range(n):
            for l in range(L):
                sums = chip_sums[a * L + l]
                g = sums[0]
                for s in range(1, N_CHIPS):
                    g = g + sums[s]
                g_out[a][l] = g
                d_out[a][l], m_out[a][l], v_out[a][l] = _adamw_math(w_in[a][l], m_in[a][l], v_in[a][l], g)

    vmem = pl.BlockSpec(memory_space=pltpu.VMEM)
    shapes = [jax.ShapeDtypeStruct(w.shape, F32) for w in ws]
    outs = pl.pallas_call(
        body, name=name, in_specs=[vmem] * (np_ + 3 * n), out_specs=[vmem] * (4 * n), out_shape=shapes * 4,
        scratch_shapes=[pltpu.VMEM(g.shape, F32) for g in pieces] + [pltpu.VMEM((N_CHIPS,) + g.shape, F32) for g in pieces]
        + [pltpu.SemaphoreType.DMA((np_,)), pltpu.SemaphoreType.DMA((np_,)),
           pltpu.SemaphoreType.DMA((3 * np_,)), pltpu.SemaphoreType.DMA((3 * np_,))],
        compiler_params=pltpu.CompilerParams(vmem_limit_bytes=VMEM_LIMIT),
    )(*pieces, *ws, *ms, *vs)
    return outs[:n], outs[n:2 * n], outs[2 * n:3 * n], outs[3 * n:]


def _relu2(p):
    return p, jnp.square(jnp.maximum(p, 0.0))


def _relu2_grad(p, a):
    return (p * (2.0 * jnp.maximum(a.astype(F32), 0.0)),)


def _mixer_constants(w_pool, w_spatial, b_spatial):
    eye = jnp.eye(len(POOL_WINDOWS), dtype=F32)
    w_bd = (eye[:, None, :, None] * w_pool[:, :, None, :]).reshape(POOL_WIDTH, POOL_WIDTH).astype(BF16)
    causal = jnp.tril(jnp.ones((GM_CHUNK, GM_CHUNK), dtype=bool))
    ws = jnp.where(causal[None], w_spatial, 0.0).astype(BF16)
    ws_cat = ws.transpose(1, 0, 2).reshape(GM_CHUNK, GM_GROUPS * GM_CHUNK)
    wst_cat = ws.transpose(2, 0, 1).reshape(GM_CHUNK, GM_GROUPS * GM_CHUNK)
    bias = jnp.repeat(b_spatial.T, GM_GROUP_DIM, axis=1)
    return w_bd, ws_cat, wst_cat, bias


def _local_step(x, target, half_shards, small, core):
    L = small["g_mix_pre"].shape[0]
    vec = lambda name, l: small[name][l][None, :]
    consts = [_mixer_constants(small["w_pool"][l], small["w_spatial"][l], small["b_spatial"][l]) for l in range(L)]
    core_index = core.astype(jnp.int32).reshape(1)
    first_used, used_later = BIG_WEIGHTS[:1], BIG_WEIGHTS[1:]
    weights = [{} for _ in range(L)]

    def finish_gather(wanted, gathered):
        for (n, ll), both in zip(wanted, _sibling_fill(gathered, name="swap_weight_halves")):
            weights[ll][n] = _full_weight(n, both)

    def core_sums(names, l):
        parts = [_parts_by_chip(n, gb[n][l]) for n in names]
        from_sibling = _sibling_other_half(parts, name="swap_grad_halves")
        return [(n, l, _add_own_half(p, f, core_index, name="sum_cores")) for n, p, f in zip(names, parts, from_sibling)]

    wanted = [(n, 0) for n in first_used]
    finish_gather(wanted, _chip_exchange([half_shards[n][ll] for n, ll in wanted], scatter=False, name="gather_weights"))
    saved = []
    h = _rms_fwd(x, vec("g_mix_pre", 0), name="rms_in")
    for l in range(L):
        w_bd, ws_cat, wst_cat, bias = consts[l]
        proj = lambda n, off, dtype, name: _matmul(h, weights[l]["w_in"], tb=True, n=n, bn=PROJ_BLOCK,
                                                   b_col_off=off // PROJ_BLOCK, out_dtypes=(dtype,), name=name)
        qkv = proj(QKV_WIDTH, 0, BF16, "proj_qkv")
        rest = proj(MIX_WIDTH, QKV_WIDTH, F32, "proj_mix")
        gates = proj(GATE_WIDTH, QKV_WIDTH + MIX_WIDTH, BF16, "proj_gates")
        wanted = [(n, l) for n in used_later] + [(n, l + 1) for n in first_used if l + 1 < L]
        o_sb, tot, first, *gathered = _sba_fwd(qkv, cargo=[half_shards[n][ll] for n, ll in wanted], name="sba_fwd_gather")
        finish_gather(wanted, gathered)
        o_pool = _pool_fwd(rest, w_bd, vec("pool_scale", l), name="pool_fwd")
        o_gm = _gm_fwd(rest, vec("gm_gain", l), ws_cat, bias, name="gm_fwd")
        branches = (_matmul(o_sb, weights[l]["w_br_sb"], out_dtypes=(BF16,), name="br_sb"),
                    _matmul(o_pool, weights[l]["w_br_pool"], out_dtypes=(BF16,), name="br_pool"),
                    _matmul(o_gm, weights[l]["w_br_gm"], out_dtypes=(BF16,), name="br_gm"))
        merged = _merge_fwd(gates, branches, name="merge_fwd")
        y = _matmul(merged, weights[l]["w_out"], name="out_proj")
        x1, h2 = _resid_rms(x, y, vec("g_mix_post", l), vec("g_ff_pre", l), name="resid_mix")
        a, r = _matmul(h2, weights[l]["w_ff_in"], out_dtypes=(BF16, BF16), epilogue=_relu2, name="ff_in")
        ff = _matmul(r, weights[l]["w_ff_out"], name="ff_out")
        g_next = vec("g_mix_pre", l + 1) if l + 1 < L else None
        x2, h_next = _resid_rms(x1, ff, vec("g_ff_post", l), g_next, name="resid_ff" if l + 1 < L else "resid_last")
        saved.append(dict(x=x, h=h, qkv=qkv, rest=rest, gates=gates, o_sb=o_sb, tot=tot, first=first, o_pool=o_pool,
                          o_gm=o_gm, branches=branches, merged=merged, y=y, x1=x1, h2=h2, a=a, r=r, ff=ff))
        x, h = x2, h_next

    dx2, loss = _loss_head(x, target, name="loss_head")
    gb = {k: [None] * L for k in ("w_in", "w_br_sb", "w_br_pool", "w_br_gm", "w_out", "w_ff_in", "w_ff_out")}
    gs = {k: [None] * L for k in ("w_pool", "pool_scale", "gm_gain", "w_spatial", "b_spatial", "g_mix_pre",
                                  "g_mix_post", "g_ff_pre", "g_ff_post")}
    d_ff, gs["g_ff_post"][L - 1] = _rms_bwd(saved[-1]["ff"], vec("g_ff_post", L - 1), dx2, name="rms_bwd_last")
    received = [{} for _ in range(L)]
    waiting = []
    for l in reversed(range(L)):
        s = saved[l]
        w_bd, ws_cat, wst_cat, bias = consts[l]
        da = _matmul(d_ff, weights[l]["w_ff_out"], tb=True, out_dtypes=(BF16,), extras=(s["a"],), epilogue=_relu2_grad,
                     name="ff_out_dx")
        gb["w_ff_out"][l] = _matmul(s["r"], d_ff, ta=True, name="ff_out_dw")
        dh2 = _matmul(da, weights[l]["w_ff_in"], tb=True, out_dtypes=(BF16,), name="ff_in_dx")
        gb["w_ff_in"][l] = _matmul(s["h2"], da, ta=True, name="ff_in_dw")
        dx1, gs["g_ff_pre"][l], dy, gs["g_mix_post"][l] = _rms_bwd_chain(
            s["x1"], vec("g_ff_pre", l), dh2, dx2, s["y"], vec("g_mix_post", l), name="rms_bwd_mid")
        dmerged = _matmul(dy, weights[l]["w_out"], tb=True, out_dtypes=(BF16,), name="out_proj_dx")
        gb["w_out"][l] = _matmul(s["merged"], dy, ta=True, name="out_proj_dw")
        dg0, dg1, dg2, db_sb, db_pool, db_gm = _merge_bwd(s["gates"], s["branches"], dmerged, name="merge_bwd")
        do_sb = _matmul(db_sb, weights[l]["w_br_sb"], tb=True, out_dtypes=(BF16,), name="br_sb_dx")
        gb["w_br_sb"][l] = _matmul(s["o_sb"], db_sb, ta=True, name="br_sb_dw")
        do_pool = _matmul(db_pool, weights[l]["w_br_pool"], tb=True, name="br_pool_dx")
        gb["w_br_pool"][l] = _matmul(s["o_pool"], db_pool, ta=True, name="br_pool_dw")
        do_gm = _matmul(db_gm, weights[l]["w_br_gm"], tb=True, name="br_gm_dx")
        gb["w_br_gm"][l] = _matmul(s["o_gm"], db_gm, ta=True, name="br_gm_dw")
        waiting += core_sums(used_later, l)
        dq, dk, dv, *arrived = _sba_bwd(s["qkv"], do_sb, s["tot"], s["first"], cargo=[c for _, _, c in waiting],
                                        name="sba_bwd_exchange")
        for (n, ll, _), got in zip(waiting, arrived):
            received[ll][n] = got
        dp, dw_bd, gs["pool_scale"][l] = _pool_bwd(s["rest"], do_pool, w_bd, vec("pool_scale", l), name="pool_bwd")
        du, dgv, gs["gm_gain"][l], dws, db = _gm_bwd(s["rest"], do_gm, vec("gm_gain", l), ws_cat, wst_cat, bias,
                                                      name="gm_bwd")
        gs["w_pool"][l] = jnp.stack([dw_bd[g * 64:(g + 1) * 64, g * 64:(g + 1) * 64] for g in range(len(POOL_WINDOWS))])
        gs["w_spatial"][l] = jnp.where(jnp.tril(jnp.ones((GM_CHUNK, GM_CHUNK), dtype=bool))[None], dws, 0.0)
        gs["b_spatial"][l] = db[:, :GM_GROUPS].T
        dproj = jnp.concatenate([dq, dk, dv, dp, du, dgv, dg0, dg1, dg2], axis=1)
        dh = _matmul(dproj, weights[l]["w_in"], bk=D_IN // 3, out_dtypes=(BF16,), name="proj_dx")
        gb["w_in"][l] = _matmul(dproj, s["h"], ta=True, bm=PROJ_BLOCK, name="proj_dw")
        waiting = core_sums(first_used, l)
        if l == 0:
            for (n, ll, _), got in zip(waiting, _chip_exchange([c for _, _, c in waiting], scatter=True, name="exchange_grads")):
                received[ll][n] = got
        if l > 0:
            dx2, gs["g_mix_pre"][l], d_ff, gs["g_ff_post"][l - 1] = _rms_bwd_chain(
                s["x"], vec("g_mix_pre", l), dh, dx1, saved[l - 1]["ff"], vec("g_ff_post", l - 1), name="rms_bwd_mid")
        else:
            dx2, gs["g_mix_pre"][l], _, _ = _rms_bwd_chain(s["x"], vec("g_mix_pre", l), dh, dx1, None, None,
                                                           name="rms_bwd_first")
    return loss, dx2, received, gs


TRANSPOSED = ("w_in",)
COLUMN_SHARDED = ("w_br_sb", "w_br_pool", "w_br_gm", "w_ff_in")
ROW_SHARDED = ("w_in", "w_out", "w_ff_out")
BIG_WEIGHTS = ("w_in", "w_br_sb", "w_br_pool", "w_br_gm", "w_ff_in", "w_out", "w_ff_out")
SMALL_WEIGHTS = ("w_pool", "pool_scale", "gm_gain", "w_spatial", "b_spatial", "g_mix_pre", "g_mix_post", "g_ff_pre",
                 "g_ff_post")
WEIGHT_ORDER = ("w_in", "w_pool", "pool_scale", "gm_gain", "w_spatial", "b_spatial", "w_br_sb", "w_br_pool", "w_br_gm",
                "w_out", "g_mix_pre", "g_mix_post", "g_ff_pre", "g_ff_post", "w_ff_in", "w_ff_out")


def _full_weight(name, g):
    half, cols = g.shape[2], g.shape[3]
    if name in COLUMN_SHARDED:
        return g.transpose(1, 2, 0, 3).reshape(2 * half, N_CHIPS * cols)
    return g.reshape(N_CHIPS * 2 * half, cols)


def _parts_by_chip(name, grad):
    if name in COLUMN_SHARDED:
        r, c = grad.shape[0], grad.shape[1] // N_CHIPS
        return grad.reshape(r, N_CHIPS, c).transpose(1, 0, 2)
    return grad.reshape(N_CHIPS, grad.shape[0] // N_CHIPS, grad.shape[1])


def kernel(x, w_in, w_pool, pool_scale, gm_gain, w_spatial, b_spatial, w_br_sb, w_br_pool, w_br_gm, w_out, g_mix_pre, g_mix_post, g_ff_pre, g_ff_post, w_ff_in, w_ff_out, loss_target, m_w_in, m_w_pool, m_pool_scale, m_gm_gain, m_w_spatial, m_b_spatial, m_w_br_sb, m_w_br_pool, m_w_br_gm, m_w_out, m_g_mix_pre, m_g_mix_post, m_g_ff_pre, m_g_ff_post, m_w_ff_in, m_w_ff_out, v_w_in, v_w_pool, v_pool_scale, v_gm_gain, v_w_spatial, v_b_spatial, v_w_br_sb, v_w_br_pool, v_w_br_gm, v_w_out, v_g_mix_pre, v_g_mix_post, v_g_ff_pre, v_g_ff_post, v_w_ff_in, v_w_ff_out):
    given = dict(locals())
    w = {n: given[n] for n in WEIGHT_ORDER}
    m = {n: given["m_" + n] for n in WEIGHT_ORDER}
    v = {n: given["v_" + n] for n in WEIGHT_ORDER}
    L = w_in.shape[0]

    core = lax.axis_index("c")

    def my_rows(a):
        half = a.shape[1] // 2
        return lax.dynamic_slice_in_dim(a.astype(BF16), core * half, half, axis=1)

    view = lambda n, a: jnp.swapaxes(a, 1, 2) if n in TRANSPOSED else a
    half_shards = {n: my_rows(view(n, w[n])) for n in BIG_WEIGHTS}
    small = {n: w[n] for n in SMALL_WEIGHTS}
    loss, dx, received, small_grads = _local_step(x[0], loss_target[0], half_shards, small, core)

    core_index = core.astype(jnp.int32).reshape(1)
    reduced = []
    for n in BIG_WEIGHTS:
        from_chips = jnp.concatenate([received[l][n] for l in range(L)], axis=1)
        reduced.append(_sum_slots(from_chips, name="sum_chips"))
    reduced_by_sibling = _sibling_swap(reduced, name="swap_reduced_halves")
    grads, deltas, new_m, new_v = {}, {}, {}, {}
    for n, mine, theirs in zip(BIG_WEIGHTS, reduced, reduced_by_sibling):
        by_layer = lambda a: a.reshape(L, -1, a.shape[-1])
        outs = _adamw_halves(view(n, w[n]), view(n, m[n]), view(n, v[n]), by_layer(mine), by_layer(theirs), core_index,
                             name="adamw_big")
        grads[n], deltas[n], new_m[n], new_v[n] = [view(n, o) for o in outs]

    by_layer = lambda a: a.reshape(L, -1, a.shape[-1])
    flat = lambda a: a.reshape(-1, a.shape[-1])
    outs = _small_update([[flat(g) for g in small_grads[n]] for n in SMALL_WEIGHTS],
                         *[[by_layer(t[n]) for n in SMALL_WEIGHTS] for t in (w, m, v)], name="small_update")
    for store, arrays in zip((grads, deltas, new_m, new_v), outs):
        store.update({n: a.reshape(w[n].shape) for n, a in zip(SMALL_WEIGHTS, arrays)})

    total_loss = lax.psum(loss[0, 0], ("x", "y", "c"))
    return (total_loss, dx[None], *[grads[n] for n in WEIGHT_ORDER], *[deltas[n] for n in WEIGHT_ORDER],
            *[new_m[n] for n in WEIGHT_ORDER], *[new_v[n] for n in WEIGHT_ORDER])
```
